```python
import jax, jax.numpy as jnp
from jax import lax
import numpy as np

D_MODEL = 1024
BATCH = 8
SEQ = 8192
DEPTH = 4
DEC_BATCH = 4
DEC_SEQ = 4096
PAST_LEN = 128

D_A = D_MODEL
HEAD_A = 64
H_A = D_A // HEAD_A
R_W = 64
R_A = 64
GN_EPS = 64e-5
HD = 64
HQ = D_MODEL // HD
HKV = HQ // 4
GRP = HQ // HKV
D_B = HQ * HD
D_KV = HKV * HD
WINDOW = 128
BLOCK = 128
ROPE_THETA = 10000.0
NORM_EPS = 1e-6
C_SHIFT = 3 * D_A + 2 * R_W + 2 * R_A
SPLITS = [C_SHIFT,
          C_SHIFT + D_A,
          C_SHIFT + D_A + D_B,
          C_SHIFT + D_A + D_B + D_KV,
          C_SHIFT + D_A + D_B + 2 * D_KV,
          C_SHIFT + D_A + 2 * D_B + 2 * D_KV]
N_IN = SPLITS[-1] + 2 * D_MODEL

kernel_name = "hybrid_rwkv7_swa_gated_encoder"


def rms_norm(x, g, eps=NORM_EPS):
    xf = x.astype(jnp.float32)
    y = xf * lax.rsqrt(jnp.mean(xf * xf, axis=-1, keepdims=True) + eps)
    return (y * g.astype(jnp.float32)).astype(x.dtype)


def centred_shift(p, mu):
    pad = jnp.pad(p, ((0, 0), (1, 1), (0, 0)))
    nbr = 0.5 * (pad[:, :-2] + pad[:, 2:])
    return p + mu * (nbr - p)


def rope(x, T):
    inv = 1.0 / (ROPE_THETA ** (jnp.arange(0, HD, 2, dtype=jnp.float32) / HD))
    ang = jnp.arange(T, dtype=jnp.float32)[:, None] * inv[None, :]
    cos = jnp.cos(ang)[None, :, None, :]
    sin = jnp.sin(ang)[None, :, None, :]
    x1, x2 = x[..., :HD // 2], x[..., HD // 2:]
    return jnp.concatenate([x1 * cos - x2 * sin, x2 * cos + x1 * sin], axis=-1)


def rwkv_step(S, inp):
    r, w, k, v, kk, akk = inp
    s_kk = jnp.einsum('dbhvk,dbhk->dbhv', S, kk)
    S = S * w[..., None, :] - s_kk[..., None] * akk[..., None, :] + v[..., None] * k[..., None, :]
    y = jnp.einsum('dbhvk,dbhk->dbhv', S, r)
    return S, y


def rwkv7_bidir(slab, w0, w_lora_up, a0, a_lora_up, k_k, k_a, r_k, ln_g, ln_b):
    B, T, _ = slab.shape
    f = jnp.swapaxes(slab, 0, 1).astype(jnp.float32)
    r, k, v, wd, ad = jnp.split(f, [D_A, 2 * D_A, 3 * D_A, 3 * D_A + 2 * R_W], axis=-1)

    def both(fwd, bwd):
        return jnp.stack([fwd, bwd[::-1]], axis=1)

    r2, k2, v2 = both(r, r), both(k, k), both(v, v)
    wd2 = both(wd[..., :R_W], wd[..., R_W:])
    ad2 = both(ad[..., :R_A], ad[..., R_A:])
    w_log = -jax.nn.softplus(-(w0.astype(jnp.float32)[None, :, None, :]
                               + jnp.einsum('tdbr,drc->tdbc', jnp.tanh(wd2), w_lora_up.astype(jnp.float32)))) - 0.5
    decay = jnp.exp(-jnp.exp(w_log))
    a = jax.nn.sigmoid(a0.astype(jnp.float32)[None, :, None, :]
                       + jnp.einsum('tdbr,drc->tdbc', ad2, a_lora_up.astype(jnp.float32)))
    heads = lambda z: z.reshape(T, 2, B, H_A, HEAD_A)
    kk = heads(k2 * k_k.astype(jnp.float32))
    kk = kk / jnp.maximum(jnp.sqrt(jnp.sum(kk * kk, axis=-1, keepdims=True)), 1e-12)
    kmod = heads(k2 * (1.0 + (a - 1.0) * k_a.astype(jnp.float32)))
    r2h, v2h, ah, dh = heads(r2), heads(v2), heads(a), heads(decay)
    bonus2 = jnp.sum(r2h * kmod * r_k.astype(jnp.float32), axis=-1, keepdims=True) * v2h
    S0 = jnp.zeros((2, B, H_A, HEAD_A, HEAD_A), jnp.float32)
    _, y2 = lax.scan(rwkv_step, S0, (r2h, dh, kmod, v2h, kk, kk * ah))
    y = y2[:, 0] + y2[::-1, 1]
    bonus = bonus2[:, 0] + bonus2[::-1, 1]
    mu = jnp.mean(y, axis=-1, keepdims=True)
    var = jnp.mean(jnp.square(y - mu), axis=-1, keepdims=True)
    yn = ((y - mu) * lax.rsqrt(var + GN_EPS)).reshape(T, B, D_A)
    yn = yn * ln_g.astype(jnp.float32) + ln_b.astype(jnp.float32) + bonus.reshape(T, B, D_A)
    return jnp.swapaxes(yn, 0, 1)


def banded_attention(q, k, v, q_g, k_g, sink):
    B, T, _ = q.shape
    q = q.astype(jnp.float32).reshape(B, T, HQ, HD)
    k = k.astype(jnp.float32).reshape(B, T, HKV, HD)
    v = v.astype(jnp.float32).reshape(B, T, HKV, HD)
    q = rope(rms_norm(q, q_g), T) * (HD ** -0.5)
    k = rope(rms_norm(k, k_g), T)
    NB = T // BLOCK
    CTX = BLOCK + 2 * WINDOW
    qb = jnp.transpose(q.reshape(B, NB, BLOCK, HKV, GRP, HD), (1, 0, 2, 3, 4, 5))
    kpad = jnp.pad(k, ((0, 0), (WINDOW, WINDOW), (0, 0), (0, 0)))
    vpad = jnp.pad(v, ((0, 0), (WINDOW, WINDOW), (0, 0), (0, 0)))
    sink_f = sink.astype(jnp.float32).reshape(HKV, GRP)[None, :, :, None, None]

    def one_block(args):
        n, qn = args
        start = n * BLOCK
        ks = lax.dynamic_slice_in_dim(kpad, start, CTX, axis=1)
        vs = lax.dynamic_slice_in_dim(vpad, start, CTX, axis=1)
        s = jnp.einsum('bqhgd,bkhd->bhgqk', qn, ks)
        qpos = start + jnp.arange(BLOCK)
        kpos = start - WINDOW + jnp.arange(CTX)
        valid = (jnp.abs(qpos[:, None] - kpos[None, :]) <= WINDOW) & (kpos >= 0)[None, :] & (kpos < T)[None, :]
        s = jnp.where(valid, s, -jnp.inf)
        m = jnp.maximum(jnp.max(s, axis=-1, keepdims=True), sink_f)
        p = jnp.exp(s - m)
        denom = jnp.sum(p, axis=-1, keepdims=True) + jnp.exp(sink_f - m)
        return jnp.einsum('bhgqk,bkhd->bqhgd', p / denom, vs)

    o = lax.map(one_block, (jnp.arange(NB), qb))
    return jnp.transpose(o, (1, 0, 2, 3, 4, 5)).reshape(B, T, D_B)


def layer(x, norm_g, w_in, shift_mu, w0, w_lora_up, a0, a_lora_up, k_k, k_a, r_k, ln_g, ln_b,
          q_g, k_g, sink, w_proj_a, w_proj_b, w_out):
    h = rms_norm(x, norm_g)
    p = jnp.einsum('btd,dc->btc', h, w_in)
    slab, gate_a, q, k, v, gate_b, merge = jnp.split(p, SPLITS, axis=-1)
    slab = centred_shift(slab, shift_mu)
    ya = rwkv7_bidir(slab, w0, w_lora_up, a0, a_lora_up, k_k, k_a, r_k, ln_g, ln_b)
    ya = jnp.einsum('btc,cd->btd', (ya * jax.nn.silu(gate_a.astype(jnp.float32))).astype(x.dtype), w_proj_a)
    yb = banded_attention(q, k, v, q_g, k_g, sink)
    yb = jnp.einsum('btc,cd->btd', (yb * jax.nn.silu(gate_b.astype(jnp.float32))).astype(x.dtype), w_proj_b)
    g_a, g_b = jnp.split(merge, 2, axis=-1)
    mixed = jax.nn.sigmoid(g_a) * ya + jax.nn.sigmoid(g_b) * yb
    return x + jnp.einsum('btd,de->bte', mixed, w_out)


def trunk(x, norm_g, w_in, shift_mu, w0, w_lora_up, a0, a_lora_up, k_k, k_a, r_k, ln_x_g, ln_x_b,
          q_norm_g, k_norm_g, sink, w_proj_a, w_proj_b, w_out):
    for l in range(DEPTH):
        x = layer(x, norm_g[l], w_in[l], shift_mu[l], w0[l], w_lora_up[l], a0[l], a_lora_up[l],
                  k_k[l], k_a[l], r_k[l], ln_x_g[l], ln_x_b[l], q_norm_g[l], k_norm_g[l], sink[l],
                  w_proj_a[l], w_proj_b[l], w_out[l])
    return x


def setup_inputs(seed: int = 0) -> dict:
    key = jax.random.key(seed)
    ks = jax.random.split(key, 20)
    n = lambda i, shape: jax.random.normal(ks[i], shape, jnp.float32)
    L = DEPTH
    return {
        "x_prompt": n(0, (BATCH, SEQ, D_MODEL)),
        "x_sample": n(1, (DEC_BATCH, DEC_SEQ, D_MODEL)),
        "norm_g": 1.0 + 0.05 * n(2, (L, D_MODEL)),
        "w_in": n(3, (L, D_MODEL, N_IN)) * D_MODEL ** -0.5,
        "shift_mu": jax.random.uniform(ks[4], (L, C_SHIFT), jnp.float32),
        "w0": -1.5 + n(5, (L, 2, D_A)),
        "w_lora_up": n(6, (L, 2, R_W, D_A)) * 0.3 * R_W ** -0.5,
        "a0": 0.5 * n(7, (L, 2, D_A)),
        "a_lora_up": n(8, (L, 2, R_A, D_A)) * 0.3 * R_A ** -0.5,
        "k_k": 0.85 + 0.05 * n(9, (L, D_A)),
        "k_a": 1.0 + 0.05 * n(10, (L, D_A)),
        "r_k": 0.1 * n(11, (L, H_A, HEAD_A)),
        "ln_x_g": 1.0 + 0.05 * n(12, (L, D_A)),
        "ln_x_b": 0.02 * n(13, (L, D_A)),
        "q_norm_g": 1.0 + 0.05 * n(14, (L, HD)),
        "k_norm_g": 1.0 + 0.05 * n(15, (L, HD)),
        "sink": 0.5 * n(16, (L, HQ)),
        "w_proj_a": n(17, (L, D_A, D_MODEL)) * D_A ** -0.5,
        "w_proj_b": n(18, (L, D_B, D_MODEL)) * D_B ** -0.5,
        "w_out": n(19, (L, D_MODEL, D_MODEL)) * D_MODEL ** -0.5,
    }


def reference(x_prompt, x_sample, norm_g, w_in, shift_mu, w0, w_lora_up, a0, a_lora_up, k_k, k_a, r_k,
              ln_x_g, ln_x_b, q_norm_g, k_norm_g, sink, w_proj_a, w_proj_b, w_out):
    y_prompt = trunk(x_prompt, norm_g, w_in, shift_mu, w0, w_lora_up, a0, a_lora_up, k_k, k_a, r_k,
                     ln_x_g, ln_x_b, q_norm_g, k_norm_g, sink, w_proj_a, w_proj_b, w_out)
    y_sample = trunk(x_sample, norm_g, w_in, shift_mu, w0, w_lora_up, a0, a_lora_up, k_k, k_a, r_k,
                     ln_x_g, ln_x_b, q_norm_g, k_norm_g, sink, w_proj_a, w_proj_b, w_out)
    return (y_prompt, y_sample)
```

```python
import functools

import jax
import jax.numpy as jnp
from jax import lax
from jax.experimental import pallas as pl
from jax.experimental.pallas import tpu as pltpu

D_MODEL = 1024
HEAD = 64
N_HEADS = D_MODEL // HEAD
R_LORA = 64
HKV = 4
GRP = N_HEADS // HKV
D_KV = HKV * HEAD
WINDOW = 128
BLOCK = 128
GN_EPS = 64e-5
NORM_EPS = 1e-6
ROPE_THETA = 10000.0
C_SHIFT = 3 * D_MODEL + 4 * R_LORA
N_PLANES = 9
CHUNK = 64
HALO = 8
VMEM_LIMIT = 56 * 1024 * 1024

P_R, P_K, P_V, P_GA, P_Q, P_GB, P_MA, P_MB, P_X = range(9)

f32 = jnp.float32
bf16 = jnp.bfloat16


def _dot(a, b):
    return jnp.dot(a.astype(bf16), b.astype(bf16), preferred_element_type=f32)


def _dot_nt(a, b):
    return lax.dot_general(a.astype(bf16), b.astype(bf16), (((1,), (1,)), ((), ())),
                           preferred_element_type=f32)


def _dot_tn(a, b):
    return lax.dot_general(a.astype(bf16), b.astype(bf16), (((0,), (0,)), ((), ())),
                           preferred_element_type=f32)


def _dot_split(a_exact, b):
    b0 = b.astype(bf16)
    r1 = b - b0.astype(f32)
    b1 = r1.astype(bf16)
    b2 = (r1 - b1.astype(f32)).astype(bf16)
    a = a_exact.astype(bf16)
    return (jnp.dot(a, b0, preferred_element_type=f32) + jnp.dot(a, b1, preferred_element_type=f32)
            + jnp.dot(a, b2, preferred_element_type=f32))


def _sigmoid(x):
    return 1.0 / (1.0 + jnp.exp(-x))


def _in_proj_body(x_ref, xp_ref, xn_ref, g_ref, w_ref, mu_ref, o_ref, h_ref, *, tm, seq):
    i = pl.program_id(0)
    j = pl.program_id(1)

    def norm(x):
        ms = jnp.mean(x * x, axis=-1, keepdims=True)
        return x * lax.rsqrt(ms + NORM_EPS) * g_ref[...]

    @pl.when(j == 0)
    def _():
        first = (i * tm) % seq == 0
        last = ((i + 1) * tm) % seq == 0
        h_ref[HALO:HALO + tm, :] = norm(x_ref[...]).astype(bf16)
        h_ref[0:HALO, :] = jnp.where(first, 0.0, norm(xp_ref[...])).astype(bf16)
        h_ref[HALO + tm:, :] = jnp.where(last, 0.0, norm(xn_ref[...])).astype(bf16)

    shifted = (j <= P_V) | (j == P_X)

    @pl.when(shifted)
    def _():
        p = jnp.dot(h_ref[...], w_ref[...], preferred_element_type=f32)
        pc = p[HALO:HALO + tm]
        nbr = 0.5 * (p[HALO - 1:HALO - 1 + tm] + p[HALO + 1:HALO + 1 + tm])
        o_ref[0] = pc + mu_ref[...] * (nbr - pc)

    @pl.when(jnp.logical_not(shifted))
    def _():
        o_ref[0] = jnp.dot(h_ref[HALO:HALO + tm, :], w_ref[...], preferred_element_type=f32)


def _in_proj(x2d, norm_g, w_p, mu_p, *, seq, tm):
    m = x2d.shape[0]
    nb8 = m // HALO
    return pl.pallas_call(
        functools.partial(_in_proj_body, tm=tm, seq=seq),
        grid=(m // tm, N_PLANES),
        in_specs=[
            pl.BlockSpec((tm, D_MODEL), lambda i, j: (i, 0)),
            pl.BlockSpec((HALO, D_MODEL), lambda i, j: (jnp.maximum(i * (tm // HALO) - 1, 0), 0)),
            pl.BlockSpec((HALO, D_MODEL), lambda i, j: (jnp.minimum((i + 1) * (tm // HALO), nb8 - 1), 0)),
            pl.BlockSpec((1, D_MODEL), lambda i, j: (0, 0)),
            pl.BlockSpec((D_MODEL, D_MODEL), lambda i, j: (0, j)),
            pl.BlockSpec((1, D_MODEL), lambda i, j: (0, j)),
        ],
        out_specs=pl.BlockSpec((1, tm, D_MODEL), lambda i, j: (j, i, 0)),
        out_shape=jax.ShapeDtypeStruct((N_PLANES, m, D_MODEL), f32),
        scratch_shapes=[pltpu.VMEM((tm + 2 * HALO, D_MODEL), bf16)],
        compiler_params=pltpu.CompilerParams(
            dimension_semantics=("parallel", "arbitrary"), vmem_limit_bytes=VMEM_LIMIT),
        name="in_proj",
    )(x2d, x2d, x2d, norm_g, w_p, mu_p)


def _prep_layer_weights(w_in, shift_mu):
    c = C_SHIFT
    d = D_MODEL
    cols = [
        (0, d), (d, 2 * d), (2 * d, 3 * d),
        (c, c + d),
        (c + d, c + 2 * d),
        (c + 2 * d + 2 * D_KV, c + 3 * d + 2 * D_KV),
        (c + 3 * d + 2 * D_KV, c + 4 * d + 2 * D_KV),
        (c + 4 * d + 2 * D_KV, c + 5 * d + 2 * D_KV),
    ]
    planes = [w_in[:, a:b] for a, b in cols]
    small = jnp.concatenate([w_in[:, 3 * d:c], w_in[:, c + 2 * d:c + 2 * d + 2 * D_KV],
                             jnp.zeros((d, d - 4 * R_LORA - 2 * D_KV), w_in.dtype)], axis=1)
    w_p = jnp.concatenate(planes + [small], axis=1).astype(bf16)
    zeros = jnp.zeros((d,), f32)
    mu_small = jnp.concatenate([shift_mu[3 * d:c], jnp.zeros((d - 4 * R_LORA,), f32)])
    mu_p = jnp.concatenate([shift_mu[0:d], shift_mu[d:2 * d], shift_mu[2 * d:3 * d]] + [zeros] * 5 + [mu_small])
    return w_p, mu_p.reshape(1, N_PLANES * d)


PAIR = 2 * HEAD


def _rwkv_p_body(r_ref, k_ref, v_ref, x_ref, w0_ref, wlu_ref, a0_ref, alu_ref, kk_ref, ka_ref, rk_ref,
                 y0_ref, bonus_ref, rp_ref, mn_ref,
                 at_s, rt_s, kh_s, bh_s, kp_s, bp_s, v_s, pl_s, *, tt):
    nc = tt // CHUNK
    r = r_ref[0]
    k = k_ref[0]
    v = v_ref[0]
    v_s[...] = v.astype(bf16)

    lane = lax.broadcasted_iota(jnp.int32, (PAIR, PAIR), 0) // HEAD
    seg = jnp.where(lane == lax.broadcasted_iota(jnp.int32, (PAIR, PAIR), 1) // HEAD, 1.0, 0.0)

    kkv = k * kk_ref[...]
    kk = kkv / jnp.maximum(jnp.sqrt(_segsum(kkv * kkv, seg)), 1e-12)

    rows = lax.broadcasted_iota(jnp.int32, (tt, tt), 0)
    cols = lax.broadcasted_iota(jnp.int32, (tt, tt), 1)
    same = (rows // CHUNK) == (cols // CHUNK)
    r64 = lax.broadcasted_iota(jnp.int32, (CHUNK, CHUNK), 0)
    c64 = lax.broadcasted_iota(jnp.int32, (CHUNK, CHUNK), 1)
    eye = r64 == c64
    ones_bd = jnp.where(same, 1.0, 0.0)

    kmod_sum = jnp.zeros_like(k)
    for d in range(2):
        if d == 0:
            tri_bd = jnp.where(same & (cols <= rows), 1.0, 0.0)
            strict, incl = c64 < r64, c64 <= r64
        else:
            tri_bd = jnp.where(same & (cols >= rows), 1.0, 0.0)
            strict, incl = c64 > r64, c64 >= r64
        wd = x_ref[0, :, d * R_LORA:(d + 1) * R_LORA]
        ad = x_ref[0, :, 2 * R_LORA + d * R_LORA:2 * R_LORA + (d + 1) * R_LORA]
        z = w0_ref[d:d + 1, :] + _dot(jnp.tanh(wd), wlu_ref[d])
        u = -z
        w_log = -(jnp.maximum(u, 0.0) + jnp.log(1.0 + jnp.exp(-jnp.abs(u)))) - 0.5
        lw = -jnp.exp(w_log)
        a = _sigmoid(a0_ref[d:d + 1, :] + _dot(ad, alu_ref[d]))
        kmod = k * (1.0 + (a - 1.0) * ka_ref[...])
        kmod_sum = kmod_sum + kmod
        b = -(kk * a)
        cum = _dot_split(tri_bd, lw)
        tot = _dot_split(ones_bd, lw)
        p_inv = jnp.exp(-cum)
        p_end = jnp.exp(tot - cum)
        at_s[...] = (kk * jnp.exp(cum - lw)).astype(bf16)
        rt_s[...] = (r * jnp.exp(cum)).astype(bf16)
        kh_s[...] = (kmod * p_inv).astype(bf16)
        bh_s[...] = (b * p_inv).astype(bf16)
        kp_s[...] = (kmod * p_end).astype(bf16)
        bp_s[...] = (b * p_end).astype(bf16)
        pl_s[...] = jnp.exp(tot)

        for c in range(nc):
            cs = slice(c * CHUNK, (c + 1) * CHUNK)
            for h in range(2):
                hs = slice(h * HEAD, (h + 1) * HEAD)
                at = at_s[cs, hs]
                rt = rt_s[cs, hs]
                vv = v_s[cs, hs]
                lhs = jnp.concatenate([at, rt], axis=0)
                sk = _dot_nt(lhs, kh_s[cs, hs])
                sb = _dot_nt(lhs, bh_s[cs, hs])
                a_ak = jnp.where(strict, sk[:CHUNK], 0.0)
                a_rk = jnp.where(incl, sk[CHUNK:], 0.0)
                a_ab = jnp.where(strict, sb[:CHUNK], 0.0)
                a_rb = jnp.where(incl, sb[CHUNK:], 0.0)
                x = jnp.concatenate([at.astype(f32), _dot(a_ak, vv)], axis=1)
                ap = a_ab
                n_dbl = CHUNK.bit_length() - 1
                for it in range(n_dbl):
                    x = x + _dot(ap, x)
                    if it + 1 < n_dbl:
                        ap = _dot(ap, ap)
                ry = _dot(a_rb, x) + jnp.concatenate([rt.astype(f32), _dot(a_rk, vv)], axis=1)
                diag = jnp.where(eye, pl_s[c * CHUNK:c * CHUNK + 1, hs], 0.0)
                mn = _dot_tn(bp_s[cs, hs], x) + jnp.concatenate([diag, _dot_tn(kp_s[cs, hs], vv)], axis=1)
                rp_ref[d, cs, hs] = ry[:, :HEAD].astype(bf16)
                if d == 0:
                    y0_ref[cs, hs] = ry[:, HEAD:]
                else:
                    y0_ref[cs, hs] += ry[:, HEAD:]
                mn_ref[d, c, h] = mn

    bonus_ref[...] = _segsum(r * kmod_sum * rk_ref[...], seg) * v


def _segsum(x, seg):
    return _dot_split_rhs(x, seg)


def _dot_split_rhs(x, m_exact):
    x0 = x.astype(bf16)
    r1 = x - x0.astype(f32)
    x1 = r1.astype(bf16)
    x2 = (r1 - x1.astype(f32)).astype(bf16)
    m = m_exact.astype(bf16)
    return (jnp.dot(x0, m, preferred_element_type=f32) + jnp.dot(x1, m, preferred_element_type=f32)
            + jnp.dot(x2, m, preferred_element_type=f32))


def _rwkv_p(p9, w0, wlu, a0, alu, k_k, k_a, r_k, *, tt):
    m = p9.shape[1]
    nt = m // tt
    npair = D_MODEL // PAIR
    cpt = tt // CHUNK
    tile = lambda plane: pl.BlockSpec((1, tt, PAIR), lambda i, j, plane=plane: (plane, i, j))
    vec = pl.BlockSpec((1, PAIR), lambda i, j: (0, j))
    return pl.pallas_call(
        functools.partial(_rwkv_p_body, tt=tt),
        grid=(nt, npair),
        in_specs=[
            tile(P_R), tile(P_K), tile(P_V),
            pl.BlockSpec((1, tt, 4 * R_LORA), lambda i, j: (P_X, i, 0)),
            pl.BlockSpec((2, PAIR), lambda i, j: (0, j)),
            pl.BlockSpec((2, R_LORA, PAIR), lambda i, j: (0, 0, j)),
            pl.BlockSpec((2, PAIR), lambda i, j: (0, j)),
            pl.BlockSpec((2, R_LORA, PAIR), lambda i, j: (0, 0, j)),
            vec, vec, vec,
        ],
        out_specs=[
            pl.BlockSpec((tt, PAIR), lambda i, j: (i, j)),
            pl.BlockSpec((tt, PAIR), lambda i, j: (i, j)),
            pl.BlockSpec((2, tt, PAIR), lambda i, j: (0, i, j)),
            pl.BlockSpec((2, cpt, 2, HEAD, PAIR), lambda i, j: (0, i, j, 0, 0)),
        ],
        out_shape=[
            jax.ShapeDtypeStruct((m, D_MODEL), f32),
            jax.ShapeDtypeStruct((m, D_MODEL), f32),
            jax.ShapeDtypeStruct((2, m, D_MODEL), bf16),
            jax.ShapeDtypeStruct((2, m // CHUNK, N_HEADS, HEAD, PAIR), f32),
        ],
        scratch_shapes=[pltpu.VMEM((tt, PAIR), bf16) for _ in range(7)] + [pltpu.VMEM((tt, PAIR), f32)],
        compiler_params=pltpu.CompilerParams(
            dimension_semantics=("parallel", "parallel"), vmem_limit_bytes=VMEM_LIMIT),
        name="rwkv_p",
    )(p9, p9, p9, p9, w0, wlu, a0, alu, k_k, k_a, r_k)


def _rwkv_s_body(mnf_ref, mnb_ref, sf_ref, sb_ref, st_ref, *, grp):
    @pl.when(pl.program_id(1) == 0)
    def _():
        st_ref[...] = jnp.zeros_like(st_ref)

    def step(d, mn_ref, s_out_ref, cc):
        s = st_ref[d]
        s_out_ref[cc] = s.astype(bf16)
        m_t = mn_ref[0, cc, :, :, 0:HEAD].astype(bf16)
        n_t = mn_ref[0, cc, :, :, HEAD:PAIR]
        st_ref[d] = jnp.einsum('hij,hjk->hik', m_t, s.astype(bf16), preferred_element_type=f32) + n_t

    for cc in range(grp):
        step(0, mnf_ref, sf_ref, cc)
        step(1, mnb_ref, sb_ref, grp - 1 - cc)


def _rwkv_s(mn, *, n_seq, grp):
    nchunk = mn.shape[1]
    ng = nchunk // n_seq // grp
    blk_in = (1, grp, N_HEADS, HEAD, PAIR)
    blk_out = (grp, N_HEADS, HEAD, HEAD)
    out = jax.ShapeDtypeStruct((nchunk, N_HEADS, HEAD, HEAD), bf16)
    return pl.pallas_call(
        functools.partial(_rwkv_s_body, grp=grp),
        grid=(n_seq, ng),
        in_specs=[pl.BlockSpec(blk_in, lambda b, g: (0, b * ng + g, 0, 0, 0)),
                  pl.BlockSpec(blk_in, lambda b, g: (1, b * ng + ng - 1 - g, 0, 0, 0))],
        out_specs=[pl.BlockSpec(blk_out, lambda b, g: (b * ng + g, 0, 0, 0)),
                   pl.BlockSpec(blk_out, lambda b, g: (b * ng + ng - 1 - g, 0, 0, 0))],
        out_shape=[out, out],
        scratch_shapes=[pltpu.VMEM((2, N_HEADS, HEAD, HEAD), f32)],
        compiler_params=pltpu.CompilerParams(
            dimension_semantics=("parallel", "arbitrary"), vmem_limit_bytes=VMEM_LIMIT),
        name="rwkv_s",
    )(mn, mn)


def _rwkv_f_body(y0_ref, bonus_ref, rp_ref, sf_ref, sb_ref, gate_ref, lng_ref, lnb_ref, o_ref, y_s, *, tt):
    for c in range(tt // CHUNK):
        cs = slice(c * CHUNK, (c + 1) * CHUNK)
        for h in range(2):
            hs = slice(h * HEAD, (h + 1) * HEAD)
            y_s[cs, hs] = (y0_ref[cs, hs]
                           + jnp.dot(rp_ref[0, cs, hs], sf_ref[c, h], preferred_element_type=f32)
                           + jnp.dot(rp_ref[1, cs, hs], sb_ref[c, h], preferred_element_type=f32))
    lane = lax.broadcasted_iota(jnp.int32, (PAIR, PAIR), 0) // HEAD
    seg = jnp.where(lane == lax.broadcasted_iota(jnp.int32, (PAIR, PAIR), 1) // HEAD, 1.0, 0.0)
    y = y_s[...]
    mu = _segsum(y, seg) * (1.0 / HEAD)
    dv = y - mu
    var = _segsum(dv * dv, seg) * (1.0 / HEAD)
    yn = dv * lax.rsqrt(var + GN_EPS) * lng_ref[...] + lnb_ref[...] + bonus_ref[...]
    g = gate_ref[0]
    o_ref[...] = (yn * (g * _sigmoid(g))).astype(bf16)


def _rwkv_f(y0, bonus, rp, sf, sb, p9, ln_g, ln_b, *, tt):
    m = y0.shape[0]
    cpt = tt // CHUNK
    tile = pl.BlockSpec((tt, PAIR), lambda i, j: (i, j))
    vec = pl.BlockSpec((1, PAIR), lambda i, j: (0, j))
    state = pl.BlockSpec((cpt, 2, HEAD, HEAD), lambda i, j: (i, j, 0, 0))
    return pl.pallas_call(
        functools.partial(_rwkv_f_body, tt=tt),
        grid=(m // tt, D_MODEL // PAIR),
        in_specs=[
            tile, tile,
            pl.BlockSpec((2, tt, PAIR), lambda i, j: (0, i, j)),
            state, state,
            pl.BlockSpec((1, tt, PAIR), lambda i, j: (P_GA, i, j)),
            vec, vec,
        ],
        out_specs=tile,
        out_shape=jax.ShapeDtypeStruct((m, D_MODEL), bf16),
        scratch_shapes=[pltpu.VMEM((tt, PAIR), f32)],
        compiler_params=pltpu.CompilerParams(
            dimension_semantics=("parallel", "parallel"), vmem_limit_bytes=VMEM_LIMIT),
        name="rwkv_f",
    )(y0, bonus, rp, sf, sb, p9, ln_g, ln_b)


def _attn_body(q_ref, kp_ref, kc_ref, kn_ref, vp_ref, vc_ref, vn_ref, tp_ref, tc_ref, tn_ref, gate_ref,
               qg_ref, kg_ref, sink_ref, o_ref, *, seq):
    nblk = seq // BLOCK
    start = (pl.program_id(0) % nblk) * BLOCK
    ctx = BLOCK + 2 * WINDOW
    qpos = start + lax.broadcasted_iota(jnp.int32, (BLOCK, ctx), 0)
    kpos = start - WINDOW + lax.broadcasted_iota(jnp.int32, (BLOCK, ctx), 1)
    valid = (jnp.abs(qpos - kpos) <= WINDOW) & (kpos >= 0) & (kpos < seq)

    def norm_rope(x, gain, tab):
        xn = x * lax.rsqrt(jnp.mean(x * x, axis=-1, keepdims=True) + NORM_EPS) * gain
        rot = jnp.concatenate([xn[:, HEAD // 2:], xn[:, :HEAD // 2]], axis=1)
        return xn * tab[:, :HEAD] + rot * tab[:, HEAD:]

    tab_k = jnp.concatenate([tp_ref[...], tc_ref[...], tn_ref[...]], axis=0)
    tab_q = tc_ref[...]
    for g in range(HKV):
        gs = slice(g * HEAD, (g + 1) * HEAD)
        kx = jnp.concatenate([kp_ref[0, :, gs], kc_ref[0, :, gs], kn_ref[0, :, gs]], axis=0)
        vx = jnp.concatenate([vp_ref[0, :, gs], vc_ref[0, :, gs], vn_ref[0, :, gs]], axis=0).astype(bf16)
        kr = norm_rope(kx, kg_ref[...], tab_k).astype(bf16)
        for e in range(GRP):
            hq = g * GRP + e
            hs = slice(hq * HEAD, (hq + 1) * HEAD)
            qr = norm_rope(q_ref[0, :, hs], qg_ref[...], tab_q) * (HEAD ** -0.5)
            s = jnp.where(valid, _dot_nt(qr, kr), -1e30)
            sink = sink_ref[0:1, hq:hq + 1]
            mx = jnp.maximum(jnp.max(s, axis=-1, keepdims=True), sink)
            p = jnp.exp(s - mx)
            den = jnp.sum(p, axis=-1, keepdims=True) + jnp.exp(sink - mx)
            o = jnp.dot(p.astype(bf16), vx, preferred_element_type=f32) / den
            gt = gate_ref[0, :, hs]
            o_ref[:, hs] = (o * (gt * _sigmoid(gt))).astype(bf16)


def _attn(p9, rope_tab, q_g, k_g, sink, *, seq):
    m = p9.shape[1]
    nblk = seq // BLOCK

    def nbr(delta):
        def f(i):
            t = i % nblk
            return i - t + jnp.clip(t + delta, 0, nblk - 1)
        return f

    def kv(col, delta):
        return pl.BlockSpec((1, BLOCK, D_KV), lambda i, col=col, f=nbr(delta): (P_X, f(i), col))

    def tab(delta):
        return pl.BlockSpec((BLOCK, PAIR), lambda i, f=nbr(delta): (f(i) % nblk, 0))

    return pl.pallas_call(
        functools.partial(_attn_body, seq=seq),
        grid=(m // BLOCK,),
        in_specs=[
            pl.BlockSpec((1, BLOCK, D_MODEL), lambda i: (P_Q, i, 0)),
            kv(1, -1), kv(1, 0), kv(1, 1), kv(2, -1), kv(2, 0), kv(2, 1),
            tab(-1), tab(0), tab(1),
            pl.BlockSpec((1, BLOCK, D_MODEL), lambda i: (P_GB, i, 0)),
            pl.BlockSpec((1, HEAD), lambda i: (0, 0)),
            pl.BlockSpec((1, HEAD), lambda i: (0, 0)),
            pl.BlockSpec((1, N_HEADS), lambda i: (0, 0)),
        ],
        out_specs=pl.BlockSpec((BLOCK, D_MODEL), lambda i: (i, 0)),
        out_shape=jax.ShapeDtypeStruct((m, D_MODEL), bf16),
        compiler_params=pltpu.CompilerParams(dimension_semantics=("parallel",), vmem_limit_bytes=VMEM_LIMIT),
        name="attn",
    )(p9, p9, p9, p9, p9, p9, p9, rope_tab, rope_tab, rope_tab, p9, q_g, k_g, sink)


def _rope_table(seq):
    inv = 1.0 / (ROPE_THETA ** (jnp.arange(0, HEAD, 2, dtype=f32) / HEAD))
    ang = jnp.arange(seq, dtype=f32)[:, None] * inv[None, :]
    cos, sin = jnp.cos(ang), jnp.sin(ang)
    return jnp.concatenate([cos, cos, -sin, sin], axis=1)


def _out_proj_body(x_ref, ga_ref, gb_ref, ma_ref, mb_ref, wa_ref, wb_ref, wo_ref, o_ref):
    ya = jnp.dot(ga_ref[...], wa_ref[...], preferred_element_type=f32)
    yb = jnp.dot(gb_ref[...], wb_ref[...], preferred_element_type=f32)
    mixed = _sigmoid(ma_ref[0]) * ya + _sigmoid(mb_ref[0]) * yb
    o_ref[...] = x_ref[...] + jnp.dot(mixed.astype(bf16), wo_ref[...], preferred_element_type=f32)


def _out_proj(x2d, ga, gb, p9, wa, wb, wo, *, tm):
    m = x2d.shape[0]
    row = pl.BlockSpec((tm, D_MODEL), lambda i: (i, 0))
    wspec = pl.BlockSpec((D_MODEL, D_MODEL), lambda i: (0, 0))
    return pl.pallas_call(
        _out_proj_body,
        grid=(m // tm,),
        in_specs=[row, row, row,
                  pl.BlockSpec((1, tm, D_MODEL), lambda i: (P_MA, i, 0)),
                  pl.BlockSpec((1, tm, D_MODEL), lambda i: (P_MB, i, 0)),
                  wspec, wspec, wspec],
        out_specs=row,
        out_shape=jax.ShapeDtypeStruct((m, D_MODEL), f32),
        compiler_params=pltpu.CompilerParams(dimension_semantics=("parallel",), vmem_limit_bytes=VMEM_LIMIT),
        name="out_proj",
    )(x2d, ga, gb, p9, p9, wa, wb, wo)


def _layer(x2d, lw, rope_tab, *, n_seq, seq):
    tm = min(1024, seq)
    tt = min(256, seq)
    grp = min(8, seq // CHUNK)
    p9 = _in_proj(x2d, lw["norm_g"], lw["w_p"], lw["mu_p"], seq=seq, tm=tm)
    y0, bonus, rp, mn = _rwkv_p(p9, lw["w0"], lw["wlu"], lw["a0"], lw["alu"], lw["k_k"], lw["k_a"], lw["r_k"], tt=tt)
    sf, sb = _rwkv_s(mn, n_seq=n_seq, grp=grp)
    ga = _rwkv_f(y0, bonus, rp, sf, sb, p9, lw["ln_g"], lw["ln_b"], tt=tt)
    gb = _attn(p9, rope_tab, lw["q_g"], lw["k_g"], lw["sink"], seq=seq)
    return _out_proj(x2d, ga, gb, p9, lw["wa"], lw["wb"], lw["wo"], tm=tm)


def _trunk(x, layers):
    n_seq, seq, _ = x.shape
    rope_tab = _rope_table(seq)
    x2d = x.reshape(n_seq * seq, D_MODEL)
    for lw in layers:
        x2d = _layer(x2d, lw, rope_tab, n_seq=n_seq, seq=seq)
    return x2d.reshape(n_seq, seq, D_MODEL)


def _prep_layers(norm_g, w_in, shift_mu, w0, w_lora_up, a0, a_lora_up, k_k, k_a, r_k, ln_x_g, ln_x_b,
                 q_norm_g, k_norm_g, sink, w_proj_a, w_proj_b, w_out):
    layers = []
    row = lambda v: v.reshape(1, -1).astype(f32)
    for l in range(norm_g.shape[0]):
        w_p, mu_p = _prep_layer_weights(w_in[l], shift_mu[l])
        layers.append(dict(
            norm_g=row(norm_g[l]), w_p=w_p, mu_p=mu_p,
            w0=w0[l].astype(f32), wlu=w_lora_up[l].astype(bf16), a0=a0[l].astype(f32), alu=a_lora_up[l].astype(bf16),
            k_k=row(k_k[l]), k_a=row(k_a[l]), r_k=row(r_k[l]), ln_g=row(ln_x_g[l]), ln_b=row(ln_x_b[l]),
            q_g=row(q_norm_g[l]), k_g=row(k_norm_g[l]), sink=row(sink[l]),
            wa=w_proj_a[l].astype(bf16), wb=w_proj_b[l].astype(bf16), wo=w_out[l].astype(bf16)))
    return layers


def kernel(x_prompt, x_sample, norm_g, w_in, shift_mu, w0, w_lora_up, a0, a_lora_up, k_k, k_a, r_k,
           ln_x_g, ln_x_b, q_norm_g, k_norm_g, sink, w_proj_a, w_proj_b, w_out):
    layers = _prep_layers(norm_g, w_in, shift_mu, w0, w_lora_up, a0, a_lora_up, k_k, k_a, r_k, ln_x_g, ln_x_b,
                          q_norm_g, k_norm_g, sink, w_proj_a, w_proj_b, w_out)
    return _trunk(x_prompt, layers), _trunk(x_sample, layers)
```

```python
import functools

import jax
import jax.numpy as jnp
from jax import lax
from jax.experimental import pallas as pl
from jax.experimental.pallas import tpu as pltpu

D_MODEL = 1024
HEAD = 64
N_HEADS = D_MODEL // HEAD
R_LORA = 64
HKV = 4
GRP = N_HEADS // HKV
D_KV = HKV * HEAD
WINDOW = 128
BLOCK = 128
GN_EPS = 64e-5
NORM_EPS = 1e-6
ROPE_THETA = 10000.0
C_SHIFT = 3 * D_MODEL + 4 * R_LORA
N_PLANES = 9
CHUNK = 64
HALO = 8
VMEM_LIMIT = 56 * 1024 * 1024

P_R, P_K, P_V, P_GA, P_Q, P_GB, P_MA, P_MB, P_X = range(9)

f32 = jnp.float32
bf16 = jnp.bfloat16


def _dot(a, b):
    return jnp.dot(a.astype(bf16), b.astype(bf16), preferred_element_type=f32)


def _dot_nt(a, b):
    return lax.dot_general(a.astype(bf16), b.astype(bf16), (((1,), (1,)), ((), ())),
                           preferred_element_type=f32)


def _dot_tn(a, b):
    return lax.dot_general(a.astype(bf16), b.astype(bf16), (((0,), (0,)), ((), ())),
                           preferred_element_type=f32)


def _dot_split(a_exact, b):
    b0 = b.astype(bf16)
    r1 = b - b0.astype(f32)
    b1 = r1.astype(bf16)
    b2 = (r1 - b1.astype(f32)).astype(bf16)
    a = a_exact.astype(bf16)
    return (jnp.dot(a, b0, preferred_element_type=f32) + jnp.dot(a, b1, preferred_element_type=f32)
            + jnp.dot(a, b2, preferred_element_type=f32))


def _sigmoid(x):
    return 1.0 / (1.0 + jnp.exp(-x))


def _in_proj_body(x_ref, xp_ref, xn_ref, g_ref, w_ref, mu_ref, o_ref, h_ref, *, tm, seq):
    i = pl.program_id(0)
    j = pl.program_id(1)

    def norm(x):
        ms = jnp.mean(x * x, axis=-1, keepdims=True)
        return x * lax.rsqrt(ms + NORM_EPS) * g_ref[...]

    @pl.when(j == 0)
    def _():
        first = (i * tm) % seq == 0
        last = ((i + 1) * tm) % seq == 0
        h_ref[HALO:HALO + tm, :] = norm(x_ref[...]).astype(bf16)
        h_ref[0:HALO, :] = jnp.where(first, 0.0, norm(xp_ref[...])).astype(bf16)
        h_ref[HALO + tm:, :] = jnp.where(last, 0.0, norm(xn_ref[...])).astype(bf16)

    shifted = (j <= P_V) | (j == P_X)

    @pl.when(shifted)
    def _():
        p = jnp.dot(h_ref[...], w_ref[...], preferred_element_type=f32)
        pc = p[HALO:HALO + tm]
        nbr = 0.5 * (p[HALO - 1:HALO - 1 + tm] + p[HALO + 1:HALO + 1 + tm])
        o_ref[0] = pc + mu_ref[...] * (nbr - pc)

    @pl.when(jnp.logical_not(shifted))
    def _():
        o_ref[0] = jnp.dot(h_ref[HALO:HALO + tm, :], w_ref[...], preferred_element_type=f32)


def _in_proj(x2d, norm_g, w_p, mu_p, *, seq, tm):
    m = x2d.shape[0]
    nb8 = m // HALO
    return pl.pallas_call(
        functools.partial(_in_proj_body, tm=tm, seq=seq),
        grid=(m // tm, N_PLANES),
        in_specs=[
            pl.BlockSpec((tm, D_MODEL), lambda i, j: (i, 0)),
            pl.BlockSpec((HALO, D_MODEL), lambda i, j: (jnp.maximum(i * (tm // HALO) - 1, 0), 0)),
            pl.BlockSpec((HALO, D_MODEL), lambda i, j: (jnp.minimum((i + 1) * (tm // HALO), nb8 - 1), 0)),
            pl.BlockSpec((1, D_MODEL), lambda i, j: (0, 0)),
            pl.BlockSpec((D_MODEL, D_MODEL), lambda i, j: (0, j)),
            pl.BlockSpec((1, D_MODEL), lambda i, j: (0, j)),
        ],
        out_specs=pl.BlockSpec((1, tm, D_MODEL), lambda i, j: (j, i, 0)),
        out_shape=jax.ShapeDtypeStruct((N_PLANES, m, D_MODEL), f32),
        scratch_shapes=[pltpu.VMEM((tm + 2 * HALO, D_MODEL), bf16)],
        compiler_params=pltpu.CompilerParams(
            dimension_semantics=("parallel", "arbitrary"), vmem_limit_bytes=VMEM_LIMIT),
        name="in_proj",
    )(x2d, x2d, x2d, norm_g, w_p, mu_p)


def _prep_layer_weights(w_in, shift_mu):
    c = C_SHIFT
    d = D_MODEL
    cols = [
        (0, d), (d, 2 * d), (2 * d, 3 * d),
        (c, c + d),
        (c + d, c + 2 * d),
        (c + 2 * d + 2 * D_KV, c + 3 * d + 2 * D_KV),
        (c + 3 * d + 2 * D_KV, c + 4 * d + 2 * D_KV),
        (c + 4 * d + 2 * D_KV, c + 5 * d + 2 * D_KV),
    ]
    planes = [w_in[:, a:b] for a, b in cols]
    small = jnp.concatenate([w_in[:, 3 * d:c], w_in[:, c + 2 * d:c + 2 * d + 2 * D_KV],
                             jnp.zeros((d, d - 4 * R_LORA - 2 * D_KV), w_in.dtype)], axis=1)
    w_p = jnp.concatenate(planes + [small], axis=1).astype(bf16)
    zeros = jnp.zeros((d,), f32)
    mu_small = jnp.concatenate([shift_mu[3 * d:c], jnp.zeros((d - 4 * R_LORA,), f32)])
    mu_p = jnp.concatenate([shift_mu[0:d], shift_mu[d:2 * d], shift_mu[2 * d:3 * d]] + [zeros] * 5 + [mu_small])
    return w_p, mu_p.reshape(1, N_PLANES * d)


PAIR = 2 * HEAD


def _rwkv_p_body(r_ref, k_ref, v_ref, x_ref, w0_ref, wlu_ref, a0_ref, alu_ref, kk_ref, ka_ref, rk_ref,
                 y0_ref, bonus_ref, rp_ref, mn_ref,
                 at_s, rt_s, kp_s, bp_s, khm_s, bhm_s, atx_s, rtx_s, vx_s, pl_s, *, tt):
    nc = tt // CHUNK
    r = r_ref[0]
    k = k_ref[0]
    v = v_ref[0]

    lane = lax.broadcasted_iota(jnp.int32, (PAIR, PAIR), 0) // HEAD
    seg = jnp.where(lane == lax.broadcasted_iota(jnp.int32, (PAIR, PAIR), 1) // HEAD, 1.0, 0.0)
    lo = lax.broadcasted_iota(jnp.int32, (tt, PAIR), 1) < HEAD
    swap = lambda t: pltpu.roll(t, HEAD, axis=1)

    def head_lo(t, h):
        return jnp.where(lo, t if h == 0 else swap(t), 0.0)

    def head_hi(t, h):
        return jnp.where(lo, 0.0, swap(t) if h == 0 else t)

    def head_own(t, h):
        return jnp.where(lo, t, 0.0) if h == 0 else jnp.where(lo, 0.0, t)

    for h in range(2):
        vx_s[h] = head_hi(v, h).astype(bf16)

    kkv = k * kk_ref[...]
    kk = kkv / jnp.maximum(jnp.sqrt(_segsum(kkv * kkv, seg)), 1e-12)

    rows = lax.broadcasted_iota(jnp.int32, (tt, tt), 0)
    cols = lax.broadcasted_iota(jnp.int32, (tt, tt), 1)
    same = (rows // CHUNK) == (cols // CHUNK)
    ones_bd = jnp.where(same, 1.0, 0.0)

    kmod_sum = jnp.zeros_like(k)
    for d in range(2):
        if d == 0:
            tri_bd = jnp.where(same & (cols <= rows), 1.0, 0.0)
        else:
            tri_bd = jnp.where(same & (cols >= rows), 1.0, 0.0)
        wd = x_ref[0, :, d * R_LORA:(d + 1) * R_LORA]
        ad = x_ref[0, :, 2 * R_LORA + d * R_LORA:2 * R_LORA + (d + 1) * R_LORA]
        z = w0_ref[d:d + 1, :] + _dot(jnp.tanh(wd), wlu_ref[d])
        u = -z
        w_log = -(jnp.maximum(u, 0.0) + jnp.log(1.0 + jnp.exp(-jnp.abs(u)))) - 0.5
        lw = -jnp.exp(w_log)
        a = _sigmoid(a0_ref[d:d + 1, :] + _dot(ad, alu_ref[d]))
        kmod = k * (1.0 + (a - 1.0) * ka_ref[...])
        kmod_sum = kmod_sum + kmod
        b = -(kk * a)
        cum = _dot_split(tri_bd, lw)
        tot = _dot_split(ones_bd, lw)
        p_inv = jnp.exp(-cum)
        p_end = jnp.exp(tot - cum)
        at = kk * jnp.exp(cum - lw)
        rt = r * jnp.exp(cum)
        kh = kmod * p_inv
        bh = b * p_inv
        p_tot = jnp.exp(tot)
        at_s[d] = at.astype(bf16)
        rt_s[d] = rt.astype(bf16)
        kp_s[d] = (kmod * p_end).astype(bf16)
        bp_s[d] = (b * p_end).astype(bf16)
        for h in range(2):
            khm_s[d, h] = head_own(kh, h).astype(bf16)
            bhm_s[d, h] = head_own(bh, h).astype(bf16)
            atx_s[d, h] = head_lo(at, h).astype(bf16)
            rtx_s[d, h] = head_lo(rt, h)
            pl_s[d, h] = head_lo(p_tot, h)

    bonus_ref[...] = _segsum(r * kmod_sum * rk_ref[...], seg) * v

    row = lax.broadcasted_iota(jnp.int32, (CHUNK, PAIR), 0)
    col = lax.broadcasted_iota(jnp.int32, (CHUNK, PAIR), 1)
    col_t = col % CHUNK
    lo_c = col < HEAD
    strict = (col_t < row, col_t > row)
    incl = (col_t <= row, col_t >= row)
    inst = [(d, c, h) for d in range(2) for c in range(nc) for h in range(2)]
    cs = lambda c: slice(c * CHUNK, (c + 1) * CHUNK)

    top, bot = [], []
    for d, c, h in inst:
        lhs = jnp.concatenate([at_s[d, cs(c)], rt_s[d, cs(c)]], axis=0)
        rhs = jnp.concatenate([bhm_s[d, h, cs(c)], khm_s[d, h, cs(c)]], axis=0)
        sc = lax.dot_general(lhs, rhs, (((1,), (1,)), ((), ())), preferred_element_type=f32)
        top.append(jnp.where(strict[d], sc[:CHUNK], 0.0))
        bot.append(jnp.where(incl[d], sc[CHUNK:], 0.0))
    xs, aps = [], []
    for i, (d, c, h) in enumerate(inst):
        vx = vx_s[h, cs(c)]
        a_ak = jnp.where(lo_c, 0.0, top[i]).astype(bf16)
        akv = jnp.dot(a_ak, jnp.concatenate([vx, vx], axis=0), preferred_element_type=f32)
        xs.append(atx_s[d, h, cs(c)].astype(f32) + akv)
        aps.append(top[i][:, :CHUNK])
    n_dbl = CHUNK.bit_length() - 1
    for it in range(n_dbl):
        last = it + 1 == n_dbl
        res = [_dot(ap, x if last else jnp.concatenate([x, ap], axis=1)) for ap, x in zip(aps, xs)]
        xs = [x + rs[:, :PAIR] for x, rs in zip(xs, res)]
        if not last:
            aps = [rs[:, PAIR:] for rs in res]
    for i, (d, c, h) in enumerate(inst):
        hs = slice(h * HEAD, (h + 1) * HEAD)
        rhs = jnp.concatenate([xs[i].astype(bf16), vx_s[h, cs(c)]], axis=0)
        ry = _dot(bot[i], rhs) + rtx_s[d, h, cs(c)]
        diag = jnp.where(row == col, pl_s[d, h, c * CHUNK:c * CHUNK + 1, :], 0.0)
        lhs_t = jnp.concatenate([bp_s[d, cs(c), hs], kp_s[d, cs(c), hs]], axis=0)
        mn_ref[d, c, h] = _dot_tn(lhs_t, rhs) + diag
        rp_ref[d, cs(c), hs] = ry[:, :HEAD].astype(bf16)
        if d == 0:
            y0_ref[cs(c), hs] = ry[:, HEAD:]
        else:
            y0_ref[cs(c), hs] += ry[:, HEAD:]


def _segsum(x, seg):
    return _dot_split_rhs(x, seg)


def _dot_split_rhs(x, m_exact):
    x0 = x.astype(bf16)
    r1 = x - x0.astype(f32)
    x1 = r1.astype(bf16)
    x2 = (r1 - x1.astype(f32)).astype(bf16)
    m = m_exact.astype(bf16)
    return (jnp.dot(x0, m, preferred_element_type=f32) + jnp.dot(x1, m, preferred_element_type=f32)
            + jnp.dot(x2, m, preferred_element_type=f32))


def _rwkv_p(p9, w0, wlu, a0, alu, k_k, k_a, r_k, *, tt):
    m = p9.shape[1]
    nt = m // tt
    npair = D_MODEL // PAIR
    cpt = tt // CHUNK
    tile = lambda plane: pl.BlockSpec((1, tt, PAIR), lambda i, j, plane=plane: (plane, i, j))
    vec = pl.BlockSpec((1, PAIR), lambda i, j: (0, j))
    return pl.pallas_call(
        functools.partial(_rwkv_p_body, tt=tt),
        grid=(nt, npair),
        in_specs=[
            tile(P_R), tile(P_K), tile(P_V),
            pl.BlockSpec((1, tt, 4 * R_LORA), lambda i, j: (P_X, i, 0)),
            pl.BlockSpec((2, PAIR), lambda i, j: (0, j)),
            pl.BlockSpec((2, R_LORA, PAIR), lambda i, j: (0, 0, j)),
            pl.BlockSpec((2, PAIR), lambda i, j: (0, j)),
            pl.BlockSpec((2, R_LORA, PAIR), lambda i, j: (0, 0, j)),
            vec, vec, vec,
        ],
        out_specs=[
            pl.BlockSpec((tt, PAIR), lambda i, j: (i, j)),
            pl.BlockSpec((tt, PAIR), lambda i, j: (i, j)),
            pl.BlockSpec((2, tt, PAIR), lambda i, j: (0, i, j)),
            pl.BlockSpec((2, cpt, 2, HEAD, PAIR), lambda i, j: (0, i, j, 0, 0)),
        ],
        out_shape=[
            jax.ShapeDtypeStruct((m, D_MODEL), f32),
            jax.ShapeDtypeStruct((m, D_MODEL), f32),
            jax.ShapeDtypeStruct((2, m, D_MODEL), bf16),
            jax.ShapeDtypeStruct((2, m // CHUNK, N_HEADS, HEAD, PAIR), f32),
        ],
        scratch_shapes=(
            [pltpu.VMEM((2, tt, PAIR), bf16) for _ in range(4)]
            + [pltpu.VMEM((2, 2, tt, PAIR), bf16) for _ in range(3)]
            + [pltpu.VMEM((2, 2, tt, PAIR), f32),
               pltpu.VMEM((2, tt, PAIR), bf16),
               pltpu.VMEM((2, 2, tt, PAIR), f32)]),
        compiler_params=pltpu.CompilerParams(
            dimension_semantics=("parallel", "parallel"), vmem_limit_bytes=VMEM_LIMIT),
        name="rwkv_p",
    )(p9, p9, p9, p9, w0, wlu, a0, alu, k_k, k_a, r_k)


def _rwkv_s_body(mnf_ref, mnb_ref, sf_ref, sb_ref, st_ref, *, grp):
    @pl.when(pl.program_id(1) == 0)
    def _():
        st_ref[...] = jnp.zeros_like(st_ref)

    def step(d, mn_ref, s_out_ref, cc):
        s = st_ref[d]
        s_out_ref[cc] = s.astype(bf16)
        m_t = mn_ref[0, cc, :, :, 0:HEAD].astype(bf16)
        n_t = mn_ref[0, cc, :, :, HEAD:PAIR]
        st_ref[d] = jnp.einsum('hij,hjk->hik', m_t, s.astype(bf16), preferred_element_type=f32) + n_t

    for cc in range(grp):
        step(0, mnf_ref, sf_ref, cc)
        step(1, mnb_ref, sb_ref, grp - 1 - cc)


def _rwkv_s(mn, *, n_seq, grp):
    nchunk = mn.shape[1]
    ng = nchunk // n_seq // grp
    blk_in = (1, grp, N_HEADS, HEAD, PAIR)
    blk_out = (grp, N_HEADS, HEAD, HEAD)
    out = jax.ShapeDtypeStruct((nchunk, N_HEADS, HEAD, HEAD), bf16)
    return pl.pallas_call(
        functools.partial(_rwkv_s_body, grp=grp),
        grid=(n_seq, ng),
        in_specs=[pl.BlockSpec(blk_in, lambda b, g: (0, b * ng + g, 0, 0, 0)),
                  pl.BlockSpec(blk_in, lambda b, g: (1, b * ng + ng - 1 - g, 0, 0, 0))],
        out_specs=[pl.BlockSpec(blk_out, lambda b, g: (b * ng + g, 0, 0, 0)),
                   pl.BlockSpec(blk_out, lambda b, g: (b * ng + ng - 1 - g, 0, 0, 0))],
        out_shape=[out, out],
        scratch_shapes=[pltpu.VMEM((2, N_HEADS, HEAD, HEAD), f32)],
        compiler_params=pltpu.CompilerParams(
            dimension_semantics=("parallel", "arbitrary"), vmem_limit_bytes=VMEM_LIMIT),
        name="rwkv_s",
    )(mn, mn)


def _rwkv_f_body(y0_ref, bonus_ref, rp_ref, sf_ref, sb_ref, gate_ref, lng_ref, lnb_ref, o_ref, y_s, *, tt):
    inst = [(slice(c * CHUNK, (c + 1) * CHUNK), slice(h * HEAD, (h + 1) * HEAD), c, h)
            for c in range(tt // CHUNK) for h in range(2)]
    yf = [jnp.dot(rp_ref[0, cs, hs], sf_ref[c, h], preferred_element_type=f32) for cs, hs, c, h in inst]
    yb = [jnp.dot(rp_ref[1, cs, hs], sb_ref[c, h], preferred_element_type=f32) for cs, hs, c, h in inst]
    for (cs, hs, c, h), a, b in zip(inst, yf, yb):
        y_s[cs, hs] = y0_ref[cs, hs] + a + b
    lane = lax.broadcasted_iota(jnp.int32, (PAIR, PAIR), 0) // HEAD
    seg = jnp.where(lane == lax.broadcasted_iota(jnp.int32, (PAIR, PAIR), 1) // HEAD, 1.0, 0.0)
    y = y_s[...]
    mu = _segsum(y, seg) * (1.0 / HEAD)
    dv = y - mu
    var = _segsum(dv * dv, seg) * (1.0 / HEAD)
    yn = dv * lax.rsqrt(var + GN_EPS) * lng_ref[...] + lnb_ref[...] + bonus_ref[...]
    g = gate_ref[0]
    o_ref[...] = (yn * (g * _sigmoid(g))).astype(bf16)


def _rwkv_f(y0, bonus, rp, sf, sb, p9, ln_g, ln_b, *, tt):
    m = y0.shape[0]
    cpt = tt // CHUNK
    tile = pl.BlockSpec((tt, PAIR), lambda i, j: (i, j))
    vec = pl.BlockSpec((1, PAIR), lambda i, j: (0, j))
    state = pl.BlockSpec((cpt, 2, HEAD, HEAD), lambda i, j: (i, j, 0, 0))
    return pl.pallas_call(
        functools.partial(_rwkv_f_body, tt=tt),
        grid=(m // tt, D_MODEL // PAIR),
        in_specs=[
            tile, tile,
            pl.BlockSpec((2, tt, PAIR), lambda i, j: (0, i, j)),
            state, state,
            pl.BlockSpec((1, tt, PAIR), lambda i, j: (P_GA, i, j)),
            vec, vec,
        ],
        out_specs=tile,
        out_shape=jax.ShapeDtypeStruct((m, D_MODEL), bf16),
        scratch_shapes=[pltpu.VMEM((tt, PAIR), f32)],
        compiler_params=pltpu.CompilerParams(
            dimension_semantics=("parallel", "parallel"), vmem_limit_bytes=VMEM_LIMIT),
        name="rwkv_f",
    )(y0, bonus, rp, sf, sb, p9, ln_g, ln_b)


def _attn_body(q_ref, kp_ref, kc_ref, kn_ref, vp_ref, vc_ref, vn_ref, tp_ref, tc_ref, tn_ref, gate_ref,
               qg_ref, kg_ref, sink_ref, o_ref, *, seq):
    nblk = seq // BLOCK
    start = (pl.program_id(0) % nblk) * BLOCK
    ctx = BLOCK + 2 * WINDOW
    qpos = start + lax.broadcasted_iota(jnp.int32, (BLOCK, ctx), 0)
    kpos = start - WINDOW + lax.broadcasted_iota(jnp.int32, (BLOCK, ctx), 1)
    valid = (jnp.abs(qpos - kpos) <= WINDOW) & (kpos >= 0) & (kpos < seq)

    def norm_rope(x, gain, tab):
        xn = x * lax.rsqrt(jnp.mean(x * x, axis=-1, keepdims=True) + NORM_EPS) * gain
        rot = jnp.concatenate([xn[:, HEAD // 2:], xn[:, :HEAD // 2]], axis=1)
        return xn * tab[:, :HEAD] + rot * tab[:, HEAD:]

    tab_k = jnp.concatenate([tp_ref[...], tc_ref[...], tn_ref[...]], axis=0)
    tab_q = tc_ref[...]
    heads = range(N_HEADS)
    hsl = lambda hq: slice(hq * HEAD, (hq + 1) * HEAD)
    krs, vxs = [], []
    for g in range(HKV):
        gs = hsl(g)
        kx = jnp.concatenate([kp_ref[0, :, gs], kc_ref[0, :, gs], kn_ref[0, :, gs]], axis=0)
        vxs.append(jnp.concatenate([vp_ref[0, :, gs], vc_ref[0, :, gs], vn_ref[0, :, gs]], axis=0).astype(bf16))
        krs.append(norm_rope(kx, kg_ref[...], tab_k).astype(bf16))
    qrs = [(norm_rope(q_ref[0, :, hsl(hq)], qg_ref[...], tab_q) * (HEAD ** -0.5)).astype(bf16) for hq in heads]
    ss = [jnp.where(valid, lax.dot_general(qrs[hq], krs[hq // GRP], (((1,), (1,)), ((), ())),
                                           preferred_element_type=f32), -1e30) for hq in heads]
    sinks = [sink_ref[0:1, hq:hq + 1] for hq in heads]
    mxs = [jnp.maximum(jnp.max(ss[hq], axis=-1, keepdims=True), sinks[hq]) for hq in heads]
    ps = [jnp.exp(ss[hq] - mxs[hq]) for hq in heads]
    dens = [jnp.sum(ps[hq], axis=-1, keepdims=True) + jnp.exp(sinks[hq] - mxs[hq]) for hq in heads]
    os_ = [jnp.dot(ps[hq].astype(bf16), vxs[hq // GRP], preferred_element_type=f32) for hq in heads]
    for hq in heads:
        gt = gate_ref[0, :, hsl(hq)]
        o_ref[:, hsl(hq)] = (os_[hq] / dens[hq] * (gt * _sigmoid(gt))).astype(bf16)


def _attn(p9, rope_tab, q_g, k_g, sink, *, seq):
    m = p9.shape[1]
    nblk = seq // BLOCK

    def nbr(delta):
        def f(i):
            t = i % nblk
            return i - t + jnp.clip(t + delta, 0, nblk - 1)
        return f

    def kv(col, delta):
        return pl.BlockSpec((1, BLOCK, D_KV), lambda i, col=col, f=nbr(delta): (P_X, f(i), col))

    def tab(delta):
        return pl.BlockSpec((BLOCK, PAIR), lambda i, f=nbr(delta): (f(i) % nblk, 0))

    return pl.pallas_call(
        functools.partial(_attn_body, seq=seq),
        grid=(m // BLOCK,),
        in_specs=[
            pl.BlockSpec((1, BLOCK, D_MODEL), lambda i: (P_Q, i, 0)),
            kv(1, -1), kv(1, 0), kv(1, 1), kv(2, -1), kv(2, 0), kv(2, 1),
            tab(-1), tab(0), tab(1),
            pl.BlockSpec((1, BLOCK, D_MODEL), lambda i: (P_GB, i, 0)),
            pl.BlockSpec((1, HEAD), lambda i: (0, 0)),
            pl.BlockSpec((1, HEAD), lambda i: (0, 0)),
            pl.BlockSpec((1, N_HEADS), lambda i: (0, 0)),
        ],
        out_specs=pl.BlockSpec((BLOCK, D_MODEL), lambda i: (i, 0)),
        out_shape=jax.ShapeDtypeStruct((m, D_MODEL), bf16),
        compiler_params=pltpu.CompilerParams(dimension_semantics=("parallel",), vmem_limit_bytes=VMEM_LIMIT),
        name="attn",
    )(p9, p9, p9, p9, p9, p9, p9, rope_tab, rope_tab, rope_tab, p9, q_g, k_g, sink)


def _rope_table(seq):
    inv = 1.0 / (ROPE_THETA ** (jnp.arange(0, HEAD, 2, dtype=f32) / HEAD))
    ang = jnp.arange(seq, dtype=f32)[:, None] * inv[None, :]
    cos, sin = jnp.cos(ang), jnp.sin(ang)
    return jnp.concatenate([cos, cos, -sin, sin], axis=1)


def _out_proj_body(x_ref, ga_ref, gb_ref, ma_ref, mb_ref, wa_ref, wb_ref, wo_ref, o_ref):
    ya = jnp.dot(ga_ref[...], wa_ref[...], preferred_element_type=f32)
    yb = jnp.dot(gb_ref[...], wb_ref[...], preferred_element_type=f32)
    mixed = _sigmoid(ma_ref[0]) * ya + _sigmoid(mb_ref[0]) * yb
    o_ref[...] = x_ref[...] + jnp.dot(mixed.astype(bf16), wo_ref[...], preferred_element_type=f32)


def _out_proj(x2d, ga, gb, p9, wa, wb, wo, *, tm):
    m = x2d.shape[0]
    row = pl.BlockSpec((tm, D_MODEL), lambda i: (i, 0))
    wspec = pl.BlockSpec((D_MODEL, D_MODEL), lambda i: (0, 0))
    return pl.pallas_call(
        _out_proj_body,
        grid=(m // tm,),
        in_specs=[row, row, row,
                  pl.BlockSpec((1, tm, D_MODEL), lambda i: (P_MA, i, 0)),
                  pl.BlockSpec((1, tm, D_MODEL), lambda i: (P_MB, i, 0)),
                  wspec, wspec, wspec],
        out_specs=row,
        out_shape=jax.ShapeDtypeStruct((m, D_MODEL), f32),
        compiler_params=pltpu.CompilerParams(dimension_semantics=("parallel",), vmem_limit_bytes=VMEM_LIMIT),
        name="out_proj",
    )(x2d, ga, gb, p9, p9, wa, wb, wo)


def _layer(x2d, lw, rope_tab, *, n_seq, seq):
    tm = min(1024, seq)
    tt = min(256, seq)
    grp = min(8, seq // CHUNK)
    p9 = _in_proj(x2d, lw["norm_g"], lw["w_p"], lw["mu_p"], seq=seq, tm=tm)
    y0, bonus, rp, mn = _rwkv_p(p9, lw["w0"], lw["wlu"], lw["a0"], lw["alu"], lw["k_k"], lw["k_a"], lw["r_k"], tt=tt)
    sf, sb = _rwkv_s(mn, n_seq=n_seq, grp=grp)
    ga = _rwkv_f(y0, bonus, rp, sf, sb, p9, lw["ln_g"], lw["ln_b"], tt=tt)
    gb = _attn(p9, rope_tab, lw["q_g"], lw["k_g"], lw["sink"], seq=seq)
    return _out_proj(x2d, ga, gb, p9, lw["wa"], lw["wb"], lw["wo"], tm=tm)


def _trunk(x, layers):
    n_seq, seq, _ = x.shape
    rope_tab = _rope_table(seq)
    x2d = x.reshape(n_seq * seq, D_MODEL)
    for lw in layers:
        x2d = _layer(x2d, lw, rope_tab, n_seq=n_seq, seq=seq)
    return x2d.reshape(n_seq, seq, D_MODEL)


def _prep_layers(norm_g, w_in, shift_mu, w0, w_lora_up, a0, a_lora_up, k_k, k_a, r_k, ln_x_g, ln_x_b,
                 q_norm_g, k_norm_g, sink, w_proj_a, w_proj_b, w_out):
    layers = []
    row = lambda v: v.reshape(1, -1).astype(f32)
    for l in range(norm_g.shape[0]):
        w_p, mu_p = _prep_layer_weights(w_in[l], shift_mu[l])
        layers.append(dict(
            norm_g=row(norm_g[l]), w_p=w_p, mu_p=mu_p,
            w0=w0[l].astype(f32), wlu=w_lora_up[l].astype(bf16), a0=a0[l].astype(f32), alu=a_lora_up[l].astype(bf16),
            k_k=row(k_k[l]), k_a=row(k_a[l]), r_k=row(r_k[l]), ln_g=row(ln_x_g[l]), ln_b=row(ln_x_b[l]),
            q_g=row(q_norm_g[l]), k_g=row(k_norm_g[l]), sink=row(sink[l]),
            wa=w_proj_a[l].astype(bf16), wb=w_proj_b[l].astype(bf16), wo=w_out[l].astype(bf16)))
    return layers


def kernel(x_prompt, x_sample, norm_g, w_in, shift_mu, w0, w_lora_up, a0, a_lora_up, k_k, k_a, r_k,
           ln_x_g, ln_x_b, q_norm_g, k_norm_g, sink, w_proj_a, w_proj_b, w_out):
    layers = _prep_layers(norm_g, w_in, shift_mu, w0, w_lora_up, a0, a_lora_up, k_k, k_a, r_k, ln_x_g, ln_x_b,
                          q_norm_g, k_norm_g, sink, w_proj_a, w_proj_b, w_out)
    return _trunk(x_prompt, layers), _trunk(x_sample, layers)
```

```python
import functools

import jax
import jax.numpy as jnp
from jax import lax
from jax.experimental import pallas as pl
from jax.experimental.pallas import tpu as pltpu

D_MODEL = 1024
HEAD = 64
N_HEADS = D_MODEL // HEAD
R_LORA = 64
HKV = 4
GRP = N_HEADS // HKV
D_KV = HKV * HEAD
WINDOW = 128
BLOCK = 128
GN_EPS = 64e-5
NORM_EPS = 1e-6
ROPE_THETA = 10000.0
C_SHIFT = 3 * D_MODEL + 4 * R_LORA
N_PLANES = 9
CHUNK = 64
HALO = 8
VMEM_LIMIT = 56 * 1024 * 1024

P_R, P_K, P_V, P_GA, P_Q, P_GB, P_MA, P_MB, P_X = range(9)

f32 = jnp.float32
bf16 = jnp.bfloat16


def _dot(a, b):
    return jnp.dot(a.astype(bf16), b.astype(bf16), preferred_element_type=f32)


def _dot_nt(a, b):
    return lax.dot_general(a.astype(bf16), b.astype(bf16), (((1,), (1,)), ((), ())),
                           preferred_element_type=f32)


def _dot_tn(a, b):
    return lax.dot_general(a.astype(bf16), b.astype(bf16), (((0,), (0,)), ((), ())),
                           preferred_element_type=f32)


def _dot_split(a_exact, b):
    b0 = b.astype(bf16)
    r1 = b - b0.astype(f32)
    b1 = r1.astype(bf16)
    b2 = (r1 - b1.astype(f32)).astype(bf16)
    a = a_exact.astype(bf16)
    return (jnp.dot(a, b0, preferred_element_type=f32) + jnp.dot(a, b1, preferred_element_type=f32)
            + jnp.dot(a, b2, preferred_element_type=f32))


def _sigmoid(x):
    return 1.0 / (1.0 + jnp.exp(-x))


def _in_proj_body(x_ref, xp_ref, xn_ref, g_ref, w_ref, mu_ref, o_ref, h_ref, *, tm, seq):
    i = pl.program_id(0)
    j = pl.program_id(1)

    def norm(x):
        ms = jnp.mean(x * x, axis=-1, keepdims=True)
        return x * lax.rsqrt(ms + NORM_EPS) * g_ref[...]

    @pl.when(j == 0)
    def _():
        first = (i * tm) % seq == 0
        last = ((i + 1) * tm) % seq == 0
        h_ref[HALO:HALO + tm, :] = norm(x_ref[...]).astype(bf16)
        h_ref[0:HALO, :] = jnp.where(first, 0.0, norm(xp_ref[...])).astype(bf16)
        h_ref[HALO + tm:, :] = jnp.where(last, 0.0, norm(xn_ref[...])).astype(bf16)

    shifted = (j <= P_V) | (j == P_X)

    @pl.when(shifted)
    def _():
        p = jnp.dot(h_ref[...], w_ref[...], preferred_element_type=f32)
        pc = p[HALO:HALO + tm]
        nbr = 0.5 * (p[HALO - 1:HALO - 1 + tm] + p[HALO + 1:HALO + 1 + tm])
        o_ref[0] = pc + mu_ref[...] * (nbr - pc)

    @pl.when(jnp.logical_not(shifted))
    def _():
        o_ref[0] = jnp.dot(h_ref[HALO:HALO + tm, :], w_ref[...], preferred_element_type=f32)


def _in_proj(x2d, norm_g, w_p, mu_p, *, seq, tm):
    m = x2d.shape[0]
    nb8 = m // HALO
    return pl.pallas_call(
        functools.partial(_in_proj_body, tm=tm, seq=seq),
        grid=(m // tm, N_PLANES),
        in_specs=[
            pl.BlockSpec((tm, D_MODEL), lambda i, j: (i, 0)),
            pl.BlockSpec((HALO, D_MODEL), lambda i, j: (jnp.maximum(i * (tm // HALO) - 1, 0), 0)),
            pl.BlockSpec((HALO, D_MODEL), lambda i, j: (jnp.minimum((i + 1) * (tm // HALO), nb8 - 1), 0)),
            pl.BlockSpec((1, D_MODEL), lambda i, j: (0, 0)),
            pl.BlockSpec((D_MODEL, D_MODEL), lambda i, j: (0, j)),
            pl.BlockSpec((1, D_MODEL), lambda i, j: (0, j)),
        ],
        out_specs=pl.BlockSpec((1, tm, D_MODEL), lambda i, j: (j, i, 0)),
        out_shape=jax.ShapeDtypeStruct((N_PLANES, m, D_MODEL), f32),
        scratch_shapes=[pltpu.VMEM((tm + 2 * HALO, D_MODEL), bf16)],
        compiler_params=pltpu.CompilerParams(
            dimension_semantics=("parallel", "arbitrary"), vmem_limit_bytes=VMEM_LIMIT),
        name="in_proj",
    )(x2d, x2d, x2d, norm_g, w_p, mu_p)


def _prep_layer_weights(w_in, shift_mu):
    c = C_SHIFT
    d = D_MODEL
    cols = [
        (0, d), (d, 2 * d), (2 * d, 3 * d),
        (c, c + d),
        (c + d, c + 2 * d),
        (c + 2 * d + 2 * D_KV, c + 3 * d + 2 * D_KV),
        (c + 3 * d + 2 * D_KV, c + 4 * d + 2 * D_KV),
        (c + 4 * d + 2 * D_KV, c + 5 * d + 2 * D_KV),
    ]
    planes = [w_in[:, a:b] for a, b in cols]
    small = jnp.concatenate([w_in[:, 3 * d:c], w_in[:, c + 2 * d:c + 2 * d + 2 * D_KV],
                             jnp.zeros((d, d - 4 * R_LORA - 2 * D_KV), w_in.dtype)], axis=1)
    w_p = jnp.concatenate(planes + [small], axis=1).astype(bf16)
    zeros = jnp.zeros((d,), f32)
    mu_small = jnp.concatenate([shift_mu[3 * d:c], jnp.zeros((d - 4 * R_LORA,), f32)])
    mu_p = jnp.concatenate([shift_mu[0:d], shift_mu[d:2 * d], shift_mu[2 * d:3 * d]] + [zeros] * 5 + [mu_small])
    return w_p, mu_p.reshape(1, N_PLANES * d)


PAIR = 2 * HEAD


def _rwkv_p_body(r_ref, k_ref, v_ref, x_ref, w0_ref, wlu_ref, a0_ref, alu_ref, kk_ref, ka_ref, rk_ref,
                 y0_ref, bonus_ref, rp_ref, mn_ref,
                 at_s, rt_s, kpm_s, bpm_s, khm_s, bhm_s, atx_s, rtx_s, vx_s, vo_s, pl_s, *, tt):
    nc = tt // CHUNK
    r = r_ref[0]
    k = k_ref[0]
    v = v_ref[0]

    lane = lax.broadcasted_iota(jnp.int32, (PAIR, PAIR), 0) // HEAD
    seg = jnp.where(lane == lax.broadcasted_iota(jnp.int32, (PAIR, PAIR), 1) // HEAD, 1.0, 0.0)
    lo = lax.broadcasted_iota(jnp.int32, (tt, PAIR), 1) < HEAD
    swap = lambda t: pltpu.roll(t, HEAD, axis=1)

    def head_lo(t, h):
        return jnp.where(lo, t if h == 0 else swap(t), 0.0)

    def head_hi(t, h):
        return jnp.where(lo, 0.0, swap(t) if h == 0 else t)

    def head_own(t, h):
        return jnp.where(lo, t, 0.0) if h == 0 else jnp.where(lo, 0.0, t)

    for h in range(2):
        vx_s[h] = head_hi(v, h).astype(bf16)
    vo_s[...] = head_own(v, 0).astype(bf16)

    kkv = k * kk_ref[...]
    kk = kkv / jnp.maximum(jnp.sqrt(_segsum(kkv * kkv, seg)), 1e-12)

    rows = lax.broadcasted_iota(jnp.int32, (tt, tt), 0)
    cols = lax.broadcasted_iota(jnp.int32, (tt, tt), 1)
    same = (rows // CHUNK) == (cols // CHUNK)
    ones_bd = jnp.where(same, 1.0, 0.0)

    kmod_sum = jnp.zeros_like(k)
    for d in range(2):
        if d == 0:
            tri_bd = jnp.where(same & (cols <= rows), 1.0, 0.0)
        else:
            tri_bd = jnp.where(same & (cols >= rows), 1.0, 0.0)
        wd = x_ref[0, :, d * R_LORA:(d + 1) * R_LORA]
        ad = x_ref[0, :, 2 * R_LORA + d * R_LORA:2 * R_LORA + (d + 1) * R_LORA]
        z = w0_ref[d:d + 1, :] + _dot(jnp.tanh(wd), wlu_ref[d])
        u = -z
        w_log = -(jnp.maximum(u, 0.0) + jnp.log(1.0 + jnp.exp(-jnp.abs(u)))) - 0.5
        lw = -jnp.exp(w_log)
        a = _sigmoid(a0_ref[d:d + 1, :] + _dot(ad, alu_ref[d]))
        kmod = k * (1.0 + (a - 1.0) * ka_ref[...])
        kmod_sum = kmod_sum + kmod
        b = -(kk * a)
        cum = _dot_split(tri_bd, lw)
        tot = _dot_split(ones_bd, lw)
        p_inv = jnp.exp(-cum)
        p_end = jnp.exp(tot - cum)
        at = kk * jnp.exp(cum - lw)
        rt = r * jnp.exp(cum)
        kh = kmod * p_inv
        bh = b * p_inv
        p_tot = jnp.exp(tot)
        at_s[d] = at.astype(bf16)
        rt_s[d] = rt.astype(bf16)
        pl_s[d] = p_tot
        kp = kmod * p_end
        bp = b * p_end
        for h in range(2):
            khm_s[d, h] = head_own(kh, h).astype(bf16)
            bhm_s[d, h] = head_own(bh, h).astype(bf16)
            kpm_s[d, h] = head_own(kp, h).astype(bf16)
            bpm_s[d, h] = head_own(bp, h).astype(bf16)
            atx_s[d, h] = head_lo(at, h).astype(bf16)
            rtx_s[d, h] = head_lo(rt, h)

    bonus_ref[...] = _segsum(r * kmod_sum * rk_ref[...], seg) * v

    row = lax.broadcasted_iota(jnp.int32, (CHUNK, PAIR), 0)
    col = lax.broadcasted_iota(jnp.int32, (CHUNK, PAIR), 1)
    col_t = col % CHUNK
    lo_c = col < HEAD
    strict = (col_t < row, col_t > row)
    incl = (col_t <= row, col_t >= row)
    pairs = [(d, c) for d in range(2) for c in range(nc)]
    inst = [(d, c, h) for d, c in pairs for h in range(2)]
    cs = lambda c: slice(c * CHUNK, (c + 1) * CHUNK)
    swap_c = lambda t: pltpu.roll(t, HEAD, axis=1)
    zeros = jnp.zeros((CHUNK, PAIR), bf16)

    top, bot = [], []
    for d, c in pairs:
        lhs = jnp.concatenate([at_s[d, cs(c)], rt_s[d, cs(c)]], axis=0)
        rhs = jnp.concatenate([bhm_s[d, 0, cs(c)], khm_s[d, 0, cs(c)],
                               bhm_s[d, 1, cs(c)], khm_s[d, 1, cs(c)]], axis=0)
        sc = lax.dot_general(lhs, rhs, (((1,), (1,)), ((), ())), preferred_element_type=f32)
        for h in range(2):
            top.append(jnp.where(strict[d], sc[:CHUNK, h * PAIR:(h + 1) * PAIR], 0.0))
            bot.append(jnp.where(incl[d], sc[CHUNK:, h * PAIR:(h + 1) * PAIR], 0.0))
    xs, aps = [], []
    for i, (d, c, h) in enumerate(inst):
        vx = vx_s[h, cs(c)]
        a_ak = jnp.where(lo_c, 0.0, top[i]).astype(bf16)
        akv = jnp.dot(a_ak, jnp.concatenate([vx, vx], axis=0), preferred_element_type=f32)
        xs.append(atx_s[d, h, cs(c)].astype(f32) + akv)
        aps.append(top[i][:, :CHUNK])
    n_dbl = CHUNK.bit_length() - 1
    for it in range(n_dbl):
        last = it + 1 == n_dbl
        res = [_dot(ap, x if last else jnp.concatenate([x, ap], axis=1)) for ap, x in zip(aps, xs)]
        xs = [x + rs[:, :PAIR] for x, rs in zip(xs, res)]
        if not last:
            aps = [rs[:, PAIR:] for rs in res]
    rys = [_dot(bot[i], jnp.concatenate([xs[i].astype(bf16), vx_s[h, cs(c)]], axis=0)) + rtx_s[d, h, cs(c)]
           for i, (d, c, h) in enumerate(inst)]
    mns = []
    for j, (d, c) in enumerate(pairs):
        x0, x1 = xs[2 * j], xs[2 * j + 1]
        r0 = jnp.concatenate([jnp.where(lo_c, x0, 0.0), jnp.where(lo_c, swap_c(x0), 0.0)], axis=1).astype(bf16)
        r1 = jnp.concatenate([jnp.where(lo_c, 0.0, swap_c(x1)), jnp.where(lo_c, 0.0, x1)], axis=1).astype(bf16)
        r2 = jnp.concatenate([zeros, vo_s[cs(c)]], axis=1)
        r3 = jnp.concatenate([zeros, vx_s[1, cs(c)]], axis=1)
        rhs = jnp.concatenate([r0, r1, r2, r3], axis=0)
        lhs_t = jnp.concatenate([bpm_s[d, 0, cs(c)], bpm_s[d, 1, cs(c)],
                                 kpm_s[d, 0, cs(c)], kpm_s[d, 1, cs(c)]], axis=0)
        mns.append(lax.dot_general(lhs_t, rhs, (((0,), (0,)), ((), ())), preferred_element_type=f32))
    rowp = lax.broadcasted_iota(jnp.int32, (PAIR, 2 * PAIR), 0)
    colp = lax.broadcasted_iota(jnp.int32, (PAIR, 2 * PAIR), 1)
    for j, (d, c) in enumerate(pairs):
        ry0, ry1 = rys[2 * j], rys[2 * j + 1]
        rp_ref[d, cs(c), :] = jnp.where(lo_c, ry0, swap_c(ry1)).astype(bf16)
        y0 = jnp.where(lo_c, swap_c(ry0), ry1)
        if d == 0:
            y0_ref[cs(c), :] = y0
        else:
            y0_ref[cs(c), :] += y0
        p_row = pl_s[d, c * CHUNK:c * CHUNK + 1, :]
        diag = jnp.where(rowp == colp, jnp.concatenate([p_row, p_row], axis=1), 0.0)
        mn_ref[d, c, 0] = (mns[j] + diag).astype(bf16)


def _segsum(x, seg):
    return _dot_split_rhs(x, seg)


def _dot_split_rhs(x, m_exact):
    x0 = x.astype(bf16)
    r1 = x - x0.astype(f32)
    x1 = r1.astype(bf16)
    x2 = (r1 - x1.astype(f32)).astype(bf16)
    m = m_exact.astype(bf16)
    return (jnp.dot(x0, m, preferred_element_type=f32) + jnp.dot(x1, m, preferred_element_type=f32)
            + jnp.dot(x2, m, preferred_element_type=f32))


def _rwkv_p(p9, w0, wlu, a0, alu, k_k, k_a, r_k, *, tt):
    m = p9.shape[1]
    nt = m // tt
    npair = D_MODEL // PAIR
    cpt = tt // CHUNK
    tile = lambda plane: pl.BlockSpec((1, tt, PAIR), lambda i, j, plane=plane: (plane, i, j))
    vec = pl.BlockSpec((1, PAIR), lambda i, j: (0, j))
    return pl.pallas_call(
        functools.partial(_rwkv_p_body, tt=tt),
        grid=(nt, npair),
        in_specs=[
            tile(P_R), tile(P_K), tile(P_V),
            pl.BlockSpec((1, tt, 4 * R_LORA), lambda i, j: (P_X, i, 0)),
            pl.BlockSpec((2, PAIR), lambda i, j: (0, j)),
            pl.BlockSpec((2, R_LORA, PAIR), lambda i, j: (0, 0, j)),
            pl.BlockSpec((2, PAIR), lambda i, j: (0, j)),
            pl.BlockSpec((2, R_LORA, PAIR), lambda i, j: (0, 0, j)),
            vec, vec, vec,
        ],
        out_specs=[
            pl.BlockSpec((tt, PAIR), lambda i, j: (i, j)),
            pl.BlockSpec((tt, PAIR), lambda i, j: (i, j)),
            pl.BlockSpec((2, tt, PAIR), lambda i, j: (0, i, j)),
            pl.BlockSpec((2, cpt, 1, PAIR, 2 * PAIR), lambda i, j: (0, i, j, 0, 0)),
        ],
        out_shape=[
            jax.ShapeDtypeStruct((m, D_MODEL), f32),
            jax.ShapeDtypeStruct((m, D_MODEL), f32),
            jax.ShapeDtypeStruct((2, m, D_MODEL), bf16),
            jax.ShapeDtypeStruct((2, m // CHUNK, npair, PAIR, 2 * PAIR), bf16),
        ],
        scratch_shapes=(
            [pltpu.VMEM((2, tt, PAIR), bf16) for _ in range(2)]
            + [pltpu.VMEM((2, 2, tt, PAIR), bf16) for _ in range(5)]
            + [pltpu.VMEM((2, 2, tt, PAIR), f32),
               pltpu.VMEM((2, tt, PAIR), bf16),
               pltpu.VMEM((tt, PAIR), bf16),
               pltpu.VMEM((2, tt, PAIR), f32)]),
        compiler_params=pltpu.CompilerParams(
            dimension_semantics=("parallel", "parallel"), vmem_limit_bytes=VMEM_LIMIT),
        name="rwkv_p",
    )(p9, p9, p9, p9, w0, wlu, a0, alu, k_k, k_a, r_k)


def _rwkv_s_body(mnf_ref, mnb_ref, sf_ref, sb_ref, st_ref, *, grp):
    @pl.when(pl.program_id(1) == 0)
    def _():
        st_ref[...] = jnp.zeros_like(st_ref)

    def step(d, mn_ref, s_out_ref, cc):
        s = st_ref[d].astype(bf16)
        s_out_ref[cc] = s
        m_t = mn_ref[0, cc, :, :, 0:PAIR]
        n_t = mn_ref[0, cc, :, :, PAIR:].astype(f32)
        st_ref[d] = jnp.einsum('pij,pjk->pik', m_t, s, preferred_element_type=f32) + n_t

    for cc in range(grp):
        step(0, mnf_ref, sf_ref, cc)
        step(1, mnb_ref, sb_ref, grp - 1 - cc)


def _rwkv_s(mn, *, n_seq, grp):
    nchunk, npair = mn.shape[1], mn.shape[2]
    ng = nchunk // n_seq // grp
    blk_in = (1, grp, npair, PAIR, 2 * PAIR)
    blk_out = (grp, npair, PAIR, PAIR)
    out = jax.ShapeDtypeStruct((nchunk, npair, PAIR, PAIR), bf16)
    return pl.pallas_call(
        functools.partial(_rwkv_s_body, grp=grp),
        grid=(n_seq, ng),
        in_specs=[pl.BlockSpec(blk_in, lambda b, g: (0, b * ng + g, 0, 0, 0)),
                  pl.BlockSpec(blk_in, lambda b, g: (1, b * ng + ng - 1 - g, 0, 0, 0))],
        out_specs=[pl.BlockSpec(blk_out, lambda b, g: (b * ng + g, 0, 0, 0)),
                   pl.BlockSpec(blk_out, lambda b, g: (b * ng + ng - 1 - g, 0, 0, 0))],
        out_shape=[out, out],
        scratch_shapes=[pltpu.VMEM((2, npair, PAIR, PAIR), f32)],
        compiler_params=pltpu.CompilerParams(
            dimension_semantics=("parallel", "arbitrary"), vmem_limit_bytes=VMEM_LIMIT),
        name="rwkv_s",
    )(mn, mn)


def _rwkv_f_body(y0_ref, bonus_ref, rp_ref, sf_ref, sb_ref, gate_ref, lng_ref, lnb_ref, o_ref, *, tt):
    npair = D_MODEL // PAIR
    nc = tt // CHUNK
    cs = lambda c: slice(c * CHUNK, (c + 1) * CHUNK)
    ps = lambda p: slice(p * PAIR, (p + 1) * PAIR)
    yf = [[jnp.dot(rp_ref[0, cs(c), ps(p)], sf_ref[c, p], preferred_element_type=f32) for c in range(nc)]
          for p in range(npair)]
    yb = [[jnp.dot(rp_ref[1, cs(c), ps(p)], sb_ref[c, p], preferred_element_type=f32) for c in range(nc)]
          for p in range(npair)]
    lane = lax.broadcasted_iota(jnp.int32, (PAIR, PAIR), 0) // HEAD
    seg = jnp.where(lane == lax.broadcasted_iota(jnp.int32, (PAIR, PAIR), 1) // HEAD, 1.0, 0.0)
    ys = [y0_ref[:, ps(p)] + jnp.concatenate(yf[p], axis=0) + jnp.concatenate(yb[p], axis=0) for p in range(npair)]
    mus = [_segsum(y, seg) * (1.0 / HEAD) for y in ys]
    dvs = [y - mu for y, mu in zip(ys, mus)]
    vrs = [_segsum(dv * dv, seg) * (1.0 / HEAD) for dv in dvs]
    for p in range(npair):
        yn = dvs[p] * lax.rsqrt(vrs[p] + GN_EPS) * lng_ref[:, ps(p)] + lnb_ref[:, ps(p)] + bonus_ref[:, ps(p)]
        g = gate_ref[0, :, ps(p)]
        o_ref[:, ps(p)] = (yn * (g * _sigmoid(g))).astype(bf16)


def _rwkv_f(y0, bonus, rp, sf, sb, p9, ln_g, ln_b, *, tt):
    m = y0.shape[0]
    cpt = tt // CHUNK
    npair = D_MODEL // PAIR
    tile = pl.BlockSpec((tt, D_MODEL), lambda i: (i, 0))
    vec = pl.BlockSpec((1, D_MODEL), lambda i: (0, 0))
    state = pl.BlockSpec((cpt, npair, PAIR, PAIR), lambda i: (i, 0, 0, 0))
    return pl.pallas_call(
        functools.partial(_rwkv_f_body, tt=tt),
        grid=(m // tt,),
        in_specs=[
            tile, tile,
            pl.BlockSpec((2, tt, D_MODEL), lambda i: (0, i, 0)),
            state, state,
            pl.BlockSpec((1, tt, D_MODEL), lambda i: (P_GA, i, 0)),
            vec, vec,
        ],
        out_specs=tile,
        out_shape=jax.ShapeDtypeStruct((m, D_MODEL), bf16),
        compiler_params=pltpu.CompilerParams(dimension_semantics=("parallel",), vmem_limit_bytes=VMEM_LIMIT),
        name="rwkv_f",
    )(y0, bonus, rp, sf, sb, p9, ln_g, ln_b)


def _attn_body(q_ref, kp_ref, kc_ref, kn_ref, vp_ref, vc_ref, vn_ref, tp_ref, tc_ref, tn_ref, gate_ref,
               qg_ref, kg_ref, sink_ref, o_ref, *, seq):
    nblk = seq // BLOCK
    start = (pl.program_id(0) % nblk) * BLOCK
    ctx = BLOCK + 2 * WINDOW
    qpos = start + lax.broadcasted_iota(jnp.int32, (BLOCK, ctx), 0)
    kpos = start - WINDOW + lax.broadcasted_iota(jnp.int32, (BLOCK, ctx), 1)
    valid = (jnp.abs(qpos - kpos) <= WINDOW) & (kpos >= 0) & (kpos < seq)

    lane_r = lax.broadcasted_iota(jnp.int32, (PAIR, PAIR), 0) // HEAD
    seg = jnp.where(lane_r == lax.broadcasted_iota(jnp.int32, (PAIR, PAIR), 1) // HEAD, 1.0, 0.0)
    swap = lambda t: pltpu.roll(t, HEAD, axis=1)

    def norm_rope(x, gain, tab):
        lane = lax.broadcasted_iota(jnp.int32, x.shape, 1)
        xn = x * lax.rsqrt(_segsum(x * x, seg) * (1.0 / HEAD) + NORM_EPS) * gain
        rot = jnp.where(lane % HEAD < HEAD // 2, pltpu.roll(xn, PAIR - HEAD // 2, axis=1),
                        pltpu.roll(xn, HEAD // 2, axis=1))
        return xn * tab[:, :PAIR] + rot * tab[:, PAIR:]

    def place(t, e, want_lo):
        lo = lax.broadcasted_iota(jnp.int32, t.shape, 1) < HEAD
        src = t if (e == 0) == want_lo else swap(t)
        return (jnp.where(lo, src, 0.0) if want_lo else jnp.where(lo, 0.0, src)).astype(bf16)

    tab_k = jnp.concatenate([tp_ref[...], tc_ref[...], tn_ref[...]], axis=0)
    tab_q = tc_ref[...]
    k_lo, k_hi, v_lo, v_hi = [], [], [], []
    for kp in range(D_KV // PAIR):
        ps_ = slice(kp * PAIR, (kp + 1) * PAIR)
        kx = jnp.concatenate([kp_ref[0, :, ps_], kc_ref[0, :, ps_], kn_ref[0, :, ps_]], axis=0)
        vx = jnp.concatenate([vp_ref[0, :, ps_], vc_ref[0, :, ps_], vn_ref[0, :, ps_]], axis=0)
        kr = norm_rope(kx, kg_ref[...], tab_k)
        for e in range(2):
            k_lo.append(place(kr, e, True))
            k_hi.append(place(kr, e, False))
            v_lo.append(place(vx, e, True))
            v_hi.append(place(vx, e, False))
    npair = D_MODEL // PAIR
    psl = lambda p: slice(p * PAIR, (p + 1) * PAIR)
    qrs = [(norm_rope(q_ref[0, :, psl(p)], qg_ref[...], tab_q) * (HEAD ** -0.5)).astype(bf16) for p in range(npair)]
    heads = range(N_HEADS)
    ss = [jnp.where(valid, lax.dot_general(qrs[hq // 2], (k_lo if hq % 2 == 0 else k_hi)[hq // GRP],
                                           (((1,), (1,)), ((), ())), preferred_element_type=f32), -1e30)
          for hq in heads]
    sinks = [sink_ref[0:1, hq:hq + 1] for hq in heads]
    mxs = [jnp.maximum(jnp.max(ss[hq], axis=-1, keepdims=True), sinks[hq]) for hq in heads]
    ps = [jnp.exp(ss[hq] - mxs[hq]) for hq in heads]
    dens = [jnp.sum(ps[hq], axis=-1, keepdims=True) + jnp.exp(sinks[hq] - mxs[hq]) for hq in heads]
    os_ = [jnp.dot(ps[hq].astype(bf16), (v_lo if hq % 2 == 0 else v_hi)[hq // GRP], preferred_element_type=f32)
           / dens[hq] for hq in heads]
    for p in range(npair):
        gt = gate_ref[0, :, psl(p)]
        o_ref[:, psl(p)] = ((os_[2 * p] + os_[2 * p + 1]) * (gt * _sigmoid(gt))).astype(bf16)


def _attn(p9, rope_tab, q_g, k_g, sink, *, seq):
    m = p9.shape[1]
    nblk = seq // BLOCK

    def nbr(delta):
        def f(i):
            t = i % nblk
            return i - t + jnp.clip(t + delta, 0, nblk - 1)
        return f

    def kv(col, delta):
        return pl.BlockSpec((1, BLOCK, D_KV), lambda i, col=col, f=nbr(delta): (P_X, f(i), col))

    def tab(delta):
        return pl.BlockSpec((BLOCK, 2 * PAIR), lambda i, f=nbr(delta): (f(i) % nblk, 0))

    return pl.pallas_call(
        functools.partial(_attn_body, seq=seq),
        grid=(m // BLOCK,),
        in_specs=[
            pl.BlockSpec((1, BLOCK, D_MODEL), lambda i: (P_Q, i, 0)),
            kv(1, -1), kv(1, 0), kv(1, 1), kv(2, -1), kv(2, 0), kv(2, 1),
            tab(-1), tab(0), tab(1),
            pl.BlockSpec((1, BLOCK, D_MODEL), lambda i: (P_GB, i, 0)),
            pl.BlockSpec((1, PAIR), lambda i: (0, 0)),
            pl.BlockSpec((1, PAIR), lambda i: (0, 0)),
            pl.BlockSpec((1, N_HEADS), lambda i: (0, 0)),
        ],
        out_specs=pl.BlockSpec((BLOCK, D_MODEL), lambda i: (i, 0)),
        out_shape=jax.ShapeDtypeStruct((m, D_MODEL), bf16),
        compiler_params=pltpu.CompilerParams(dimension_semantics=("parallel",), vmem_limit_bytes=VMEM_LIMIT),
        name="attn",
    )(p9, p9, p9, p9, p9, p9, p9, rope_tab, rope_tab, rope_tab, p9, q_g, k_g, sink)


def _rope_table(seq):
    inv = 1.0 / (ROPE_THETA ** (jnp.arange(0, HEAD, 2, dtype=f32) / HEAD))
    ang = jnp.arange(seq, dtype=f32)[:, None] * inv[None, :]
    cos, sin = jnp.cos(ang), jnp.sin(ang)
    return jnp.concatenate([cos, cos, cos, cos, -sin, sin, -sin, sin], axis=1)


def _out_proj_body(x_ref, ga_ref, gb_ref, ma_ref, mb_ref, wa_ref, wb_ref, wo_ref, o_ref):
    ya = jnp.dot(ga_ref[...], wa_ref[...], preferred_element_type=f32)
    yb = jnp.dot(gb_ref[...], wb_ref[...], preferred_element_type=f32)
    mixed = _sigmoid(ma_ref[0]) * ya + _sigmoid(mb_ref[0]) * yb
    o_ref[...] = x_ref[...] + jnp.dot(mixed.astype(bf16), wo_ref[...], preferred_element_type=f32)


def _out_proj(x2d, ga, gb, p9, wa, wb, wo, *, tm):
    m = x2d.shape[0]
    row = pl.BlockSpec((tm, D_MODEL), lambda i: (i, 0))
    wspec = pl.BlockSpec((D_MODEL, D_MODEL), lambda i: (0, 0))
    return pl.pallas_call(
        _out_proj_body,
        grid=(m // tm,),
        in_specs=[row, row, row,
                  pl.BlockSpec((1, tm, D_MODEL), lambda i: (P_MA, i, 0)),
                  pl.BlockSpec((1, tm, D_MODEL), lambda i: (P_MB, i, 0)),
                  wspec, wspec, wspec],
        out_specs=row,
        out_shape=jax.ShapeDtypeStruct((m, D_MODEL), f32),
        compiler_params=pltpu.CompilerParams(dimension_semantics=("parallel",), vmem_limit_bytes=VMEM_LIMIT),
        name="out_proj",
    )(x2d, ga, gb, p9, p9, wa, wb, wo)


def _layer(x2d, lw, rope_tab, *, n_seq, seq):
    tm = min(1024, seq)
    tt = min(256, seq)
    grp = min(8, seq // CHUNK)
    p9 = _in_proj(x2d, lw["norm_g"], lw["w_p"], lw["mu_p"], seq=seq, tm=tm)
    y0, bonus, rp, mn = _rwkv_p(p9, lw["w0"], lw["wlu"], lw["a0"], lw["alu"], lw["k_k"], lw["k_a"], lw["r_k"], tt=tt)
    sf, sb = _rwkv_s(mn, n_seq=n_seq, grp=grp)
    ga = _rwkv_f(y0, bonus, rp, sf, sb, p9, lw["ln_g"], lw["ln_b"], tt=tt)
    gb = _attn(p9, rope_tab, lw["q_g"], lw["k_g"], lw["sink"], seq=seq)
    return _out_proj(x2d, ga, gb, p9, lw["wa"], lw["wb"], lw["wo"], tm=tm)


def _trunk(x, layers):
    n_seq, seq, _ = x.shape
    rope_tab = _rope_table(seq)
    x2d = x.reshape(n_seq * seq, D_MODEL)
    for lw in layers:
        x2d = _layer(x2d, lw, rope_tab, n_seq=n_seq, seq=seq)
    return x2d.reshape(n_seq, seq, D_MODEL)


def _prep_layers(norm_g, w_in, shift_mu, w0, w_lora_up, a0, a_lora_up, k_k, k_a, r_k, ln_x_g, ln_x_b,
                 q_norm_g, k_norm_g, sink, w_proj_a, w_proj_b, w_out):
    layers = []
    row = lambda v: v.reshape(1, -1).astype(f32)
    for l in range(norm_g.shape[0]):
        w_p, mu_p = _prep_layer_weights(w_in[l], shift_mu[l])
        layers.append(dict(
            norm_g=row(norm_g[l]), w_p=w_p, mu_p=mu_p,
            w0=w0[l].astype(f32), wlu=w_lora_up[l].astype(bf16), a0=a0[l].astype(f32), alu=a_lora_up[l].astype(bf16),
            k_k=row(k_k[l]), k_a=row(k_a[l]), r_k=row(r_k[l]), ln_g=row(ln_x_g[l]), ln_b=row(ln_x_b[l]),
            q_g=row(jnp.tile(q_norm_g[l], 2)), k_g=row(jnp.tile(k_norm_g[l], 2)), sink=row(sink[l]),
            wa=w_proj_a[l].astype(bf16), wb=w_proj_b[l].astype(bf16), wo=w_out[l].astype(bf16)))
    return layers


def kernel(x_prompt, x_sample, norm_g, w_in, shift_mu, w0, w_lora_up, a0, a_lora_up, k_k, k_a, r_k,
           ln_x_g, ln_x_b, q_norm_g, k_norm_g, sink, w_proj_a, w_proj_b, w_out):
    layers = _prep_layers(norm_g, w_in, shift_mu, w0, w_lora_up, a0, a_lora_up, k_k, k_a, r_k, ln_x_g, ln_x_b,
                          q_norm_g, k_norm_g, sink, w_proj_a, w_proj_b, w_out)
    return _trunk(x_prompt, layers), _trunk(x_sample, layers)
```

```python
import functools

import jax
import jax.numpy as jnp
from jax import lax
from jax.experimental import pallas as pl
from jax.experimental.pallas import tpu as pltpu

D_MODEL = 1024
HEAD = 64
N_HEADS = D_MODEL // HEAD
R_LORA = 64
HKV = 4
GRP = N_HEADS // HKV
D_KV = HKV * HEAD
WINDOW = 128
BLOCK = 128
GN_EPS = 64e-5
NORM_EPS = 1e-6
ROPE_THETA = 10000.0
C_SHIFT = 3 * D_MODEL + 4 * R_LORA
N_PLANES = 9
CHUNK = 64
HALO = 8
VMEM_LIMIT = 56 * 1024 * 1024

P_R, P_K, P_V, P_GA, P_Q, P_GB, P_MA, P_MB, P_X = range(9)

f32 = jnp.float32
bf16 = jnp.bfloat16


def _dot(a, b):
    return jnp.dot(a.astype(bf16), b.astype(bf16), preferred_element_type=f32)


def _dot_nt(a, b):
    return lax.dot_general(a.astype(bf16), b.astype(bf16), (((1,), (1,)), ((), ())),
                           preferred_element_type=f32)


def _dot_tn(a, b):
    return lax.dot_general(a.astype(bf16), b.astype(bf16), (((0,), (0,)), ((), ())),
                           preferred_element_type=f32)


def _dot_split(a_exact, b):
    b0 = b.astype(bf16)
    b1 = (b - b0.astype(f32)).astype(bf16)
    a = a_exact.astype(bf16)
    return jnp.dot(a, b0, preferred_element_type=f32) + jnp.dot(a, b1, preferred_element_type=f32)


def _sigmoid(x):
    return 1.0 / (1.0 + jnp.exp(-x))


def _in_proj_body(x_ref, xp_ref, xn_ref, g_ref, w_ref, mu_ref, o_ref, ox_ref, h_ref, *, tm, seq):
    i = pl.program_id(0)
    j = pl.program_id(1)

    def norm(x):
        ms = jnp.mean(x * x, axis=-1, keepdims=True)
        return x * lax.rsqrt(ms + NORM_EPS) * g_ref[...]

    @pl.when(j == 0)
    def _():
        first = (i * tm) % seq == 0
        last = ((i + 1) * tm) % seq == 0
        h_ref[HALO:HALO + tm, :] = norm(x_ref[...]).astype(bf16)
        h_ref[0:HALO, :] = jnp.where(first, 0.0, norm(xp_ref[...])).astype(bf16)
        h_ref[HALO + tm:, :] = jnp.where(last, 0.0, norm(xn_ref[...])).astype(bf16)

    def shifted():
        p = jnp.dot(h_ref[...], w_ref[...], preferred_element_type=f32)
        pc = p[HALO:HALO + tm]
        nbr = 0.5 * (p[HALO - 1:HALO - 1 + tm] + p[HALO + 1:HALO + 1 + tm])
        return pc + mu_ref[...] * (nbr - pc)

    @pl.when(j <= P_V)
    def _():
        o_ref[0] = shifted().astype(bf16)

    @pl.when(j == P_X)
    def _():
        ox_ref[...] = shifted()

    @pl.when((j > P_V) & (j < P_X))
    def _():
        o_ref[0] = jnp.dot(h_ref[HALO:HALO + tm, :], w_ref[...], preferred_element_type=f32).astype(bf16)


def _in_proj(x2d, norm_g, w_p, mu_p, *, seq, tm):
    m = x2d.shape[0]
    nb8 = m // HALO
    return pl.pallas_call(
        functools.partial(_in_proj_body, tm=tm, seq=seq),
        grid=(m // tm, N_PLANES),
        in_specs=[
            pl.BlockSpec((tm, D_MODEL), lambda i, j: (i, 0)),
            pl.BlockSpec((HALO, D_MODEL), lambda i, j: (jnp.maximum(i * (tm // HALO) - 1, 0), 0)),
            pl.BlockSpec((HALO, D_MODEL), lambda i, j: (jnp.minimum((i + 1) * (tm // HALO), nb8 - 1), 0)),
            pl.BlockSpec((1, D_MODEL), lambda i, j: (0, 0)),
            pl.BlockSpec((D_MODEL, D_MODEL), lambda i, j: (0, j)),
            pl.BlockSpec((1, D_MODEL), lambda i, j: (0, j)),
        ],
        out_specs=[pl.BlockSpec((1, tm, D_MODEL), lambda i, j: (jnp.minimum(j, P_X - 1), i, 0)),
                   pl.BlockSpec((tm, D_MODEL), lambda i, j: (i, 0))],
        out_shape=[jax.ShapeDtypeStruct((N_PLANES - 1, m, D_MODEL), bf16),
                   jax.ShapeDtypeStruct((m, D_MODEL), f32)],
        scratch_shapes=[pltpu.VMEM((tm + 2 * HALO, D_MODEL), bf16)],
        compiler_params=pltpu.CompilerParams(
            dimension_semantics=("parallel", "arbitrary"), vmem_limit_bytes=VMEM_LIMIT),
        name="in_proj",
    )(x2d, x2d, x2d, norm_g, w_p, mu_p)


def _prep_layer_weights(w_in, shift_mu):
    c = C_SHIFT
    d = D_MODEL
    cols = [
        (0, d), (d, 2 * d), (2 * d, 3 * d),
        (c, c + d),
        (c + d, c + 2 * d),
        (c + 2 * d + 2 * D_KV, c + 3 * d + 2 * D_KV),
        (c + 3 * d + 2 * D_KV, c + 4 * d + 2 * D_KV),
        (c + 4 * d + 2 * D_KV, c + 5 * d + 2 * D_KV),
    ]
    planes = [w_in[:, a:b] for a, b in cols]
    small = jnp.concatenate([w_in[:, 3 * d:c], w_in[:, c + 2 * d:c + 2 * d + 2 * D_KV],
                             jnp.zeros((d, d - 4 * R_LORA - 2 * D_KV), w_in.dtype)], axis=1)
    w_p = jnp.concatenate(planes + [small], axis=1).astype(bf16)
    zeros = jnp.zeros((d,), f32)
    mu_small = jnp.concatenate([shift_mu[3 * d:c], jnp.zeros((d - 4 * R_LORA,), f32)])
    mu_p = jnp.concatenate([shift_mu[0:d], shift_mu[d:2 * d], shift_mu[2 * d:3 * d]] + [zeros] * 5 + [mu_small])
    return w_p, mu_p.reshape(1, N_PLANES * d)


PAIR = 2 * HEAD


def _rwkv_p_body(r_ref, k_ref, v_ref, x_ref, w0_ref, wlu_ref, a0_ref, alu_ref, kk_ref, ka_ref, rk_ref,
                 y0_ref, bonus_ref, rp_ref, mn_ref,
                 at_s, rt_s, kpm_s, bpm_s, khm_s, bhm_s, atx_s, rtx_s, vx_s, vo_s, pl_s, *, tt):
    nc = tt // CHUNK
    r = r_ref[0].astype(f32)
    k = k_ref[0].astype(f32)
    v = v_ref[0].astype(f32)

    lane = lax.broadcasted_iota(jnp.int32, (PAIR, PAIR), 0) // HEAD
    seg = jnp.where(lane == lax.broadcasted_iota(jnp.int32, (PAIR, PAIR), 1) // HEAD, 1.0, 0.0)
    lo = lax.broadcasted_iota(jnp.int32, (tt, PAIR), 1) < HEAD
    swap = lambda t: pltpu.roll(t, HEAD, axis=1)

    def head_lo(t, h):
        return jnp.where(lo, t if h == 0 else swap(t), 0.0)

    def head_hi(t, h):
        return jnp.where(lo, 0.0, swap(t) if h == 0 else t)

    def head_own(t, h):
        return jnp.where(lo, t, 0.0) if h == 0 else jnp.where(lo, 0.0, t)

    for h in range(2):
        vx_s[h] = head_hi(v, h).astype(bf16)
    vo_s[...] = head_own(v, 0).astype(bf16)

    kkv = k * kk_ref[...]
    kk = kkv / jnp.maximum(jnp.sqrt(_segsum(kkv * kkv, seg)), 1e-12)

    rows = lax.broadcasted_iota(jnp.int32, (tt, tt), 0)
    cols = lax.broadcasted_iota(jnp.int32, (tt, tt), 1)
    same = (rows // CHUNK) == (cols // CHUNK)

    kmod_sum = jnp.zeros_like(k)
    for d in range(2):
        if d == 0:
            tri_bd = jnp.where(same & (cols <= rows), 1.0, 0.0)
        else:
            tri_bd = jnp.where(same & (cols >= rows), 1.0, 0.0)
        wd = x_ref[:, d * R_LORA:(d + 1) * R_LORA]
        ad = x_ref[:, 2 * R_LORA + d * R_LORA:2 * R_LORA + (d + 1) * R_LORA]
        z = w0_ref[d:d + 1, :] + _dot(jnp.tanh(wd), wlu_ref[d])
        u = -z
        w_log = -(jnp.maximum(u, 0.0) + jnp.log(1.0 + jnp.exp(-jnp.abs(u)))) - 0.5
        lw = -jnp.exp(w_log)
        a = _sigmoid(a0_ref[d:d + 1, :] + _dot(ad, alu_ref[d]))
        kmod = k * (1.0 + (a - 1.0) * ka_ref[...])
        kmod_sum = kmod_sum + kmod
        b = -(kk * a)
        cum = _dot_split(tri_bd, lw)
        edge = CHUNK - 1 if d == 0 else 0
        tot = jnp.concatenate([jnp.broadcast_to(cum[c * CHUNK + edge:c * CHUNK + edge + 1, :], (CHUNK, PAIR))
                               for c in range(nc)], axis=0)
        p_inv = jnp.exp(-cum)
        p_end = jnp.exp(tot - cum)
        at = kk * jnp.exp(cum - lw)
        rt = r * jnp.exp(cum)
        kh = kmod * p_inv
        bh = b * p_inv
        p_tot = jnp.exp(tot)
        at_s[d] = at.astype(bf16)
        rt_s[d] = rt.astype(bf16)
        pl_s[d] = p_tot
        kp = kmod * p_end
        bp = b * p_end
        for h in range(2):
            khm_s[d, h] = head_own(kh, h).astype(bf16)
            bhm_s[d, h] = head_own(bh, h).astype(bf16)
            kpm_s[d, h] = head_own(kp, h).astype(bf16)
            bpm_s[d, h] = head_own(bp, h).astype(bf16)
            atx_s[d, h] = head_lo(at, h).astype(bf16)
            rtx_s[d, h] = head_lo(rt, h)

    bonus_ref[...] = _segsum(r * kmod_sum * rk_ref[...], seg) * v

    row = lax.broadcasted_iota(jnp.int32, (CHUNK, PAIR), 0)
    col = lax.broadcasted_iota(jnp.int32, (CHUNK, PAIR), 1)
    col_t = col % CHUNK
    lo_c = col < HEAD
    strict = (col_t < row, col_t > row)
    incl = (col_t <= row, col_t >= row)
    pairs = [(d, c) for d in range(2) for c in range(nc)]
    inst = [(d, c, h) for d, c in pairs for h in range(2)]
    cs = lambda c: slice(c * CHUNK, (c + 1) * CHUNK)
    swap_c = lambda t: pltpu.roll(t, HEAD, axis=1)
    zeros = jnp.zeros((CHUNK, PAIR), bf16)

    top, bot = [], []
    for d, c in pairs:
        lhs = jnp.concatenate([at_s[d, cs(c)], rt_s[d, cs(c)]], axis=0)
        rhs = jnp.concatenate([bhm_s[d, 0, cs(c)], khm_s[d, 0, cs(c)],
                               bhm_s[d, 1, cs(c)], khm_s[d, 1, cs(c)]], axis=0)
        sc = lax.dot_general(lhs, rhs, (((1,), (1,)), ((), ())), preferred_element_type=f32)
        for h in range(2):
            top.append(jnp.where(strict[d], sc[:CHUNK, h * PAIR:(h + 1) * PAIR], 0.0))
            bot.append(jnp.where(incl[d], sc[CHUNK:, h * PAIR:(h + 1) * PAIR], 0.0))
    xs, aps = [], []
    for i, (d, c, h) in enumerate(inst):
        vx = vx_s[h, cs(c)]
        a_ak = jnp.where(lo_c, 0.0, top[i]).astype(bf16)
        akv = jnp.dot(a_ak, jnp.concatenate([vx, vx], axis=0), preferred_element_type=f32)
        xs.append(atx_s[d, h, cs(c)].astype(f32) + akv)
        aps.append(top[i][:, :CHUNK])
    n_dbl = CHUNK.bit_length() - 1
    for it in range(n_dbl):
        last = it + 1 == n_dbl
        res = [_dot(ap, x if last else jnp.concatenate([x, ap], axis=1)) for ap, x in zip(aps, xs)]
        xs = [x + rs[:, :PAIR] for x, rs in zip(xs, res)]
        if not last:
            aps = [rs[:, PAIR:] for rs in res]
    rys = [_dot(bot[i], jnp.concatenate([xs[i].astype(bf16), vx_s[h, cs(c)]], axis=0)) + rtx_s[d, h, cs(c)]
           for i, (d, c, h) in enumerate(inst)]
    mns = []
    for j, (d, c) in enumerate(pairs):
        x0, x1 = xs[2 * j], xs[2 * j + 1]
        r0 = jnp.concatenate([jnp.where(lo_c, x0, 0.0), jnp.where(lo_c, swap_c(x0), 0.0)], axis=1).astype(bf16)
        r1 = jnp.concatenate([jnp.where(lo_c, 0.0, swap_c(x1)), jnp.where(lo_c, 0.0, x1)], axis=1).astype(bf16)
        r2 = jnp.concatenate([zeros, vo_s[cs(c)]], axis=1)
        r3 = jnp.concatenate([zeros, vx_s[1, cs(c)]], axis=1)
        rhs = jnp.concatenate([r0, r1, r2, r3], axis=0)
        lhs_t = jnp.concatenate([bpm_s[d, 0, cs(c)], bpm_s[d, 1, cs(c)],
                                 kpm_s[d, 0, cs(c)], kpm_s[d, 1, cs(c)]], axis=0)
        mns.append(lax.dot_general(lhs_t, rhs, (((0,), (0,)), ((), ())), preferred_element_type=f32))
    rowp = lax.broadcasted_iota(jnp.int32, (PAIR, 2 * PAIR), 0)
    colp = lax.broadcasted_iota(jnp.int32, (PAIR, 2 * PAIR), 1)
    for j, (d, c) in enumerate(pairs):
        ry0, ry1 = rys[2 * j], rys[2 * j + 1]
        rp_ref[d, cs(c), :] = jnp.where(lo_c, ry0, swap_c(ry1)).astype(bf16)
        y0 = jnp.where(lo_c, swap_c(ry0), ry1)
        if d == 0:
            y0_ref[cs(c), :] = y0
        else:
            y0_ref[cs(c), :] += y0
        p_row = pl_s[d, c * CHUNK:c * CHUNK + 1, :]
        diag = jnp.where(rowp == colp, jnp.concatenate([p_row, p_row], axis=1), 0.0)
        mn_ref[d, c, 0] = (mns[j] + diag).astype(bf16)


def _segsum(x, seg):
    x0 = x.astype(bf16)
    x1 = (x - x0.astype(f32)).astype(bf16)
    m = seg.astype(bf16)
    return jnp.dot(x0, m, preferred_element_type=f32) + jnp.dot(x1, m, preferred_element_type=f32)


def _rwkv_p(p8, px, w0, wlu, a0, alu, k_k, k_a, r_k, *, tt):
    m = px.shape[0]
    nt = m // tt
    npair = D_MODEL // PAIR
    cpt = tt // CHUNK
    tile = lambda plane: pl.BlockSpec((1, tt, PAIR), lambda i, j, plane=plane: (plane, i, j))
    vec = pl.BlockSpec((1, PAIR), lambda i, j: (0, j))
    return pl.pallas_call(
        functools.partial(_rwkv_p_body, tt=tt),
        grid=(nt, npair),
        in_specs=[
            tile(P_R), tile(P_K), tile(P_V),
            pl.BlockSpec((tt, 4 * R_LORA), lambda i, j: (i, 0)),
            pl.BlockSpec((2, PAIR), lambda i, j: (0, j)),
            pl.BlockSpec((2, R_LORA, PAIR), lambda i, j: (0, 0, j)),
            pl.BlockSpec((2, PAIR), lambda i, j: (0, j)),
            pl.BlockSpec((2, R_LORA, PAIR), lambda i, j: (0, 0, j)),
            vec, vec, vec,
        ],
        out_specs=[
            pl.BlockSpec((tt, PAIR), lambda i, j: (i, j)),
            pl.BlockSpec((tt, PAIR), lambda i, j: (i, j)),
            pl.BlockSpec((2, tt, PAIR), lambda i, j: (0, i, j)),
            pl.BlockSpec((2, cpt, 1, PAIR, 2 * PAIR), lambda i, j: (0, i, j, 0, 0)),
        ],
        out_shape=[
            jax.ShapeDtypeStruct((m, D_MODEL), f32),
            jax.ShapeDtypeStruct((m, D_MODEL), f32),
            jax.ShapeDtypeStruct((2, m, D_MODEL), bf16),
            jax.ShapeDtypeStruct((2, m // CHUNK, npair, PAIR, 2 * PAIR), bf16),
        ],
        scratch_shapes=(
            [pltpu.VMEM((2, tt, PAIR), bf16) for _ in range(2)]
            + [pltpu.VMEM((2, 2, tt, PAIR), bf16) for _ in range(5)]
            + [pltpu.VMEM((2, 2, tt, PAIR), f32),
               pltpu.VMEM((2, tt, PAIR), bf16),
               pltpu.VMEM((tt, PAIR), bf16),
               pltpu.VMEM((2, tt, PAIR), f32)]),
        compiler_params=pltpu.CompilerParams(
            dimension_semantics=("parallel", "parallel"), vmem_limit_bytes=VMEM_LIMIT),
        name="rwkv_p",
    )(p8, p8, p8, px, w0, wlu, a0, alu, k_k, k_a, r_k)


def _rwkv_s_body(mnf_ref, mnb_ref, sf_ref, sb_ref, st_ref, *, grp):
    @pl.when(pl.program_id(1) == 0)
    def _():
        st_ref[...] = jnp.zeros_like(st_ref)

    def step(d, mn_ref, s_out_ref, cc):
        s = st_ref[d].astype(bf16)
        s_out_ref[cc] = s
        m_t = mn_ref[0, cc, :, :, 0:PAIR]
        n_t = mn_ref[0, cc, :, :, PAIR:].astype(f32)
        st_ref[d] = jnp.einsum('pij,pjk->pik', m_t, s, preferred_element_type=f32) + n_t

    for cc in range(grp):
        step(0, mnf_ref, sf_ref, cc)
        step(1, mnb_ref, sb_ref, grp - 1 - cc)


def _rwkv_s(mn, *, n_seq, grp):
    nchunk, npair = mn.shape[1], mn.shape[2]
    ng = nchunk // n_seq // grp
    blk_in = (1, grp, npair, PAIR, 2 * PAIR)
    blk_out = (grp, npair, PAIR, PAIR)
    out = jax.ShapeDtypeStruct((nchunk, npair, PAIR, PAIR), bf16)
    return pl.pallas_call(
        functools.partial(_rwkv_s_body, grp=grp),
        grid=(n_seq, ng),
        in_specs=[pl.BlockSpec(blk_in, lambda b, g: (0, b * ng + g, 0, 0, 0)),
                  pl.BlockSpec(blk_in, lambda b, g: (1, b * ng + ng - 1 - g, 0, 0, 0))],
        out_specs=[pl.BlockSpec(blk_out, lambda b, g: (b * ng + g, 0, 0, 0)),
                   pl.BlockSpec(blk_out, lambda b, g: (b * ng + ng - 1 - g, 0, 0, 0))],
        out_shape=[out, out],
        scratch_shapes=[pltpu.VMEM((2, npair, PAIR, PAIR), f32)],
        compiler_params=pltpu.CompilerParams(
            dimension_semantics=("parallel", "arbitrary"), vmem_limit_bytes=VMEM_LIMIT),
        name="rwkv_s",
    )(mn, mn)


def _rwkv_f_body(y0_ref, bonus_ref, rp_ref, sf_ref, sb_ref, gate_ref, lng_ref, lnb_ref, o_ref, *, tt):
    npair = D_MODEL // PAIR
    nc = tt // CHUNK
    cs = lambda c: slice(c * CHUNK, (c + 1) * CHUNK)
    ps = lambda p: slice(p * PAIR, (p + 1) * PAIR)
    yf = [[jnp.dot(rp_ref[0, cs(c), ps(p)], sf_ref[c, p], preferred_element_type=f32) for c in range(nc)]
          for p in range(npair)]
    yb = [[jnp.dot(rp_ref[1, cs(c), ps(p)], sb_ref[c, p], preferred_element_type=f32) for c in range(nc)]
          for p in range(npair)]
    lane = lax.broadcasted_iota(jnp.int32, (PAIR, PAIR), 0) // HEAD
    seg = jnp.where(lane == lax.broadcasted_iota(jnp.int32, (PAIR, PAIR), 1) // HEAD, 1.0, 0.0)
    ys = [y0_ref[:, ps(p)] + jnp.concatenate(yf[p], axis=0) + jnp.concatenate(yb[p], axis=0) for p in range(npair)]
    mus = [_segsum(y, seg) * (1.0 / HEAD) for y in ys]
    dvs = [y - mu for y, mu in zip(ys, mus)]
    vrs = [_segsum(dv * dv, seg) * (1.0 / HEAD) for dv in dvs]
    for p in range(npair):
        yn = dvs[p] * lax.rsqrt(vrs[p] + GN_EPS) * lng_ref[:, ps(p)] + lnb_ref[:, ps(p)] + bonus_ref[:, ps(p)]
        g = gate_ref[0, :, ps(p)].astype(f32)
        o_ref[:, ps(p)] = (yn * (g * _sigmoid(g))).astype(bf16)


def _rwkv_f(y0, bonus, rp, sf, sb, p8, ln_g, ln_b, *, tt):
    m = y0.shape[0]
    cpt = tt // CHUNK
    npair = D_MODEL // PAIR
    tile = pl.BlockSpec((tt, D_MODEL), lambda i: (i, 0))
    vec = pl.BlockSpec((1, D_MODEL), lambda i: (0, 0))
    state = pl.BlockSpec((cpt, npair, PAIR, PAIR), lambda i: (i, 0, 0, 0))
    return pl.pallas_call(
        functools.partial(_rwkv_f_body, tt=tt),
        grid=(m // tt,),
        in_specs=[
            tile, tile,
            pl.BlockSpec((2, tt, D_MODEL), lambda i: (0, i, 0)),
            state, state,
            pl.BlockSpec((1, tt, D_MODEL), lambda i: (P_GA, i, 0)),
            vec, vec,
        ],
        out_specs=tile,
        out_shape=jax.ShapeDtypeStruct((m, D_MODEL), bf16),
        compiler_params=pltpu.CompilerParams(dimension_semantics=("parallel",), vmem_limit_bytes=VMEM_LIMIT),
        name="rwkv_f",
    )(y0, bonus, rp, sf, sb, p8, ln_g, ln_b)


def _attn_body(q_ref, kp_ref, kc_ref, kn_ref, vp_ref, vc_ref, vn_ref, tp_ref, tc_ref, tn_ref, gate_ref,
               qg_ref, kg_ref, sink_ref, o_ref, *, seq):
    nblk = seq // BLOCK
    start = (pl.program_id(0) % nblk) * BLOCK
    ctx = BLOCK + 2 * WINDOW
    qpos = start + lax.broadcasted_iota(jnp.int32, (BLOCK, ctx), 0)
    kpos = start - WINDOW + lax.broadcasted_iota(jnp.int32, (BLOCK, ctx), 1)
    valid = (jnp.abs(qpos - kpos) <= WINDOW) & (kpos >= 0) & (kpos < seq)

    lane_r = lax.broadcasted_iota(jnp.int32, (PAIR, PAIR), 0) // HEAD
    seg = jnp.where(lane_r == lax.broadcasted_iota(jnp.int32, (PAIR, PAIR), 1) // HEAD, 1.0, 0.0)
    swap = lambda t: pltpu.roll(t, HEAD, axis=1)

    def norm_rope(x, gain, tab):
        lane = lax.broadcasted_iota(jnp.int32, x.shape, 1)
        xn = x * lax.rsqrt(_segsum(x * x, seg) * (1.0 / HEAD) + NORM_EPS) * gain
        rot = jnp.where(lane % HEAD < HEAD // 2, pltpu.roll(xn, PAIR - HEAD // 2, axis=1),
                        pltpu.roll(xn, HEAD // 2, axis=1))
        return xn * tab[:, :PAIR] + rot * tab[:, PAIR:]

    def place(t, e, want_lo):
        lo = lax.broadcasted_iota(jnp.int32, t.shape, 1) < HEAD
        src = t if (e == 0) == want_lo else swap(t)
        return (jnp.where(lo, src, 0.0) if want_lo else jnp.where(lo, 0.0, src)).astype(bf16)

    tab_k = jnp.concatenate([tp_ref[...], tc_ref[...], tn_ref[...]], axis=0)
    tab_q = tc_ref[...]
    k_lo, k_hi, v_lo, v_hi = [], [], [], []
    for kp in range(D_KV // PAIR):
        ps_ = slice(kp * PAIR, (kp + 1) * PAIR)
        kx = jnp.concatenate([kp_ref[:, ps_], kc_ref[:, ps_], kn_ref[:, ps_]], axis=0)
        vx = jnp.concatenate([vp_ref[:, ps_], vc_ref[:, ps_], vn_ref[:, ps_]], axis=0)
        kr = norm_rope(kx, kg_ref[...], tab_k)
        for e in range(2):
            k_lo.append(place(kr, e, True))
            k_hi.append(place(kr, e, False))
            v_lo.append(place(vx, e, True))
            v_hi.append(place(vx, e, False))
    npair = D_MODEL // PAIR
    psl = lambda p: slice(p * PAIR, (p + 1) * PAIR)
    qrs = [(norm_rope(q_ref[0, :, psl(p)].astype(f32), qg_ref[...], tab_q) * (HEAD ** -0.5)).astype(bf16)
           for p in range(npair)]
    heads = range(N_HEADS)
    ss = [jnp.where(valid, lax.dot_general(qrs[hq // 2], (k_lo if hq % 2 == 0 else k_hi)[hq // GRP],
                                           (((1,), (1,)), ((), ())), preferred_element_type=f32), -1e30)
          for hq in heads]
    sinks = [sink_ref[0:1, hq:hq + 1] for hq in heads]
    mxs = [jnp.maximum(jnp.max(ss[hq], axis=-1, keepdims=True), sinks[hq]) for hq in heads]
    ps = [jnp.exp(ss[hq] - mxs[hq]) for hq in heads]
    dens = [jnp.sum(ps[hq], axis=-1, keepdims=True) + jnp.exp(sinks[hq] - mxs[hq]) for hq in heads]
    os_ = [jnp.dot(ps[hq].astype(bf16), (v_lo if hq % 2 == 0 else v_hi)[hq // GRP], preferred_element_type=f32)
           / dens[hq] for hq in heads]
    for p in range(npair):
        gt = gate_ref[0, :, psl(p)].astype(f32)
        o_ref[:, psl(p)] = ((os_[2 * p] + os_[2 * p + 1]) * (gt * _sigmoid(gt))).astype(bf16)


def _attn(p8, px, rope_tab, q_g, k_g, sink, *, seq):
    m = px.shape[0]
    nblk = seq // BLOCK

    def nbr(delta):
        def f(i):
            t = i % nblk
            return i - t + jnp.clip(t + delta, 0, nblk - 1)
        return f

    def kv(col, delta):
        return pl.BlockSpec((BLOCK, D_KV), lambda i, col=col, f=nbr(delta): (f(i), col))

    def tab(delta):
        return pl.BlockSpec((BLOCK, 2 * PAIR), lambda i, f=nbr(delta): (f(i) % nblk, 0))

    return pl.pallas_call(
        functools.partial(_attn_body, seq=seq),
        grid=(m // BLOCK,),
        in_specs=[
            pl.BlockSpec((1, BLOCK, D_MODEL), lambda i: (P_Q, i, 0)),
            kv(1, -1), kv(1, 0), kv(1, 1), kv(2, -1), kv(2, 0), kv(2, 1),
            tab(-1), tab(0), tab(1),
            pl.BlockSpec((1, BLOCK, D_MODEL), lambda i: (P_GB, i, 0)),
            pl.BlockSpec((1, PAIR), lambda i: (0, 0)),
            pl.BlockSpec((1, PAIR), lambda i: (0, 0)),
            pl.BlockSpec((1, N_HEADS), lambda i: (0, 0)),
        ],
        out_specs=pl.BlockSpec((BLOCK, D_MODEL), lambda i: (i, 0)),
        out_shape=jax.ShapeDtypeStruct((m, D_MODEL), bf16),
        compiler_params=pltpu.CompilerParams(dimension_semantics=("parallel",), vmem_limit_bytes=VMEM_LIMIT),
        name="attn",
    )(p8, px, px, px, px, px, px, rope_tab, rope_tab, rope_tab, p8, q_g, k_g, sink)


def _rope_table(seq):
    inv = 1.0 / (ROPE_THETA ** (jnp.arange(0, HEAD, 2, dtype=f32) / HEAD))
    ang = jnp.arange(seq, dtype=f32)[:, None] * inv[None, :]
    cos, sin = jnp.cos(ang), jnp.sin(ang)
    return jnp.concatenate([cos, cos, cos, cos, -sin, sin, -sin, sin], axis=1)


def _out_proj_body(x_ref, ga_ref, gb_ref, ma_ref, mb_ref, wa_ref, wb_ref, wo_ref, o_ref):
    ya = jnp.dot(ga_ref[...], wa_ref[...], preferred_element_type=f32)
    yb = jnp.dot(gb_ref[...], wb_ref[...], preferred_element_type=f32)
    mixed = _sigmoid(ma_ref[0].astype(f32)) * ya + _sigmoid(mb_ref[0].astype(f32)) * yb
    o_ref[...] = x_ref[...] + jnp.dot(mixed.astype(bf16), wo_ref[...], preferred_element_type=f32)


def _out_proj(x2d, ga, gb, p8, wa, wb, wo, *, tm):
    m = x2d.shape[0]
    row = pl.BlockSpec((tm, D_MODEL), lambda i: (i, 0))
    wspec = pl.BlockSpec((D_MODEL, D_MODEL), lambda i: (0, 0))
    return pl.pallas_call(
        _out_proj_body,
        grid=(m // tm,),
        in_specs=[row, row, row,
                  pl.BlockSpec((1, tm, D_MODEL), lambda i: (P_MA, i, 0)),
                  pl.BlockSpec((1, tm, D_MODEL), lambda i: (P_MB, i, 0)),
                  wspec, wspec, wspec],
        out_specs=row,
        out_shape=jax.ShapeDtypeStruct((m, D_MODEL), f32),
        compiler_params=pltpu.CompilerParams(dimension_semantics=("parallel",), vmem_limit_bytes=VMEM_LIMIT),
        name="out_proj",
    )(x2d, ga, gb, p8, p8, wa, wb, wo)


def _layer(x2d, lw, rope_tab, *, n_seq, seq):
    tm = min(1024, seq)
    tt = min(256, seq)
    grp = min(8, seq // CHUNK)
    p8, px = _in_proj(x2d, lw["norm_g"], lw["w_p"], lw["mu_p"], seq=seq, tm=tm)
    y0, bonus, rp, mn = _rwkv_p(p8, px, lw["w0"], lw["wlu"], lw["a0"], lw["alu"], lw["k_k"], lw["k_a"], lw["r_k"], tt=tt)
    sf, sb = _rwkv_s(mn, n_seq=n_seq, grp=grp)
    ga = _rwkv_f(y0, bonus, rp, sf, sb, p8, lw["ln_g"], lw["ln_b"], tt=tt)
    gb = _attn(p8, px, rope_tab, lw["q_g"], lw["k_g"], lw["sink"], seq=seq)
    return _out_proj(x2d, ga, gb, p8, lw["wa"], lw["wb"], lw["wo"], tm=tm)


def _trunk(x, layers):
    n_seq, seq, _ = x.shape
    rope_tab = _rope_table(seq)
    x2d = x.reshape(n_seq * seq, D_MODEL)
    for lw in layers:
        x2d = _layer(x2d, lw, rope_tab, n_seq=n_seq, seq=seq)
    return x2d.reshape(n_seq, seq, D_MODEL)


def _prep_layers(norm_g, w_in, shift_mu, w0, w_lora_up, a0, a_lora_up, k_k, k_a, r_k, ln_x_g, ln_x_b,
                 q_norm_g, k_norm_g, sink, w_proj_a, w_proj_b, w_out):
    layers = []
    row = lambda v: v.reshape(1, -1).astype(f32)
    for l in range(norm_g.shape[0]):
        w_p, mu_p = _prep_layer_weights(w_in[l], shift_mu[l])
        layers.append(dict(
            norm_g=row(norm_g[l]), w_p=w_p, mu_p=mu_p,
            w0=w0[l].astype(f32), wlu=w_lora_up[l].astype(bf16), a0=a0[l].astype(f32), alu=a_lora_up[l].astype(bf16),
            k_k=row(k_k[l]), k_a=row(k_a[l]), r_k=row(r_k[l]), ln_g=row(ln_x_g[l]), ln_b=row(ln_x_b[l]),
            q_g=row(jnp.tile(q_norm_g[l], 2)), k_g=row(jnp.tile(k_norm_g[l], 2)), sink=row(sink[l]),
            wa=w_proj_a[l].astype(bf16), wb=w_proj_b[l].astype(bf16), wo=w_out[l].astype(bf16)))
    return layers


def kernel(x_prompt, x_sample, norm_g, w_in, shift_mu, w0, w_lora_up, a0, a_lora_up, k_k, k_a, r_k,
           ln_x_g, ln_x_b, q_norm_g, k_norm_g, sink, w_proj_a, w_proj_b, w_out):
    layers = _prep_layers(norm_g, w_in, shift_mu, w0, w_lora_up, a0, a_lora_up, k_k, k_a, r_k, ln_x_g, ln_x_b,
                          q_norm_g, k_norm_g, sink, w_proj_a, w_proj_b, w_out)
    return _trunk(x_prompt, layers), _trunk(x_sample, layers)
```

```python
import functools
import math

import jax
import jax.numpy as jnp
from jax import lax
from jax.experimental import pallas as pl
from jax.experimental.pallas import tpu as pltpu

D_MODEL = 1024
HEAD = 64
N_HEADS = D_MODEL // HEAD
R_LORA = 64
HKV = 4
GRP = N_HEADS // HKV
D_KV = HKV * HEAD
WINDOW = 128
BLOCK = 128
GN_EPS = 64e-5
NORM_EPS = 1e-6
ROPE_THETA = 10000.0
C_SHIFT = 3 * D_MODEL + 4 * R_LORA
N_PLANES = 9
CHUNK = 64
VMEM_LIMIT = 56 * 1024 * 1024

P_R, P_K, P_V, P_GA, P_Q, P_GB, P_MA, P_MB, P_X = range(9)

f32 = jnp.float32
bf16 = jnp.bfloat16


def _dot(a, b):
    return jnp.dot(a.astype(bf16), b.astype(bf16), preferred_element_type=f32)


def _dot_nt(a, b):
    return lax.dot_general(a.astype(bf16), b.astype(bf16), (((1,), (1,)), ((), ())),
                           preferred_element_type=f32)


def _dot_tn(a, b):
    return lax.dot_general(a.astype(bf16), b.astype(bf16), (((0,), (0,)), ((), ())),
                           preferred_element_type=f32)


def _dot_split(a_exact, b):
    b0 = b.astype(bf16)
    b1 = (b - b0.astype(f32)).astype(bf16)
    a = a_exact.astype(bf16)
    return jnp.dot(a, b0, preferred_element_type=f32) + jnp.dot(a, b1, preferred_element_type=f32)


def _sigmoid(x):
    return 1.0 / (1.0 + jnp.exp(-x))


def _in_proj_body(x_ref, g_ref, w_ref, o_ref, ox_ref, h_ref):
    j = pl.program_id(1)

    @pl.when(j == 0)
    def _():
        x = x_ref[...]
        ms = jnp.mean(x * x, axis=-1, keepdims=True)
        h_ref[...] = (x * lax.rsqrt(ms + NORM_EPS) * g_ref[...]).astype(bf16)

    @pl.when(j < P_X)
    def _():
        o_ref[0] = jnp.dot(h_ref[...], w_ref[...], preferred_element_type=f32).astype(bf16)

    @pl.when(j == P_X)
    def _():
        ox_ref[...] = jnp.dot(h_ref[...], w_ref[...], preferred_element_type=f32)


def _in_proj(x2d, norm_g, w_p, *, tm):
    m = x2d.shape[0]
    return pl.pallas_call(
        _in_proj_body,
        grid=(m // tm, N_PLANES),
        in_specs=[
            pl.BlockSpec((tm, D_MODEL), lambda i, j: (i, 0)),
            pl.BlockSpec((1, D_MODEL), lambda i, j: (0, 0)),
            pl.BlockSpec((D_MODEL, D_MODEL), lambda i, j: (0, j)),
        ],
        out_specs=[pl.BlockSpec((1, tm, D_MODEL), lambda i, j: (jnp.minimum(j, P_X - 1), i, 0)),
                   pl.BlockSpec((tm, D_MODEL), lambda i, j: (i, 0))],
        out_shape=[jax.ShapeDtypeStruct((N_PLANES - 1, m, D_MODEL), bf16),
                   jax.ShapeDtypeStruct((m, D_MODEL), f32)],
        scratch_shapes=[pltpu.VMEM((tm, D_MODEL), bf16)],
        compiler_params=pltpu.CompilerParams(
            dimension_semantics=("parallel", "arbitrary"), vmem_limit_bytes=VMEM_LIMIT),
        name="in_proj",
    )(x2d, norm_g, w_p)


def _prep_layer_weights(w_in, shift_mu):
    c = C_SHIFT
    d = D_MODEL
    cols = [
        (0, d), (d, 2 * d), (2 * d, 3 * d),
        (c, c + d),
        (c + d, c + 2 * d),
        (c + 2 * d + 2 * D_KV, c + 3 * d + 2 * D_KV),
        (c + 3 * d + 2 * D_KV, c + 4 * d + 2 * D_KV),
        (c + 4 * d + 2 * D_KV, c + 5 * d + 2 * D_KV),
    ]
    planes = [w_in[:, a:b] for a, b in cols]
    small = jnp.concatenate([w_in[:, 3 * d:c], w_in[:, c + 2 * d:c + 2 * d + 2 * D_KV],
                             jnp.zeros((d, d - 4 * R_LORA - 2 * D_KV), w_in.dtype)], axis=1)
    w_p = jnp.concatenate(planes + [small], axis=1).astype(bf16)
    zeros = jnp.zeros((d,), f32)
    mu_small = jnp.concatenate([shift_mu[3 * d:c], jnp.zeros((d - 4 * R_LORA,), f32)])
    mu_p = jnp.concatenate([shift_mu[0:d], shift_mu[d:2 * d], shift_mu[2 * d:3 * d]] + [zeros] * 5 + [mu_small])
    return w_p, mu_p.reshape(1, N_PLANES * d)


PAIR = 2 * HEAD
PRE, PRE_DONE, CHUNK_STAGE = "pre", "pre_done", "chunk"


def _token_shift(tt, seq):
    i = pl.program_id(0)
    first = (i * tt) % seq == 0
    last = ((i + 1) * tt) % seq == 0
    trow = lax.broadcasted_iota(jnp.int32, (tt, tt), 0)
    tcol = lax.broadcasted_iota(jnp.int32, (tt, tt), 1)
    nbr_mat = jnp.where(jnp.abs(trow - tcol) == 1, 0.5, 0.0)
    sub = lax.broadcasted_iota(jnp.int32, (8, 1), 0)

    def shift(main, pv, nx, mu):
        c = main.astype(f32)
        nbr = _dot(nbr_mat, main) if main.dtype == bf16 else _dot_split(nbr_mat, c)
        up = jnp.where(first, 0.0, pv[pv.shape[0] - 1:, :].astype(f32))
        dn = jnp.where(last, 0.0, nx[:1, :].astype(f32))
        head = nbr[:8] + jnp.where(sub == 0, 0.5 * up, 0.0)
        tail = nbr[tt - 8:] + jnp.where(sub == 7, 0.5 * dn, 0.0)
        nbr = jnp.concatenate([head, nbr[8:tt - 8], tail], axis=0)
        return c + mu * (nbr - c)

    return shift


def _rwkv_p_body(r_ref, r_pv, r_nx, k_ref, k_pv, k_nx, v_ref, v_pv, v_nx, mur_ref, muk_ref, muv_ref,
                 kk_ref, ka_ref, rk_ref,
                 y0_ref, bonus_ref, rp_ref, mn_ref,
                 at_s, rt_s, kpm_s, bpm_s, khm_s, bhm_s, atx_s, rtx_s, vx_s, vo_s, pl_s, *, tt, shift, zs, a_pre):
    nc = tt // CHUNK
    r = shift(r_ref[0], r_pv[0], r_nx[0], mur_ref[...])
    k = shift(k_ref[0], k_pv[0], k_nx[0], muk_ref[...])
    v = shift(v_ref[0], v_pv[0], v_nx[0], muv_ref[...])
    yield PRE

    lane = lax.broadcasted_iota(jnp.int32, (PAIR, PAIR), 0) // HEAD
    seg = jnp.where(lane == lax.broadcasted_iota(jnp.int32, (PAIR, PAIR), 1) // HEAD, 1.0, 0.0)
    lo = lax.broadcasted_iota(jnp.int32, (tt, PAIR), 1) < HEAD
    swap = lambda t: pltpu.roll(t, HEAD, axis=1)

    def head_lo(t, h):
        return jnp.where(lo, t if h == 0 else swap(t), 0.0)

    def head_hi(t, h):
        return jnp.where(lo, 0.0, swap(t) if h == 0 else t)

    def head_own(t, h):
        return jnp.where(lo, t, 0.0) if h == 0 else jnp.where(lo, 0.0, t)

    for h in range(2):
        vx_s[h] = head_hi(v, h).astype(bf16)
    vo_s[...] = head_own(v, 0).astype(bf16)

    kkv = k * kk_ref[...]
    kk = kkv / jnp.maximum(jnp.sqrt(_segsum(kkv * kkv, seg)), 1e-12)
    yield PRE

    rows = lax.broadcasted_iota(jnp.int32, (tt, tt), 0)
    cols = lax.broadcasted_iota(jnp.int32, (tt, tt), 1)
    same = (rows // CHUNK) == (cols // CHUNK)

    kmod_sum = jnp.zeros_like(k)
    for d in range(2):
        if d == 0:
            tri_bd = jnp.where(same & (cols <= rows), 1.0, 0.0)
        else:
            tri_bd = jnp.where(same & (cols >= rows), 1.0, 0.0)
        lw = -math.exp(-0.5) * _sigmoid(zs[d])
        a = _sigmoid(a_pre[d])
        yield PRE
        kmod = k * (1.0 + (a - 1.0) * ka_ref[...])
        kmod_sum = kmod_sum + kmod
        b = -(kk * a)
        cum = _dot_split(tri_bd, lw)
        edge = CHUNK - 1 if d == 0 else 0
        tot = jnp.concatenate([jnp.broadcast_to(cum[c * CHUNK + edge:c * CHUNK + edge + 1, :], (CHUNK, PAIR))
                               for c in range(nc)], axis=0)
        yield PRE
        p_inv = jnp.exp(-cum)
        p_end = jnp.exp(tot - cum)
        at = kk * jnp.exp(cum - lw)
        rt = r * jnp.exp(cum)
        kh = kmod * p_inv
        bh = b * p_inv
        p_tot = jnp.exp(tot)
        at_s[d] = at.astype(bf16)
        rt_s[d] = rt.astype(bf16)
        pl_s[d] = p_tot
        yield PRE
        kp = kmod * p_end
        bp = b * p_end
        for h in range(2):
            khm_s[d, h] = head_own(kh, h).astype(bf16)
            bhm_s[d, h] = head_own(bh, h).astype(bf16)
            kpm_s[d, h] = head_own(kp, h).astype(bf16)
            bpm_s[d, h] = head_own(bp, h).astype(bf16)
            atx_s[d, h] = head_lo(at, h).astype(bf16)
            rtx_s[d, h] = head_lo(rt, h)
        yield PRE

    bonus = _segsum(r * kmod_sum * rk_ref[...], seg) * v
    yield PRE_DONE

    row = lax.broadcasted_iota(jnp.int32, (CHUNK, PAIR), 0)
    col = lax.broadcasted_iota(jnp.int32, (CHUNK, PAIR), 1)
    col_t = col % CHUNK
    lo_c = col < HEAD
    strict = (col_t < row, col_t > row)
    incl = (col_t <= row, col_t >= row)
    pairs = [(d, c) for d in range(2) for c in range(nc)]
    inst = [(d, c, h) for d, c in pairs for h in range(2)]
    cs = lambda c: slice(c * CHUNK, (c + 1) * CHUNK)
    swap_c = lambda t: pltpu.roll(t, HEAD, axis=1)
    zeros = jnp.zeros((CHUNK, PAIR), bf16)

    top, bot = [], []
    for d, c in pairs:
        lhs = jnp.concatenate([at_s[d, cs(c)], rt_s[d, cs(c)]], axis=0)
        rhs = jnp.concatenate([bhm_s[d, 0, cs(c)], khm_s[d, 0, cs(c)],
                               bhm_s[d, 1, cs(c)], khm_s[d, 1, cs(c)]], axis=0)
        sc = lax.dot_general(lhs, rhs, (((1,), (1,)), ((), ())), preferred_element_type=f32)
        for h in range(2):
            top.append(jnp.where(strict[d], sc[:CHUNK, h * PAIR:(h + 1) * PAIR], 0.0))
            bot.append(jnp.where(incl[d], sc[CHUNK:, h * PAIR:(h + 1) * PAIR], 0.0))
    yield CHUNK_STAGE
    xs, aps = [], []
    for i, (d, c, h) in enumerate(inst):
        vx = vx_s[h, cs(c)]
        a_ak = jnp.where(lo_c, 0.0, top[i]).astype(bf16)
        akv = jnp.dot(a_ak, jnp.concatenate([vx, vx], axis=0), preferred_element_type=f32)
        xs.append(atx_s[d, h, cs(c)].astype(f32) + akv)
        aps.append(top[i][:, :CHUNK])
    n_dbl = CHUNK.bit_length() - 1
    for it in range(n_dbl):
        yield CHUNK_STAGE
        last = it + 1 == n_dbl
        res = [_dot(ap, x if last else jnp.concatenate([x, ap], axis=1)) for ap, x in zip(aps, xs)]
        xs = [x + rs[:, :PAIR] for x, rs in zip(xs, res)]
        if not last:
            aps = [rs[:, PAIR:] for rs in res]
    yield CHUNK_STAGE
    rys =[_dot(bot[i], jnp.concatenate([xs[i].astype(bf16), vx_s[h, cs(c)]], axis=0)) + rtx_s[d, h, cs(c)]
           for i, (d, c, h) in enumerate(inst)]
    yield CHUNK_STAGE
    mns = []
    for j, (d, c) in enumerate(pairs):
        x0, x1 = xs[2 * j], xs[2 * j + 1]
        r0 = jnp.concatenate([jnp.where(lo_c, x0, 0.0), jnp.where(lo_c, swap_c(x0), 0.0)], axis=1).astype(bf16)
        r1 = jnp.concatenate([jnp.where(lo_c, 0.0, swap_c(x1)), jnp.where(lo_c, 0.0, x1)], axis=1).astype(bf16)
        r2 = jnp.concatenate([zeros, vo_s[cs(c)]], axis=1)
        r3 = jnp.concatenate([zeros, vx_s[1, cs(c)]], axis=1)
        rhs = jnp.concatenate([r0, r1, r2, r3], axis=0)
        lhs_t = jnp.concatenate([bpm_s[d, 0, cs(c)], bpm_s[d, 1, cs(c)],
                                 kpm_s[d, 0, cs(c)], kpm_s[d, 1, cs(c)]], axis=0)
        mns.append(lax.dot_general(lhs_t, rhs, (((0,), (0,)), ((), ())), preferred_element_type=f32))
    yield CHUNK_STAGE
    rowp = lax.broadcasted_iota(jnp.int32, (PAIR, 2 * PAIR), 0)
    colp = lax.broadcasted_iota(jnp.int32, (PAIR, 2 * PAIR), 1)
    for j, (d, c) in enumerate(pairs):
        ry0, ry1 = rys[2 * j], rys[2 * j + 1]
        rp_ref[d, cs(c), :] = jnp.where(lo_c, ry0, swap_c(ry1)).astype(bf16)
        y0 = jnp.where(lo_c, swap_c(ry0), ry1)
        if d == 0:
            y0_ref[cs(c), :] = y0
        else:
            y0_ref[cs(c), :] += y0
        p_row = pl_s[d, c * CHUNK:c * CHUNK + 1, :]
        diag = jnp.where(rowp == colp, jnp.concatenate([p_row, p_row], axis=1), 0.0)
        mn_ref[d, c, 0] = (mns[j] + diag).astype(bf16)
    bonus_ref[...] = bonus


N_IN_P = 23
N_OUT_P = 4


def _rwkv_p_multi(*refs, tt, seq, pp):
    ins, outs, scr = refs[:N_IN_P], refs[N_IN_P:N_IN_P + N_OUT_P], refs[N_IN_P + N_OUT_P:]
    x_ref, x_pv, x_nx, mux_ref = ins[9], ins[10], ins[11], ins[15]
    w0_ref, wlu_ref, a0_ref, alu_ref = ins[16:20]
    shift = _token_shift(tt, seq)
    xs = shift(x_ref[...], x_pv[...], x_nx[...], mux_ref[...])
    zs = [w0_ref[d:d + 1, :] + _dot(jnp.tanh(xs[:, d * R_LORA:(d + 1) * R_LORA]), wlu_ref[d]) for d in range(2)]
    a_pre = [a0_ref[d:d + 1, :] + _dot(xs[:, (2 + d) * R_LORA:(3 + d) * R_LORA], alu_ref[d]) for d in range(2)]
    stages = []
    for q in range(pp):
        lanes = pl.ds(q * PAIR, PAIR)
        lsl = slice(q * PAIR, (q + 1) * PAIR)
        view = lambda ref, lanes=lanes: ref.at[..., lanes]
        planes = [view(ref) for ref in ins[0:9]]
        gains = [view(ref) for ref in ins[12:15]]
        params = [view(ref) for ref in ins[20:23]]
        y0_ref, bonus_ref, rp_ref, mn_ref = outs
        stages.append(_rwkv_p_body(*planes, *gains, *params,
                                   view(y0_ref), view(bonus_ref), view(rp_ref), mn_ref.at[:, :, pl.ds(q, 1)],
                                   *[s.at[q] for s in scr], tt=tt, shift=shift,
                                   zs=[z[:, lsl] for z in zs], a_pre=[a[:, lsl] for a in a_pre]))
    def run_pre(stage):
        while next(stage) != PRE_DONE:
            pass

    run_pre(stages[0])
    for q in range(pp):
        chunk, pre = stages[q], (stages[q + 1] if q + 1 < pp else None)
        chunk_live, pre_live = True, pre is not None
        while chunk_live or pre_live:
            if chunk_live:
                chunk_live = next(chunk, None) is not None
            if pre_live:
                pre_live = next(pre) != PRE_DONE


def _segsum(x, seg):
    x0 = x.astype(bf16)
    x1 = (x - x0.astype(f32)).astype(bf16)
    m = seg.astype(bf16)
    return jnp.dot(x0, m, preferred_element_type=f32) + jnp.dot(x1, m, preferred_element_type=f32)


def _rwkv_p(p8, px, mu_p, w0, wlu, a0, alu, k_k, k_a, r_k, *, tt, seq, pp):
    m = px.shape[0]
    nt = m // tt
    npair = D_MODEL // PAIR
    cpt = tt // CHUNK
    wl = pp * PAIR
    hb, hx = 16, 8

    def tiles(plane):
        return [pl.BlockSpec((1, tt, wl), lambda i, j: (plane, i, j)),
                pl.BlockSpec((1, hb, wl), lambda i, j: (plane, jnp.maximum(i * (tt // hb) - 1, 0), j)),
                pl.BlockSpec((1, hb, wl), lambda i, j: (plane, jnp.minimum((i + 1) * (tt // hb), m // hb - 1), j))]

    vec = pl.BlockSpec((1, wl), lambda i, j: (0, j))
    mu = lambda plane: pl.BlockSpec((1, wl), lambda i, j: (0, plane * (npair // pp) + j))
    wx = 4 * R_LORA
    return pl.pallas_call(
        functools.partial(_rwkv_p_multi, tt=tt, seq=seq, pp=pp),
        grid=(nt, npair // pp),
        in_specs=tiles(P_R) + tiles(P_K) + tiles(P_V) + [
            pl.BlockSpec((tt, wx), lambda i, j: (i, 0)),
            pl.BlockSpec((hx, wx), lambda i, j: (jnp.maximum(i * (tt // hx) - 1, 0), 0)),
            pl.BlockSpec((hx, wx), lambda i, j: (jnp.minimum((i + 1) * (tt // hx), m // hx - 1), 0)),
            mu(P_R), mu(P_K), mu(P_V),
            pl.BlockSpec((1, wx), lambda i, j: (0, P_X * D_MODEL // wx)),
            pl.BlockSpec((2, wl), lambda i, j: (0, j)),
            pl.BlockSpec((2, R_LORA, wl), lambda i, j: (0, 0, j)),
            pl.BlockSpec((2, wl), lambda i, j: (0, j)),
            pl.BlockSpec((2, R_LORA, wl), lambda i, j: (0, 0, j)),
            vec, vec, vec,
        ],
        out_specs=[
            pl.BlockSpec((tt, wl), lambda i, j: (i, j)),
            pl.BlockSpec((tt, wl), lambda i, j: (i, j)),
            pl.BlockSpec((2, tt, wl), lambda i, j: (0, i, j)),
            pl.BlockSpec((2, cpt, pp, PAIR, 2 * PAIR), lambda i, j: (0, i, j, 0, 0)),
        ],
        out_shape=[
            jax.ShapeDtypeStruct((m, D_MODEL), f32),
            jax.ShapeDtypeStruct((m, D_MODEL), f32),
            jax.ShapeDtypeStruct((2, m, D_MODEL), bf16),
            jax.ShapeDtypeStruct((2, m // CHUNK, npair, PAIR, 2 * PAIR), bf16),
        ],
        scratch_shapes=(
            [pltpu.VMEM((pp, 2, tt, PAIR), bf16) for _ in range(2)]
            + [pltpu.VMEM((pp, 2, 2, tt, PAIR), bf16) for _ in range(5)]
            + [pltpu.VMEM((pp, 2, 2, tt, PAIR), f32),
               pltpu.VMEM((pp, 2, tt, PAIR), bf16),
               pltpu.VMEM((pp, tt, PAIR), bf16),
               pltpu.VMEM((pp, 2, tt, PAIR), f32)]),
        compiler_params=pltpu.CompilerParams(
            dimension_semantics=("parallel", "parallel"), vmem_limit_bytes=VMEM_LIMIT),
        name="rwkv_p",
    )(p8, p8, p8, p8, p8, p8, p8, p8, p8, px, px, px, mu_p, mu_p, mu_p, mu_p, w0, wlu, a0, alu, k_k, k_a, r_k)


def _rwkv_s_body(mnf_ref, mnb_ref, sf_ref, sb_ref, st_ref, *, grp):
    @pl.when(pl.program_id(1) == 0)
    def _():
        st_ref[...] = jnp.zeros_like(st_ref)

    def step(d, mn_ref, s_out_ref, cc):
        s = st_ref[d].astype(bf16)
        s_out_ref[cc] = s
        m_t = mn_ref[0, cc, :, :, 0:PAIR]
        n_t = mn_ref[0, cc, :, :, PAIR:].astype(f32)
        st_ref[d] = jnp.einsum('pij,pjk->pik', m_t, s, preferred_element_type=f32) + n_t

    for cc in range(grp):
        step(0, mnf_ref, sf_ref, cc)
        step(1, mnb_ref, sb_ref, grp - 1 - cc)


def _rwkv_s(mn, *, n_seq, grp):
    nchunk, npair = mn.shape[1], mn.shape[2]
    ng = nchunk // n_seq // grp
    blk_in = (1, grp, npair, PAIR, 2 * PAIR)
    blk_out = (grp, npair, PAIR, PAIR)
    out = jax.ShapeDtypeStruct((nchunk, npair, PAIR, PAIR), bf16)
    return pl.pallas_call(
        functools.partial(_rwkv_s_body, grp=grp),
        grid=(n_seq, ng),
        in_specs=[pl.BlockSpec(blk_in, lambda b, g: (0, b * ng + g, 0, 0, 0)),
                  pl.BlockSpec(blk_in, lambda b, g: (1, b * ng + ng - 1 - g, 0, 0, 0))],
        out_specs=[pl.BlockSpec(blk_out, lambda b, g: (b * ng + g, 0, 0, 0)),
                   pl.BlockSpec(blk_out, lambda b, g: (b * ng + ng - 1 - g, 0, 0, 0))],
        out_shape=[out, out],
        scratch_shapes=[pltpu.VMEM((2, npair, PAIR, PAIR), f32)],
        compiler_params=pltpu.CompilerParams(
            dimension_semantics=("parallel", "arbitrary"), vmem_limit_bytes=VMEM_LIMIT),
        name="rwkv_s",
    )(mn, mn)


def _rwkv_f_body(y0_ref, bonus_ref, rp_ref, sf_ref, sb_ref, gate_ref, lng_ref, lnb_ref, o_ref, *, tt):
    npair = D_MODEL // PAIR
    nc = tt // CHUNK
    cs = lambda c: slice(c * CHUNK, (c + 1) * CHUNK)
    ps = lambda p: slice(p * PAIR, (p + 1) * PAIR)
    yf = [[jnp.dot(rp_ref[0, cs(c), ps(p)], sf_ref[c, p], preferred_element_type=f32) for c in range(nc)]
          for p in range(npair)]
    yb = [[jnp.dot(rp_ref[1, cs(c), ps(p)], sb_ref[c, p], preferred_element_type=f32) for c in range(nc)]
          for p in range(npair)]
    lane = lax.broadcasted_iota(jnp.int32, (PAIR, PAIR), 0) // HEAD
    seg = jnp.where(lane == lax.broadcasted_iota(jnp.int32, (PAIR, PAIR), 1) // HEAD, 1.0, 0.0)
    ys = [y0_ref[:, ps(p)] + jnp.concatenate(yf[p], axis=0) + jnp.concatenate(yb[p], axis=0) for p in range(npair)]
    mus = [_segsum(y, seg) * (1.0 / HEAD) for y in ys]
    dvs = [y - mu for y, mu in zip(ys, mus)]
    vrs = [_segsum(dv * dv, seg) * (1.0 / HEAD) for dv in dvs]
    for p in range(npair):
        yn = dvs[p] * lax.rsqrt(vrs[p] + GN_EPS) * lng_ref[:, ps(p)] + lnb_ref[:, ps(p)] + bonus_ref[:, ps(p)]
        g = gate_ref[0, :, ps(p)].astype(f32)
        o_ref[:, ps(p)] = (yn * (g * _sigmoid(g))).astype(bf16)


def _rwkv_f(y0, bonus, rp, sf, sb, p8, ln_g, ln_b, *, tt):
    m = y0.shape[0]
    cpt = tt // CHUNK
    npair = D_MODEL // PAIR
    tile = pl.BlockSpec((tt, D_MODEL), lambda i: (i, 0))
    vec = pl.BlockSpec((1, D_MODEL), lambda i: (0, 0))
    state = pl.BlockSpec((cpt, npair, PAIR, PAIR), lambda i: (i, 0, 0, 0))
    return pl.pallas_call(
        functools.partial(_rwkv_f_body, tt=tt),
        grid=(m // tt,),
        in_specs=[
            tile, tile,
            pl.BlockSpec((2, tt, D_MODEL), lambda i: (0, i, 0)),
            state, state,
            pl.BlockSpec((1, tt, D_MODEL), lambda i: (P_GA, i, 0)),
            vec, vec,
        ],
        out_specs=tile,
        out_shape=jax.ShapeDtypeStruct((m, D_MODEL), bf16),
        compiler_params=pltpu.CompilerParams(dimension_semantics=("parallel",), vmem_limit_bytes=VMEM_LIMIT),
        name="rwkv_f",
    )(y0, bonus, rp, sf, sb, p8, ln_g, ln_b)


def _attn_body(q_ref, kp_ref, kc_ref, kn_ref, vp_ref, vc_ref, vn_ref, tp_ref, tc_ref, tn_ref, gate_ref,
               qg_ref, kg_ref, sink_ref, o_ref, *, seq):
    nblk = seq // BLOCK
    start = (pl.program_id(0) % nblk) * BLOCK
    ctx = BLOCK + 2 * WINDOW
    qpos = start + lax.broadcasted_iota(jnp.int32, (BLOCK, ctx), 0)
    kpos = start - WINDOW + lax.broadcasted_iota(jnp.int32, (BLOCK, ctx), 1)
    valid = (jnp.abs(qpos - kpos) <= WINDOW) & (kpos >= 0) & (kpos < seq)

    lane_r = lax.broadcasted_iota(jnp.int32, (PAIR, PAIR), 0) // HEAD
    seg = jnp.where(lane_r == lax.broadcasted_iota(jnp.int32, (PAIR, PAIR), 1) // HEAD, 1.0, 0.0)
    swap = lambda t: pltpu.roll(t, HEAD, axis=1)

    def norm_rope(x, gain, tab):
        lane = lax.broadcasted_iota(jnp.int32, x.shape, 1)
        xn = x * lax.rsqrt(_segsum(x * x, seg) * (1.0 / HEAD) + NORM_EPS) * gain
        rot = jnp.where(lane % HEAD < HEAD // 2, pltpu.roll(xn, PAIR - HEAD // 2, axis=1),
                        pltpu.roll(xn, HEAD // 2, axis=1))
        return xn * tab[:, :PAIR] + rot * tab[:, PAIR:]

    def place(t, e, want_lo):
        lo = lax.broadcasted_iota(jnp.int32, t.shape, 1) < HEAD
        src = t if (e == 0) == want_lo else swap(t)
        return (jnp.where(lo, src, 0.0) if want_lo else jnp.where(lo, 0.0, src)).astype(bf16)

    tab_k = jnp.concatenate([tp_ref[...], tc_ref[...], tn_ref[...]], axis=0)
    tab_q = tc_ref[...]
    k_lo, k_hi, v_lo, v_hi = [], [], [], []
    for kp in range(D_KV // PAIR):
        ps_ = slice(kp * PAIR, (kp + 1) * PAIR)
        kx = jnp.concatenate([kp_ref[:, ps_], kc_ref[:, ps_], kn_ref[:, ps_]], axis=0)
        vx = jnp.concatenate([vp_ref[:, ps_], vc_ref[:, ps_], vn_ref[:, ps_]], axis=0)
        kr = norm_rope(kx, kg_ref[...], tab_k)
        for e in range(2):
            k_lo.append(place(kr, e, True))
            k_hi.append(place(kr, e, False))
            v_lo.append(place(vx, e, True))
            v_hi.append(place(vx, e, False))
    npair = D_MODEL // PAIR
    psl = lambda p: slice(p * PAIR, (p + 1) * PAIR)
    qrs = [(norm_rope(q_ref[0, :, psl(p)].astype(f32), qg_ref[...], tab_q) * (HEAD ** -0.5)).astype(bf16)
           for p in range(npair)]
    heads = range(N_HEADS)
    ss = [jnp.where(valid, lax.dot_general(qrs[hq // 2], (k_lo if hq % 2 == 0 else k_hi)[hq // GRP],
                                           (((1,), (1,)), ((), ())), preferred_element_type=f32), -1e30)
          for hq in heads]
    sinks = [sink_ref[0:1, hq:hq + 1] for hq in heads]
    mxs = [jnp.maximum(jnp.max(ss[hq], axis=-1, keepdims=True), sinks[hq]) for hq in heads]
    ps = [jnp.exp(ss[hq] - mxs[hq]) for hq in heads]
    dens = [jnp.sum(ps[hq], axis=-1, keepdims=True) + jnp.exp(sinks[hq] - mxs[hq]) for hq in heads]
    os_ = [jnp.dot(ps[hq].astype(bf16), (v_lo if hq % 2 == 0 else v_hi)[hq // GRP], preferred_element_type=f32)
           / dens[hq] for hq in heads]
    for p in range(npair):
        gt = gate_ref[0, :, psl(p)].astype(f32)
        o_ref[:, psl(p)] = ((os_[2 * p] + os_[2 * p + 1]) * (gt * _sigmoid(gt))).astype(bf16)


def _attn(p8, px, rope_tab, q_g, k_g, sink, *, seq):
    m = px.shape[0]
    nblk = seq // BLOCK

    def nbr(delta):
        def f(i):
            t = i % nblk
            return i - t + jnp.clip(t + delta, 0, nblk - 1)
        return f

    def kv(col, delta):
        return pl.BlockSpec((BLOCK, D_KV), lambda i, col=col, f=nbr(delta): (f(i), col))

    def tab(delta):
        return pl.BlockSpec((BLOCK, 2 * PAIR), lambda i, f=nbr(delta): (f(i) % nblk, 0))

    return pl.pallas_call(
        functools.partial(_attn_body, seq=seq),
        grid=(m // BLOCK,),
        in_specs=[
            pl.BlockSpec((1, BLOCK, D_MODEL), lambda i: (P_Q, i, 0)),
            kv(1, -1), kv(1, 0), kv(1, 1), kv(2, -1), kv(2, 0), kv(2, 1),
            tab(-1), tab(0), tab(1),
            pl.BlockSpec((1, BLOCK, D_MODEL), lambda i: (P_GB, i, 0)),
            pl.BlockSpec((1, PAIR), lambda i: (0, 0)),
            pl.BlockSpec((1, PAIR), lambda i: (0, 0)),
            pl.BlockSpec((1, N_HEADS), lambda i: (0, 0)),
        ],
        out_specs=pl.BlockSpec((BLOCK, D_MODEL), lambda i: (i, 0)),
        out_shape=jax.ShapeDtypeStruct((m, D_MODEL), bf16),
        compiler_params=pltpu.CompilerParams(dimension_semantics=("parallel",), vmem_limit_bytes=VMEM_LIMIT),
        name="attn",
    )(p8, px, px, px, px, px, px, rope_tab, rope_tab, rope_tab, p8, q_g, k_g, sink)


def _rope_table(seq):
    inv = 1.0 / (ROPE_THETA ** (jnp.arange(0, HEAD, 2, dtype=f32) / HEAD))
    ang = jnp.arange(seq, dtype=f32)[:, None] * inv[None, :]
    cos, sin = jnp.cos(ang), jnp.sin(ang)
    return jnp.concatenate([cos, cos, cos, cos, -sin, sin, -sin, sin], axis=1)


def _out_proj_body(x_ref, ga_ref, gb_ref, ma_ref, mb_ref, wa_ref, wb_ref, wo_ref, o_ref):
    ya = jnp.dot(ga_ref[...], wa_ref[...], preferred_element_type=f32)
    yb = jnp.dot(gb_ref[...], wb_ref[...], preferred_element_type=f32)
    mixed = _sigmoid(ma_ref[0].astype(f32)) * ya + _sigmoid(mb_ref[0].astype(f32)) * yb
    o_ref[...] = x_ref[...] + jnp.dot(mixed.astype(bf16), wo_ref[...], preferred_element_type=f32)


def _out_proj(x2d, ga, gb, p8, wa, wb, wo, *, tm):
    m = x2d.shape[0]
    row = pl.BlockSpec((tm, D_MODEL), lambda i: (i, 0))
    wspec = pl.BlockSpec((D_MODEL, D_MODEL), lambda i: (0, 0))
    return pl.pallas_call(
        _out_proj_body,
        grid=(m // tm,),
        in_specs=[row, row, row,
                  pl.BlockSpec((1, tm, D_MODEL), lambda i: (P_MA, i, 0)),
                  pl.BlockSpec((1, tm, D_MODEL), lambda i: (P_MB, i, 0)),
                  wspec, wspec, wspec],
        out_specs=row,
        out_shape=jax.ShapeDtypeStruct((m, D_MODEL), f32),
        compiler_params=pltpu.CompilerParams(dimension_semantics=("parallel",), vmem_limit_bytes=VMEM_LIMIT),
        name="out_proj",
    )(x2d, ga, gb, p8, p8, wa, wb, wo)


def _layer(x2d, lw, rope_tab, *, n_seq, seq):
    tm = min(1024, seq)
    tt = min(256, seq)
    grp = min(8, seq // CHUNK)
    p8, px = _in_proj(x2d, lw["norm_g"], lw["w_p"], tm=tm)
    y0, bonus, rp, mn = _rwkv_p(p8, px, lw["mu_p"], lw["w0"], lw["wlu"], lw["a0"], lw["alu"], lw["k_k"], lw["k_a"],
                                lw["r_k"], tt=tt, seq=seq, pp=4)
    sf, sb = _rwkv_s(mn, n_seq=n_seq, grp=grp)
    ga = _rwkv_f(y0, bonus, rp, sf, sb, p8, lw["ln_g"], lw["ln_b"], tt=tt)
    gb = _attn(p8, px, rope_tab, lw["q_g"], lw["k_g"], lw["sink"], seq=seq)
    return _out_proj(x2d, ga, gb, p8, lw["wa"], lw["wb"], lw["wo"], tm=tm)


def _trunk(x, layers):
    n_seq, seq, _ = x.shape
    rope_tab = _rope_table(seq)
    x2d = x.reshape(n_seq * seq, D_MODEL)
    for lw in layers:
        x2d = _layer(x2d, lw, rope_tab, n_seq=n_seq, seq=seq)
    return x2d.reshape(n_seq, seq, D_MODEL)


def _prep_layers(norm_g, w_in, shift_mu, w0, w_lora_up, a0, a_lora_up, k_k, k_a, r_k, ln_x_g, ln_x_b,
                 q_norm_g, k_norm_g, sink, w_proj_a, w_proj_b, w_out):
    layers = []
    row = lambda v: v.reshape(1, -1).astype(f32)
    for l in range(norm_g.shape[0]):
        w_p, mu_p = _prep_layer_weights(w_in[l], shift_mu[l])
        layers.append(dict(
            norm_g=row(norm_g[l]), w_p=w_p, mu_p=mu_p,
            w0=w0[l].astype(f32), wlu=w_lora_up[l].astype(bf16), a0=a0[l].astype(f32), alu=a_lora_up[l].astype(bf16),
            k_k=row(k_k[l]), k_a=row(k_a[l]), r_k=row(r_k[l]), ln_g=row(ln_x_g[l]), ln_b=row(ln_x_b[l]),
            q_g=row(jnp.tile(q_norm_g[l], 2)), k_g=row(jnp.tile(k_norm_g[l], 2)), sink=row(sink[l]),
            wa=w_proj_a[l].astype(bf16), wb=w_proj_b[l].astype(bf16), wo=w_out[l].astype(bf16)))
    return layers


def kernel(x_prompt, x_sample, norm_g, w_in, shift_mu, w0, w_lora_up, a0, a_lora_up, k_k, k_a, r_k,
           ln_x_g, ln_x_b, q_norm_g, k_norm_g, sink, w_proj_a, w_proj_b, w_out):
    layers = _prep_layers(norm_g, w_in, shift_mu, w0, w_lora_up, a0, a_lora_up, k_k, k_a, r_k, ln_x_g, ln_x_b,
                          q_norm_g, k_norm_g, sink, w_proj_a, w_proj_b, w_out)
    return _trunk(x_prompt, layers), _trunk(x_sample, layers)
```

```python
import functools
import math

import jax
import jax.numpy as jnp
from jax import lax
from jax.experimental import pallas as pl
from jax.experimental.pallas import tpu as pltpu

D_MODEL = 1024
HEAD = 64
N_HEADS = D_MODEL // HEAD
R_LORA = 64
HKV = 4
GRP = N_HEADS // HKV
D_KV = HKV * HEAD
WINDOW = 128
BLOCK = 128
GN_EPS = 64e-5
NORM_EPS = 1e-6
ROPE_THETA = 10000.0
C_SHIFT = 3 * D_MODEL + 4 * R_LORA
N_PLANES = 9
CHUNK = 64
VMEM_LIMIT = 56 * 1024 * 1024

P_R, P_K, P_V, P_GA, P_Q, P_GB, P_MA, P_MB, P_X = range(9)

f32 = jnp.float32
bf16 = jnp.bfloat16


def _dot(a, b):
    return jnp.dot(a.astype(bf16), b.astype(bf16), preferred_element_type=f32)


def _dot_nt(a, b):
    return lax.dot_general(a.astype(bf16), b.astype(bf16), (((1,), (1,)), ((), ())),
                           preferred_element_type=f32)


def _dot_tn(a, b):
    return lax.dot_general(a.astype(bf16), b.astype(bf16), (((0,), (0,)), ((), ())),
                           preferred_element_type=f32)


def _dot_split(a_exact, b):
    b0 = b.astype(bf16)
    b1 = (b - b0.astype(f32)).astype(bf16)
    a = a_exact.astype(bf16)
    return jnp.dot(a, b0, preferred_element_type=f32) + jnp.dot(a, b1, preferred_element_type=f32)


def _sigmoid(x):
    return 1.0 / (1.0 + jnp.exp(-x))


def _in_proj_body(x_ref, g_ref, w_ref, o_ref, ox_ref, h_ref):
    j = pl.program_id(1)

    @pl.when(j == 0)
    def _():
        x = x_ref[...]
        ms = jnp.mean(x * x, axis=-1, keepdims=True)
        h_ref[...] = (x * lax.rsqrt(ms + NORM_EPS) * g_ref[...]).astype(bf16)

    @pl.when(j < P_X)
    def _():
        o_ref[0] = jnp.dot(h_ref[...], w_ref[...], preferred_element_type=f32).astype(bf16)

    @pl.when(j == P_X)
    def _():
        ox_ref[...] = jnp.dot(h_ref[...], w_ref[...], preferred_element_type=f32)


def _in_proj(x2d, norm_g, w_p, *, tm):
    m = x2d.shape[0]
    return pl.pallas_call(
        _in_proj_body,
        grid=(m // tm, N_PLANES),
        in_specs=[
            pl.BlockSpec((tm, D_MODEL), lambda i, j: (i, 0)),
            pl.BlockSpec((1, D_MODEL), lambda i, j: (0, 0)),
            pl.BlockSpec((D_MODEL, D_MODEL), lambda i, j: (0, j)),
        ],
        out_specs=[pl.BlockSpec((1, tm, D_MODEL), lambda i, j: (jnp.minimum(j, P_X - 1), i, 0)),
                   pl.BlockSpec((tm, D_MODEL), lambda i, j: (i, 0))],
        out_shape=[jax.ShapeDtypeStruct((N_PLANES - 1, m, D_MODEL), bf16),
                   jax.ShapeDtypeStruct((m, D_MODEL), f32)],
        scratch_shapes=[pltpu.VMEM((tm, D_MODEL), bf16)],
        compiler_params=pltpu.CompilerParams(
            dimension_semantics=("parallel", "arbitrary"), vmem_limit_bytes=VMEM_LIMIT),
        name="in_proj",
    )(x2d, norm_g, w_p)


def _prep_layer_weights(w_in, shift_mu):
    c = C_SHIFT
    d = D_MODEL
    cols = [
        (0, d), (d, 2 * d), (2 * d, 3 * d),
        (c, c + d),
        (c + d, c + 2 * d),
        (c + 2 * d + 2 * D_KV, c + 3 * d + 2 * D_KV),
        (c + 3 * d + 2 * D_KV, c + 4 * d + 2 * D_KV),
        (c + 4 * d + 2 * D_KV, c + 5 * d + 2 * D_KV),
    ]
    planes = [w_in[:, a:b] for a, b in cols]
    small = jnp.concatenate([w_in[:, 3 * d:c], w_in[:, c + 2 * d:c + 2 * d + 2 * D_KV],
                             jnp.zeros((d, d - 4 * R_LORA - 2 * D_KV), w_in.dtype)], axis=1)
    w_p = jnp.concatenate(planes + [small], axis=1).astype(bf16)
    zeros = jnp.zeros((d,), f32)
    mu_small = jnp.concatenate([shift_mu[3 * d:c], jnp.zeros((d - 4 * R_LORA,), f32)])
    mu_p = jnp.concatenate([shift_mu[0:d], shift_mu[d:2 * d], shift_mu[2 * d:3 * d]] + [zeros] * 5 + [mu_small])
    return w_p, mu_p.reshape(1, N_PLANES * d)


PAIR = 2 * HEAD
PRE, PRE_DONE, CHUNK_STAGE = "pre", "pre_done", "chunk"


def _token_shift(tt, seq):
    i = pl.program_id(0)
    first = (i * tt) % seq == 0
    last = ((i + 1) * tt) % seq == 0
    trow = lax.broadcasted_iota(jnp.int32, (tt, tt), 0)
    tcol = lax.broadcasted_iota(jnp.int32, (tt, tt), 1)
    nbr_mat = jnp.where(jnp.abs(trow - tcol) == 1, 0.5, 0.0)
    sub = lax.broadcasted_iota(jnp.int32, (8, 1), 0)

    def shift(main, pv, nx, mu):
        c = main.astype(f32)
        nbr = _dot(nbr_mat, main) if main.dtype == bf16 else _dot_split(nbr_mat, c)
        up = jnp.where(first, 0.0, pv[pv.shape[0] - 1:, :].astype(f32))
        dn = jnp.where(last, 0.0, nx[:1, :].astype(f32))
        head = nbr[:8] + jnp.where(sub == 0, 0.5 * up, 0.0)
        tail = nbr[tt - 8:] + jnp.where(sub == 7, 0.5 * dn, 0.0)
        nbr = jnp.concatenate([head, nbr[8:tt - 8], tail], axis=0)
        return c + mu * (nbr - c)

    return shift


def _rwkv_p_body(r_ref, r_pv, r_nx, k_ref, k_pv, k_nx, v_ref, v_pv, v_nx, mur_ref, muk_ref, muv_ref,
                 kk_ref, ka_ref, rk_ref,
                 y0_ref, bonus_ref, rp_ref, mn_ref,
                 at_s, rt_s, kpm_s, bpm_s, khm_s, bhm_s, atx_s, rtx_s, vx_s, vo_s, pl_s, *, tt, shift, zs, a_pre):
    nc = tt // CHUNK
    r = shift(r_ref[0], r_pv[0], r_nx[0], mur_ref[...])
    k = shift(k_ref[0], k_pv[0], k_nx[0], muk_ref[...])
    v = shift(v_ref[0], v_pv[0], v_nx[0], muv_ref[...])
    yield PRE

    lane = lax.broadcasted_iota(jnp.int32, (PAIR, PAIR), 0) // HEAD
    seg = jnp.where(lane == lax.broadcasted_iota(jnp.int32, (PAIR, PAIR), 1) // HEAD, 1.0, 0.0)
    lo = lax.broadcasted_iota(jnp.int32, (tt, PAIR), 1) < HEAD
    swap = lambda t: pltpu.roll(t, HEAD, axis=1)

    def head_lo(t, h):
        return jnp.where(lo, t if h == 0 else swap(t), 0.0)

    def head_hi(t, h):
        return jnp.where(lo, 0.0, swap(t) if h == 0 else t)

    def head_own(t, h):
        return jnp.where(lo, t, 0.0) if h == 0 else jnp.where(lo, 0.0, t)

    for h in range(2):
        vx_s[h] = head_hi(v, h).astype(bf16)
    vo_s[...] = head_own(v, 0).astype(bf16)

    kkv = k * kk_ref[...]
    kk = kkv / jnp.maximum(jnp.sqrt(_segsum(kkv * kkv, seg)), 1e-12)
    yield PRE

    rows = lax.broadcasted_iota(jnp.int32, (tt, tt), 0)
    cols = lax.broadcasted_iota(jnp.int32, (tt, tt), 1)
    same = (rows // CHUNK) == (cols // CHUNK)

    kmod_sum = jnp.zeros_like(k)
    for d in range(2):
        if d == 0:
            tri_bd = jnp.where(same & (cols <= rows), 1.0, 0.0)
        else:
            tri_bd = jnp.where(same & (cols >= rows), 1.0, 0.0)
        lw = -math.exp(-0.5) * _sigmoid(zs[d])
        a = _sigmoid(a_pre[d])
        yield PRE
        kmod = k * (1.0 + (a - 1.0) * ka_ref[...])
        kmod_sum = kmod_sum + kmod
        b = -(kk * a)
        cum = _dot_split(tri_bd, lw)
        edge = CHUNK - 1 if d == 0 else 0
        tot = jnp.concatenate([jnp.broadcast_to(cum[c * CHUNK + edge:c * CHUNK + edge + 1, :], (CHUNK, PAIR))
                               for c in range(nc)], axis=0)
        yield PRE
        p_inv = jnp.exp(-cum)
        p_end = jnp.exp(tot - cum)
        at = kk * jnp.exp(cum - lw)
        rt = r * jnp.exp(cum)
        kh = kmod * p_inv
        bh = b * p_inv
        p_tot = jnp.exp(tot)
        at_s[d] = at.astype(bf16)
        rt_s[d] = rt.astype(bf16)
        pl_s[d] = p_tot
        yield PRE
        kp = kmod * p_end
        bp = b * p_end
        for h in range(2):
            khm_s[d, h] = head_own(kh, h).astype(bf16)
            bhm_s[d, h] = head_own(bh, h).astype(bf16)
            kpm_s[d, h] = head_own(kp, h).astype(bf16)
            bpm_s[d, h] = head_own(bp, h).astype(bf16)
            atx_s[d, h] = head_lo(at, h).astype(bf16)
            rtx_s[d, h] = head_lo(rt, h)
        yield PRE

    bonus = _segsum(r * kmod_sum * rk_ref[...], seg) * v
    yield PRE_DONE

    row = lax.broadcasted_iota(jnp.int32, (CHUNK, PAIR), 0)
    col = lax.broadcasted_iota(jnp.int32, (CHUNK, PAIR), 1)
    col_t = col % CHUNK
    lo_c = col < HEAD
    strict = (col_t < row, col_t > row)
    incl = (col_t <= row, col_t >= row)
    pairs = [(d, c) for d in range(2) for c in range(nc)]
    inst = [(d, c, h) for d, c in pairs for h in range(2)]
    cs = lambda c: slice(c * CHUNK, (c + 1) * CHUNK)
    swap_c = lambda t: pltpu.roll(t, HEAD, axis=1)
    zeros = jnp.zeros((CHUNK, PAIR), bf16)

    top, bot = [], []
    for d, c in pairs:
        lhs = jnp.concatenate([at_s[d, cs(c)], rt_s[d, cs(c)]], axis=0)
        rhs = jnp.concatenate([bhm_s[d, 0, cs(c)], khm_s[d, 0, cs(c)],
                               bhm_s[d, 1, cs(c)], khm_s[d, 1, cs(c)]], axis=0)
        sc = lax.dot_general(lhs, rhs, (((1,), (1,)), ((), ())), preferred_element_type=f32)
        for h in range(2):
            top.append(jnp.where(strict[d], sc[:CHUNK, h * PAIR:(h + 1) * PAIR], 0.0))
            bot.append(jnp.where(incl[d], sc[CHUNK:, h * PAIR:(h + 1) * PAIR], 0.0))
    yield CHUNK_STAGE
    xs, aps = [], []
    for i, (d, c, h) in enumerate(inst):
        vx = vx_s[h, cs(c)]
        a_ak = jnp.where(lo_c, 0.0, top[i]).astype(bf16)
        akv = jnp.dot(a_ak, jnp.concatenate([vx, vx], axis=0), preferred_element_type=f32)
        xs.append(atx_s[d, h, cs(c)].astype(f32) + akv)
        aps.append(top[i][:, :CHUNK])
    n_dbl = CHUNK.bit_length() - 1
    for it in range(n_dbl):
        yield CHUNK_STAGE
        last = it + 1 == n_dbl
        res = [_dot(ap, x if last else jnp.concatenate([x, ap], axis=1)) for ap, x in zip(aps, xs)]
        xs = [x + rs[:, :PAIR] for x, rs in zip(xs, res)]
        if not last:
            aps = [rs[:, PAIR:] for rs in res]
    yield CHUNK_STAGE
    rys = [_dot(bot[i], jnp.concatenate([xs[i].astype(bf16), vx_s[h, cs(c)]], axis=0)) + rtx_s[d, h, cs(c)]
           for i, (d, c, h) in enumerate(inst)]
    yield CHUNK_STAGE
    mns = []
    for j, (d, c) in enumerate(pairs):
        x0, x1 = xs[2 * j], xs[2 * j + 1]
        r0 = jnp.concatenate([jnp.where(lo_c, x0, 0.0), jnp.where(lo_c, swap_c(x0), 0.0)], axis=1).astype(bf16)
        r1 = jnp.concatenate([jnp.where(lo_c, 0.0, swap_c(x1)), jnp.where(lo_c, 0.0, x1)], axis=1).astype(bf16)
        r2 = jnp.concatenate([zeros, vo_s[cs(c)]], axis=1)
        r3 = jnp.concatenate([zeros, vx_s[1, cs(c)]], axis=1)
        rhs = jnp.concatenate([r0, r1, r2, r3], axis=0)
        lhs_t = jnp.concatenate([bpm_s[d, 0, cs(c)], bpm_s[d, 1, cs(c)],
                                 kpm_s[d, 0, cs(c)], kpm_s[d, 1, cs(c)]], axis=0)
        mns.append(lax.dot_general(lhs_t, rhs, (((0,), (0,)), ((), ())), preferred_element_type=f32))
    yield CHUNK_STAGE
    rowp = lax.broadcasted_iota(jnp.int32, (PAIR, 2 * PAIR), 0)
    colp = lax.broadcasted_iota(jnp.int32, (PAIR, 2 * PAIR), 1)
    for j, (d, c) in enumerate(pairs):
        ry0, ry1 = rys[2 * j], rys[2 * j + 1]
        rp_ref[d, cs(c), :] = jnp.where(lo_c, ry0, swap_c(ry1)).astype(bf16)
        y0 = jnp.where(lo_c, swap_c(ry0), ry1)
        if d == 0:
            y0_ref[cs(c), :] = y0
        else:
            y0_ref[cs(c), :] += y0
        p_row = pl_s[d, c * CHUNK:c * CHUNK + 1, :]
        diag = jnp.where(rowp == colp, jnp.concatenate([p_row, p_row], axis=1), 0.0)
        mn_ref[d, c, 0] = (mns[j] + diag).astype(bf16)
    bonus_ref[...] = bonus


N_IN_P = 23
N_OUT_P = 4


def _rwkv_p_multi(*refs, tt, seq, pp):
    ins, outs, scr = refs[:N_IN_P], refs[N_IN_P:N_IN_P + N_OUT_P], refs[N_IN_P + N_OUT_P:]
    x_ref, x_pv, x_nx, mux_ref = ins[9], ins[10], ins[11], ins[15]
    w0_ref, wlu_ref, a0_ref, alu_ref = ins[16:20]
    shift = _token_shift(tt, seq)
    xs = shift(x_ref[...], x_pv[...], x_nx[...], mux_ref[...])
    zs = [w0_ref[d:d + 1, :] + _dot(jnp.tanh(xs[:, d * R_LORA:(d + 1) * R_LORA]), wlu_ref[d]) for d in range(2)]
    a_pre = [a0_ref[d:d + 1, :] + _dot(xs[:, (2 + d) * R_LORA:(3 + d) * R_LORA], alu_ref[d]) for d in range(2)]
    stages = []
    for q in range(pp):
        lanes = pl.ds(q * PAIR, PAIR)
        lsl = slice(q * PAIR, (q + 1) * PAIR)
        view = lambda ref, lanes=lanes: ref.at[..., lanes]
        planes = [view(ref) for ref in ins[0:9]]
        gains = [view(ref) for ref in ins[12:15]]
        params = [view(ref) for ref in ins[20:23]]
        y0_ref, bonus_ref, rp_ref, mn_ref = outs
        stages.append(_rwkv_p_body(*planes, *gains, *params,
                                   view(y0_ref), view(bonus_ref), view(rp_ref), mn_ref.at[:, :, pl.ds(q, 1)],
                                   *[s.at[q] for s in scr], tt=tt, shift=shift,
                                   zs=[z[:, lsl] for z in zs], a_pre=[a[:, lsl] for a in a_pre]))
    def run_pre(stage):
        while next(stage) != PRE_DONE:
            pass

    run_pre(stages[0])
    for q in range(pp):
        chunk, pre = stages[q], (stages[q + 1] if q + 1 < pp else None)
        chunk_live, pre_live = True, pre is not None
        while chunk_live or pre_live:
            if chunk_live:
                chunk_live = next(chunk, None) is not None
            if pre_live:
                pre_live = next(pre) != PRE_DONE


def _segsum(x, seg, terms=2):
    x0 = x.astype(bf16)
    m = seg.astype(bf16)
    s = jnp.dot(x0, m, preferred_element_type=f32)
    if terms == 1:
        return s
    x1 = (x - x0.astype(f32)).astype(bf16)
    return s + jnp.dot(x1, m, preferred_element_type=f32)


def _rwkv_p(p8, px, mu_p, w0, wlu, a0, alu, k_k, k_a, r_k, *, tt, seq, pp):
    m = px.shape[0]
    nt = m // tt
    npair = D_MODEL // PAIR
    cpt = tt // CHUNK
    wl = pp * PAIR
    hb, hx = 16, 8

    def tiles(plane):
        return [pl.BlockSpec((1, tt, wl), lambda i, j: (plane, i, j)),
                pl.BlockSpec((1, hb, wl), lambda i, j: (plane, jnp.maximum(i * (tt // hb) - 1, 0), j)),
                pl.BlockSpec((1, hb, wl), lambda i, j: (plane, jnp.minimum((i + 1) * (tt // hb), m // hb - 1), j))]

    vec = pl.BlockSpec((1, wl), lambda i, j: (0, j))
    mu = lambda plane: pl.BlockSpec((1, wl), lambda i, j: (0, plane * (npair // pp) + j))
    wx = 4 * R_LORA
    return pl.pallas_call(
        functools.partial(_rwkv_p_multi, tt=tt, seq=seq, pp=pp),
        grid=(nt, npair // pp),
        in_specs=tiles(P_R) + tiles(P_K) + tiles(P_V) + [
            pl.BlockSpec((tt, wx), lambda i, j: (i, 0)),
            pl.BlockSpec((hx, wx), lambda i, j: (jnp.maximum(i * (tt // hx) - 1, 0), 0)),
            pl.BlockSpec((hx, wx), lambda i, j: (jnp.minimum((i + 1) * (tt // hx), m // hx - 1), 0)),
            mu(P_R), mu(P_K), mu(P_V),
            pl.BlockSpec((1, wx), lambda i, j: (0, P_X * D_MODEL // wx)),
            pl.BlockSpec((2, wl), lambda i, j: (0, j)),
            pl.BlockSpec((2, R_LORA, wl), lambda i, j: (0, 0, j)),
            pl.BlockSpec((2, wl), lambda i, j: (0, j)),
            pl.BlockSpec((2, R_LORA, wl), lambda i, j: (0, 0, j)),
            vec, vec, vec,
        ],
        out_specs=[
            pl.BlockSpec((tt, wl), lambda i, j: (i, j)),
            pl.BlockSpec((tt, wl), lambda i, j: (i, j)),
            pl.BlockSpec((2, tt, wl), lambda i, j: (0, i, j)),
            pl.BlockSpec((2, cpt, pp, PAIR, 2 * PAIR), lambda i, j: (0, i, j, 0, 0)),
        ],
        out_shape=[
            jax.ShapeDtypeStruct((m, D_MODEL), f32),
            jax.ShapeDtypeStruct((m, D_MODEL), f32),
            jax.ShapeDtypeStruct((2, m, D_MODEL), bf16),
            jax.ShapeDtypeStruct((2, m // CHUNK, npair, PAIR, 2 * PAIR), bf16),
        ],
        scratch_shapes=(
            [pltpu.VMEM((pp, 2, tt, PAIR), bf16) for _ in range(2)]
            + [pltpu.VMEM((pp, 2, 2, tt, PAIR), bf16) for _ in range(5)]
            + [pltpu.VMEM((pp, 2, 2, tt, PAIR), f32),
               pltpu.VMEM((pp, 2, tt, PAIR), bf16),
               pltpu.VMEM((pp, tt, PAIR), bf16),
               pltpu.VMEM((pp, 2, tt, PAIR), f32)]),
        compiler_params=pltpu.CompilerParams(
            dimension_semantics=("parallel", "parallel"), vmem_limit_bytes=VMEM_LIMIT),
        name="rwkv_p",
    )(p8, p8, p8, p8, p8, p8, p8, p8, p8, px, px, px, mu_p, mu_p, mu_p, mu_p, w0, wlu, a0, alu, k_k, k_a, r_k)


def _rwkv_s_body(mnf_ref, mnb_ref, sf_ref, sb_ref, st_ref, *, grp):
    @pl.when(pl.program_id(1) == 0)
    def _():
        st_ref[...] = jnp.zeros_like(st_ref)

    def step(d, mn_ref, s_out_ref, cc):
        s = st_ref[d].astype(bf16)
        s_out_ref[cc] = s
        m_t = mn_ref[0, cc, :, :, 0:PAIR]
        n_t = mn_ref[0, cc, :, :, PAIR:].astype(f32)
        st_ref[d] = jnp.einsum('pij,pjk->pik', m_t, s, preferred_element_type=f32) + n_t

    for cc in range(grp):
        step(0, mnf_ref, sf_ref, cc)
        step(1, mnb_ref, sb_ref, grp - 1 - cc)


def _rwkv_s(mn, *, n_seq, grp):
    nchunk, npair = mn.shape[1], mn.shape[2]
    ng = nchunk // n_seq // grp
    blk_in = (1, grp, npair, PAIR, 2 * PAIR)
    blk_out = (grp, npair, PAIR, PAIR)
    out = jax.ShapeDtypeStruct((nchunk, npair, PAIR, PAIR), bf16)
    return pl.pallas_call(
        functools.partial(_rwkv_s_body, grp=grp),
        grid=(n_seq, ng),
        in_specs=[pl.BlockSpec(blk_in, lambda b, g: (0, b * ng + g, 0, 0, 0)),
                  pl.BlockSpec(blk_in, lambda b, g: (1, b * ng + ng - 1 - g, 0, 0, 0))],
        out_specs=[pl.BlockSpec(blk_out, lambda b, g: (b * ng + g, 0, 0, 0)),
                   pl.BlockSpec(blk_out, lambda b, g: (b * ng + ng - 1 - g, 0, 0, 0))],
        out_shape=[out, out],
        scratch_shapes=[pltpu.VMEM((2, npair, PAIR, PAIR), f32)],
        compiler_params=pltpu.CompilerParams(
            dimension_semantics=("parallel", "arbitrary"), vmem_limit_bytes=VMEM_LIMIT),
        name="rwkv_s",
    )(mn, mn)


def _rwkv_f_body(y0_ref, bonus_ref, rp_ref, sf_ref, sb_ref, gate_ref, lng_ref, lnb_ref, o_ref, *, tt):
    npair = D_MODEL // PAIR
    nc = tt // CHUNK
    cs = lambda c: slice(c * CHUNK, (c + 1) * CHUNK)
    ps = lambda p: slice(p * PAIR, (p + 1) * PAIR)
    yf = [[jnp.dot(rp_ref[0, cs(c), ps(p)], sf_ref[c, p], preferred_element_type=f32) for c in range(nc)]
          for p in range(npair)]
    yb = [[jnp.dot(rp_ref[1, cs(c), ps(p)], sb_ref[c, p], preferred_element_type=f32) for c in range(nc)]
          for p in range(npair)]
    lane = lax.broadcasted_iota(jnp.int32, (PAIR, PAIR), 0) // HEAD
    seg = jnp.where(lane == lax.broadcasted_iota(jnp.int32, (PAIR, PAIR), 1) // HEAD, 1.0, 0.0)
    ys = [y0_ref[:, ps(p)] + jnp.concatenate(yf[p], axis=0) + jnp.concatenate(yb[p], axis=0) for p in range(npair)]
    mus = [_segsum(y, seg) * (1.0 / HEAD) for y in ys]
    dvs = [y - mu for y, mu in zip(ys, mus)]
    vrs = [_segsum(dv * dv, seg) * (1.0 / HEAD) for dv in dvs]
    for p in range(npair):
        yn = dvs[p] * lax.rsqrt(vrs[p] + GN_EPS) * lng_ref[:, ps(p)] + lnb_ref[:, ps(p)] + bonus_ref[:, ps(p)]
        g = gate_ref[0, :, ps(p)].astype(f32)
        o_ref[:, ps(p)] = (yn * (g * _sigmoid(g))).astype(bf16)


def _rwkv_f(y0, bonus, rp, sf, sb, p8, ln_g, ln_b, *, tt):
    m = y0.shape[0]
    cpt = tt // CHUNK
    npair = D_MODEL // PAIR
    tile = pl.BlockSpec((tt, D_MODEL), lambda i: (i, 0))
    vec = pl.BlockSpec((1, D_MODEL), lambda i: (0, 0))
    state = pl.BlockSpec((cpt, npair, PAIR, PAIR), lambda i: (i, 0, 0, 0))
    return pl.pallas_call(
        functools.partial(_rwkv_f_body, tt=tt),
        grid=(m // tt,),
        in_specs=[
            tile, tile,
            pl.BlockSpec((2, tt, D_MODEL), lambda i: (0, i, 0)),
            state, state,
            pl.BlockSpec((1, tt, D_MODEL), lambda i: (P_GA, i, 0)),
            vec, vec,
        ],
        out_specs=tile,
        out_shape=jax.ShapeDtypeStruct((m, D_MODEL), bf16),
        compiler_params=pltpu.CompilerParams(dimension_semantics=("parallel",), vmem_limit_bytes=VMEM_LIMIT),
        name="rwkv_f",
    )(y0, bonus, rp, sf, sb, p8, ln_g, ln_b)


def _attn_body(q_ref, kp_ref, kc_ref, kn_ref, vp_ref, vc_ref, vn_ref, tp_ref, tc_ref, tn_ref, gate_ref,
               qg_ref, kg_ref, sink_ref, o_ref, *, seq, qb):
    step_rows = qb * BLOCK
    start = (pl.program_id(0) % (seq // step_rows)) * step_rows
    ctx = BLOCK + 2 * WINDOW
    qoff = lax.broadcasted_iota(jnp.int32, (BLOCK, ctx), 0)
    koff = lax.broadcasted_iota(jnp.int32, (BLOCK, ctx), 1) - WINDOW
    valid = []
    for b in range(qb):
        kpos = start + b * BLOCK + koff
        valid.append((jnp.abs(qoff - koff) <= WINDOW) & (kpos >= 0) & (kpos < seq))

    lane_r = lax.broadcasted_iota(jnp.int32, (PAIR, PAIR), 0) // HEAD
    seg = jnp.where(lane_r == lax.broadcasted_iota(jnp.int32, (PAIR, PAIR), 1) // HEAD, 1.0, 0.0)
    swap = lambda t: pltpu.roll(t, HEAD, axis=1)

    def norm_rope(x, gain, tab):
        lane = lax.broadcasted_iota(jnp.int32, x.shape, 1)
        xn = x * lax.rsqrt(_segsum(x * x, seg, terms=1) * (1.0 / HEAD) + NORM_EPS) * gain
        rot = jnp.where(lane % HEAD < HEAD // 2, pltpu.roll(xn, PAIR - HEAD // 2, axis=1),
                        pltpu.roll(xn, HEAD // 2, axis=1))
        return xn * tab[:, :PAIR] + rot * tab[:, PAIR:]

    def place(t, e, want_lo):
        lo = lax.broadcasted_iota(jnp.int32, t.shape, 1) < HEAD
        src = t if (e == 0) == want_lo else swap(t)
        return (jnp.where(lo, src, 0.0) if want_lo else jnp.where(lo, 0.0, src)).astype(bf16)

    tab_k = jnp.concatenate([tp_ref[...], tc_ref[...], tn_ref[...]], axis=0)
    tab_q = tc_ref[...]
    k_lo, k_hi, v_lo, v_hi = [], [], [], []
    for kp in range(D_KV // PAIR):
        ps_ = slice(kp * PAIR, (kp + 1) * PAIR)
        kx = jnp.concatenate([kp_ref[:, ps_], kc_ref[:, ps_], kn_ref[:, ps_]], axis=0)
        vx = jnp.concatenate([vp_ref[:, ps_], vc_ref[:, ps_], vn_ref[:, ps_]], axis=0)
        kr = norm_rope(kx, kg_ref[...], tab_k)
        for e in range(2):
            k_lo.append(place(kr, e, True))
            k_hi.append(place(kr, e, False))
            v_lo.append(place(vx, e, True))
            v_hi.append(place(vx, e, False))
    npair = D_MODEL // PAIR
    psl = lambda p: slice(p * PAIR, (p + 1) * PAIR)
    log2e = math.log2(math.e)
    qrs = [(norm_rope(q_ref[0, :, psl(p)].astype(f32), qg_ref[...], tab_q) * (HEAD ** -0.5 * log2e)).astype(bf16)
           for p in range(npair)]
    inst = [(b, hq) for b in range(qb) for hq in range(N_HEADS)]
    rows = lambda b: slice(b * BLOCK, (b + 1) * BLOCK)
    keys = lambda b: slice(b * BLOCK, b * BLOCK + ctx)
    def masked(s, ok):
        return jnp.concatenate([jnp.where(ok[:, :WINDOW], s[:, :WINDOW], -1e30), s[:, WINDOW:WINDOW + BLOCK],
                                jnp.where(ok[:, WINDOW + BLOCK:], s[:, WINDOW + BLOCK:], -1e30)], axis=1)

    ss = [masked(lax.dot_general(qrs[hq // 2][rows(b)], (k_lo if hq % 2 == 0 else k_hi)[hq // GRP][keys(b)],
                                 (((1,), (1,)), ((), ())), preferred_element_type=f32), valid[b])
          for b, hq in inst]
    sinks = [sink_ref[0:1, hq:hq + 1] * log2e for b, hq in inst]
    mxs = [jnp.maximum(jnp.max(s, axis=-1, keepdims=True), sk) for s, sk in zip(ss, sinks)]
    ps = [jnp.exp2(s - mx) for s, mx in zip(ss, mxs)]
    dens = [jnp.sum(p, axis=-1, keepdims=True) + jnp.exp2(sk - mx) for p, sk, mx in zip(ps, sinks, mxs)]
    os_ = [jnp.dot(p.astype(bf16), (v_lo if hq % 2 == 0 else v_hi)[hq // GRP][keys(b)], preferred_element_type=f32) / den
           for (b, hq), p, den in zip(inst, ps, dens)]
    for b in range(qb):
        for p in range(npair):
            gt = gate_ref[0, rows(b), psl(p)].astype(f32)
            i0 = b * N_HEADS + 2 * p
            o_ref[rows(b), psl(p)] = ((os_[i0] + os_[i0 + 1]) * (gt * _sigmoid(gt))).astype(bf16)


def _attn(p8, px, rope_tab, q_g, k_g, sink, *, seq, qb):
    m = px.shape[0]
    nblk = seq // BLOCK
    rows = qb * BLOCK

    def nbr(delta):
        def f(i):
            t = (i * qb) % nblk
            return i * qb - t + jnp.clip(t + delta, 0, nblk - 1)
        return f

    def kv(col):
        return [pl.BlockSpec((BLOCK, D_KV), lambda i, f=nbr(-1): (f(i), col)),
                pl.BlockSpec((rows, D_KV), lambda i: (i, col)),
                pl.BlockSpec((BLOCK, D_KV), lambda i, f=nbr(qb): (f(i), col))]

    tabs = [pl.BlockSpec((BLOCK, 2 * PAIR), lambda i, f=nbr(-1): (f(i) % nblk, 0)),
            pl.BlockSpec((rows, 2 * PAIR), lambda i: (i % (nblk // qb), 0)),
            pl.BlockSpec((BLOCK, 2 * PAIR), lambda i, f=nbr(qb): (f(i) % nblk, 0))]

    return pl.pallas_call(
        functools.partial(_attn_body, seq=seq, qb=qb),
        grid=(m // rows,),
        in_specs=[pl.BlockSpec((1, rows, D_MODEL), lambda i: (P_Q, i, 0))] + kv(1) + kv(2) + tabs + [
            pl.BlockSpec((1, rows, D_MODEL), lambda i: (P_GB, i, 0)),
            pl.BlockSpec((1, PAIR), lambda i: (0, 0)),
            pl.BlockSpec((1, PAIR), lambda i: (0, 0)),
            pl.BlockSpec((1, N_HEADS), lambda i: (0, 0)),
        ],
        out_specs=pl.BlockSpec((rows, D_MODEL), lambda i: (i, 0)),
        out_shape=jax.ShapeDtypeStruct((m, D_MODEL), bf16),
        compiler_params=pltpu.CompilerParams(dimension_semantics=("parallel",), vmem_limit_bytes=VMEM_LIMIT),
        name="attn",
    )(p8, px, px, px, px, px, px, rope_tab, rope_tab, rope_tab, p8, q_g, k_g, sink)


def _rope_table(seq):
    inv = 1.0 / (ROPE_THETA ** (jnp.arange(0, HEAD, 2, dtype=f32) / HEAD))
    ang = jnp.arange(seq, dtype=f32)[:, None] * inv[None, :]
    cos, sin = jnp.cos(ang), jnp.sin(ang)
    return jnp.concatenate([cos, cos, cos, cos, -sin, sin, -sin, sin], axis=1)


def _out_proj_body(x_ref, ga_ref, gb_ref, ma_ref, mb_ref, wa_ref, wb_ref, wo_ref, o_ref):
    ya = jnp.dot(ga_ref[...], wa_ref[...], preferred_element_type=f32)
    yb = jnp.dot(gb_ref[...], wb_ref[...], preferred_element_type=f32)
    mixed = _sigmoid(ma_ref[0].astype(f32)) * ya + _sigmoid(mb_ref[0].astype(f32)) * yb
    o_ref[...] = x_ref[...] + jnp.dot(mixed.astype(bf16), wo_ref[...], preferred_element_type=f32)


def _out_proj(x2d, ga, gb, p8, wa, wb, wo, *, tm):
    m = x2d.shape[0]
    row = pl.BlockSpec((tm, D_MODEL), lambda i: (i, 0))
    wspec = pl.BlockSpec((D_MODEL, D_MODEL), lambda i: (0, 0))
    return pl.pallas_call(
        _out_proj_body,
        grid=(m // tm,),
        in_specs=[row, row, row,
                  pl.BlockSpec((1, tm, D_MODEL), lambda i: (P_MA, i, 0)),
                  pl.BlockSpec((1, tm, D_MODEL), lambda i: (P_MB, i, 0)),
                  wspec, wspec, wspec],
        out_specs=row,
        out_shape=jax.ShapeDtypeStruct((m, D_MODEL), f32),
        compiler_params=pltpu.CompilerParams(dimension_semantics=("parallel",), vmem_limit_bytes=VMEM_LIMIT),
        name="out_proj",
    )(x2d, ga, gb, p8, p8, wa, wb, wo)


def _layer(x2d, lw, rope_tab, *, n_seq, seq):
    tm = min(1024, seq)
    tt = min(256, seq)
    grp = min(8, seq // CHUNK)
    p8, px = _in_proj(x2d, lw["norm_g"], lw["w_p"], tm=tm)
    y0, bonus, rp, mn = _rwkv_p(p8, px, lw["mu_p"], lw["w0"], lw["wlu"], lw["a0"], lw["alu"], lw["k_k"], lw["k_a"],
                                lw["r_k"], tt=tt, seq=seq, pp=4)
    sf, sb = _rwkv_s(mn, n_seq=n_seq, grp=grp)
    ga = _rwkv_f(y0, bonus, rp, sf, sb, p8, lw["ln_g"], lw["ln_b"], tt=tt)
    gb = _attn(p8, px, rope_tab, lw["q_g"], lw["k_g"], lw["sink"], seq=seq, qb=2)
    return _out_proj(x2d, ga, gb, p8, lw["wa"], lw["wb"], lw["wo"], tm=tm)


def _trunk(x, layers):
    n_seq, seq, _ = x.shape
    rope_tab = _rope_table(seq)
    x2d = x.reshape(n_seq * seq, D_MODEL)
    for lw in layers:
        x2d = _layer(x2d, lw, rope_tab, n_seq=n_seq, seq=seq)
    return x2d.reshape(n_seq, seq, D_MODEL)


def _prep_layers(norm_g, w_in, shift_mu, w0, w_lora_up, a0, a_lora_up, k_k, k_a, r_k, ln_x_g, ln_x_b,
                 q_norm_g, k_norm_g, sink, w_proj_a, w_proj_b, w_out):
    layers = []
    row = lambda v: v.reshape(1, -1).astype(f32)
    for l in range(norm_g.shape[0]):
        w_p, mu_p = _prep_layer_weights(w_in[l], shift_mu[l])
        layers.append(dict(
            norm_g=row(norm_g[l]), w_p=w_p, mu_p=mu_p,
            w0=w0[l].astype(f32), wlu=w_lora_up[l].astype(bf16), a0=a0[l].astype(f32), alu=a_lora_up[l].astype(bf16),
            k_k=row(k_k[l]), k_a=row(k_a[l]), r_k=row(r_k[l]), ln_g=row(ln_x_g[l]), ln_b=row(ln_x_b[l]),
            q_g=row(jnp.tile(q_norm_g[l], 2)), k_g=row(jnp.tile(k_norm_g[l], 2)), sink=row(sink[l]),
            wa=w_proj_a[l].astype(bf16), wb=w_proj_b[l].astype(bf16), wo=w_out[l].astype(bf16)))
    return layers


def kernel(x_prompt, x_sample, norm_g, w_in, shift_mu, w0, w_lora_up, a0, a_lora_up, k_k, k_a, r_k,
           ln_x_g, ln_x_b, q_norm_g, k_norm_g, sink, w_proj_a, w_proj_b, w_out):
    layers = _prep_layers(norm_g, w_in, shift_mu, w0, w_lora_up, a0, a_lora_up, k_k, k_a, r_k, ln_x_g, ln_x_b,
                          q_norm_g, k_norm_g, sink, w_proj_a, w_proj_b, w_out)
    return _trunk(x_prompt, layers), _trunk(x_sample, layers)
```

```python
import functools
import math

import jax
import jax.numpy as jnp
from jax import lax
from jax.experimental import pallas as pl
from jax.experimental.pallas import tpu as pltpu

D_MODEL = 1024
HEAD = 64
N_HEADS = D_MODEL // HEAD
R_LORA = 64
HKV = 4
GRP = N_HEADS // HKV
D_KV = HKV * HEAD
WINDOW = 128
BLOCK = 128
GN_EPS = 64e-5
NORM_EPS = 1e-6
ROPE_THETA = 10000.0
C_SHIFT = 3 * D_MODEL + 4 * R_LORA
N_PLANES = 9
CHUNK = 64
VMEM_LIMIT = 56 * 1024 * 1024

P_R, P_K, P_V, P_GA, P_Q, P_GB, P_MA, P_MB, P_X = range(9)

f32 = jnp.float32
bf16 = jnp.bfloat16


def _dot(a, b):
    return jnp.dot(a.astype(bf16), b.astype(bf16), preferred_element_type=f32)


def _dot_nt(a, b):
    return lax.dot_general(a.astype(bf16), b.astype(bf16), (((1,), (1,)), ((), ())),
                           preferred_element_type=f32)


def _dot_tn(a, b):
    return lax.dot_general(a.astype(bf16), b.astype(bf16), (((0,), (0,)), ((), ())),
                           preferred_element_type=f32)


def _dot_split(a_exact, b):
    b0 = b.astype(bf16)
    b1 = (b - b0.astype(f32)).astype(bf16)
    a = a_exact.astype(bf16)
    return jnp.dot(a, b0, preferred_element_type=f32) + jnp.dot(a, b1, preferred_element_type=f32)


def _sigmoid(x):
    return 1.0 / (1.0 + jnp.exp(-x))


def _in_proj_body(x_ref, g_ref, w_ref, o_ref, ox_ref, h_ref):
    j = pl.program_id(1)

    @pl.when(j == 0)
    def _():
        x = x_ref[...]
        ms = jnp.mean(x * x, axis=-1, keepdims=True)
        h_ref[...] = (x * lax.rsqrt(ms + NORM_EPS) * g_ref[...]).astype(bf16)

    @pl.when(j < P_X)
    def _():
        o_ref[0] = jnp.dot(h_ref[...], w_ref[...], preferred_element_type=f32).astype(bf16)

    @pl.when(j == P_X)
    def _():
        ox_ref[...] = jnp.dot(h_ref[...], w_ref[...], preferred_element_type=f32)


def _in_proj(x2d, norm_g, w_p, *, tm):
    m = x2d.shape[0]
    return pl.pallas_call(
        _in_proj_body,
        grid=(m // tm, N_PLANES),
        in_specs=[
            pl.BlockSpec((tm, D_MODEL), lambda i, j: (i, 0)),
            pl.BlockSpec((1, D_MODEL), lambda i, j: (0, 0)),
            pl.BlockSpec((D_MODEL, D_MODEL), lambda i, j: (0, j)),
        ],
        out_specs=[pl.BlockSpec((1, tm, D_MODEL), lambda i, j: (jnp.minimum(j, P_X - 1), i, 0)),
                   pl.BlockSpec((tm, D_MODEL), lambda i, j: (i, 0))],
        out_shape=[jax.ShapeDtypeStruct((N_PLANES - 1, m, D_MODEL), bf16),
                   jax.ShapeDtypeStruct((m, D_MODEL), f32)],
        scratch_shapes=[pltpu.VMEM((tm, D_MODEL), bf16)],
        compiler_params=pltpu.CompilerParams(
            dimension_semantics=("parallel", "arbitrary"), vmem_limit_bytes=VMEM_LIMIT),
        name="in_proj",
    )(x2d, norm_g, w_p)


def _prep_layer_weights(w_in, shift_mu):
    c = C_SHIFT
    d = D_MODEL
    cols = [
        (0, d), (d, 2 * d), (2 * d, 3 * d),
        (c, c + d),
        (c + d, c + 2 * d),
        (c + 2 * d + 2 * D_KV, c + 3 * d + 2 * D_KV),
        (c + 3 * d + 2 * D_KV, c + 4 * d + 2 * D_KV),
        (c + 4 * d + 2 * D_KV, c + 5 * d + 2 * D_KV),
    ]
    planes = [w_in[:, a:b] for a, b in cols]
    small = jnp.concatenate([w_in[:, 3 * d:c], w_in[:, c + 2 * d:c + 2 * d + 2 * D_KV],
                             jnp.zeros((d, d - 4 * R_LORA - 2 * D_KV), w_in.dtype)], axis=1)
    w_p = jnp.concatenate(planes + [small], axis=1).astype(bf16)
    zeros = jnp.zeros((d,), f32)
    mu_small = jnp.concatenate([shift_mu[3 * d:c], jnp.zeros((d - 4 * R_LORA,), f32)])
    mu_p = jnp.concatenate([shift_mu[0:d], shift_mu[d:2 * d], shift_mu[2 * d:3 * d]] + [zeros] * 5 + [mu_small])
    return w_p, mu_p.reshape(1, N_PLANES * d)


PAIR = 2 * HEAD
PRE, PRE_DONE, CHUNK_STAGE = "pre", "pre_done", "chunk"


def _token_shift(tt, seq):
    i = pl.program_id(0)
    first = (i * tt) % seq == 0
    last = ((i + 1) * tt) % seq == 0
    trow = lax.broadcasted_iota(jnp.int32, (tt, tt), 0)
    tcol = lax.broadcasted_iota(jnp.int32, (tt, tt), 1)
    nbr_mat = jnp.where(jnp.abs(trow - tcol) == 1, 0.5, 0.0)
    sub = lax.broadcasted_iota(jnp.int32, (8, 1), 0)

    def shift(main, pv, nx, mu):
        c = main.astype(f32)
        nbr = _dot(nbr_mat, main) if main.dtype == bf16 else _dot_split(nbr_mat, c)
        up = jnp.where(first, 0.0, pv[pv.shape[0] - 1:, :].astype(f32))
        dn = jnp.where(last, 0.0, nx[:1, :].astype(f32))
        head = nbr[:8] + jnp.where(sub == 0, 0.5 * up, 0.0)
        tail = nbr[tt - 8:] + jnp.where(sub == 7, 0.5 * dn, 0.0)
        nbr = jnp.concatenate([head, nbr[8:tt - 8], tail], axis=0)
        return c + mu * (nbr - c)

    return shift


def _rwkv_p_body(r_ref, r_pv, r_nx, k_ref, k_pv, k_nx, v_ref, v_pv, v_nx, mur_ref, muk_ref, muv_ref,
                 kk_ref, ka_ref, rk_ref,
                 y0_ref, bonus_ref, rp_ref, mn_ref,
                 at_s, rt_s, kpm_s, bpm_s, khm_s, bhm_s, atx_s, rtx_s, vx_s, vo_s, pl_s, *, tt, shift, zs, a_pre):
    nc = tt // CHUNK
    r = shift(r_ref[0], r_pv[0], r_nx[0], mur_ref[...])
    k = shift(k_ref[0], k_pv[0], k_nx[0], muk_ref[...])
    v = shift(v_ref[0], v_pv[0], v_nx[0], muv_ref[...])
    yield PRE

    lane = lax.broadcasted_iota(jnp.int32, (PAIR, PAIR), 0) // HEAD
    seg = jnp.where(lane == lax.broadcasted_iota(jnp.int32, (PAIR, PAIR), 1) // HEAD, 1.0, 0.0)
    lo = lax.broadcasted_iota(jnp.int32, (tt, PAIR), 1) < HEAD
    swap = lambda t: pltpu.roll(t, HEAD, axis=1)

    def head_lo(t, h):
        return jnp.where(lo, t if h == 0 else swap(t), 0.0)

    def head_hi(t, h):
        return jnp.where(lo, 0.0, swap(t) if h == 0 else t)

    def head_own(t, h):
        return jnp.where(lo, t, 0.0) if h == 0 else jnp.where(lo, 0.0, t)

    for h in range(2):
        vx_s[h] = head_hi(v, h).astype(bf16)
    vo_s[...] = head_own(v, 0).astype(bf16)

    kkv = k * kk_ref[...]
    kk = kkv / jnp.maximum(jnp.sqrt(_segsum(kkv * kkv, seg)), 1e-12)
    yield PRE

    rows = lax.broadcasted_iota(jnp.int32, (tt, tt), 0)
    cols = lax.broadcasted_iota(jnp.int32, (tt, tt), 1)
    same = (rows // CHUNK) == (cols // CHUNK)

    kmod_sum = jnp.zeros_like(k)
    for d in range(2):
        if d == 0:
            tri_bd = jnp.where(same & (cols <= rows), 1.0, 0.0)
        else:
            tri_bd = jnp.where(same & (cols >= rows), 1.0, 0.0)
        lw = -math.exp(-0.5) * _sigmoid(zs[d])
        a = _sigmoid(a_pre[d])
        yield PRE
        kmod = k * (1.0 + (a - 1.0) * ka_ref[...])
        kmod_sum = kmod_sum + kmod
        b = -(kk * a)
        cum = _dot_split(tri_bd, lw)
        edge = CHUNK - 1 if d == 0 else 0
        tot = jnp.concatenate([jnp.broadcast_to(cum[c * CHUNK + edge:c * CHUNK + edge + 1, :], (CHUNK, PAIR))
                               for c in range(nc)], axis=0)
        yield PRE
        p_inv = jnp.exp(-cum)
        p_end = jnp.exp(tot - cum)
        at = kk * jnp.exp(cum - lw)
        rt = r * jnp.exp(cum)
        kh = kmod * p_inv
        bh = b * p_inv
        p_tot = jnp.exp(tot)
        at_s[d] = at.astype(bf16)
        rt_s[d] = rt.astype(bf16)
        pl_s[d] = p_tot
        yield PRE
        kp = kmod * p_end
        bp = b * p_end
        for h in range(2):
            khm_s[d, h] = head_own(kh, h).astype(bf16)
            bhm_s[d, h] = head_own(bh, h).astype(bf16)
            kpm_s[d, h] = head_own(kp, h).astype(bf16)
            bpm_s[d, h] = head_own(bp, h).astype(bf16)
            atx_s[d, h] = head_lo(at, h).astype(bf16)
            rtx_s[d, h] = head_lo(rt, h)
        yield PRE

    bonus = _segsum(r * kmod_sum * rk_ref[...], seg) * v
    yield PRE_DONE

    row = lax.broadcasted_iota(jnp.int32, (CHUNK, PAIR), 0)
    col = lax.broadcasted_iota(jnp.int32, (CHUNK, PAIR), 1)
    col_t = col % CHUNK
    lo_c = col < HEAD
    strict = (col_t < row, col_t > row)
    incl = (col_t <= row, col_t >= row)
    pairs = [(d, c) for d in range(2) for c in range(nc)]
    inst = [(d, c, h) for d, c in pairs for h in range(2)]
    cs = lambda c: slice(c * CHUNK, (c + 1) * CHUNK)
    swap_c = lambda t: pltpu.roll(t, HEAD, axis=1)
    zeros = jnp.zeros((CHUNK, PAIR), bf16)

    top, bot = [], []
    for d, c in pairs:
        lhs = jnp.concatenate([at_s[d, cs(c)], rt_s[d, cs(c)]], axis=0)
        rhs = jnp.concatenate([bhm_s[d, 0, cs(c)], khm_s[d, 0, cs(c)],
                               bhm_s[d, 1, cs(c)], khm_s[d, 1, cs(c)]], axis=0)
        sc = lax.dot_general(lhs, rhs, (((1,), (1,)), ((), ())), preferred_element_type=f32)
        for h in range(2):
            top.append(jnp.where(strict[d], sc[:CHUNK, h * PAIR:(h + 1) * PAIR], 0.0))
            bot.append(jnp.where(incl[d], sc[CHUNK:, h * PAIR:(h + 1) * PAIR], 0.0))
    yield CHUNK_STAGE
    xs, aps = [], []
    for i, (d, c, h) in enumerate(inst):
        vx = vx_s[h, cs(c)]
        a_ak = jnp.where(lo_c, 0.0, top[i]).astype(bf16)
        akv = jnp.dot(a_ak, jnp.concatenate([vx, vx], axis=0), preferred_element_type=f32)
        xs.append(atx_s[d, h, cs(c)].astype(f32) + akv)
        aps.append(top[i][:, :CHUNK])
    n_dbl = CHUNK.bit_length() - 1
    for it in range(n_dbl):
        yield CHUNK_STAGE
        last = it + 1 == n_dbl
        res = [_dot(ap, x if last else jnp.concatenate([x, ap], axis=1)) for ap, x in zip(aps, xs)]
        xs = [x + rs[:, :PAIR] for x, rs in zip(xs, res)]
        if not last:
            aps = [rs[:, PAIR:] for rs in res]
    yield CHUNK_STAGE
    rys = [_dot(bot[i], jnp.concatenate([xs[i].astype(bf16), vx_s[h, cs(c)]], axis=0)) + rtx_s[d, h, cs(c)]
           for i, (d, c, h) in enumerate(inst)]
    yield CHUNK_STAGE
    mns = []
    for j, (d, c) in enumerate(pairs):
        x0, x1 = xs[2 * j], xs[2 * j + 1]
        r0 = jnp.concatenate([jnp.where(lo_c, x0, 0.0), jnp.where(lo_c, swap_c(x0), 0.0)], axis=1).astype(bf16)
        r1 = jnp.concatenate([jnp.where(lo_c, 0.0, swap_c(x1)), jnp.where(lo_c, 0.0, x1)], axis=1).astype(bf16)
        r2 = jnp.concatenate([zeros, vo_s[cs(c)]], axis=1)
        r3 = jnp.concatenate([zeros, vx_s[1, cs(c)]], axis=1)
        rhs = jnp.concatenate([r0, r1, r2, r3], axis=0)
        lhs_t = jnp.concatenate([bpm_s[d, 0, cs(c)], bpm_s[d, 1, cs(c)],
                                 kpm_s[d, 0, cs(c)], kpm_s[d, 1, cs(c)]], axis=0)
        mns.append(lax.dot_general(lhs_t, rhs, (((0,), (0,)), ((), ())), preferred_element_type=f32))
    yield CHUNK_STAGE
    rowp = lax.broadcasted_iota(jnp.int32, (PAIR, 2 * PAIR), 0)
    colp = lax.broadcasted_iota(jnp.int32, (PAIR, 2 * PAIR), 1)
    y0_fwd = {}
    for j, (d, c) in enumerate(pairs):
        ry0, ry1 = rys[2 * j], rys[2 * j + 1]
        rp_ref[d, cs(c), :] = jnp.where(lo_c, ry0, swap_c(ry1)).astype(bf16)
        y0 = jnp.where(lo_c, swap_c(ry0), ry1)
        if d == 0:
            y0_fwd[c] = y0
        else:
            y0_ref[cs(c), :] = (y0_fwd[c] + y0).astype(y0_ref.dtype)
        p_row = pl_s[d, c * CHUNK:c * CHUNK + 1, :]
        diag = jnp.where(rowp == colp, jnp.concatenate([p_row, p_row], axis=1), 0.0)
        mn_ref[d, c, 0] = (mns[j] + diag).astype(bf16)
    bonus_ref[...] = bonus.astype(bonus_ref.dtype)


N_IN_P = 23
N_OUT_P = 4


def _rwkv_p_multi(*refs, tt, seq, pp):
    ins, outs, scr = refs[:N_IN_P], refs[N_IN_P:N_IN_P + N_OUT_P], refs[N_IN_P + N_OUT_P:]
    x_ref, x_pv, x_nx, mux_ref = ins[9], ins[10], ins[11], ins[15]
    w0_ref, wlu_ref, a0_ref, alu_ref = ins[16:20]
    shift = _token_shift(tt, seq)
    xs = shift(x_ref[...], x_pv[...], x_nx[...], mux_ref[...])
    zs = [w0_ref[d:d + 1, :] + _dot(jnp.tanh(xs[:, d * R_LORA:(d + 1) * R_LORA]), wlu_ref[d]) for d in range(2)]
    a_pre = [a0_ref[d:d + 1, :] + _dot(xs[:, (2 + d) * R_LORA:(3 + d) * R_LORA], alu_ref[d]) for d in range(2)]
    stages = []
    for q in range(pp):
        lanes = pl.ds(q * PAIR, PAIR)
        lsl = slice(q * PAIR, (q + 1) * PAIR)
        view = lambda ref, lanes=lanes: ref.at[..., lanes]
        planes = [view(ref) for ref in ins[0:9]]
        gains = [view(ref) for ref in ins[12:15]]
        params = [view(ref) for ref in ins[20:23]]
        y0_ref, bonus_ref, rp_ref, mn_ref = outs
        stages.append(_rwkv_p_body(*planes, *gains, *params,
                                   view(y0_ref), view(bonus_ref), view(rp_ref), mn_ref.at[:, :, pl.ds(q, 1)],
                                   *[s.at[q] for s in scr], tt=tt, shift=shift,
                                   zs=[z[:, lsl] for z in zs], a_pre=[a[:, lsl] for a in a_pre]))
    def run_pre(stage):
        while next(stage) != PRE_DONE:
            pass

    run_pre(stages[0])
    for q in range(pp):
        chunk, pre = stages[q], (stages[q + 1] if q + 1 < pp else None)
        chunk_live, pre_live = True, pre is not None
        while chunk_live or pre_live:
            if chunk_live:
                chunk_live = next(chunk, None) is not None
            if pre_live:
                pre_live = next(pre) != PRE_DONE


def _segsum(x, seg, terms=2):
    x0 = x.astype(bf16)
    m = seg.astype(bf16)
    s = jnp.dot(x0, m, preferred_element_type=f32)
    if terms == 1:
        return s
    x1 = (x - x0.astype(f32)).astype(bf16)
    return s + jnp.dot(x1, m, preferred_element_type=f32)


def _rwkv_p(p8, px, mu_p, w0, wlu, a0, alu, k_k, k_a, r_k, *, tt, seq, pp):
    m = px.shape[0]
    nt = m // tt
    npair = D_MODEL // PAIR
    cpt = tt // CHUNK
    wl = pp * PAIR
    hb, hx = 16, 8

    def tiles(plane):
        return [pl.BlockSpec((1, tt, wl), lambda i, j: (plane, i, j)),
                pl.BlockSpec((1, hb, wl), lambda i, j: (plane, jnp.maximum(i * (tt // hb) - 1, 0), j)),
                pl.BlockSpec((1, hb, wl), lambda i, j: (plane, jnp.minimum((i + 1) * (tt // hb), m // hb - 1), j))]

    vec = pl.BlockSpec((1, wl), lambda i, j: (0, j))
    mu = lambda plane: pl.BlockSpec((1, wl), lambda i, j: (0, plane * (npair // pp) + j))
    wx = 4 * R_LORA
    return pl.pallas_call(
        functools.partial(_rwkv_p_multi, tt=tt, seq=seq, pp=pp),
        grid=(nt, npair // pp),
        in_specs=tiles(P_R) + tiles(P_K) + tiles(P_V) + [
            pl.BlockSpec((tt, wx), lambda i, j: (i, 0)),
            pl.BlockSpec((hx, wx), lambda i, j: (jnp.maximum(i * (tt // hx) - 1, 0), 0)),
            pl.BlockSpec((hx, wx), lambda i, j: (jnp.minimum((i + 1) * (tt // hx), m // hx - 1), 0)),
            mu(P_R), mu(P_K), mu(P_V),
            pl.BlockSpec((1, wx), lambda i, j: (0, P_X * D_MODEL // wx)),
            pl.BlockSpec((2, wl), lambda i, j: (0, j)),
            pl.BlockSpec((2, R_LORA, wl), lambda i, j: (0, 0, j)),
            pl.BlockSpec((2, wl), lambda i, j: (0, j)),
            pl.BlockSpec((2, R_LORA, wl), lambda i, j: (0, 0, j)),
            vec, vec, vec,
        ],
        out_specs=[
            pl.BlockSpec((tt, wl), lambda i, j: (i, j)),
            pl.BlockSpec((tt, wl), lambda i, j: (i, j)),
            pl.BlockSpec((2, tt, wl), lambda i, j: (0, i, j)),
            pl.BlockSpec((2, cpt, pp, PAIR, 2 * PAIR), lambda i, j: (0, i, j, 0, 0)),
        ],
        out_shape=[
            jax.ShapeDtypeStruct((m, D_MODEL), bf16),
            jax.ShapeDtypeStruct((m, D_MODEL), bf16),
            jax.ShapeDtypeStruct((2, m, D_MODEL), bf16),
            jax.ShapeDtypeStruct((2, m // CHUNK, npair, PAIR, 2 * PAIR), bf16),
        ],
        scratch_shapes=(
            [pltpu.VMEM((pp, 2, tt, PAIR), bf16) for _ in range(2)]
            + [pltpu.VMEM((pp, 2, 2, tt, PAIR), bf16) for _ in range(5)]
            + [pltpu.VMEM((pp, 2, 2, tt, PAIR), f32),
               pltpu.VMEM((pp, 2, tt, PAIR), bf16),
               pltpu.VMEM((pp, tt, PAIR), bf16),
               pltpu.VMEM((pp, 2, tt, PAIR), f32)]),
        compiler_params=pltpu.CompilerParams(
            dimension_semantics=("parallel", "parallel"), vmem_limit_bytes=VMEM_LIMIT),
        name="rwkv_p",
    )(p8, p8, p8, p8, p8, p8, p8, p8, p8, px, px, px, mu_p, mu_p, mu_p, mu_p, w0, wlu, a0, alu, k_k, k_a, r_k)


def _rwkv_s_body(mnf_ref, mnb_ref, sf_ref, sb_ref, st_ref, *, grp):
    @pl.when(pl.program_id(1) == 0)
    def _():
        st_ref[...] = jnp.zeros_like(st_ref)

    def step(d, mn_ref, s_out_ref, cc):
        s = st_ref[d].astype(bf16)
        s_out_ref[cc] = s
        m_t = mn_ref[0, cc, :, :, 0:PAIR]
        n_t = mn_ref[0, cc, :, :, PAIR:].astype(f32)
        st_ref[d] = jnp.einsum('pij,pjk->pik', m_t, s, preferred_element_type=f32) + n_t

    for cc in range(grp):
        step(0, mnf_ref, sf_ref, cc)
        step(1, mnb_ref, sb_ref, grp - 1 - cc)


def _rwkv_s(mn, *, n_seq, grp):
    nchunk, npair = mn.shape[1], mn.shape[2]
    ng = nchunk // n_seq // grp
    blk_in = (1, grp, npair, PAIR, 2 * PAIR)
    blk_out = (grp, npair, PAIR, PAIR)
    out = jax.ShapeDtypeStruct((nchunk, npair, PAIR, PAIR), bf16)
    return pl.pallas_call(
        functools.partial(_rwkv_s_body, grp=grp),
        grid=(n_seq, ng),
        in_specs=[pl.BlockSpec(blk_in, lambda b, g: (0, b * ng + g, 0, 0, 0)),
                  pl.BlockSpec(blk_in, lambda b, g: (1, b * ng + ng - 1 - g, 0, 0, 0))],
        out_specs=[pl.BlockSpec(blk_out, lambda b, g: (b * ng + g, 0, 0, 0)),
                   pl.BlockSpec(blk_out, lambda b, g: (b * ng + ng - 1 - g, 0, 0, 0))],
        out_shape=[out, out],
        scratch_shapes=[pltpu.VMEM((2, npair, PAIR, PAIR), f32)],
        compiler_params=pltpu.CompilerParams(
            dimension_semantics=("parallel", "arbitrary"), vmem_limit_bytes=VMEM_LIMIT),
        name="rwkv_s",
    )(mn, mn)


def _rwkv_f_body(y0_ref, bonus_ref, rp_ref, sf_ref, sb_ref, gate_ref, lng_ref, lnb_ref, o_ref, *, tt):
    npair = D_MODEL // PAIR
    nc = tt // CHUNK
    cs = lambda c: slice(c * CHUNK, (c + 1) * CHUNK)
    ps = lambda p: slice(p * PAIR, (p + 1) * PAIR)
    yf = [[jnp.dot(rp_ref[0, cs(c), ps(p)], sf_ref[c, p], preferred_element_type=f32) for c in range(nc)]
          for p in range(npair)]
    yb = [[jnp.dot(rp_ref[1, cs(c), ps(p)], sb_ref[c, p], preferred_element_type=f32) for c in range(nc)]
          for p in range(npair)]
    lane = lax.broadcasted_iota(jnp.int32, (PAIR, PAIR), 0) // HEAD
    seg = jnp.where(lane == lax.broadcasted_iota(jnp.int32, (PAIR, PAIR), 1) // HEAD, 1.0, 0.0)
    ys = [y0_ref[:, ps(p)].astype(f32) + jnp.concatenate(yf[p], axis=0) + jnp.concatenate(yb[p], axis=0) for p in range(npair)]
    mus = [_segsum(y, seg) * (1.0 / HEAD) for y in ys]
    dvs = [y - mu for y, mu in zip(ys, mus)]
    vrs = [_segsum(dv * dv, seg) * (1.0 / HEAD) for dv in dvs]
    for p in range(npair):
        yn = dvs[p] * lax.rsqrt(vrs[p] + GN_EPS) * lng_ref[:, ps(p)] + lnb_ref[:, ps(p)] + bonus_ref[:, ps(p)].astype(f32)
        g = gate_ref[0, :, ps(p)].astype(f32)
        o_ref[:, ps(p)] = (yn * (g * _sigmoid(g))).astype(bf16)


def _rwkv_f(y0, bonus, rp, sf, sb, p8, ln_g, ln_b, *, tt):
    m = y0.shape[0]
    cpt = tt // CHUNK
    npair = D_MODEL // PAIR
    tile = pl.BlockSpec((tt, D_MODEL), lambda i: (i, 0))
    vec = pl.BlockSpec((1, D_MODEL), lambda i: (0, 0))
    state = pl.BlockSpec((cpt, npair, PAIR, PAIR), lambda i: (i, 0, 0, 0))
    return pl.pallas_call(
        functools.partial(_rwkv_f_body, tt=tt),
        grid=(m // tt,),
        in_specs=[
            tile, tile,
            pl.BlockSpec((2, tt, D_MODEL), lambda i: (0, i, 0)),
            state, state,
            pl.BlockSpec((1, tt, D_MODEL), lambda i: (P_GA, i, 0)),
            vec, vec,
        ],
        out_specs=tile,
        out_shape=jax.ShapeDtypeStruct((m, D_MODEL), bf16),
        compiler_params=pltpu.CompilerParams(dimension_semantics=("parallel",), vmem_limit_bytes=VMEM_LIMIT),
        name="rwkv_f",
    )(y0, bonus, rp, sf, sb, p8, ln_g, ln_b)


def _attn_body(q_ref, kp_ref, kc_ref, kn_ref, vp_ref, vc_ref, vn_ref, tp_ref, tc_ref, tn_ref, gate_ref,
               qg_ref, kg_ref, sink_ref, o_ref, *, seq, qb):
    step_rows = qb * BLOCK
    start = (pl.program_id(0) % (seq // step_rows)) * step_rows
    ctx = BLOCK + 2 * WINDOW
    qoff = lax.broadcasted_iota(jnp.int32, (BLOCK, ctx), 0)
    koff = lax.broadcasted_iota(jnp.int32, (BLOCK, ctx), 1) - WINDOW
    valid = []
    for b in range(qb):
        kpos = start + b * BLOCK + koff
        valid.append((jnp.abs(qoff - koff) <= WINDOW) & (kpos >= 0) & (kpos < seq))

    lane_r = lax.broadcasted_iota(jnp.int32, (PAIR, PAIR), 0) // HEAD
    seg = jnp.where(lane_r == lax.broadcasted_iota(jnp.int32, (PAIR, PAIR), 1) // HEAD, 1.0, 0.0)
    swap = lambda t: pltpu.roll(t, HEAD, axis=1)

    def norm_rope(x, gain, tab):
        lane = lax.broadcasted_iota(jnp.int32, x.shape, 1)
        xn = x * lax.rsqrt(_segsum(x * x, seg, terms=1) * (1.0 / HEAD) + NORM_EPS) * gain
        rot = jnp.where(lane % HEAD < HEAD // 2, pltpu.roll(xn, PAIR - HEAD // 2, axis=1),
                        pltpu.roll(xn, HEAD // 2, axis=1))
        return xn * tab[:, :PAIR] + rot * tab[:, PAIR:]

    def place(t, e, want_lo):
        lo = lax.broadcasted_iota(jnp.int32, t.shape, 1) < HEAD
        src = t if (e == 0) == want_lo else swap(t)
        return (jnp.where(lo, src, 0.0) if want_lo else jnp.where(lo, 0.0, src)).astype(bf16)

    tab_k = jnp.concatenate([tp_ref[...], tc_ref[...], tn_ref[...]], axis=0)
    tab_q = tc_ref[...]
    k_lo, k_hi, v_lo, v_hi = [], [], [], []
    for kp in range(D_KV // PAIR):
        ps_ = slice(kp * PAIR, (kp + 1) * PAIR)
        kx = jnp.concatenate([kp_ref[:, ps_], kc_ref[:, ps_], kn_ref[:, ps_]], axis=0)
        vx = jnp.concatenate([vp_ref[:, ps_], vc_ref[:, ps_], vn_ref[:, ps_]], axis=0)
        kr = norm_rope(kx, kg_ref[...], tab_k)
        for e in range(2):
            k_lo.append(place(kr, e, True))
            k_hi.append(place(kr, e, False))
            v_lo.append(place(vx, e, True))
            v_hi.append(place(vx, e, False))
    npair = D_MODEL // PAIR
    psl = lambda p: slice(p * PAIR, (p + 1) * PAIR)
    log2e = math.log2(math.e)
    qrs = [(norm_rope(q_ref[0, :, psl(p)].astype(f32), qg_ref[...], tab_q) * (HEAD ** -0.5 * log2e)).astype(bf16)
           for p in range(npair)]
    inst = [(b, hq) for b in range(qb) for hq in range(N_HEADS)]
    rows = lambda b: slice(b * BLOCK, (b + 1) * BLOCK)
    keys = lambda b: slice(b * BLOCK, b * BLOCK + ctx)
    def masked(s, ok):
        return jnp.concatenate([jnp.where(ok[:, :WINDOW], s[:, :WINDOW], -1e30), s[:, WINDOW:WINDOW + BLOCK],
                                jnp.where(ok[:, WINDOW + BLOCK:], s[:, WINDOW + BLOCK:], -1e30)], axis=1)

    ss = [masked(lax.dot_general(qrs[hq // 2][rows(b)], (k_lo if hq % 2 == 0 else k_hi)[hq // GRP][keys(b)],
                                 (((1,), (1,)), ((), ())), preferred_element_type=f32), valid[b])
          for b, hq in inst]
    sinks = [sink_ref[0:1, hq:hq + 1] * log2e for b, hq in inst]
    mxs = [jnp.maximum(jnp.max(s, axis=-1, keepdims=True), sk) for s, sk in zip(ss, sinks)]
    ps = [jnp.exp2(s - mx) for s, mx in zip(ss, mxs)]
    dens = [jnp.sum(p, axis=-1, keepdims=True) + jnp.exp2(sk - mx) for p, sk, mx in zip(ps, sinks, mxs)]
    os_ = [jnp.dot(p.astype(bf16), (v_lo if hq % 2 == 0 else v_hi)[hq // GRP][keys(b)], preferred_element_type=f32) / den
           for (b, hq), p, den in zip(inst, ps, dens)]
    for b in range(qb):
        for p in range(npair):
            gt = gate_ref[0, rows(b), psl(p)].astype(f32)
            i0 = b * N_HEADS + 2 * p
            o_ref[rows(b), psl(p)] = ((os_[i0] + os_[i0 + 1]) * (gt * _sigmoid(gt))).astype(bf16)


def _attn(p8, px, rope_tab, q_g, k_g, sink, *, seq, qb):
    m = px.shape[0]
    nblk = seq // BLOCK
    rows = qb * BLOCK

    def nbr(delta):
        def f(i):
            t = (i * qb) % nblk
            return i * qb - t + jnp.clip(t + delta, 0, nblk - 1)
        return f

    def kv(col):
        return [pl.BlockSpec((BLOCK, D_KV), lambda i, f=nbr(-1): (f(i), col)),
                pl.BlockSpec((rows, D_KV), lambda i: (i, col)),
                pl.BlockSpec((BLOCK, D_KV), lambda i, f=nbr(qb): (f(i), col))]

    tabs = [pl.BlockSpec((BLOCK, 2 * PAIR), lambda i, f=nbr(-1): (f(i) % nblk, 0)),
            pl.BlockSpec((rows, 2 * PAIR), lambda i: (i % (nblk // qb), 0)),
            pl.BlockSpec((BLOCK, 2 * PAIR), lambda i, f=nbr(qb): (f(i) % nblk, 0))]

    return pl.pallas_call(
        functools.partial(_attn_body, seq=seq, qb=qb),
        grid=(m // rows,),
        in_specs=[pl.BlockSpec((1, rows, D_MODEL), lambda i: (P_Q, i, 0))] + kv(1) + kv(2) + tabs + [
            pl.BlockSpec((1, rows, D_MODEL), lambda i: (P_GB, i, 0)),
            pl.BlockSpec((1, PAIR), lambda i: (0, 0)),
            pl.BlockSpec((1, PAIR), lambda i: (0, 0)),
            pl.BlockSpec((1, N_HEADS), lambda i: (0, 0)),
        ],
        out_specs=pl.BlockSpec((rows, D_MODEL), lambda i: (i, 0)),
        out_shape=jax.ShapeDtypeStruct((m, D_MODEL), bf16),
        compiler_params=pltpu.CompilerParams(dimension_semantics=("parallel",), vmem_limit_bytes=VMEM_LIMIT),
        name="attn",
    )(p8, px, px, px, px, px, px, rope_tab, rope_tab, rope_tab, p8, q_g, k_g, sink)


def _rope_table(seq):
    inv = 1.0 / (ROPE_THETA ** (jnp.arange(0, HEAD, 2, dtype=f32) / HEAD))
    ang = jnp.arange(seq, dtype=f32)[:, None] * inv[None, :]
    cos, sin = jnp.cos(ang), jnp.sin(ang)
    return jnp.concatenate([cos, cos, cos, cos, -sin, sin, -sin, sin], axis=1)


def _out_proj_body(x_ref, ga_ref, gb_ref, ma_ref, mb_ref, wa_ref, wb_ref, wo_ref, o_ref):
    ya = jnp.dot(ga_ref[...], wa_ref[...], preferred_element_type=f32)
    yb = jnp.dot(gb_ref[...], wb_ref[...], preferred_element_type=f32)
    mixed = _sigmoid(ma_ref[0].astype(f32)) * ya + _sigmoid(mb_ref[0].astype(f32)) * yb
    o_ref[...] = x_ref[...] + jnp.dot(mixed.astype(bf16), wo_ref[...], preferred_element_type=f32)


def _out_proj(x2d, ga, gb, p8, wa, wb, wo, *, tm):
    m = x2d.shape[0]
    row = pl.BlockSpec((tm, D_MODEL), lambda i: (i, 0))
    wspec = pl.BlockSpec((D_MODEL, D_MODEL), lambda i: (0, 0))
    return pl.pallas_call(
        _out_proj_body,
        grid=(m // tm,),
        in_specs=[row, row, row,
                  pl.BlockSpec((1, tm, D_MODEL), lambda i: (P_MA, i, 0)),
                  pl.BlockSpec((1, tm, D_MODEL), lambda i: (P_MB, i, 0)),
                  wspec, wspec, wspec],
        out_specs=row,
        out_shape=jax.ShapeDtypeStruct((m, D_MODEL), f32),
        compiler_params=pltpu.CompilerParams(dimension_semantics=("parallel",), vmem_limit_bytes=VMEM_LIMIT),
        name="out_proj",
    )(x2d, ga, gb, p8, p8, wa, wb, wo)


def _layer(x2d, lw, rope_tab, *, n_seq, seq):
    tm = min(1024, seq)
    tt = min(256, seq)
    grp = min(8, seq // CHUNK)
    p8, px = _in_proj(x2d, lw["norm_g"], lw["w_p"], tm=min(2 * tm, seq))
    y0, bonus, rp, mn = _rwkv_p(p8, px, lw["mu_p"], lw["w0"], lw["wlu"], lw["a0"], lw["alu"], lw["k_k"], lw["k_a"],
                                lw["r_k"], tt=tt, seq=seq, pp=4)
    sf, sb = _rwkv_s(mn, n_seq=n_seq, grp=grp)
    ga = _rwkv_f(y0, bonus, rp, sf, sb, p8, lw["ln_g"], lw["ln_b"], tt=tt)
    gb = _attn(p8, px, rope_tab, lw["q_g"], lw["k_g"], lw["sink"], seq=seq, qb=2)
    return _out_proj(x2d, ga, gb, p8, lw["wa"], lw["wb"], lw["wo"], tm=tm)


def _trunk(x, layers):
    n_seq, seq, _ = x.shape
    rope_tab = _rope_table(seq)
    x2d = x.reshape(n_seq * seq, D_MODEL)
    for lw in layers:
        x2d = _layer(x2d, lw, rope_tab, n_seq=n_seq, seq=seq)
    return x2d.reshape(n_seq, seq, D_MODEL)


def _prep_layers(norm_g, w_in, shift_mu, w0, w_lora_up, a0, a_lora_up, k_k, k_a, r_k, ln_x_g, ln_x_b,
                 q_norm_g, k_norm_g, sink, w_proj_a, w_proj_b, w_out):
    layers = []
    row = lambda v: v.reshape(1, -1).astype(f32)
    for l in range(norm_g.shape[0]):
        w_p, mu_p = _prep_layer_weights(w_in[l], shift_mu[l])
        layers.append(dict(
            norm_g=row(norm_g[l]), w_p=w_p, mu_p=mu_p,
            w0=w0[l].astype(f32), wlu=w_lora_up[l].astype(bf16), a0=a0[l].astype(f32), alu=a_lora_up[l].astype(bf16),
            k_k=row(k_k[l]), k_a=row(k_a[l]), r_k=row(r_k[l]), ln_g=row(ln_x_g[l]), ln_b=row(ln_x_b[l]),
            q_g=row(jnp.tile(q_norm_g[l], 2)), k_g=row(jnp.tile(k_norm_g[l], 2)), sink=row(sink[l]),
            wa=w_proj_a[l].astype(bf16), wb=w_proj_b[l].astype(bf16), wo=w_out[l].astype(bf16)))
    return layers


def kernel(x_prompt, x_sample, norm_g, w_in, shift_mu, w0, w_lora_up, a0, a_lora_up, k_k, k_a, r_k,
           ln_x_g, ln_x_b, q_norm_g, k_norm_g, sink, w_proj_a, w_proj_b, w_out):
    layers = _prep_layers(norm_g, w_in, shift_mu, w0, w_lora_up, a0, a_lora_up, k_k, k_a, r_k, ln_x_g, ln_x_b,
                          q_norm_g, k_norm_g, sink, w_proj_a, w_proj_b, w_out)
    return _trunk(x_prompt, layers), _trunk(x_sample, layers)
```

```python
import functools
import math

import jax
import jax.numpy as jnp
from jax import lax
from jax.experimental import pallas as pl
from jax.experimental.pallas import tpu as pltpu

D_MODEL = 1024
HEAD = 64
N_HEADS = D_MODEL // HEAD
R_LORA = 64
HKV = 4
GRP = N_HEADS // HKV
D_KV = HKV * HEAD
WINDOW = 128
BLOCK = 128
GN_EPS = 64e-5
NORM_EPS = 1e-6
ROPE_THETA = 10000.0
C_SHIFT = 3 * D_MODEL + 4 * R_LORA
N_PLANES = 9
CHUNK = 64
VMEM_LIMIT = 56 * 1024 * 1024

P_R, P_K, P_V, P_GA, P_Q, P_GB, P_MA, P_MB, P_X = range(9)

f32 = jnp.float32
bf16 = jnp.bfloat16


def _dot(a, b):
    return jnp.dot(a.astype(bf16), b.astype(bf16), preferred_element_type=f32)


def _dot_nt(a, b):
    return lax.dot_general(a.astype(bf16), b.astype(bf16), (((1,), (1,)), ((), ())),
                           preferred_element_type=f32)


def _dot_tn(a, b):
    return lax.dot_general(a.astype(bf16), b.astype(bf16), (((0,), (0,)), ((), ())),
                           preferred_element_type=f32)


def _dot_split(a_exact, b):
    b0 = b.astype(bf16)
    b1 = (b - b0.astype(f32)).astype(bf16)
    a = a_exact.astype(bf16)
    return jnp.dot(a, b0, preferred_element_type=f32) + jnp.dot(a, b1, preferred_element_type=f32)


def _sigmoid(x):
    return 1.0 / (1.0 + jnp.exp(-x))


def _in_proj_body(x_ref, g_ref, w_ref, o_ref, ox_ref, h_ref):
    j = pl.program_id(1)

    @pl.when(j == 0)
    def _():
        x = x_ref[...]
        ms = jnp.mean(x * x, axis=-1, keepdims=True)
        h_ref[...] = (x * lax.rsqrt(ms + NORM_EPS) * g_ref[...]).astype(bf16)

    @pl.when(j < P_X)
    def _():
        o_ref[0] = jnp.dot(h_ref[...], w_ref[...], preferred_element_type=f32).astype(bf16)

    @pl.when(j == P_X)
    def _():
        ox_ref[...] = jnp.dot(h_ref[...], w_ref[...], preferred_element_type=f32)


def _in_proj(x2d, norm_g, w_p, *, tm):
    m = x2d.shape[0]
    return pl.pallas_call(
        _in_proj_body,
        grid=(m // tm, N_PLANES),
        in_specs=[
            pl.BlockSpec((tm, D_MODEL), lambda i, j: (i, 0)),
            pl.BlockSpec((1, D_MODEL), lambda i, j: (0, 0)),
            pl.BlockSpec((D_MODEL, D_MODEL), lambda i, j: (0, j)),
        ],
        out_specs=[pl.BlockSpec((1, tm, D_MODEL), lambda i, j: (jnp.minimum(j, P_X - 1), i, 0)),
                   pl.BlockSpec((tm, D_MODEL), lambda i, j: (i, 0))],
        out_shape=[jax.ShapeDtypeStruct((N_PLANES - 1, m, D_MODEL), bf16),
                   jax.ShapeDtypeStruct((m, D_MODEL), f32)],
        scratch_shapes=[pltpu.VMEM((tm, D_MODEL), bf16)],
        compiler_params=pltpu.CompilerParams(
            dimension_semantics=("parallel", "arbitrary"), vmem_limit_bytes=VMEM_LIMIT),
        name="in_proj",
    )(x2d, norm_g, w_p)


def _prep_layer_weights(w_in, shift_mu):
    c = C_SHIFT
    d = D_MODEL
    cols = [
        (0, d), (d, 2 * d), (2 * d, 3 * d),
        (c, c + d),
        (c + d, c + 2 * d),
        (c + 2 * d + 2 * D_KV, c + 3 * d + 2 * D_KV),
        (c + 3 * d + 2 * D_KV, c + 4 * d + 2 * D_KV),
        (c + 4 * d + 2 * D_KV, c + 5 * d + 2 * D_KV),
    ]
    planes = [w_in[:, a:b] for a, b in cols]
    small = jnp.concatenate([w_in[:, 3 * d:c], w_in[:, c + 2 * d:c + 2 * d + 2 * D_KV],
                             jnp.zeros((d, d - 4 * R_LORA - 2 * D_KV), w_in.dtype)], axis=1)
    w_p = jnp.concatenate(planes + [small], axis=1).astype(bf16)
    zeros = jnp.zeros((d,), f32)
    mu_small = jnp.concatenate([shift_mu[3 * d:c], jnp.zeros((d - 4 * R_LORA,), f32)])
    mu_p = jnp.concatenate([shift_mu[0:d], shift_mu[d:2 * d], shift_mu[2 * d:3 * d]] + [zeros] * 5 + [mu_small])
    return w_p, mu_p.reshape(1, N_PLANES * d)


PAIR = 2 * HEAD
PRE, PRE_DONE, CHUNK_STAGE = "pre", "pre_done", "chunk"


def _token_shift(tt, seq):
    i = pl.program_id(0)
    first = (i * tt) % seq == 0
    last = ((i + 1) * tt) % seq == 0
    trow = lax.broadcasted_iota(jnp.int32, (tt, tt), 0)
    tcol = lax.broadcasted_iota(jnp.int32, (tt, tt), 1)
    nbr_mat = jnp.where(jnp.abs(trow - tcol) == 1, 0.5, 0.0)
    sub = lax.broadcasted_iota(jnp.int32, (8, 1), 0)

    def shift(main, pv, nx, mu):
        c = main.astype(f32)
        nbr = _dot(nbr_mat, main) if main.dtype == bf16 else _dot_split(nbr_mat, c)
        up = jnp.where(first, 0.0, pv[pv.shape[0] - 1:, :].astype(f32))
        dn = jnp.where(last, 0.0, nx[:1, :].astype(f32))
        head = nbr[:8] + jnp.where(sub == 0, 0.5 * up, 0.0)
        tail = nbr[tt - 8:] + jnp.where(sub == 7, 0.5 * dn, 0.0)
        nbr = jnp.concatenate([head, nbr[8:tt - 8], tail], axis=0)
        return c + mu * (nbr - c)

    return shift


def _rwkv_p_body(r_ref, r_pv, r_nx, k_ref, k_pv, k_nx, v_ref, v_pv, v_nx, mur_ref, muk_ref, muv_ref,
                 kk_ref, ka_ref, rk_ref,
                 y0_ref, bonus_ref, rp_ref, mn_ref,
                 at_s, rt_s, kp_s, bp_s, khm_s, bhm_s, atx_s, rtx_s, vx_s, pl_s, *, tt, shift, zs, a_pre):
    nc = tt // CHUNK
    r = shift(r_ref[0], r_pv[0], r_nx[0], mur_ref[...])
    k = shift(k_ref[0], k_pv[0], k_nx[0], muk_ref[...])
    v = shift(v_ref[0], v_pv[0], v_nx[0], muv_ref[...])
    yield PRE

    lane = lax.broadcasted_iota(jnp.int32, (PAIR, PAIR), 0) // HEAD
    seg = jnp.where(lane == lax.broadcasted_iota(jnp.int32, (PAIR, PAIR), 1) // HEAD, 1.0, 0.0)
    lo = lax.broadcasted_iota(jnp.int32, (tt, PAIR), 1) < HEAD
    swap = lambda t: pltpu.roll(t, HEAD, axis=1)

    def head_lo(t, h):
        return jnp.where(lo, t if h == 0 else swap(t), 0.0)

    def head_hi(t, h):
        return jnp.where(lo, 0.0, swap(t) if h == 0 else t)

    def head_own(t, h):
        return jnp.where(lo, t, 0.0) if h == 0 else jnp.where(lo, 0.0, t)

    for h in range(2):
        vx_s[h] = head_hi(v, h).astype(bf16)

    kkv = k * kk_ref[...]
    kk = kkv / jnp.maximum(jnp.sqrt(_segsum(kkv * kkv, seg)), 1e-12)
    yield PRE

    rows = lax.broadcasted_iota(jnp.int32, (tt, tt), 0)
    cols = lax.broadcasted_iota(jnp.int32, (tt, tt), 1)
    same = (rows // CHUNK) == (cols // CHUNK)

    kmod_sum = jnp.zeros_like(k)
    for d in range(2):
        if d == 0:
            tri_bd = jnp.where(same & (cols <= rows), 1.0, 0.0)
        else:
            tri_bd = jnp.where(same & (cols >= rows), 1.0, 0.0)
        lw = -math.exp(-0.5) * _sigmoid(zs[d])
        a = _sigmoid(a_pre[d])
        yield PRE
        kmod = k * (1.0 + (a - 1.0) * ka_ref[...])
        kmod_sum = kmod_sum + kmod
        b = -(kk * a)
        cum = _dot_split(tri_bd, lw)
        edge = CHUNK - 1 if d == 0 else 0
        tot = jnp.concatenate([jnp.broadcast_to(cum[c * CHUNK + edge:c * CHUNK + edge + 1, :], (CHUNK, PAIR))
                               for c in range(nc)], axis=0)
        yield PRE
        p_inv = jnp.exp(-cum)
        p_end = jnp.exp(tot - cum)
        at = kk * jnp.exp(cum - lw)
        rt = r * jnp.exp(cum)
        kh = kmod * p_inv
        bh = b * p_inv
        p_tot = jnp.exp(tot)
        at_s[d] = at.astype(bf16)
        rt_s[d] = rt.astype(bf16)
        pl_s[d] = p_tot
        yield PRE
        kp_s[d] = (kmod * p_end).astype(bf16)
        bp_s[d] = (b * p_end).astype(bf16)
        for h in range(2):
            khm_s[d, h] = head_own(kh, h).astype(bf16)
            bhm_s[d, h] = head_own(bh, h).astype(bf16)
            atx_s[d, h] = head_lo(at, h).astype(bf16)
            rtx_s[d, h] = head_lo(rt, h)
        yield PRE

    bonus = _segsum(r * kmod_sum * rk_ref[...], seg) * v
    yield PRE_DONE

    row = lax.broadcasted_iota(jnp.int32, (CHUNK, PAIR), 0)
    col = lax.broadcasted_iota(jnp.int32, (CHUNK, PAIR), 1)
    col_t = col % CHUNK
    lo_c = col < HEAD
    strict = (col_t < row, col_t > row)
    incl = (col_t <= row, col_t >= row)
    pairs = [(d, c) for d in range(2) for c in range(nc)]
    inst = [(d, c, h) for d, c in pairs for h in range(2)]
    cs = lambda c: slice(c * CHUNK, (c + 1) * CHUNK)
    swap_c = lambda t: pltpu.roll(t, HEAD, axis=1)

    top, bot = [], []
    for d, c in pairs:
        lhs = jnp.concatenate([at_s[d, cs(c)], rt_s[d, cs(c)]], axis=0)
        rhs = jnp.concatenate([bhm_s[d, 0, cs(c)], khm_s[d, 0, cs(c)],
                               bhm_s[d, 1, cs(c)], khm_s[d, 1, cs(c)]], axis=0)
        sc = lax.dot_general(lhs, rhs, (((1,), (1,)), ((), ())), preferred_element_type=f32)
        for h in range(2):
            top.append(jnp.where(strict[d], sc[:CHUNK, h * PAIR:(h + 1) * PAIR], 0.0))
            bot.append(jnp.where(incl[d], sc[CHUNK:, h * PAIR:(h + 1) * PAIR], 0.0))
    yield CHUNK_STAGE
    xs, aps = [], []
    for i, (d, c, h) in enumerate(inst):
        vx = vx_s[h, cs(c)]
        a_ak = jnp.where(lo_c, 0.0, top[i]).astype(bf16)
        akv = jnp.dot(a_ak, jnp.concatenate([vx, vx], axis=0), preferred_element_type=f32)
        xs.append(atx_s[d, h, cs(c)].astype(f32) + akv)
        aps.append(top[i][:, :CHUNK])
    n_dbl = CHUNK.bit_length() - 1
    for it in range(n_dbl):
        yield CHUNK_STAGE
        last = it + 1 == n_dbl
        res = [_dot(ap, x if last else jnp.concatenate([x, ap], axis=1)) for ap, x in zip(aps, xs)]
        xs = [x + rs[:, :PAIR] for x, rs in zip(xs, res)]
        if not last:
            aps = [rs[:, PAIR:] for rs in res]
    yield CHUNK_STAGE
    rhs_o = [jnp.concatenate([xs[i].astype(bf16), vx_s[h, cs(c)]], axis=0) for i, (d, c, h) in enumerate(inst)]
    rys = [_dot(bot[i], rhs_o[i]) + rtx_s[d, h, cs(c)] for i, (d, c, h) in enumerate(inst)]
    yield CHUNK_STAGE
    mns = [_dot_tn(jnp.concatenate([bp_s[d, cs(c), h * HEAD:(h + 1) * HEAD],
                                    kp_s[d, cs(c), h * HEAD:(h + 1) * HEAD]], axis=0), rhs_o[i])
           for i, (d, c, h) in enumerate(inst)]
    yield CHUNK_STAGE
    for i, (d, c, h) in enumerate(inst):
        p_blk = pl_s[d, c * CHUNK:c * CHUNK + 8, :]
        p_row = (p_blk if h == 0 else swap_c(p_blk))[0:1]
        mn_ref[d, c, h] = (mns[i] + jnp.where(row == col, p_row, 0.0)).astype(bf16)
    y0_fwd = {}
    for j, (d, c) in enumerate(pairs):
        ry0, ry1 = rys[2 * j], rys[2 * j + 1]
        rp_ref[d, cs(c), :] = jnp.where(lo_c, ry0, swap_c(ry1)).astype(bf16)
        y0 = jnp.where(lo_c, swap_c(ry0), ry1)
        if d == 0:
            y0_fwd[c] = y0
        else:
            y0_ref[cs(c), :] = (y0_fwd[c] + y0).astype(y0_ref.dtype)
    bonus_ref[...] = bonus.astype(bonus_ref.dtype)


N_IN_P = 23
N_OUT_P = 4


def _rwkv_p_multi(*refs, tt, seq, pp):
    ins, outs, scr = refs[:N_IN_P], refs[N_IN_P:N_IN_P + N_OUT_P], refs[N_IN_P + N_OUT_P:]
    x_ref, x_pv, x_nx, mux_ref = ins[9], ins[10], ins[11], ins[15]
    w0_ref, wlu_ref, a0_ref, alu_ref = ins[16:20]
    shift = _token_shift(tt, seq)
    xs = shift(x_ref[...], x_pv[...], x_nx[...], mux_ref[...])
    zs = [w0_ref[d:d + 1, :] + _dot(jnp.tanh(xs[:, d * R_LORA:(d + 1) * R_LORA]), wlu_ref[d]) for d in range(2)]
    a_pre = [a0_ref[d:d + 1, :] + _dot(xs[:, (2 + d) * R_LORA:(3 + d) * R_LORA], alu_ref[d]) for d in range(2)]
    stages = []
    for q in range(pp):
        lanes = pl.ds(q * PAIR, PAIR)
        lsl = slice(q * PAIR, (q + 1) * PAIR)
        view = lambda ref, lanes=lanes: ref.at[..., lanes]
        planes = [view(ref) for ref in ins[0:9]]
        gains = [view(ref) for ref in ins[12:15]]
        params = [view(ref) for ref in ins[20:23]]
        y0_ref, bonus_ref, rp_ref, mn_ref = outs
        stages.append(_rwkv_p_body(*planes, *gains, *params,
                                   view(y0_ref), view(bonus_ref), view(rp_ref), mn_ref.at[:, :, pl.ds(2 * q, 2)],
                                   *[s.at[q] for s in scr], tt=tt, shift=shift,
                                   zs=[z[:, lsl] for z in zs], a_pre=[a[:, lsl] for a in a_pre]))
    def run_pre(stage):
        while next(stage) != PRE_DONE:
            pass

    run_pre(stages[0])
    for q in range(pp):
        chunk, pre = stages[q], (stages[q + 1] if q + 1 < pp else None)
        chunk_live, pre_live = True, pre is not None
        while chunk_live or pre_live:
            if chunk_live:
                chunk_live = next(chunk, None) is not None
            if pre_live:
                pre_live = next(pre) != PRE_DONE


def _segsum(x, seg, terms=2):
    x0 = x.astype(bf16)
    m = seg.astype(bf16)
    s = jnp.dot(x0, m, preferred_element_type=f32)
    if terms == 1:
        return s
    x1 = (x - x0.astype(f32)).astype(bf16)
    return s + jnp.dot(x1, m, preferred_element_type=f32)


def _rwkv_p(p8, px, mu_p, w0, wlu, a0, alu, k_k, k_a, r_k, *, tt, seq, pp):
    m = px.shape[0]
    nt = m // tt
    npair = D_MODEL // PAIR
    cpt = tt // CHUNK
    wl = pp * PAIR
    hb, hx = 16, 8

    def tiles(plane):
        return [pl.BlockSpec((1, tt, wl), lambda i, j: (plane, i, j)),
                pl.BlockSpec((1, hb, wl), lambda i, j: (plane, jnp.maximum(i * (tt // hb) - 1, 0), j)),
                pl.BlockSpec((1, hb, wl), lambda i, j: (plane, jnp.minimum((i + 1) * (tt // hb), m // hb - 1), j))]

    vec = pl.BlockSpec((1, wl), lambda i, j: (0, j))
    mu = lambda plane: pl.BlockSpec((1, wl), lambda i, j: (0, plane * (npair // pp) + j))
    wx = 4 * R_LORA
    return pl.pallas_call(
        functools.partial(_rwkv_p_multi, tt=tt, seq=seq, pp=pp),
        grid=(nt, npair // pp),
        in_specs=tiles(P_R) + tiles(P_K) + tiles(P_V) + [
            pl.BlockSpec((tt, wx), lambda i, j: (i, 0)),
            pl.BlockSpec((hx, wx), lambda i, j: (jnp.maximum(i * (tt // hx) - 1, 0), 0)),
            pl.BlockSpec((hx, wx), lambda i, j: (jnp.minimum((i + 1) * (tt // hx), m // hx - 1), 0)),
            mu(P_R), mu(P_K), mu(P_V),
            pl.BlockSpec((1, wx), lambda i, j: (0, P_X * D_MODEL // wx)),
            pl.BlockSpec((2, wl), lambda i, j: (0, j)),
            pl.BlockSpec((2, R_LORA, wl), lambda i, j: (0, 0, j)),
            pl.BlockSpec((2, wl), lambda i, j: (0, j)),
            pl.BlockSpec((2, R_LORA, wl), lambda i, j: (0, 0, j)),
            vec, vec, vec,
        ],
        out_specs=[
            pl.BlockSpec((tt, wl), lambda i, j: (i, j)),
            pl.BlockSpec((tt, wl), lambda i, j: (i, j)),
            pl.BlockSpec((2, tt, wl), lambda i, j: (0, i, j)),
            pl.BlockSpec((2, cpt, 2 * pp, HEAD, PAIR), lambda i, j: (0, i, j, 0, 0)),
        ],
        out_shape=[
            jax.ShapeDtypeStruct((m, D_MODEL), bf16),
            jax.ShapeDtypeStruct((m, D_MODEL), bf16),
            jax.ShapeDtypeStruct((2, m, D_MODEL), bf16),
            jax.ShapeDtypeStruct((2, m // CHUNK, N_HEADS, HEAD, PAIR), bf16),
        ],
        scratch_shapes=(
            [pltpu.VMEM((pp, 2, tt, PAIR), bf16) for _ in range(4)]
            + [pltpu.VMEM((pp, 2, 2, tt, PAIR), bf16) for _ in range(3)]
            + [pltpu.VMEM((pp, 2, 2, tt, PAIR), f32),
               pltpu.VMEM((pp, 2, tt, PAIR), bf16),
               pltpu.VMEM((pp, 2, tt, PAIR), f32)]),
        compiler_params=pltpu.CompilerParams(
            dimension_semantics=("parallel", "parallel"), vmem_limit_bytes=VMEM_LIMIT),
        name="rwkv_p",
    )(p8, p8, p8, p8, p8, p8, p8, p8, p8, px, px, px, mu_p, mu_p, mu_p, mu_p, w0, wlu, a0, alu, k_k, k_a, r_k)


def _rwkv_s_body(mnf_ref, mnb_ref, sf_ref, sb_ref, st_ref, *, grp):
    @pl.when(pl.program_id(1) == 0)
    def _():
        st_ref[...] = jnp.zeros_like(st_ref)

    zero = jnp.zeros((HEAD, HEAD), bf16)

    def step(d, mn_ref, s_out_ref, cc):
        s = st_ref[d].astype(bf16)
        for p in range(N_HEADS // 2):
            s_out_ref[cc, p] = jnp.concatenate([jnp.concatenate([s[2 * p], zero], axis=1),
                                                jnp.concatenate([zero, s[2 * p + 1]], axis=1)], axis=0)
        m_t = mn_ref[0, cc, :, :, 0:HEAD]
        n_t = mn_ref[0, cc, :, :, HEAD:].astype(f32)
        st_ref[d] = jnp.einsum('hij,hjk->hik', m_t, s, preferred_element_type=f32) + n_t

    for cc in range(grp):
        step(0, mnf_ref, sf_ref, cc)
        step(1, mnb_ref, sb_ref, grp - 1 - cc)


def _rwkv_s(mn, *, n_seq, grp):
    nchunk = mn.shape[1]
    npair = N_HEADS // 2
    ng = nchunk // n_seq // grp
    blk_in = (1, grp, N_HEADS, HEAD, PAIR)
    blk_out = (grp, npair, PAIR, PAIR)
    out = jax.ShapeDtypeStruct((nchunk, npair, PAIR, PAIR), bf16)
    return pl.pallas_call(
        functools.partial(_rwkv_s_body, grp=grp),
        grid=(n_seq, ng),
        in_specs=[pl.BlockSpec(blk_in, lambda b, g: (0, b * ng + g, 0, 0, 0)),
                  pl.BlockSpec(blk_in, lambda b, g: (1, b * ng + ng - 1 - g, 0, 0, 0))],
        out_specs=[pl.BlockSpec(blk_out, lambda b, g: (b * ng + g, 0, 0, 0)),
                   pl.BlockSpec(blk_out, lambda b, g: (b * ng + ng - 1 - g, 0, 0, 0))],
        out_shape=[out, out],
        scratch_shapes=[pltpu.VMEM((2, N_HEADS, HEAD, HEAD), f32)],
        compiler_params=pltpu.CompilerParams(
            dimension_semantics=("parallel", "arbitrary"), vmem_limit_bytes=VMEM_LIMIT),
        name="rwkv_s",
    )(mn, mn)


def _rwkv_f_body(y0_ref, bonus_ref, rp_ref, sf_ref, sb_ref, gate_ref, lng_ref, lnb_ref, o_ref, *, tt):
    npair = D_MODEL // PAIR
    nc = tt // CHUNK
    cs = lambda c: slice(c * CHUNK, (c + 1) * CHUNK)
    ps = lambda p: slice(p * PAIR, (p + 1) * PAIR)
    yf = [[jnp.dot(rp_ref[0, cs(c), ps(p)], sf_ref[c, p], preferred_element_type=f32) for c in range(nc)]
          for p in range(npair)]
    yb = [[jnp.dot(rp_ref[1, cs(c), ps(p)], sb_ref[c, p], preferred_element_type=f32) for c in range(nc)]
          for p in range(npair)]
    lane = lax.broadcasted_iota(jnp.int32, (PAIR, PAIR), 0) // HEAD
    seg = jnp.where(lane == lax.broadcasted_iota(jnp.int32, (PAIR, PAIR), 1) // HEAD, 1.0, 0.0)
    ys = [y0_ref[:, ps(p)].astype(f32) + jnp.concatenate(yf[p], axis=0) + jnp.concatenate(yb[p], axis=0) for p in range(npair)]
    mus = [_segsum(y, seg) * (1.0 / HEAD) for y in ys]
    dvs = [y - mu for y, mu in zip(ys, mus)]
    vrs = [_segsum(dv * dv, seg) * (1.0 / HEAD) for dv in dvs]
    for p in range(npair):
        yn = dvs[p] * lax.rsqrt(vrs[p] + GN_EPS) * lng_ref[:, ps(p)] + lnb_ref[:, ps(p)] + bonus_ref[:, ps(p)].astype(f32)
        g = gate_ref[0, :, ps(p)].astype(f32)
        o_ref[:, ps(p)] = (yn * (g * _sigmoid(g))).astype(bf16)


def _rwkv_f(y0, bonus, rp, sf, sb, p8, ln_g, ln_b, *, tt):
    m = y0.shape[0]
    cpt = tt // CHUNK
    npair = D_MODEL // PAIR
    tile = pl.BlockSpec((tt, D_MODEL), lambda i: (i, 0))
    vec = pl.BlockSpec((1, D_MODEL), lambda i: (0, 0))
    state = pl.BlockSpec((cpt, npair, PAIR, PAIR), lambda i: (i, 0, 0, 0))
    return pl.pallas_call(
        functools.partial(_rwkv_f_body, tt=tt),
        grid=(m // tt,),
        in_specs=[
            tile, tile,
            pl.BlockSpec((2, tt, D_MODEL), lambda i: (0, i, 0)),
            state, state,
            pl.BlockSpec((1, tt, D_MODEL), lambda i: (P_GA, i, 0)),
            vec, vec,
        ],
        out_specs=tile,
        out_shape=jax.ShapeDtypeStruct((m, D_MODEL), bf16),
        compiler_params=pltpu.CompilerParams(dimension_semantics=("parallel",), vmem_limit_bytes=VMEM_LIMIT),
        name="rwkv_f",
    )(y0, bonus, rp, sf, sb, p8, ln_g, ln_b)


def _attn_body(q_ref, kp_ref, kc_ref, kn_ref, vp_ref, vc_ref, vn_ref, tp_ref, tc_ref, tn_ref, gate_ref,
               qg_ref, kg_ref, sink_ref, o_ref, *, seq, qb):
    step_rows = qb * BLOCK
    start = (pl.program_id(0) % (seq // step_rows)) * step_rows
    ctx = BLOCK + 2 * WINDOW
    qoff = lax.broadcasted_iota(jnp.int32, (BLOCK, ctx), 0)
    koff = lax.broadcasted_iota(jnp.int32, (BLOCK, ctx), 1) - WINDOW
    valid = []
    for b in range(qb):
        kpos = start + b * BLOCK + koff
        valid.append((jnp.abs(qoff - koff) <= WINDOW) & (kpos >= 0) & (kpos < seq))

    lane_r = lax.broadcasted_iota(jnp.int32, (PAIR, PAIR), 0) // HEAD
    seg = jnp.where(lane_r == lax.broadcasted_iota(jnp.int32, (PAIR, PAIR), 1) // HEAD, 1.0, 0.0)
    swap = lambda t: pltpu.roll(t, HEAD, axis=1)

    def norm_rope(x, gain, tab):
        lane = lax.broadcasted_iota(jnp.int32, x.shape, 1)
        xn = x * lax.rsqrt(_segsum(x * x, seg, terms=1) * (1.0 / HEAD) + NORM_EPS) * gain
        rot = jnp.where(lane % HEAD < HEAD // 2, pltpu.roll(xn, PAIR - HEAD // 2, axis=1),
                        pltpu.roll(xn, HEAD // 2, axis=1))
        return xn * tab[:, :PAIR] + rot * tab[:, PAIR:]

    def place(t, e, want_lo):
        lo = lax.broadcasted_iota(jnp.int32, t.shape, 1) < HEAD
        src = t if (e == 0) == want_lo else swap(t)
        return (jnp.where(lo, src, 0.0) if want_lo else jnp.where(lo, 0.0, src)).astype(bf16)

    tab_k = jnp.concatenate([tp_ref[...], tc_ref[...], tn_ref[...]], axis=0)
    tab_q = tc_ref[...]
    k_lo, k_hi, v_lo, v_hi = [], [], [], []
    for kp in range(D_KV // PAIR):
        ps_ = slice(kp * PAIR, (kp + 1) * PAIR)
        kx = jnp.concatenate([kp_ref[:, ps_], kc_ref[:, ps_], kn_ref[:, ps_]], axis=0)
        vx = jnp.concatenate([vp_ref[:, ps_], vc_ref[:, ps_], vn_ref[:, ps_]], axis=0)
        kr = norm_rope(kx, kg_ref[...], tab_k)
        for e in range(2):
            k_lo.append(place(kr, e, True))
            k_hi.append(place(kr, e, False))
            v_lo.append(place(vx, e, True))
            v_hi.append(place(vx, e, False))
    npair = D_MODEL // PAIR
    psl = lambda p: slice(p * PAIR, (p + 1) * PAIR)
    log2e = math.log2(math.e)
    qrs = [(norm_rope(q_ref[0, :, psl(p)].astype(f32), qg_ref[...], tab_q) * (HEAD ** -0.5 * log2e)).astype(bf16)
           for p in range(npair)]
    inst = [(b, hq) for b in range(qb) for hq in range(N_HEADS)]
    rows = lambda b: slice(b * BLOCK, (b + 1) * BLOCK)
    keys = lambda b: slice(b * BLOCK, b * BLOCK + ctx)
    def masked(s, ok):
        return jnp.concatenate([jnp.where(ok[:, :WINDOW], s[:, :WINDOW], -1e30), s[:, WINDOW:WINDOW + BLOCK],
                                jnp.where(ok[:, WINDOW + BLOCK:], s[:, WINDOW + BLOCK:], -1e30)], axis=1)

    ss = [masked(lax.dot_general(qrs[hq // 2][rows(b)], (k_lo if hq % 2 == 0 else k_hi)[hq // GRP][keys(b)],
                                 (((1,), (1,)), ((), ())), preferred_element_type=f32), valid[b])
          for b, hq in inst]
    sinks = [sink_ref[0:1, hq:hq + 1] * log2e for b, hq in inst]
    mxs = [jnp.maximum(jnp.max(s, axis=-1, keepdims=True), sk) for s, sk in zip(ss, sinks)]
    ps = [jnp.exp2(s - mx) for s, mx in zip(ss, mxs)]
    dens = [jnp.sum(p, axis=-1, keepdims=True) + jnp.exp2(sk - mx) for p, sk, mx in zip(ps, sinks, mxs)]
    os_ = [jnp.dot(p.astype(bf16), (v_lo if hq % 2 == 0 else v_hi)[hq // GRP][keys(b)], preferred_element_type=f32) / den
           for (b, hq), p, den in zip(inst, ps, dens)]
    for b in range(qb):
        for p in range(npair):
            gt = gate_ref[0, rows(b), psl(p)].astype(f32)
            i0 = b * N_HEADS + 2 * p
            o_ref[rows(b), psl(p)] = ((os_[i0] + os_[i0 + 1]) * (gt * _sigmoid(gt))).astype(bf16)


def _attn(p8, px, rope_tab, q_g, k_g, sink, *, seq, qb):
    m = px.shape[0]
    nblk = seq // BLOCK
    rows = qb * BLOCK

    def nbr(delta):
        def f(i):
            t = (i * qb) % nblk
            return i * qb - t + jnp.clip(t + delta, 0, nblk - 1)
        return f

    def kv(col):
        return [pl.BlockSpec((BLOCK, D_KV), lambda i, f=nbr(-1): (f(i), col)),
                pl.BlockSpec((rows, D_KV), lambda i: (i, col)),
                pl.BlockSpec((BLOCK, D_KV), lambda i, f=nbr(qb): (f(i), col))]

    tabs = [pl.BlockSpec((BLOCK, 2 * PAIR), lambda i, f=nbr(-1): (f(i) % nblk, 0)),
            pl.BlockSpec((rows, 2 * PAIR), lambda i: (i % (nblk // qb), 0)),
            pl.BlockSpec((BLOCK, 2 * PAIR), lambda i, f=nbr(qb): (f(i) % nblk, 0))]

    return pl.pallas_call(
        functools.partial(_attn_body, seq=seq, qb=qb),
        grid=(m // rows,),
        in_specs=[pl.BlockSpec((1, rows, D_MODEL), lambda i: (P_Q, i, 0))] + kv(1) + kv(2) + tabs + [
            pl.BlockSpec((1, rows, D_MODEL), lambda i: (P_GB, i, 0)),
            pl.BlockSpec((1, PAIR), lambda i: (0, 0)),
            pl.BlockSpec((1, PAIR), lambda i: (0, 0)),
            pl.BlockSpec((1, N_HEADS), lambda i: (0, 0)),
        ],
        out_specs=pl.BlockSpec((rows, D_MODEL), lambda i: (i, 0)),
        out_shape=jax.ShapeDtypeStruct((m, D_MODEL), bf16),
        compiler_params=pltpu.CompilerParams(dimension_semantics=("parallel",), vmem_limit_bytes=VMEM_LIMIT),
        name="attn",
    )(p8, px, px, px, px, px, px, rope_tab, rope_tab, rope_tab, p8, q_g, k_g, sink)


def _rope_table(seq):
    inv = 1.0 / (ROPE_THETA ** (jnp.arange(0, HEAD, 2, dtype=f32) / HEAD))
    ang = jnp.arange(seq, dtype=f32)[:, None] * inv[None, :]
    cos, sin = jnp.cos(ang), jnp.sin(ang)
    return jnp.concatenate([cos, cos, cos, cos, -sin, sin, -sin, sin], axis=1)


def _out_proj_body(x_ref, ga_ref, gb_ref, ma_ref, mb_ref, wa_ref, wb_ref, wo_ref, o_ref):
    ya = jnp.dot(ga_ref[...], wa_ref[...], preferred_element_type=f32)
    yb = jnp.dot(gb_ref[...], wb_ref[...], preferred_element_type=f32)
    mixed = _sigmoid(ma_ref[0].astype(f32)) * ya + _sigmoid(mb_ref[0].astype(f32)) * yb
    o_ref[...] = x_ref[...] + jnp.dot(mixed.astype(bf16), wo_ref[...], preferred_element_type=f32)


def _out_proj(x2d, ga, gb, p8, wa, wb, wo, *, tm):
    m = x2d.shape[0]
    row = pl.BlockSpec((tm, D_MODEL), lambda i: (i, 0))
    wspec = pl.BlockSpec((D_MODEL, D_MODEL), lambda i: (0, 0))
    return pl.pallas_call(
        _out_proj_body,
        grid=(m // tm,),
        in_specs=[row, row, row,
                  pl.BlockSpec((1, tm, D_MODEL), lambda i: (P_MA, i, 0)),
                  pl.BlockSpec((1, tm, D_MODEL), lambda i: (P_MB, i, 0)),
                  wspec, wspec, wspec],
        out_specs=row,
        out_shape=jax.ShapeDtypeStruct((m, D_MODEL), f32),
        compiler_params=pltpu.CompilerParams(dimension_semantics=("parallel",), vmem_limit_bytes=VMEM_LIMIT),
        name="out_proj",
    )(x2d, ga, gb, p8, p8, wa, wb, wo)


def _layer(x2d, lw, rope_tab, *, n_seq, seq):
    tm = min(1024, seq)
    tt = min(256, seq)
    grp = min(8, seq // CHUNK)
    p8, px = _in_proj(x2d, lw["norm_g"], lw["w_p"], tm=min(2 * tm, seq))
    y0, bonus, rp, mn = _rwkv_p(p8, px, lw["mu_p"], lw["w0"], lw["wlu"], lw["a0"], lw["alu"], lw["k_k"], lw["k_a"],
                                lw["r_k"], tt=tt, seq=seq, pp=4)
    sf, sb = _rwkv_s(mn, n_seq=n_seq, grp=grp)
    ga = _rwkv_f(y0, bonus, rp, sf, sb, p8, lw["ln_g"], lw["ln_b"], tt=tt)
    gb = _attn(p8, px, rope_tab, lw["q_g"], lw["k_g"], lw["sink"], seq=seq, qb=2)
    return _out_proj(x2d, ga, gb, p8, lw["wa"], lw["wb"], lw["wo"], tm=tm)


def _trunk(x, layers):
    n_seq, seq, _ = x.shape
    rope_tab = _rope_table(seq)
    x2d = x.reshape(n_seq * seq, D_MODEL)
    for lw in layers:
        x2d = _layer(x2d, lw, rope_tab, n_seq=n_seq, seq=seq)
    return x2d.reshape(n_seq, seq, D_MODEL)


def _prep_layers(norm_g, w_in, shift_mu, w0, w_lora_up, a0, a_lora_up, k_k, k_a, r_k, ln_x_g, ln_x_b,
                 q_norm_g, k_norm_g, sink, w_proj_a, w_proj_b, w_out):
    layers = []
    row = lambda v: v.reshape(1, -1).astype(f32)
    for l in range(norm_g.shape[0]):
        w_p, mu_p = _prep_layer_weights(w_in[l], shift_mu[l])
        layers.append(dict(
            norm_g=row(norm_g[l]), w_p=w_p, mu_p=mu_p,
            w0=w0[l].astype(f32), wlu=w_lora_up[l].astype(bf16), a0=a0[l].astype(f32), alu=a_lora_up[l].astype(bf16),
            k_k=row(k_k[l]), k_a=row(k_a[l]), r_k=row(r_k[l]), ln_g=row(ln_x_g[l]), ln_b=row(ln_x_b[l]),
            q_g=row(jnp.tile(q_norm_g[l], 2)), k_g=row(jnp.tile(k_norm_g[l], 2)), sink=row(sink[l]),
            wa=w_proj_a[l].astype(bf16), wb=w_proj_b[l].astype(bf16), wo=w_out[l].astype(bf16)))
    return layers


def kernel(x_prompt, x_sample, norm_g, w_in, shift_mu, w0, w_lora_up, a0, a_lora_up, k_k, k_a, r_k,
           ln_x_g, ln_x_b, q_norm_g, k_norm_g, sink, w_proj_a, w_proj_b, w_out):
    layers = _prep_layers(norm_g, w_in, shift_mu, w0, w_lora_up, a0, a_lora_up, k_k, k_a, r_k, ln_x_g, ln_x_b,
                          q_norm_g, k_norm_g, sink, w_proj_a, w_proj_b, w_out)
    return _trunk(x_prompt, layers), _trunk(x_sample, layers)
```

```python
import functools
import math

import jax
import jax.numpy as jnp
from jax import lax
from jax.experimental import pallas as pl
from jax.experimental.pallas import tpu as pltpu

D_MODEL = 1024
HEAD = 64
N_HEADS = D_MODEL // HEAD
R_LORA = 64
HKV = 4
GRP = N_HEADS // HKV
D_KV = HKV * HEAD
WINDOW = 128
BLOCK = 128
GN_EPS = 64e-5
NORM_EPS = 1e-6
ROPE_THETA = 10000.0
C_SHIFT = 3 * D_MODEL + 4 * R_LORA
N_PLANES = 9
CHUNK = 64
VMEM_LIMIT = 56 * 1024 * 1024

P_R, P_K, P_V, P_GA, P_Q, P_GB, P_MA, P_MB, P_X = range(9)

f32 = jnp.float32
bf16 = jnp.bfloat16


def _dot(a, b):
    return jnp.dot(a.astype(bf16), b.astype(bf16), preferred_element_type=f32)


def _dot_nt(a, b):
    return lax.dot_general(a.astype(bf16), b.astype(bf16), (((1,), (1,)), ((), ())),
                           preferred_element_type=f32)


def _dot_tn(a, b):
    return lax.dot_general(a.astype(bf16), b.astype(bf16), (((0,), (0,)), ((), ())),
                           preferred_element_type=f32)


def _dot_split(a_exact, b):
    b0 = b.astype(bf16)
    b1 = (b - b0.astype(f32)).astype(bf16)
    a = a_exact.astype(bf16)
    return jnp.dot(a, b0, preferred_element_type=f32) + jnp.dot(a, b1, preferred_element_type=f32)


def _sigmoid(x):
    return 1.0 / (1.0 + jnp.exp(-x))


def _in_proj_body(x_ref, g_ref, w_ref, o_ref, ox_ref, h_ref):
    j = pl.program_id(1)

    @pl.when(j == 0)
    def _():
        x = x_ref[...]
        ms = jnp.mean(x * x, axis=-1, keepdims=True)
        h_ref[...] = (x * lax.rsqrt(ms + NORM_EPS) * g_ref[...]).astype(bf16)

    @pl.when(j < P_X)
    def _():
        o_ref[0] = jnp.dot(h_ref[...], w_ref[...], preferred_element_type=f32).astype(bf16)

    @pl.when(j == P_X)
    def _():
        ox_ref[...] = jnp.dot(h_ref[...], w_ref[...], preferred_element_type=f32)


def _in_proj(x2d, norm_g, w_p, *, tm):
    m = x2d.shape[0]
    return pl.pallas_call(
        _in_proj_body,
        grid=(m // tm, N_PLANES),
        in_specs=[
            pl.BlockSpec((tm, D_MODEL), lambda i, j: (i, 0)),
            pl.BlockSpec((1, D_MODEL), lambda i, j: (0, 0)),
            pl.BlockSpec((D_MODEL, D_MODEL), lambda i, j: (0, j)),
        ],
        out_specs=[pl.BlockSpec((1, tm, D_MODEL), lambda i, j: (jnp.minimum(j, P_X - 1), i, 0)),
                   pl.BlockSpec((tm, D_MODEL), lambda i, j: (i, 0))],
        out_shape=[jax.ShapeDtypeStruct((N_PLANES - 1, m, D_MODEL), bf16),
                   jax.ShapeDtypeStruct((m, D_MODEL), f32)],
        scratch_shapes=[pltpu.VMEM((tm, D_MODEL), bf16)],
        compiler_params=pltpu.CompilerParams(
            dimension_semantics=("parallel", "arbitrary"), vmem_limit_bytes=VMEM_LIMIT),
        name="in_proj",
    )(x2d, norm_g, w_p)


def _prep_layer_weights(w_in, shift_mu):
    c = C_SHIFT
    d = D_MODEL
    cols = [
        (0, d), (d, 2 * d), (2 * d, 3 * d),
        (c, c + d),
        (c + d, c + 2 * d),
        (c + 2 * d + 2 * D_KV, c + 3 * d + 2 * D_KV),
        (c + 3 * d + 2 * D_KV, c + 4 * d + 2 * D_KV),
        (c + 4 * d + 2 * D_KV, c + 5 * d + 2 * D_KV),
    ]
    planes = [w_in[:, a:b] for a, b in cols]
    small = jnp.concatenate([w_in[:, 3 * d:c], w_in[:, c + 2 * d:c + 2 * d + 2 * D_KV],
                             jnp.zeros((d, d - 4 * R_LORA - 2 * D_KV), w_in.dtype)], axis=1)
    w_p = jnp.concatenate(planes + [small], axis=1).astype(bf16)
    zeros = jnp.zeros((d,), f32)
    mu_small = jnp.concatenate([shift_mu[3 * d:c], jnp.zeros((d - 4 * R_LORA,), f32)])
    mu_p = jnp.concatenate([shift_mu[0:d], shift_mu[d:2 * d], shift_mu[2 * d:3 * d]] + [zeros] * 5 + [mu_small])
    return w_p, mu_p.reshape(1, N_PLANES * d)


PAIR = 2 * HEAD
PRE, PRE_DONE, CHUNK_STAGE = "pre", "pre_done", "chunk"


def _token_shift(tt, seq):
    i = pl.program_id(0)
    first = (i * tt) % seq == 0
    last = ((i + 1) * tt) % seq == 0
    trow = lax.broadcasted_iota(jnp.int32, (tt, tt), 0)
    tcol = lax.broadcasted_iota(jnp.int32, (tt, tt), 1)
    nbr_mat = jnp.where(jnp.abs(trow - tcol) == 1, 0.5, 0.0)
    sub = lax.broadcasted_iota(jnp.int32, (8, 1), 0)

    def shift(main, pv, nx, mu):
        c = main.astype(f32)
        nbr = _dot(nbr_mat, main) if main.dtype == bf16 else _dot_split(nbr_mat, c)
        up = jnp.where(first, 0.0, pv[pv.shape[0] - 1:, :].astype(f32))
        dn = jnp.where(last, 0.0, nx[:1, :].astype(f32))
        head = nbr[:8] + jnp.where(sub == 0, 0.5 * up, 0.0)
        tail = nbr[tt - 8:] + jnp.where(sub == 7, 0.5 * dn, 0.0)
        nbr = jnp.concatenate([head, nbr[8:tt - 8], tail], axis=0)
        return c + mu * (nbr - c)

    return shift


def _rwkv_p_body(r_ref, r_pv, r_nx, k_ref, k_pv, k_nx, v_ref, v_pv, v_nx, mur_ref, muk_ref, muv_ref,
                 kk_ref, ka_ref, rk_ref,
                 y0_ref, bonus_ref, rp_ref, mn_ref,
                 at_s, rt_s, kp_s, bp_s, khm_s, bhm_s, atx_s, rtx_s, vx_s, pl_s, *, tt, shift, zs, a_pre):
    nc = tt // CHUNK
    r = shift(r_ref[0], r_pv[0], r_nx[0], mur_ref[...])
    k = shift(k_ref[0], k_pv[0], k_nx[0], muk_ref[...])
    v = shift(v_ref[0], v_pv[0], v_nx[0], muv_ref[...])
    yield PRE

    lane = lax.broadcasted_iota(jnp.int32, (PAIR, PAIR), 0) // HEAD
    seg = jnp.where(lane == lax.broadcasted_iota(jnp.int32, (PAIR, PAIR), 1) // HEAD, 1.0, 0.0)
    lo = lax.broadcasted_iota(jnp.int32, (tt, PAIR), 1) < HEAD
    swap = lambda t: pltpu.roll(t, HEAD, axis=1)

    def head_lo(t, h):
        return jnp.where(lo, t if h == 0 else swap(t), 0.0)

    def head_hi(t, h):
        return jnp.where(lo, 0.0, swap(t) if h == 0 else t)

    def head_own(t, h):
        return jnp.where(lo, t, 0.0) if h == 0 else jnp.where(lo, 0.0, t)

    for h in range(2):
        vx_s[h] = head_hi(v, h).astype(bf16)

    kkv = k * kk_ref[...]
    kk = kkv / jnp.maximum(jnp.sqrt(_segsum(kkv * kkv, seg)), 1e-12)
    yield PRE

    rows = lax.broadcasted_iota(jnp.int32, (tt, tt), 0)
    cols = lax.broadcasted_iota(jnp.int32, (tt, tt), 1)
    same = (rows // CHUNK) == (cols // CHUNK)

    kmod_sum = jnp.zeros_like(k)
    for d in range(2):
        if d == 0:
            tri_bd = jnp.where(same & (cols <= rows), 1.0, 0.0)
        else:
            tri_bd = jnp.where(same & (cols >= rows), 1.0, 0.0)
        lw = -math.exp(-0.5) * _sigmoid(zs[d])
        a = _sigmoid(a_pre[d])
        yield PRE
        kmod = k * (1.0 + (a - 1.0) * ka_ref[...])
        kmod_sum = kmod_sum + kmod
        b = -(kk * a)
        cum = _dot_split(tri_bd, lw)
        edge = CHUNK - 1 if d == 0 else 0
        tot = jnp.concatenate([jnp.broadcast_to(cum[c * CHUNK + edge:c * CHUNK + edge + 1, :], (CHUNK, PAIR))
                               for c in range(nc)], axis=0)
        yield PRE
        p_inv = jnp.exp(-cum)
        p_end = jnp.exp(tot - cum)
        at = kk * jnp.exp(cum - lw)
        rt = r * jnp.exp(cum)
        kh = kmod * p_inv
        bh = b * p_inv
        p_tot = jnp.exp(tot)
        at_s[d] = at.astype(bf16)
        rt_s[d] = rt.astype(bf16)
        pl_s[d] = p_tot
        yield PRE
        kp_s[d] = (kmod * p_end).astype(bf16)
        bp_s[d] = (b * p_end).astype(bf16)
        for h in range(2):
            khm_s[d, h] = head_own(kh, h).astype(bf16)
            bhm_s[d, h] = head_own(bh, h).astype(bf16)
            atx_s[d, h] = head_lo(at, h).astype(bf16)
            rtx_s[d, h] = head_lo(rt, h)
        yield PRE

    bonus = _segsum(r * kmod_sum * rk_ref[...], seg) * v
    yield PRE_DONE

    row = lax.broadcasted_iota(jnp.int32, (CHUNK, PAIR), 0)
    col = lax.broadcasted_iota(jnp.int32, (CHUNK, PAIR), 1)
    col_t = col % CHUNK
    lo_c = col < HEAD
    strict = (col_t < row, col_t > row)
    incl = (col_t <= row, col_t >= row)
    pairs = [(d, c) for d in range(2) for c in range(nc)]
    inst = [(d, c, h) for d, c in pairs for h in range(2)]
    cs = lambda c: slice(c * CHUNK, (c + 1) * CHUNK)
    swap_c = lambda t: pltpu.roll(t, HEAD, axis=1)

    top, bot = [], []
    for d, c in pairs:
        lhs = jnp.concatenate([at_s[d, cs(c)], rt_s[d, cs(c)]], axis=0)
        rhs = jnp.concatenate([bhm_s[d, 0, cs(c)], khm_s[d, 0, cs(c)],
                               bhm_s[d, 1, cs(c)], khm_s[d, 1, cs(c)]], axis=0)
        sc = lax.dot_general(lhs, rhs, (((1,), (1,)), ((), ())), preferred_element_type=f32)
        for h in range(2):
            top.append(jnp.where(strict[d], sc[:CHUNK, h * PAIR:(h + 1) * PAIR], 0.0))
            bot.append(jnp.where(incl[d], sc[CHUNK:, h * PAIR:(h + 1) * PAIR], 0.0))
    yield CHUNK_STAGE
    xs, aps = [], []
    for i, (d, c, h) in enumerate(inst):
        vx = vx_s[h, cs(c)]
        a_ak = jnp.where(lo_c, 0.0, top[i]).astype(bf16)
        akv = jnp.dot(a_ak, jnp.concatenate([vx, vx], axis=0), preferred_element_type=f32)
        xs.append(atx_s[d, h, cs(c)].astype(f32) + akv)
        aps.append(top[i][:, :CHUNK])
    n_dbl = CHUNK.bit_length() - 1
    for it in range(n_dbl):
        yield CHUNK_STAGE
        last = it + 1 == n_dbl
        res = [_dot(ap, x if last else jnp.concatenate([x, ap], axis=1)) for ap, x in zip(aps, xs)]
        xs = [x + rs[:, :PAIR] for x, rs in zip(xs, res)]
        if not last:
            aps = [rs[:, PAIR:] for rs in res]
    yield CHUNK_STAGE
    rhs_o = [jnp.concatenate([xs[i].astype(bf16), vx_s[h, cs(c)]], axis=0) for i, (d, c, h) in enumerate(inst)]
    rys = [_dot(bot[i], rhs_o[i]) + rtx_s[d, h, cs(c)] for i, (d, c, h) in enumerate(inst)]
    yield CHUNK_STAGE
    mns = [_dot_tn(jnp.concatenate([bp_s[d, cs(c), h * HEAD:(h + 1) * HEAD],
                                    kp_s[d, cs(c), h * HEAD:(h + 1) * HEAD]], axis=0), rhs_o[i])
           for i, (d, c, h) in enumerate(inst)]
    yield CHUNK_STAGE
    for i, (d, c, h) in enumerate(inst):
        p_blk = pl_s[d, c * CHUNK:c * CHUNK + 8, :]
        p_row = (p_blk if h == 0 else swap_c(p_blk))[0:1]
        mn_ref[d, c, h] = (mns[i] + jnp.where(row == col, p_row, 0.0)).astype(bf16)
    y0_fwd = {}
    for j, (d, c) in enumerate(pairs):
        ry0, ry1 = rys[2 * j], rys[2 * j + 1]
        rp_ref[d, cs(c), :] = jnp.where(lo_c, ry0, swap_c(ry1)).astype(bf16)
        y0 = jnp.where(lo_c, swap_c(ry0), ry1)
        if d == 0:
            y0_fwd[c] = y0
        else:
            y0_ref[cs(c), :] = (y0_fwd[c] + y0).astype(y0_ref.dtype)
    bonus_ref[...] = bonus.astype(bonus_ref.dtype)


N_IN_P = 23
N_OUT_P = 4


def _rwkv_p_multi(*refs, tt, seq, pp):
    ins, outs, scr = refs[:N_IN_P], refs[N_IN_P:N_IN_P + N_OUT_P], refs[N_IN_P + N_OUT_P:]
    x_ref, x_pv, x_nx, mux_ref = ins[9], ins[10], ins[11], ins[15]
    w0_ref, wlu_ref, a0_ref, alu_ref = ins[16:20]
    shift = _token_shift(tt, seq)
    xs = shift(x_ref[...], x_pv[...], x_nx[...], mux_ref[...])
    zs = [w0_ref[d:d + 1, :] + _dot(jnp.tanh(xs[:, d * R_LORA:(d + 1) * R_LORA]), wlu_ref[d]) for d in range(2)]
    a_pre = [a0_ref[d:d + 1, :] + _dot(xs[:, (2 + d) * R_LORA:(3 + d) * R_LORA], alu_ref[d]) for d in range(2)]
    stages = []
    for q in range(pp):
        lanes = pl.ds(q * PAIR, PAIR)
        lsl = slice(q * PAIR, (q + 1) * PAIR)
        view = lambda ref, lanes=lanes: ref.at[..., lanes]
        planes = [view(ref) for ref in ins[0:9]]
        gains = [view(ref) for ref in ins[12:15]]
        params = [view(ref) for ref in ins[20:23]]
        y0_ref, bonus_ref, rp_ref, mn_ref = outs
        stages.append(_rwkv_p_body(*planes, *gains, *params,
                                   view(y0_ref), view(bonus_ref), view(rp_ref), mn_ref.at[:, :, pl.ds(2 * q, 2)],
                                   *[s.at[q] for s in scr], tt=tt, shift=shift,
                                   zs=[z[:, lsl] for z in zs], a_pre=[a[:, lsl] for a in a_pre]))
    def run_pre(stage):
        while next(stage) != PRE_DONE:
            pass

    run_pre(stages[0])
    for q in range(pp):
        chunk, pre = stages[q], (stages[q + 1] if q + 1 < pp else None)
        chunk_live, pre_live = True, pre is not None
        while chunk_live or pre_live:
            if chunk_live:
                chunk_live = next(chunk, None) is not None
            if pre_live:
                pre_live = next(pre) != PRE_DONE


def _segsum(x, seg, terms=2):
    x0 = x.astype(bf16)
    m = seg.astype(bf16)
    s = jnp.dot(x0, m, preferred_element_type=f32)
    if terms == 1:
        return s
    x1 = (x - x0.astype(f32)).astype(bf16)
    return s + jnp.dot(x1, m, preferred_element_type=f32)


def _rwkv_p(p8, px, mu_p, w0, wlu, a0, alu, k_k, k_a, r_k, *, tt, seq, pp):
    m = px.shape[0]
    nt = m // tt
    npair = D_MODEL // PAIR
    cpt = tt // CHUNK
    wl = pp * PAIR
    hb, hx = 16, 8

    def tiles(plane):
        return [pl.BlockSpec((1, tt, wl), lambda i, j: (plane, i, j)),
                pl.BlockSpec((1, hb, wl), lambda i, j: (plane, jnp.maximum(i * (tt // hb) - 1, 0), j)),
                pl.BlockSpec((1, hb, wl), lambda i, j: (plane, jnp.minimum((i + 1) * (tt // hb), m // hb - 1), j))]

    vec = pl.BlockSpec((1, wl), lambda i, j: (0, j))
    mu = lambda plane: pl.BlockSpec((1, wl), lambda i, j: (0, plane * (npair // pp) + j))
    wx = 4 * R_LORA
    return pl.pallas_call(
        functools.partial(_rwkv_p_multi, tt=tt, seq=seq, pp=pp),
        grid=(nt, npair // pp),
        in_specs=tiles(P_R) + tiles(P_K) + tiles(P_V) + [
            pl.BlockSpec((tt, wx), lambda i, j: (i, 0)),
            pl.BlockSpec((hx, wx), lambda i, j: (jnp.maximum(i * (tt // hx) - 1, 0), 0)),
            pl.BlockSpec((hx, wx), lambda i, j: (jnp.minimum((i + 1) * (tt // hx), m // hx - 1), 0)),
            mu(P_R), mu(P_K), mu(P_V),
            pl.BlockSpec((1, wx), lambda i, j: (0, P_X * D_MODEL // wx)),
            pl.BlockSpec((2, wl), lambda i, j: (0, j)),
            pl.BlockSpec((2, R_LORA, wl), lambda i, j: (0, 0, j)),
            pl.BlockSpec((2, wl), lambda i, j: (0, j)),
            pl.BlockSpec((2, R_LORA, wl), lambda i, j: (0, 0, j)),
            vec, vec, vec,
        ],
        out_specs=[
            pl.BlockSpec((tt, wl), lambda i, j: (i, j)),
            pl.BlockSpec((tt, wl), lambda i, j: (i, j)),
            pl.BlockSpec((2, tt, wl), lambda i, j: (0, i, j)),
            pl.BlockSpec((2, cpt, 2 * pp, HEAD, PAIR), lambda i, j: (0, i, j, 0, 0)),
        ],
        out_shape=[
            jax.ShapeDtypeStruct((m, D_MODEL), bf16),
            jax.ShapeDtypeStruct((m, D_MODEL), bf16),
            jax.ShapeDtypeStruct((2, m, D_MODEL), bf16),
            jax.ShapeDtypeStruct((2, m // CHUNK, N_HEADS, HEAD, PAIR), bf16),
        ],
        scratch_shapes=(
            [pltpu.VMEM((pp, 2, tt, PAIR), bf16) for _ in range(4)]
            + [pltpu.VMEM((pp, 2, 2, tt, PAIR), bf16) for _ in range(3)]
            + [pltpu.VMEM((pp, 2, 2, tt, PAIR), f32),
               pltpu.VMEM((pp, 2, tt, PAIR), bf16),
               pltpu.VMEM((pp, 2, tt, PAIR), f32)]),
        compiler_params=pltpu.CompilerParams(
            dimension_semantics=("parallel", "parallel"), vmem_limit_bytes=VMEM_LIMIT),
        name="rwkv_p",
    )(p8, p8, p8, p8, p8, p8, p8, p8, p8, px, px, px, mu_p, mu_p, mu_p, mu_p, w0, wlu, a0, alu, k_k, k_a, r_k)


def _rwkv_s_body(mnf_ref, mnb_ref, sf_ref, sb_ref, st_ref, *, grp):
    @pl.when(pl.program_id(1) == 0)
    def _():
        st_ref[...] = jnp.zeros_like(st_ref)

    zero = jnp.zeros((HEAD, HEAD), bf16)

    def step(d, mn_ref, s_out_ref, cc):
        s = st_ref[d].astype(bf16)
        for p in range(N_HEADS // 2):
            s_out_ref[cc, p] = jnp.concatenate([jnp.concatenate([s[2 * p], zero], axis=1),
                                                jnp.concatenate([zero, s[2 * p + 1]], axis=1)], axis=0)
        m_t = mn_ref[0, cc, :, :, 0:HEAD]
        n_t = mn_ref[0, cc, :, :, HEAD:].astype(f32)
        st_ref[d] = jnp.einsum('hij,hjk->hik', m_t, s, preferred_element_type=f32) + n_t

    for cc in range(grp):
        step(0, mnf_ref, sf_ref, cc)
        step(1, mnb_ref, sb_ref, grp - 1 - cc)


def _rwkv_s(mn, *, n_seq, grp):
    nchunk = mn.shape[1]
    npair = N_HEADS // 2
    ng = nchunk // n_seq // grp
    blk_in = (1, grp, N_HEADS, HEAD, PAIR)
    blk_out = (grp, npair, PAIR, PAIR)
    out = jax.ShapeDtypeStruct((nchunk, npair, PAIR, PAIR), bf16)
    return pl.pallas_call(
        functools.partial(_rwkv_s_body, grp=grp),
        grid=(n_seq, ng),
        in_specs=[pl.BlockSpec(blk_in, lambda b, g: (0, b * ng + g, 0, 0, 0)),
                  pl.BlockSpec(blk_in, lambda b, g: (1, b * ng + ng - 1 - g, 0, 0, 0))],
        out_specs=[pl.BlockSpec(blk_out, lambda b, g: (b * ng + g, 0, 0, 0)),
                   pl.BlockSpec(blk_out, lambda b, g: (b * ng + ng - 1 - g, 0, 0, 0))],
        out_shape=[out, out],
        scratch_shapes=[pltpu.VMEM((2, N_HEADS, HEAD, HEAD), f32)],
        compiler_params=pltpu.CompilerParams(
            dimension_semantics=("parallel", "arbitrary"), vmem_limit_bytes=VMEM_LIMIT),
        name="rwkv_s",
    )(mn, mn)


def _rwkv_f_body(y0_ref, bonus_ref, rp_ref, sf_ref, sb_ref, gate_ref, lng_ref, lnb_ref, o_ref, *, tt):
    npair = D_MODEL // PAIR
    nc = tt // CHUNK
    cs = lambda c: slice(c * CHUNK, (c + 1) * CHUNK)
    ps = lambda p: slice(p * PAIR, (p + 1) * PAIR)
    yf = [[jnp.dot(rp_ref[0, cs(c), ps(p)], sf_ref[c, p], preferred_element_type=f32) for c in range(nc)]
          for p in range(npair)]
    yb = [[jnp.dot(rp_ref[1, cs(c), ps(p)], sb_ref[c, p], preferred_element_type=f32) for c in range(nc)]
          for p in range(npair)]
    lane = lax.broadcasted_iota(jnp.int32, (PAIR, PAIR), 0) // HEAD
    seg = jnp.where(lane == lax.broadcasted_iota(jnp.int32, (PAIR, PAIR), 1) // HEAD, 1.0, 0.0)
    ys = [y0_ref[:, ps(p)].astype(f32) + jnp.concatenate(yf[p], axis=0) + jnp.concatenate(yb[p], axis=0) for p in range(npair)]
    mus = [_segsum(y, seg) * (1.0 / HEAD) for y in ys]
    dvs = [y - mu for y, mu in zip(ys, mus)]
    vrs = [_segsum(dv * dv, seg) * (1.0 / HEAD) for dv in dvs]
    for p in range(npair):
        yn = dvs[p] * lax.rsqrt(vrs[p] + GN_EPS) * lng_ref[:, ps(p)] + lnb_ref[:, ps(p)] + bonus_ref[:, ps(p)].astype(f32)
        g = gate_ref[0, :, ps(p)].astype(f32)
        o_ref[:, ps(p)] = (yn * (g * _sigmoid(g))).astype(bf16)


def _rwkv_f(y0, bonus, rp, sf, sb, p8, ln_g, ln_b, *, tt):
    m = y0.shape[0]
    cpt = tt // CHUNK
    npair = D_MODEL // PAIR
    tile = pl.BlockSpec((tt, D_MODEL), lambda i: (i, 0))
    vec = pl.BlockSpec((1, D_MODEL), lambda i: (0, 0))
    state = pl.BlockSpec((cpt, npair, PAIR, PAIR), lambda i: (i, 0, 0, 0))
    return pl.pallas_call(
        functools.partial(_rwkv_f_body, tt=tt),
        grid=(m // tt,),
        in_specs=[
            tile, tile,
            pl.BlockSpec((2, tt, D_MODEL), lambda i: (0, i, 0)),
            state, state,
            pl.BlockSpec((1, tt, D_MODEL), lambda i: (P_GA, i, 0)),
            vec, vec,
        ],
        out_specs=tile,
        out_shape=jax.ShapeDtypeStruct((m, D_MODEL), bf16),
        compiler_params=pltpu.CompilerParams(dimension_semantics=("parallel",), vmem_limit_bytes=VMEM_LIMIT),
        name="rwkv_f",
    )(y0, bonus, rp, sf, sb, p8, ln_g, ln_b)


def _attn_body(q_ref, kp_ref, kc_ref, kn_ref, vp_ref, vc_ref, vn_ref, tp_ref, tc_ref, tn_ref, gate_ref,
               qg_ref, kg_ref, sink_ref, o_ref, *, seq, qb):
    step_rows = qb * BLOCK
    start = (pl.program_id(0) % (seq // step_rows)) * step_rows
    ctx = BLOCK + 2 * WINDOW
    qoff = lax.broadcasted_iota(jnp.int32, (BLOCK, ctx), 0)
    koff = lax.broadcasted_iota(jnp.int32, (BLOCK, ctx), 1) - WINDOW
    valid = []
    for b in range(qb):
        kpos = start + b * BLOCK + koff
        valid.append((jnp.abs(qoff - koff) <= WINDOW) & (kpos >= 0) & (kpos < seq))

    lane_r = lax.broadcasted_iota(jnp.int32, (PAIR, PAIR), 0) // HEAD
    seg = jnp.where(lane_r == lax.broadcasted_iota(jnp.int32, (PAIR, PAIR), 1) // HEAD, 1.0, 0.0)
    swap = lambda t: pltpu.roll(t, HEAD, axis=1)

    def norm_rope(x, gain, tab):
        lane = lax.broadcasted_iota(jnp.int32, x.shape, 1)
        xn = x * lax.rsqrt(_segsum(x * x, seg, terms=1) * (1.0 / HEAD) + NORM_EPS) * gain
        rot = jnp.where(lane % HEAD < HEAD // 2, pltpu.roll(xn, PAIR - HEAD // 2, axis=1),
                        pltpu.roll(xn, HEAD // 2, axis=1))
        return xn * tab[:, :PAIR] + rot * tab[:, PAIR:]

    def place(t, e, want_lo):
        lo = lax.broadcasted_iota(jnp.int32, t.shape, 1) < HEAD
        src = t if (e == 0) == want_lo else swap(t)
        return (jnp.where(lo, src, 0.0) if want_lo else jnp.where(lo, 0.0, src)).astype(bf16)

    tab_k = jnp.concatenate([tp_ref[...], tc_ref[...], tn_ref[...]], axis=0)
    tab_q = tc_ref[...]
    k_lo, k_hi, v_lo, v_hi = [], [], [], []
    for kp in range(D_KV // PAIR):
        ps_ = slice(kp * PAIR, (kp + 1) * PAIR)
        kx = jnp.concatenate([kp_ref[:, ps_], kc_ref[:, ps_], kn_ref[:, ps_]], axis=0)
        vx = jnp.concatenate([vp_ref[:, ps_], vc_ref[:, ps_], vn_ref[:, ps_]], axis=0)
        kr = norm_rope(kx, kg_ref[...], tab_k)
        for e in range(2):
            k_lo.append(place(kr, e, True))
            k_hi.append(place(kr, e, False))
            v_lo.append(place(vx, e, True))
            v_hi.append(place(vx, e, False))
    npair = D_MODEL // PAIR
    psl = lambda p: slice(p * PAIR, (p + 1) * PAIR)
    log2e = math.log2(math.e)
    qrs = [(norm_rope(q_ref[0, :, psl(p)].astype(f32), qg_ref[...], tab_q) * (HEAD ** -0.5 * log2e)).astype(bf16)
           for p in range(npair)]
    inst = [(b, hq) for b in range(qb) for hq in range(N_HEADS)]
    rows = lambda b: slice(b * BLOCK, (b + 1) * BLOCK)
    keys = lambda b: slice(b * BLOCK, b * BLOCK + ctx)
    def masked(s, ok):
        return jnp.concatenate([jnp.where(ok[:, :WINDOW], s[:, :WINDOW], -1e30), s[:, WINDOW:WINDOW + BLOCK],
                                jnp.where(ok[:, WINDOW + BLOCK:], s[:, WINDOW + BLOCK:], -1e30)], axis=1)

    ss = [masked(lax.dot_general(qrs[hq // 2][rows(b)], (k_lo if hq % 2 == 0 else k_hi)[hq // GRP][keys(b)],
                                 (((1,), (1,)), ((), ())), preferred_element_type=f32), valid[b])
          for b, hq in inst]
    sinks = [sink_ref[0:1, hq:hq + 1] * log2e for b, hq in inst]
    mxs = [jnp.maximum(jnp.max(s, axis=-1, keepdims=True), sk) for s, sk in zip(ss, sinks)]
    ps = [jnp.exp2(s - mx) for s, mx in zip(ss, mxs)]
    dens = [jnp.sum(p, axis=-1, keepdims=True) + jnp.exp2(sk - mx) for p, sk, mx in zip(ps, sinks, mxs)]
    os_ = [jnp.dot(p.astype(bf16), (v_lo if hq % 2 == 0 else v_hi)[hq // GRP][keys(b)], preferred_element_type=f32) / den
           for (b, hq), p, den in zip(inst, ps, dens)]
    for b in range(qb):
        for p in range(npair):
            gt = gate_ref[0, rows(b), psl(p)].astype(f32)
            i0 = b * N_HEADS + 2 * p
            o_ref[rows(b), psl(p)] = ((os_[i0] + os_[i0 + 1]) * (gt * _sigmoid(gt))).astype(bf16)


def _attn(p8, px, rope_tab, q_g, k_g, sink, *, seq, qb):
    m = px.shape[0]
    nblk = seq // BLOCK
    rows = qb * BLOCK

    def nbr(delta):
        def f(i):
            t = (i * qb) % nblk
            return i * qb - t + jnp.clip(t + delta, 0, nblk - 1)
        return f

    def kv(col):
        return [pl.BlockSpec((BLOCK, D_KV), lambda i, f=nbr(-1): (f(i), col)),
                pl.BlockSpec((rows, D_KV), lambda i: (i, col)),
                pl.BlockSpec((BLOCK, D_KV), lambda i, f=nbr(qb): (f(i), col))]

    tabs = [pl.BlockSpec((BLOCK, 2 * PAIR), lambda i, f=nbr(-1): (f(i) % nblk, 0)),
            pl.BlockSpec((rows, 2 * PAIR), lambda i: (i % (nblk // qb), 0)),
            pl.BlockSpec((BLOCK, 2 * PAIR), lambda i, f=nbr(qb): (f(i) % nblk, 0))]

    return pl.pallas_call(
        functools.partial(_attn_body, seq=seq, qb=qb),
        grid=(m // rows,),
        in_specs=[pl.BlockSpec((1, rows, D_MODEL), lambda i: (P_Q, i, 0))] + kv(1) + kv(2) + tabs + [
            pl.BlockSpec((1, rows, D_MODEL), lambda i: (P_GB, i, 0)),
            pl.BlockSpec((1, PAIR), lambda i: (0, 0)),
            pl.BlockSpec((1, PAIR), lambda i: (0, 0)),
            pl.BlockSpec((1, N_HEADS), lambda i: (0, 0)),
        ],
        out_specs=pl.BlockSpec((rows, D_MODEL), lambda i: (i, 0)),
        out_shape=jax.ShapeDtypeStruct((m, D_MODEL), bf16),
        compiler_params=pltpu.CompilerParams(dimension_semantics=("parallel",), vmem_limit_bytes=VMEM_LIMIT),
        name="attn",
    )(p8, px, px, px, px, px, px, rope_tab, rope_tab, rope_tab, p8, q_g, k_g, sink)


def _rope_table(seq):
    inv = 1.0 / (ROPE_THETA ** (jnp.arange(0, HEAD, 2, dtype=f32) / HEAD))
    ang = jnp.arange(seq, dtype=f32)[:, None] * inv[None, :]
    cos, sin = jnp.cos(ang), jnp.sin(ang)
    return jnp.concatenate([cos, cos, cos, cos, -sin, sin, -sin, sin], axis=1)


def _out_proj_body(x_ref, ga_ref, gb_ref, ma_ref, mb_ref, wa_ref, wb_ref, wo_ref, o_ref):
    ya = jnp.dot(ga_ref[...], wa_ref[...], preferred_element_type=f32)
    yb = jnp.dot(gb_ref[...], wb_ref[...], preferred_element_type=f32)
    mixed = _sigmoid(ma_ref[0].astype(f32)) * ya + _sigmoid(mb_ref[0].astype(f32)) * yb
    o_ref[...] = x_ref[...] + jnp.dot(mixed.astype(bf16), wo_ref[...], preferred_element_type=f32)


def _out_proj(x2d, ga, gb, p8, wa, wb, wo, *, tm):
    m = x2d.shape[0]
    row = pl.BlockSpec((tm, D_MODEL), lambda i: (i, 0))
    wspec = pl.BlockSpec((D_MODEL, D_MODEL), lambda i: (0, 0))
    return pl.pallas_call(
        _out_proj_body,
        grid=(m // tm,),
        in_specs=[row, row, row,
                  pl.BlockSpec((1, tm, D_MODEL), lambda i: (P_MA, i, 0)),
                  pl.BlockSpec((1, tm, D_MODEL), lambda i: (P_MB, i, 0)),
                  wspec, wspec, wspec],
        out_specs=row,
        out_shape=jax.ShapeDtypeStruct((m, D_MODEL), f32),
        compiler_params=pltpu.CompilerParams(dimension_semantics=("parallel",), vmem_limit_bytes=VMEM_LIMIT),
        name="out_proj",
    )(x2d, ga, gb, p8, p8, wa, wb, wo)


def _layer(x2d, lw, rope_tab, *, n_seq, seq):
    tm = min(1024, seq)
    tt = min(256, seq)
    grp = min(8, seq // CHUNK)
    p8, px = _in_proj(x2d, lw["norm_g"], lw["w_p"], tm=min(2 * tm, seq))
    y0, bonus, rp, mn = _rwkv_p(p8, px, lw["mu_p"], lw["w0"], lw["wlu"], lw["a0"], lw["alu"], lw["k_k"], lw["k_a"],
                                lw["r_k"], tt=tt, seq=seq, pp=8)
    sf, sb = _rwkv_s(mn, n_seq=n_seq, grp=grp)
    ga = _rwkv_f(y0, bonus, rp, sf, sb, p8, lw["ln_g"], lw["ln_b"], tt=min(2 * tt, seq))
    gb = _attn(p8, px, rope_tab, lw["q_g"], lw["k_g"], lw["sink"], seq=seq, qb=2)
    return _out_proj(x2d, ga, gb, p8, lw["wa"], lw["wb"], lw["wo"], tm=tm)


def _trunk(x, layers):
    n_seq, seq, _ = x.shape
    rope_tab = _rope_table(seq)
    x2d = x.reshape(n_seq * seq, D_MODEL)
    for lw in layers:
        x2d = _layer(x2d, lw, rope_tab, n_seq=n_seq, seq=seq)
    return x2d.reshape(n_seq, seq, D_MODEL)


def _prep_layers(norm_g, w_in, shift_mu, w0, w_lora_up, a0, a_lora_up, k_k, k_a, r_k, ln_x_g, ln_x_b,
                 q_norm_g, k_norm_g, sink, w_proj_a, w_proj_b, w_out):
    layers = []
    row = lambda v: v.reshape(1, -1).astype(f32)
    for l in range(norm_g.shape[0]):
        w_p, mu_p = _prep_layer_weights(w_in[l], shift_mu[l])
        layers.append(dict(
            norm_g=row(norm_g[l]), w_p=w_p, mu_p=mu_p,
            w0=w0[l].astype(f32), wlu=w_lora_up[l].astype(bf16), a0=a0[l].astype(f32), alu=a_lora_up[l].astype(bf16),
            k_k=row(k_k[l]), k_a=row(k_a[l]), r_k=row(r_k[l]), ln_g=row(ln_x_g[l]), ln_b=row(ln_x_b[l]),
            q_g=row(jnp.tile(q_norm_g[l], 2)), k_g=row(jnp.tile(k_norm_g[l], 2)), sink=row(sink[l]),
            wa=w_proj_a[l].astype(bf16), wb=w_proj_b[l].astype(bf16), wo=w_out[l].astype(bf16)))
    return layers


def kernel(x_prompt, x_sample, norm_g, w_in, shift_mu, w0, w_lora_up, a0, a_lora_up, k_k, k_a, r_k,
           ln_x_g, ln_x_b, q_norm_g, k_norm_g, sink, w_proj_a, w_proj_b, w_out):
    layers = _prep_layers(norm_g, w_in, shift_mu, w0, w_lora_up, a0, a_lora_up, k_k, k_a, r_k, ln_x_g, ln_x_b,
                          q_norm_g, k_norm_g, sink, w_proj_a, w_proj_b, w_out)
    return _trunk(x_prompt, layers), _trunk(x_sample, layers)
```

```python
import functools
import math

import jax
import jax.numpy as jnp
from jax import lax
from jax.experimental import pallas as pl
from jax.experimental.pallas import tpu as pltpu

D_MODEL = 1024
HEAD = 64
N_HEADS = D_MODEL // HEAD
R_LORA = 64
HKV = 4
GRP = N_HEADS // HKV
D_KV = HKV * HEAD
WINDOW = 128
BLOCK = 128
GN_EPS = 64e-5
NORM_EPS = 1e-6
ROPE_THETA = 10000.0
C_SHIFT = 3 * D_MODEL + 4 * R_LORA
N_PLANES = 9
CHUNK = 64
VMEM_LIMIT = 56 * 1024 * 1024

P_R, P_K, P_V, P_GA, P_Q, P_GB, P_MA, P_MB, P_X = range(9)

f32 = jnp.float32
bf16 = jnp.bfloat16


def _dot(a, b):
    return jnp.dot(a.astype(bf16), b.astype(bf16), preferred_element_type=f32)


def _dot_nt(a, b):
    return lax.dot_general(a.astype(bf16), b.astype(bf16), (((1,), (1,)), ((), ())),
                           preferred_element_type=f32)


def _dot_tn(a, b):
    return lax.dot_general(a.astype(bf16), b.astype(bf16), (((0,), (0,)), ((), ())),
                           preferred_element_type=f32)


def _dot_split(a_exact, b):
    b0 = b.astype(bf16)
    b1 = (b - b0.astype(f32)).astype(bf16)
    a = a_exact.astype(bf16)
    return jnp.dot(a, b0, preferred_element_type=f32) + jnp.dot(a, b1, preferred_element_type=f32)


def _sigmoid(x):
    return 1.0 / (1.0 + jnp.exp(-x))


def _in_proj_body(x_ref, g_ref, w_ref, o_ref, ox_ref, h_ref):
    j = pl.program_id(1)

    @pl.when(j == 0)
    def _():
        x = x_ref[...]
        ms = jnp.mean(x * x, axis=-1, keepdims=True)
        h_ref[...] = (x * lax.rsqrt(ms + NORM_EPS) * g_ref[...]).astype(bf16)

    @pl.when(j < P_X)
    def _():
        o_ref[0] = jnp.dot(h_ref[...], w_ref[...], preferred_element_type=f32).astype(bf16)

    @pl.when(j == P_X)
    def _():
        ox_ref[...] = jnp.dot(h_ref[...], w_ref[...], preferred_element_type=f32)


def _in_proj(x2d, norm_g, w_p, *, tm):
    m = x2d.shape[0]
    return pl.pallas_call(
        _in_proj_body,
        grid=(m // tm, N_PLANES),
        in_specs=[
            pl.BlockSpec((tm, D_MODEL), lambda i, j: (i, 0)),
            pl.BlockSpec((1, D_MODEL), lambda i, j: (0, 0)),
            pl.BlockSpec((D_MODEL, D_MODEL), lambda i, j: (0, j)),
        ],
        out_specs=[pl.BlockSpec((1, tm, D_MODEL), lambda i, j: (jnp.minimum(j, P_X - 1), i, 0)),
                   pl.BlockSpec((tm, D_MODEL), lambda i, j: (i, 0))],
        out_shape=[jax.ShapeDtypeStruct((N_PLANES - 1, m, D_MODEL), bf16),
                   jax.ShapeDtypeStruct((m, D_MODEL), f32)],
        scratch_shapes=[pltpu.VMEM((tm, D_MODEL), bf16)],
        compiler_params=pltpu.CompilerParams(
            dimension_semantics=("parallel", "arbitrary"), vmem_limit_bytes=VMEM_LIMIT),
        name="in_proj",
    )(x2d, norm_g, w_p)


def _prep_layer_weights(w_in, shift_mu):
    c = C_SHIFT
    d = D_MODEL
    cols = [
        (0, d), (d, 2 * d), (2 * d, 3 * d),
        (c, c + d),
        (c + d, c + 2 * d),
        (c + 2 * d + 2 * D_KV, c + 3 * d + 2 * D_KV),
        (c + 3 * d + 2 * D_KV, c + 4 * d + 2 * D_KV),
        (c + 4 * d + 2 * D_KV, c + 5 * d + 2 * D_KV),
    ]
    planes = [w_in[:, a:b] for a, b in cols]
    small = jnp.concatenate([w_in[:, 3 * d:c], w_in[:, c + 2 * d:c + 2 * d + 2 * D_KV],
                             jnp.zeros((d, d - 4 * R_LORA - 2 * D_KV), w_in.dtype)], axis=1)
    w_p = jnp.concatenate(planes + [small], axis=1).astype(bf16)
    zeros = jnp.zeros((d,), f32)
    mu_small = jnp.concatenate([shift_mu[3 * d:c], jnp.zeros((d - 4 * R_LORA,), f32)])
    mu_p = jnp.concatenate([shift_mu[0:d], shift_mu[d:2 * d], shift_mu[2 * d:3 * d]] + [zeros] * 5 + [mu_small])
    return w_p, mu_p.reshape(1, N_PLANES * d)


PAIR = 2 * HEAD
PADR = 8
PRE, PRE_DONE, CHUNK_STAGE = "pre", "pre_done", "chunk"


def _token_shift(tt, seq):
    i = pl.program_id(0)
    first = (i * tt) % seq == 0
    last = ((i + 1) * tt) % seq == 0

    def shift(main, pv, nx, mu, buf):
        c = main.astype(f32)
        buf[PADR:PADR + tt, :] = c
        buf[PADR - 1:PADR, :] = jnp.where(first, 0.0, pv[pv.shape[0] - 1:, :].astype(f32))
        buf[PADR + tt:PADR + tt + 1, :] = jnp.where(last, 0.0, nx[:1, :].astype(f32))
        nbr = 0.5 * (buf[PADR - 1:PADR - 1 + tt, :] + buf[PADR + 1:PADR + 1 + tt, :])
        return c + mu * (nbr - c)

    return shift


def _rwkv_p_body(r_ref, r_pv, r_nx, k_ref, k_pv, k_nx, v_ref, v_pv, v_nx, mur_ref, muk_ref, muv_ref,
                 kk_ref, ka_ref, rk_ref,
                 y0_ref, bonus_ref, rp_ref, mn_ref,
                 at_s, rt_s, kp_s, bp_s, khm_s, bhm_s, atx_s, rtx_s, vx_s, pl_s, sh_s, *, tt, shift, zs, a_pre):
    nc = tt // CHUNK
    r = shift(r_ref[0], r_pv[0], r_nx[0], mur_ref[...], sh_s.at[0])
    k = shift(k_ref[0], k_pv[0], k_nx[0], muk_ref[...], sh_s.at[1])
    v = shift(v_ref[0], v_pv[0], v_nx[0], muv_ref[...], sh_s.at[2])
    yield PRE

    lane = lax.broadcasted_iota(jnp.int32, (PAIR, PAIR), 0) // HEAD
    seg = jnp.where(lane == lax.broadcasted_iota(jnp.int32, (PAIR, PAIR), 1) // HEAD, 1.0, 0.0)
    lo = lax.broadcasted_iota(jnp.int32, (tt, PAIR), 1) < HEAD
    swap = lambda t: pltpu.roll(t, HEAD, axis=1)

    def head_lo(t, h):
        return jnp.where(lo, t if h == 0 else swap(t), 0.0)

    def head_hi(t, h):
        return jnp.where(lo, 0.0, swap(t) if h == 0 else t)

    def head_own(t, h):
        return jnp.where(lo, t, 0.0) if h == 0 else jnp.where(lo, 0.0, t)

    for h in range(2):
        vx_s[h] = head_hi(v, h).astype(bf16)

    kkv = k * kk_ref[...]
    kk = kkv / jnp.maximum(jnp.sqrt(_segsum(kkv * kkv, seg, terms=1)), 1e-12)
    yield PRE

    rows = lax.broadcasted_iota(jnp.int32, (tt, tt), 0)
    cols = lax.broadcasted_iota(jnp.int32, (tt, tt), 1)
    same = (rows // CHUNK) == (cols // CHUNK)

    kmod_sum = jnp.zeros_like(k)
    for d in range(2):
        if d == 0:
            tri_bd = jnp.where(same & (cols <= rows), 1.0, 0.0)
        else:
            tri_bd = jnp.where(same & (cols >= rows), 1.0, 0.0)
        lw = -math.exp(-0.5) * _sigmoid(zs[d])
        a = _sigmoid(a_pre[d])
        yield PRE
        kmod = k * (1.0 + (a - 1.0) * ka_ref[...])
        kmod_sum = kmod_sum + kmod
        b = -(kk * a)
        cum = _dot_split(tri_bd, lw)
        edge = CHUNK - 1 if d == 0 else 0
        tot = jnp.concatenate([jnp.broadcast_to(cum[c * CHUNK + edge:c * CHUNK + edge + 1, :], (CHUNK, PAIR))
                               for c in range(nc)], axis=0)
        yield PRE
        p_inv = jnp.exp(-cum)
        p_end = jnp.exp(tot - cum)
        at = kk * jnp.exp(cum - lw)
        rt = r * jnp.exp(cum)
        kh = kmod * p_inv
        bh = b * p_inv
        p_tot = jnp.exp(tot)
        at_s[d] = at.astype(bf16)
        rt_s[d] = rt.astype(bf16)
        pl_s[d] = p_tot
        yield PRE
        kp_s[d] = (kmod * p_end).astype(bf16)
        bp_s[d] = (b * p_end).astype(bf16)
        for h in range(2):
            khm_s[d, h] = head_own(kh, h).astype(bf16)
            bhm_s[d, h] = head_own(bh, h).astype(bf16)
            atx_s[d, h] = head_lo(at, h).astype(bf16)
            rtx_s[d, h] = head_lo(rt, h)
        yield PRE

    bonus = _segsum(r * kmod_sum * rk_ref[...], seg) * v
    yield PRE_DONE

    row = lax.broadcasted_iota(jnp.int32, (CHUNK, PAIR), 0)
    col = lax.broadcasted_iota(jnp.int32, (CHUNK, PAIR), 1)
    col_t = col % CHUNK
    lo_c = col < HEAD
    strict = (col_t < row, col_t > row)
    incl = (col_t <= row, col_t >= row)
    pairs = [(d, c) for d in range(2) for c in range(nc)]
    inst = [(d, c, h) for d, c in pairs for h in range(2)]
    cs = lambda c: slice(c * CHUNK, (c + 1) * CHUNK)
    swap_c = lambda t: pltpu.roll(t, HEAD, axis=1)

    top, bot = [], []
    for d, c in pairs:
        lhs = jnp.concatenate([at_s[d, cs(c)], rt_s[d, cs(c)]], axis=0)
        rhs = jnp.concatenate([bhm_s[d, 0, cs(c)], khm_s[d, 0, cs(c)],
                               bhm_s[d, 1, cs(c)], khm_s[d, 1, cs(c)]], axis=0)
        sc = lax.dot_general(lhs, rhs, (((1,), (1,)), ((), ())), preferred_element_type=f32)
        for h in range(2):
            top.append(jnp.where(strict[d], sc[:CHUNK, h * PAIR:(h + 1) * PAIR], 0.0))
            bot.append(jnp.where(incl[d], sc[CHUNK:, h * PAIR:(h + 1) * PAIR], 0.0))
    yield CHUNK_STAGE
    xs, aps = [], []
    for i, (d, c, h) in enumerate(inst):
        vx = vx_s[h, cs(c)]
        a_ak = jnp.where(lo_c, 0.0, top[i]).astype(bf16)
        akv = jnp.dot(a_ak, jnp.concatenate([vx, vx], axis=0), preferred_element_type=f32)
        xs.append(atx_s[d, h, cs(c)].astype(f32) + akv)
        aps.append(top[i][:, :CHUNK])
    n_dbl = CHUNK.bit_length() - 1
    for it in range(n_dbl):
        yield CHUNK_STAGE
        last = it + 1 == n_dbl
        res = [_dot(ap, x if last else jnp.concatenate([x, ap], axis=1)) for ap, x in zip(aps, xs)]
        xs = [x + rs[:, :PAIR] for x, rs in zip(xs, res)]
        if not last:
            aps = [rs[:, PAIR:] for rs in res]
    yield CHUNK_STAGE
    rhs_o = [jnp.concatenate([xs[i].astype(bf16), vx_s[h, cs(c)]], axis=0) for i, (d, c, h) in enumerate(inst)]
    rys = [_dot(bot[i], rhs_o[i]) + rtx_s[d, h, cs(c)] for i, (d, c, h) in enumerate(inst)]
    yield CHUNK_STAGE
    mns = [_dot_tn(jnp.concatenate([bp_s[d, cs(c), h * HEAD:(h + 1) * HEAD],
                                    kp_s[d, cs(c), h * HEAD:(h + 1) * HEAD]], axis=0), rhs_o[i])
           for i, (d, c, h) in enumerate(inst)]
    yield CHUNK_STAGE
    for i, (d, c, h) in enumerate(inst):
        p_blk = pl_s[d, c * CHUNK:c * CHUNK + 8, :]
        p_row = (p_blk if h == 0 else swap_c(p_blk))[0:1]
        mn_ref[d, c, h] = (mns[i] + jnp.where(row == col, p_row, 0.0)).astype(bf16)
    y0_fwd = {}
    for j, (d, c) in enumerate(pairs):
        ry0, ry1 = rys[2 * j], rys[2 * j + 1]
        rp_ref[d, cs(c), :] = jnp.where(lo_c, ry0, swap_c(ry1)).astype(bf16)
        y0 = jnp.where(lo_c, swap_c(ry0), ry1)
        if d == 0:
            y0_fwd[c] = y0
        else:
            y0_ref[cs(c), :] = (y0_fwd[c] + y0).astype(y0_ref.dtype)
    bonus_ref[...] = bonus.astype(bonus_ref.dtype)


N_IN_P = 23
N_OUT_P = 4


def _rwkv_p_multi(*refs, tt, seq, pp):
    ins, outs = refs[:N_IN_P], refs[N_IN_P:N_IN_P + N_OUT_P]
    scr, shx_s = refs[N_IN_P + N_OUT_P:-1], refs[-1]
    x_ref, x_pv, x_nx, mux_ref = ins[9], ins[10], ins[11], ins[15]
    w0_ref, wlu_ref, a0_ref, alu_ref = ins[16:20]
    shift = _token_shift(tt, seq)
    xs = shift(x_ref[...], x_pv[...], x_nx[...], mux_ref[...], shx_s)
    zs = [w0_ref[d:d + 1, :] + _dot(jnp.tanh(xs[:, d * R_LORA:(d + 1) * R_LORA]), wlu_ref[d]) for d in range(2)]
    a_pre = [a0_ref[d:d + 1, :] + _dot(xs[:, (2 + d) * R_LORA:(3 + d) * R_LORA], alu_ref[d]) for d in range(2)]
    stages = []
    for q in range(pp):
        lanes = pl.ds(q * PAIR, PAIR)
        lsl = slice(q * PAIR, (q + 1) * PAIR)
        view = lambda ref, lanes=lanes: ref.at[..., lanes]
        planes = [view(ref) for ref in ins[0:9]]
        gains = [view(ref) for ref in ins[12:15]]
        params = [view(ref) for ref in ins[20:23]]
        y0_ref, bonus_ref, rp_ref, mn_ref = outs
        stages.append(_rwkv_p_body(*planes, *gains, *params,
                                   view(y0_ref), view(bonus_ref), view(rp_ref), mn_ref.at[:, :, pl.ds(2 * q, 2)],
                                   *[s.at[q] for s in scr], tt=tt, shift=shift,
                                   zs=[z[:, lsl] for z in zs], a_pre=[a[:, lsl] for a in a_pre]))
    def run_pre(stage):
        while next(stage) != PRE_DONE:
            pass

    run_pre(stages[0])
    for q in range(pp):
        chunk, pre = stages[q], (stages[q + 1] if q + 1 < pp else None)
        chunk_live, pre_live = True, pre is not None
        while chunk_live or pre_live:
            if chunk_live:
                chunk_live = next(chunk, None) is not None
            if pre_live:
                pre_live = next(pre) != PRE_DONE


def _segsum(x, seg, terms=2):
    x0 = x.astype(bf16)
    m = seg.astype(bf16)
    s = jnp.dot(x0, m, preferred_element_type=f32)
    if terms == 1:
        return s
    x1 = (x - x0.astype(f32)).astype(bf16)
    return s + jnp.dot(x1, m, preferred_element_type=f32)


def _rwkv_p(p8, px, mu_p, w0, wlu, a0, alu, k_k, k_a, r_k, *, tt, seq, pp):
    m = px.shape[0]
    nt = m // tt
    npair = D_MODEL // PAIR
    cpt = tt // CHUNK
    wl = pp * PAIR
    hb, hx = 16, 8

    def tiles(plane):
        return [pl.BlockSpec((1, tt, wl), lambda i, j: (plane, i, j)),
                pl.BlockSpec((1, hb, wl), lambda i, j: (plane, jnp.maximum(i * (tt // hb) - 1, 0), j)),
                pl.BlockSpec((1, hb, wl), lambda i, j: (plane, jnp.minimum((i + 1) * (tt // hb), m // hb - 1), j))]

    vec = pl.BlockSpec((1, wl), lambda i, j: (0, j))
    mu = lambda plane: pl.BlockSpec((1, wl), lambda i, j: (0, plane * (npair // pp) + j))
    wx = 4 * R_LORA
    return pl.pallas_call(
        functools.partial(_rwkv_p_multi, tt=tt, seq=seq, pp=pp),
        grid=(nt, npair // pp),
        in_specs=tiles(P_R) + tiles(P_K) + tiles(P_V) + [
            pl.BlockSpec((tt, wx), lambda i, j: (i, 0)),
            pl.BlockSpec((hx, wx), lambda i, j: (jnp.maximum(i * (tt // hx) - 1, 0), 0)),
            pl.BlockSpec((hx, wx), lambda i, j: (jnp.minimum((i + 1) * (tt // hx), m // hx - 1), 0)),
            mu(P_R), mu(P_K), mu(P_V),
            pl.BlockSpec((1, wx), lambda i, j: (0, P_X * D_MODEL // wx)),
            pl.BlockSpec((2, wl), lambda i, j: (0, j)),
            pl.BlockSpec((2, R_LORA, wl), lambda i, j: (0, 0, j)),
            pl.BlockSpec((2, wl), lambda i, j: (0, j)),
            pl.BlockSpec((2, R_LORA, wl), lambda i, j: (0, 0, j)),
            vec, vec, vec,
        ],
        out_specs=[
            pl.BlockSpec((tt, wl), lambda i, j: (i, j)),
            pl.BlockSpec((tt, wl), lambda i, j: (i, j)),
            pl.BlockSpec((2, tt, wl), lambda i, j: (0, i, j)),
            pl.BlockSpec((2, cpt, 2 * pp, HEAD, PAIR), lambda i, j: (0, i, j, 0, 0)),
        ],
        out_shape=[
            jax.ShapeDtypeStruct((m, D_MODEL), bf16),
            jax.ShapeDtypeStruct((m, D_MODEL), bf16),
            jax.ShapeDtypeStruct((2, m, D_MODEL), bf16),
            jax.ShapeDtypeStruct((2, m // CHUNK, N_HEADS, HEAD, PAIR), bf16),
        ],
        scratch_shapes=(
            [pltpu.VMEM((pp, 2, tt, PAIR), bf16) for _ in range(4)]
            + [pltpu.VMEM((pp, 2, 2, tt, PAIR), bf16) for _ in range(3)]
            + [pltpu.VMEM((pp, 2, 2, tt, PAIR), f32),
               pltpu.VMEM((pp, 2, tt, PAIR), bf16),
               pltpu.VMEM((pp, 2, tt, PAIR), f32),
               pltpu.VMEM((pp, 3, tt + 2 * PADR, PAIR), f32),
               pltpu.VMEM((tt + 2 * PADR, wx), f32)]),
        compiler_params=pltpu.CompilerParams(
            dimension_semantics=("parallel", "parallel"), vmem_limit_bytes=VMEM_LIMIT),
        name="rwkv_p",
    )(p8, p8, p8, p8, p8, p8, p8, p8, p8, px, px, px, mu_p, mu_p, mu_p, mu_p, w0, wlu, a0, alu, k_k, k_a, r_k)


def _rwkv_s_body(mnf_ref, mnb_ref, sf_ref, sb_ref, st_ref, *, grp):
    @pl.when(pl.program_id(1) == 0)
    def _():
        st_ref[...] = jnp.zeros_like(st_ref)

    zero = jnp.zeros((HEAD, HEAD), bf16)

    def step(d, mn_ref, s_out_ref, cc):
        s = st_ref[d].astype(bf16)
        for p in range(N_HEADS // 2):
            s_out_ref[cc, p] = jnp.concatenate([jnp.concatenate([s[2 * p], zero], axis=1),
                                                jnp.concatenate([zero, s[2 * p + 1]], axis=1)], axis=0)
        m_t = mn_ref[0, cc, :, :, 0:HEAD]
        n_t = mn_ref[0, cc, :, :, HEAD:].astype(f32)
        st_ref[d] = jnp.einsum('hij,hjk->hik', m_t, s, preferred_element_type=f32) + n_t

    for cc in range(grp):
        step(0, mnf_ref, sf_ref, cc)
        step(1, mnb_ref, sb_ref, grp - 1 - cc)


def _rwkv_s(mn, *, n_seq, grp):
    nchunk = mn.shape[1]
    npair = N_HEADS // 2
    ng = nchunk // n_seq // grp
    blk_in = (1, grp, N_HEADS, HEAD, PAIR)
    blk_out = (grp, npair, PAIR, PAIR)
    out = jax.ShapeDtypeStruct((nchunk, npair, PAIR, PAIR), bf16)
    return pl.pallas_call(
        functools.partial(_rwkv_s_body, grp=grp),
        grid=(n_seq, ng),
        in_specs=[pl.BlockSpec(blk_in, lambda b, g: (0, b * ng + g, 0, 0, 0)),
                  pl.BlockSpec(blk_in, lambda b, g: (1, b * ng + ng - 1 - g, 0, 0, 0))],
        out_specs=[pl.BlockSpec(blk_out, lambda b, g: (b * ng + g, 0, 0, 0)),
                   pl.BlockSpec(blk_out, lambda b, g: (b * ng + ng - 1 - g, 0, 0, 0))],
        out_shape=[out, out],
        scratch_shapes=[pltpu.VMEM((2, N_HEADS, HEAD, HEAD), f32)],
        compiler_params=pltpu.CompilerParams(
            dimension_semantics=("parallel", "arbitrary"), vmem_limit_bytes=VMEM_LIMIT),
        name="rwkv_s",
    )(mn, mn)


def _rwkv_f_body(y0_ref, bonus_ref, rp_ref, sf_ref, sb_ref, gate_ref, lng_ref, lnb_ref, o_ref, *, tt):
    npair = D_MODEL // PAIR
    nc = tt // CHUNK
    cs = lambda c: slice(c * CHUNK, (c + 1) * CHUNK)
    ps = lambda p: slice(p * PAIR, (p + 1) * PAIR)
    yf = [[jnp.dot(rp_ref[0, cs(c), ps(p)], sf_ref[c, p], preferred_element_type=f32) for c in range(nc)]
          for p in range(npair)]
    yb = [[jnp.dot(rp_ref[1, cs(c), ps(p)], sb_ref[c, p], preferred_element_type=f32) for c in range(nc)]
          for p in range(npair)]
    lane = lax.broadcasted_iota(jnp.int32, (PAIR, PAIR), 0) // HEAD
    seg = jnp.where(lane == lax.broadcasted_iota(jnp.int32, (PAIR, PAIR), 1) // HEAD, 1.0, 0.0)
    ys = [y0_ref[:, ps(p)].astype(f32) + jnp.concatenate(yf[p], axis=0) + jnp.concatenate(yb[p], axis=0) for p in range(npair)]
    mus = [_segsum(y, seg) * (1.0 / HEAD) for y in ys]
    dvs = [y - mu for y, mu in zip(ys, mus)]
    vrs = [_segsum(dv * dv, seg) * (1.0 / HEAD) for dv in dvs]
    for p in range(npair):
        yn = dvs[p] * lax.rsqrt(vrs[p] + GN_EPS) * lng_ref[:, ps(p)] + lnb_ref[:, ps(p)] + bonus_ref[:, ps(p)].astype(f32)
        g = gate_ref[0, :, ps(p)].astype(f32)
        o_ref[:, ps(p)] = (yn * (g * _sigmoid(g))).astype(bf16)


def _rwkv_f(y0, bonus, rp, sf, sb, p8, ln_g, ln_b, *, tt):
    m = y0.shape[0]
    cpt = tt // CHUNK
    npair = D_MODEL // PAIR
    tile = pl.BlockSpec((tt, D_MODEL), lambda i: (i, 0))
    vec = pl.BlockSpec((1, D_MODEL), lambda i: (0, 0))
    state = pl.BlockSpec((cpt, npair, PAIR, PAIR), lambda i: (i, 0, 0, 0))
    return pl.pallas_call(
        functools.partial(_rwkv_f_body, tt=tt),
        grid=(m // tt,),
        in_specs=[
            tile, tile,
            pl.BlockSpec((2, tt, D_MODEL), lambda i: (0, i, 0)),
            state, state,
            pl.BlockSpec((1, tt, D_MODEL), lambda i: (P_GA, i, 0)),
            vec, vec,
        ],
        out_specs=tile,
        out_shape=jax.ShapeDtypeStruct((m, D_MODEL), bf16),
        compiler_params=pltpu.CompilerParams(dimension_semantics=("parallel",), vmem_limit_bytes=VMEM_LIMIT),
        name="rwkv_f",
    )(y0, bonus, rp, sf, sb, p8, ln_g, ln_b)


def _attn_body(q_ref, kp_ref, kc_ref, kn_ref, vp_ref, vc_ref, vn_ref, tp_ref, tc_ref, tn_ref, gate_ref,
               qg_ref, kg_ref, sink_ref, o_ref, *, seq, qb):
    step_rows = qb * BLOCK
    start = (pl.program_id(0) % (seq // step_rows)) * step_rows
    ctx = BLOCK + 2 * WINDOW
    qoff = lax.broadcasted_iota(jnp.int32, (BLOCK, ctx), 0)
    koff = lax.broadcasted_iota(jnp.int32, (BLOCK, ctx), 1) - WINDOW
    valid = []
    for b in range(qb):
        kpos = start + b * BLOCK + koff
        valid.append((jnp.abs(qoff - koff) <= WINDOW) & (kpos >= 0) & (kpos < seq))

    lane_r = lax.broadcasted_iota(jnp.int32, (PAIR, PAIR), 0) // HEAD
    seg = jnp.where(lane_r == lax.broadcasted_iota(jnp.int32, (PAIR, PAIR), 1) // HEAD, 1.0, 0.0)
    swap = lambda t: pltpu.roll(t, HEAD, axis=1)

    def norm_rope(x, gain, tab):
        lane = lax.broadcasted_iota(jnp.int32, x.shape, 1)
        xn = x * lax.rsqrt(_segsum(x * x, seg, terms=1) * (1.0 / HEAD) + NORM_EPS) * gain
        rot = jnp.where(lane % HEAD < HEAD // 2, pltpu.roll(xn, PAIR - HEAD // 2, axis=1),
                        pltpu.roll(xn, HEAD // 2, axis=1))
        return xn * tab[:, :PAIR] + rot * tab[:, PAIR:]

    def place(t, e, want_lo):
        lo = lax.broadcasted_iota(jnp.int32, t.shape, 1) < HEAD
        src = t if (e == 0) == want_lo else swap(t)
        return (jnp.where(lo, src, 0.0) if want_lo else jnp.where(lo, 0.0, src)).astype(bf16)

    tab_k = jnp.concatenate([tp_ref[...], tc_ref[...], tn_ref[...]], axis=0)
    tab_q = tc_ref[...]
    k_lo, k_hi, v_lo, v_hi = [], [], [], []
    for kp in range(D_KV // PAIR):
        ps_ = slice(kp * PAIR, (kp + 1) * PAIR)
        kx = jnp.concatenate([kp_ref[:, ps_], kc_ref[:, ps_], kn_ref[:, ps_]], axis=0)
        vx = jnp.concatenate([vp_ref[:, ps_], vc_ref[:, ps_], vn_ref[:, ps_]], axis=0)
        kr = norm_rope(kx, kg_ref[...], tab_k)
        for e in range(2):
            k_lo.append(place(kr, e, True))
            k_hi.append(place(kr, e, False))
            v_lo.append(place(vx, e, True))
            v_hi.append(place(vx, e, False))
    npair = D_MODEL // PAIR
    psl = lambda p: slice(p * PAIR, (p + 1) * PAIR)
    log2e = math.log2(math.e)
    qrs = [(norm_rope(q_ref[0, :, psl(p)].astype(f32), qg_ref[...], tab_q) * (HEAD ** -0.5 * log2e)).astype(bf16)
           for p in range(npair)]
    inst = [(b, hq) for b in range(qb) for hq in range(N_HEADS)]
    rows = lambda b: slice(b * BLOCK, (b + 1) * BLOCK)
    keys = lambda b: slice(b * BLOCK, b * BLOCK + ctx)
    def masked(s, ok):
        return jnp.concatenate([jnp.where(ok[:, :WINDOW], s[:, :WINDOW], -1e30), s[:, WINDOW:WINDOW + BLOCK],
                                jnp.where(ok[:, WINDOW + BLOCK:], s[:, WINDOW + BLOCK:], -1e30)], axis=1)

    ss = [masked(lax.dot_general(qrs[hq // 2][rows(b)], (k_lo if hq % 2 == 0 else k_hi)[hq // GRP][keys(b)],
                                 (((1,), (1,)), ((), ())), preferred_element_type=f32), valid[b])
          for b, hq in inst]
    sinks = [sink_ref[0:1, hq:hq + 1] * log2e for b, hq in inst]
    mxs = [jnp.maximum(jnp.max(s, axis=-1, keepdims=True), sk) for s, sk in zip(ss, sinks)]
    ps = [jnp.exp2(s - mx) for s, mx in zip(ss, mxs)]
    dens = [jnp.sum(p, axis=-1, keepdims=True) + jnp.exp2(sk - mx) for p, sk, mx in zip(ps, sinks, mxs)]
    os_ = [jnp.dot(p.astype(bf16), (v_lo if hq % 2 == 0 else v_hi)[hq // GRP][keys(b)], preferred_element_type=f32) / den
           for (b, hq), p, den in zip(inst, ps, dens)]
    for b in range(qb):
        for p in range(npair):
            gt = gate_ref[0, rows(b), psl(p)].astype(f32)
            i0 = b * N_HEADS + 2 * p
            o_ref[rows(b), psl(p)] = ((os_[i0] + os_[i0 + 1]) * (gt * _sigmoid(gt))).astype(bf16)


def _attn(p8, px, rope_tab, q_g, k_g, sink, *, seq, qb):
    m = px.shape[0]
    nblk = seq // BLOCK
    rows = qb * BLOCK

    def nbr(delta):
        def f(i):
            t = (i * qb) % nblk
            return i * qb - t + jnp.clip(t + delta, 0, nblk - 1)
        return f

    def kv(col):
        return [pl.BlockSpec((BLOCK, D_KV), lambda i, f=nbr(-1): (f(i), col)),
                pl.BlockSpec((rows, D_KV), lambda i: (i, col)),
                pl.BlockSpec((BLOCK, D_KV), lambda i, f=nbr(qb): (f(i), col))]

    tabs = [pl.BlockSpec((BLOCK, 2 * PAIR), lambda i, f=nbr(-1): (f(i) % nblk, 0)),
            pl.BlockSpec((rows, 2 * PAIR), lambda i: (i % (nblk // qb), 0)),
            pl.BlockSpec((BLOCK, 2 * PAIR), lambda i, f=nbr(qb): (f(i) % nblk, 0))]

    return pl.pallas_call(
        functools.partial(_attn_body, seq=seq, qb=qb),
        grid=(m // rows,),
        in_specs=[pl.BlockSpec((1, rows, D_MODEL), lambda i: (P_Q, i, 0))] + kv(1) + kv(2) + tabs + [
            pl.BlockSpec((1, rows, D_MODEL), lambda i: (P_GB, i, 0)),
            pl.BlockSpec((1, PAIR), lambda i: (0, 0)),
            pl.BlockSpec((1, PAIR), lambda i: (0, 0)),
            pl.BlockSpec((1, N_HEADS), lambda i: (0, 0)),
        ],
        out_specs=pl.BlockSpec((rows, D_MODEL), lambda i: (i, 0)),
        out_shape=jax.ShapeDtypeStruct((m, D_MODEL), bf16),
        compiler_params=pltpu.CompilerParams(dimension_semantics=("parallel",), vmem_limit_bytes=VMEM_LIMIT),
        name="attn",
    )(p8, px, px, px, px, px, px, rope_tab, rope_tab, rope_tab, p8, q_g, k_g, sink)


def _rope_table(seq):
    inv = 1.0 / (ROPE_THETA ** (jnp.arange(0, HEAD, 2, dtype=f32) / HEAD))
    ang = jnp.arange(seq, dtype=f32)[:, None] * inv[None, :]
    cos, sin = jnp.cos(ang), jnp.sin(ang)
    return jnp.concatenate([cos, cos, cos, cos, -sin, sin, -sin, sin], axis=1)


def _out_proj_body(x_ref, ga_ref, gb_ref, ma_ref, mb_ref, wa_ref, wb_ref, wo_ref, o_ref):
    ya = jnp.dot(ga_ref[...], wa_ref[...], preferred_element_type=f32)
    yb = jnp.dot(gb_ref[...], wb_ref[...], preferred_element_type=f32)
    mixed = _sigmoid(ma_ref[0].astype(f32)) * ya + _sigmoid(mb_ref[0].astype(f32)) * yb
    o_ref[...] = x_ref[...] + jnp.dot(mixed.astype(bf16), wo_ref[...], preferred_element_type=f32)


def _out_proj(x2d, ga, gb, p8, wa, wb, wo, *, tm):
    m = x2d.shape[0]
    row = pl.BlockSpec((tm, D_MODEL), lambda i: (i, 0))
    wspec = pl.BlockSpec((D_MODEL, D_MODEL), lambda i: (0, 0))
    return pl.pallas_call(
        _out_proj_body,
        grid=(m // tm,),
        in_specs=[row, row, row,
                  pl.BlockSpec((1, tm, D_MODEL), lambda i: (P_MA, i, 0)),
                  pl.BlockSpec((1, tm, D_MODEL), lambda i: (P_MB, i, 0)),
                  wspec, wspec, wspec],
        out_specs=row,
        out_shape=jax.ShapeDtypeStruct((m, D_MODEL), f32),
        compiler_params=pltpu.CompilerParams(dimension_semantics=("parallel",), vmem_limit_bytes=VMEM_LIMIT),
        name="out_proj",
    )(x2d, ga, gb, p8, p8, wa, wb, wo)


def _layer(x2d, lw, rope_tab, *, n_seq, seq):
    tm = min(1024, seq)
    tt = min(256, seq)
    grp = min(8, seq // CHUNK)
    p8, px = _in_proj(x2d, lw["norm_g"], lw["w_p"], tm=min(2 * tm, seq))
    y0, bonus, rp, mn = _rwkv_p(p8, px, lw["mu_p"], lw["w0"], lw["wlu"], lw["a0"], lw["alu"], lw["k_k"], lw["k_a"],
                                lw["r_k"], tt=tt, seq=seq, pp=8)
    sf, sb = _rwkv_s(mn, n_seq=n_seq, grp=grp)
    ga = _rwkv_f(y0, bonus, rp, sf, sb, p8, lw["ln_g"], lw["ln_b"], tt=min(2 * tt, seq))
    gb = _attn(p8, px, rope_tab, lw["q_g"], lw["k_g"], lw["sink"], seq=seq, qb=2)
    return _out_proj(x2d, ga, gb, p8, lw["wa"], lw["wb"], lw["wo"], tm=tm)


def _trunk(x, layers):
    n_seq, seq, _ = x.shape
    rope_tab = _rope_table(seq)
    x2d = x.reshape(n_seq * seq, D_MODEL)
    for lw in layers:
        x2d = _layer(x2d, lw, rope_tab, n_seq=n_seq, seq=seq)
    return x2d.reshape(n_seq, seq, D_MODEL)


def _prep_layers(norm_g, w_in, shift_mu, w0, w_lora_up, a0, a_lora_up, k_k, k_a, r_k, ln_x_g, ln_x_b,
                 q_norm_g, k_norm_g, sink, w_proj_a, w_proj_b, w_out):
    layers = []
    row = lambda v: v.reshape(1, -1).astype(f32)
    for l in range(norm_g.shape[0]):
        w_p, mu_p = _prep_layer_weights(w_in[l], shift_mu[l])
        layers.append(dict(
            norm_g=row(norm_g[l]), w_p=w_p, mu_p=mu_p,
            w0=w0[l].astype(f32), wlu=w_lora_up[l].astype(bf16), a0=a0[l].astype(f32), alu=a_lora_up[l].astype(bf16),
            k_k=row(k_k[l]), k_a=row(k_a[l]), r_k=row(r_k[l]), ln_g=row(ln_x_g[l]), ln_b=row(ln_x_b[l]),
            q_g=row(jnp.tile(q_norm_g[l], 2)), k_g=row(jnp.tile(k_norm_g[l], 2)), sink=row(sink[l]),
            wa=w_proj_a[l].astype(bf16), wb=w_proj_b[l].astype(bf16), wo=w_out[l].astype(bf16)))
    return layers


def kernel(x_prompt, x_sample, norm_g, w_in, shift_mu, w0, w_lora_up, a0, a_lora_up, k_k, k_a, r_k,
           ln_x_g, ln_x_b, q_norm_g, k_norm_g, sink, w_proj_a, w_proj_b, w_out):
    layers = _prep_layers(norm_g, w_in, shift_mu, w0, w_lora_up, a0, a_lora_up, k_k, k_a, r_k, ln_x_g, ln_x_b,
                          q_norm_g, k_norm_g, sink, w_proj_a, w_proj_b, w_out)
    return _trunk(x_prompt, layers), _trunk(x_sample, layers)
```

```python
import functools
import math

import jax
import jax.numpy as jnp
from jax import lax
from jax.experimental import pallas as pl
from jax.experimental.pallas import tpu as pltpu

D_MODEL = 1024
HEAD = 64
N_HEADS = D_MODEL // HEAD
R_LORA = 64
HKV = 4
GRP = N_HEADS // HKV
D_KV = HKV * HEAD
WINDOW = 128
BLOCK = 128
GN_EPS = 64e-5
NORM_EPS = 1e-6
ROPE_THETA = 10000.0
C_SHIFT = 3 * D_MODEL + 4 * R_LORA
N_PLANES = 9
CHUNK = 64
VMEM_LIMIT = 56 * 1024 * 1024

P_R, P_K, P_V, P_GA, P_Q, P_GB, P_MA, P_MB, P_X = range(9)

f32 = jnp.float32
bf16 = jnp.bfloat16


def _dot(a, b):
    return jnp.dot(a.astype(bf16), b.astype(bf16), preferred_element_type=f32)


def _dot_tn(a, b):
    return lax.dot_general(a.astype(bf16), b.astype(bf16), (((0,), (0,)), ((), ())),
                           preferred_element_type=f32)


def _sigmoid(x):
    return 1.0 / (1.0 + jnp.exp(-x))


def _in_proj_body(x_ref, g_ref, w_ref, o_ref, ox_ref, h_ref):
    j = pl.program_id(1)

    @pl.when(j == 0)
    def _():
        x = x_ref[...]
        ms = jnp.mean(x * x, axis=-1, keepdims=True)
        h_ref[...] = (x * lax.rsqrt(ms + NORM_EPS) * g_ref[...]).astype(bf16)

    @pl.when(j < P_X)
    def _():
        o_ref[0] = jnp.dot(h_ref[...], w_ref[...], preferred_element_type=f32).astype(bf16)

    @pl.when(j == P_X)
    def _():
        ox_ref[...] = jnp.dot(h_ref[...], w_ref[...], preferred_element_type=f32)


def _in_proj(x2d, norm_g, w_p, *, tm):
    m = x2d.shape[0]
    return pl.pallas_call(
        _in_proj_body,
        grid=(m // tm, N_PLANES),
        in_specs=[
            pl.BlockSpec((tm, D_MODEL), lambda i, j: (i, 0)),
            pl.BlockSpec((1, D_MODEL), lambda i, j: (0, 0)),
            pl.BlockSpec((D_MODEL, D_MODEL), lambda i, j: (0, j)),
        ],
        out_specs=[pl.BlockSpec((1, tm, D_MODEL), lambda i, j: (jnp.minimum(j, P_X - 1), i, 0)),
                   pl.BlockSpec((tm, D_MODEL), lambda i, j: (i, 0))],
        out_shape=[jax.ShapeDtypeStruct((N_PLANES - 1, m, D_MODEL), bf16),
                   jax.ShapeDtypeStruct((m, D_MODEL), f32)],
        scratch_shapes=[pltpu.VMEM((tm, D_MODEL), bf16)],
        compiler_params=pltpu.CompilerParams(
            dimension_semantics=("parallel", "arbitrary"), vmem_limit_bytes=VMEM_LIMIT),
        name="in_proj",
    )(x2d, norm_g, w_p)


def _prep_layer_weights(w_in, shift_mu):
    c = C_SHIFT
    d = D_MODEL
    cols = [
        (0, d), (d, 2 * d), (2 * d, 3 * d),
        (c, c + d),
        (c + d, c + 2 * d),
        (c + 2 * d + 2 * D_KV, c + 3 * d + 2 * D_KV),
        (c + 3 * d + 2 * D_KV, c + 4 * d + 2 * D_KV),
        (c + 4 * d + 2 * D_KV, c + 5 * d + 2 * D_KV),
    ]
    planes = [w_in[:, a:b] for a, b in cols]
    small = jnp.concatenate([w_in[:, 3 * d:c], w_in[:, c + 2 * d:c + 2 * d + 2 * D_KV],
                             jnp.zeros((d, d - 4 * R_LORA - 2 * D_KV), w_in.dtype)], axis=1)
    w_p = jnp.concatenate(planes + [small], axis=1).astype(bf16)
    zeros = jnp.zeros((d,), f32)
    mu_small = jnp.concatenate([shift_mu[3 * d:c], jnp.zeros((d - 4 * R_LORA,), f32)])
    mu_p = jnp.concatenate([shift_mu[0:d], shift_mu[d:2 * d], shift_mu[2 * d:3 * d]] + [zeros] * 5 + [mu_small])
    return w_p, mu_p.reshape(1, N_PLANES * d)


PAIR = 2 * HEAD
PADR = 8
PRE, PRE_DONE, CHUNK_STAGE = "pre", "pre_done", "chunk"


def _token_shift(tt, seq):
    i = pl.program_id(0)
    first = (i * tt) % seq == 0
    last = ((i + 1) * tt) % seq == 0

    def shift(main, pv, nx, mu, buf):
        c = main.astype(f32)
        buf[PADR:PADR + tt, :] = c
        buf[PADR - 1:PADR, :] = jnp.where(first, 0.0, pv[pv.shape[0] - 1:, :].astype(f32))
        buf[PADR + tt:PADR + tt + 1, :] = jnp.where(last, 0.0, nx[:1, :].astype(f32))
        nbr = 0.5 * (buf[PADR - 1:PADR - 1 + tt, :] + buf[PADR + 1:PADR + 1 + tt, :])
        return c + mu * (nbr - c)

    return shift


def _rwkv_p_body(r_ref, r_pv, r_nx, k_ref, k_pv, k_nx, v_ref, v_pv, v_nx, mur_ref, muk_ref, muv_ref,
                 kk_ref, ka_ref, rk_ref,
                 y0_ref, bonus_ref, rp_ref, mn_ref,
                 at_s, rt_s, kp_s, bp_s, khm_s, bhm_s, atx_s, rtx_s, vx_s, pl_s, sh_s, *, tt, shift, zs, a_pre):
    nc = tt // CHUNK
    r = shift(r_ref[0], r_pv[0], r_nx[0], mur_ref[...], sh_s.at[0])
    k = shift(k_ref[0], k_pv[0], k_nx[0], muk_ref[...], sh_s.at[1])
    v = shift(v_ref[0], v_pv[0], v_nx[0], muv_ref[...], sh_s.at[2])
    yield PRE

    seg1, seg2 = _seg_matrix(PAIR), _seg_matrix(2 * PAIR)
    lo = lax.broadcasted_iota(jnp.int32, (tt, PAIR), 1) < HEAD
    swap = lambda t: pltpu.roll(t, HEAD, axis=1)

    def head_lo(t, h):
        return jnp.where(lo, t if h == 0 else swap(t), 0.0)

    def head_hi(t, h):
        return jnp.where(lo, 0.0, swap(t) if h == 0 else t)

    def head_own(t, h):
        return jnp.where(lo, t, 0.0) if h == 0 else jnp.where(lo, 0.0, t)

    for h in range(2):
        vx_s[h] = head_hi(v, h).astype(bf16)

    kkv = k * kk_ref[...]
    kk = kkv / jnp.maximum(jnp.sqrt(_segsum(kkv * kkv, seg1)), 1e-12)
    yield PRE

    rows = lax.broadcasted_iota(jnp.int32, (tt, tt), 0)
    cols = lax.broadcasted_iota(jnp.int32, (tt, tt), 1)
    same = (rows // CHUNK) == (cols // CHUNK)

    kmod_sum = jnp.zeros_like(k)
    for d in range(2):
        if d == 0:
            tri_bd = jnp.where(same & (cols <= rows), 1.0, 0.0)
        else:
            tri_bd = jnp.where(same & (cols >= rows), 1.0, 0.0)
        lw = -math.exp(-0.5) * _sigmoid(zs[d])
        a = _sigmoid(a_pre[d])
        yield PRE
        kmod = k * (1.0 + (a - 1.0) * ka_ref[...])
        kmod_sum = kmod_sum + kmod
        b = -(kk * a)
        lw0 = lw.astype(bf16)
        lw1 = (lw - lw0.astype(f32)).astype(bf16)
        cum2 = jnp.dot(tri_bd.astype(bf16), jnp.concatenate([lw0, lw1], axis=1), preferred_element_type=f32)
        cum = cum2[:, :PAIR] + cum2[:, PAIR:]
        edge = CHUNK - 1 if d == 0 else 0
        tot = jnp.concatenate([jnp.broadcast_to(cum[c * CHUNK + edge:c * CHUNK + edge + 1, :], (CHUNK, PAIR))
                               for c in range(nc)], axis=0)
        yield PRE
        p_inv = jnp.exp(-cum)
        p_end = jnp.exp(tot - cum)
        at = kk * jnp.exp(cum - lw)
        rt = r * jnp.exp(cum)
        kh = kmod * p_inv
        bh = b * p_inv
        p_tot = jnp.exp(tot)
        at_s[d] = at.astype(bf16)
        rt_s[d] = rt.astype(bf16)
        pl_s[d] = p_tot
        yield PRE
        kp_s[d] = (kmod * p_end).astype(bf16)
        bp_s[d] = (b * p_end).astype(bf16)
        for h in range(2):
            khm_s[d, h] = head_own(kh, h).astype(bf16)
            bhm_s[d, h] = head_own(bh, h).astype(bf16)
            atx_s[d, h] = head_lo(at, h).astype(bf16)
            rtx_s[d, h] = head_lo(rt, h)
        yield PRE

    bonus = _segsum(r * kmod_sum * rk_ref[...], seg2) * v
    yield PRE_DONE

    row = lax.broadcasted_iota(jnp.int32, (CHUNK, PAIR), 0)
    col = lax.broadcasted_iota(jnp.int32, (CHUNK, PAIR), 1)
    col_t = col % CHUNK
    lo_c = col < HEAD
    strict = (col_t < row, col_t > row)
    incl = (col_t <= row, col_t >= row)
    pairs = [(d, c) for d in range(2) for c in range(nc)]
    inst = [(d, c, h) for d, c in pairs for h in range(2)]
    cs = lambda c: slice(c * CHUNK, (c + 1) * CHUNK)
    swap_c = lambda t: pltpu.roll(t, HEAD, axis=1)

    top, bot = [], []
    for d, c in pairs:
        lhs = jnp.concatenate([at_s[d, cs(c)], rt_s[d, cs(c)]], axis=0)
        rhs = jnp.concatenate([bhm_s[d, 0, cs(c)], khm_s[d, 0, cs(c)],
                               bhm_s[d, 1, cs(c)], khm_s[d, 1, cs(c)]], axis=0)
        sc = lax.dot_general(lhs, rhs, (((1,), (1,)), ((), ())), preferred_element_type=f32)
        for h in range(2):
            top.append(jnp.where(strict[d], sc[:CHUNK, h * PAIR:(h + 1) * PAIR], 0.0))
            bot.append(jnp.where(incl[d], sc[CHUNK:, h * PAIR:(h + 1) * PAIR], 0.0))
    yield CHUNK_STAGE
    xs, aps = [], []
    for i, (d, c, h) in enumerate(inst):
        vx = vx_s[h, cs(c)]
        a_ak = jnp.where(lo_c, 0.0, top[i]).astype(bf16)
        akv = jnp.dot(a_ak, jnp.concatenate([vx, vx], axis=0), preferred_element_type=f32)
        xs.append(atx_s[d, h, cs(c)].astype(f32) + akv)
        aps.append(top[i][:, :CHUNK])
    n_dbl = CHUNK.bit_length() - 1
    for it in range(n_dbl):
        yield CHUNK_STAGE
        last = it + 1 == n_dbl
        res = [_dot(ap, x if last else jnp.concatenate([x, ap], axis=1)) for ap, x in zip(aps, xs)]
        xs = [x + rs[:, :PAIR] for x, rs in zip(xs, res)]
        if not last:
            aps = [rs[:, PAIR:] for rs in res]
    yield CHUNK_STAGE
    rhs_o = [jnp.concatenate([xs[i].astype(bf16), vx_s[h, cs(c)]], axis=0) for i, (d, c, h) in enumerate(inst)]
    rys = [_dot(bot[i], rhs_o[i]) + rtx_s[d, h, cs(c)] for i, (d, c, h) in enumerate(inst)]
    yield CHUNK_STAGE
    mns = [_dot_tn(jnp.concatenate([bp_s[d, cs(c), h * HEAD:(h + 1) * HEAD],
                                    kp_s[d, cs(c), h * HEAD:(h + 1) * HEAD]], axis=0), rhs_o[i])
           for i, (d, c, h) in enumerate(inst)]
    yield CHUNK_STAGE
    for i, (d, c, h) in enumerate(inst):
        p_blk = pl_s[d, c * CHUNK:c * CHUNK + 8, :]
        p_row = (p_blk if h == 0 else swap_c(p_blk))[0:1]
        mn_ref[d, c, h] = (mns[i] + jnp.where(row == col, p_row, 0.0)).astype(bf16)
    y0_fwd = {}
    for j, (d, c) in enumerate(pairs):
        ry0, ry1 = rys[2 * j], rys[2 * j + 1]
        rp_ref[d, cs(c), :] = jnp.where(lo_c, ry0, swap_c(ry1)).astype(bf16)
        y0 = jnp.where(lo_c, swap_c(ry0), ry1)
        if d == 0:
            y0_fwd[c] = y0
        else:
            y0_ref[cs(c), :] = (y0_fwd[c] + y0).astype(y0_ref.dtype)
    bonus_ref[...] = bonus.astype(bonus_ref.dtype)


N_IN_P = 23
N_OUT_P = 4


def _rwkv_p_multi(*refs, tt, seq, pp):
    ins, outs = refs[:N_IN_P], refs[N_IN_P:N_IN_P + N_OUT_P]
    scr, shx_s = refs[N_IN_P + N_OUT_P:-1], refs[-1]
    x_ref, x_pv, x_nx, mux_ref = ins[9], ins[10], ins[11], ins[15]
    w0_ref, wlu_ref, a0_ref, alu_ref = ins[16:20]
    shift = _token_shift(tt, seq)
    xs = shift(x_ref[...], x_pv[...], x_nx[...], mux_ref[...], shx_s)
    zs = [w0_ref[d:d + 1, :] + _dot(jnp.tanh(xs[:, d * R_LORA:(d + 1) * R_LORA]), wlu_ref[d]) for d in range(2)]
    a_pre = [a0_ref[d:d + 1, :] + _dot(xs[:, (2 + d) * R_LORA:(3 + d) * R_LORA], alu_ref[d]) for d in range(2)]
    stages = []
    for q in range(pp):
        lanes = pl.ds(q * PAIR, PAIR)
        lsl = slice(q * PAIR, (q + 1) * PAIR)
        view = lambda ref, lanes=lanes: ref.at[..., lanes]
        planes = [view(ref) for ref in ins[0:9]]
        gains = [view(ref) for ref in ins[12:15]]
        params = [view(ref) for ref in ins[20:23]]
        y0_ref, bonus_ref, rp_ref, mn_ref = outs
        stages.append(_rwkv_p_body(*planes, *gains, *params,
                                   view(y0_ref), view(bonus_ref), view(rp_ref), mn_ref.at[:, :, pl.ds(2 * q, 2)],
                                   *[s.at[q] for s in scr], tt=tt, shift=shift,
                                   zs=[z[:, lsl] for z in zs], a_pre=[a[:, lsl] for a in a_pre]))
    def run_pre(stage):
        while next(stage) != PRE_DONE:
            pass

    run_pre(stages[0])
    for q in range(pp):
        chunk, pre = stages[q], (stages[q + 1] if q + 1 < pp else None)
        chunk_live, pre_live = True, pre is not None
        while chunk_live or pre_live:
            if chunk_live:
                chunk_live = next(chunk, None) is not None
            if pre_live:
                pre_live = next(pre) != PRE_DONE


def _seg_matrix(n):
    r = lax.broadcasted_iota(jnp.int32, (n, n), 0) // HEAD
    c = lax.broadcasted_iota(jnp.int32, (n, n), 1) // HEAD
    return jnp.where(r == c, 1.0, 0.0).astype(bf16)


def _segsum(x, seg):
    x0 = x.astype(bf16)
    if seg.shape[0] == PAIR:
        return jnp.dot(x0, seg, preferred_element_type=f32)
    x1 = (x - x0.astype(f32)).astype(bf16)
    s = jnp.dot(jnp.concatenate([x0, x1], axis=1), seg, preferred_element_type=f32)
    return s[:, :PAIR] + s[:, PAIR:]


def _rwkv_p(p8, px, mu_p, w0, wlu, a0, alu, k_k, k_a, r_k, *, tt, seq, pp):
    m = px.shape[0]
    nt = m // tt
    npair = D_MODEL // PAIR
    cpt = tt // CHUNK
    wl = pp * PAIR
    hb, hx = 16, 8

    def tiles(plane):
        return [pl.BlockSpec((1, tt, wl), lambda i, j: (plane, i, j)),
                pl.BlockSpec((1, hb, wl), lambda i, j: (plane, jnp.maximum(i * (tt // hb) - 1, 0), j)),
                pl.BlockSpec((1, hb, wl), lambda i, j: (plane, jnp.minimum((i + 1) * (tt // hb), m // hb - 1), j))]

    vec = pl.BlockSpec((1, wl), lambda i, j: (0, j))
    mu = lambda plane: pl.BlockSpec((1, wl), lambda i, j: (0, plane * (npair // pp) + j))
    wx = 4 * R_LORA
    return pl.pallas_call(
        functools.partial(_rwkv_p_multi, tt=tt, seq=seq, pp=pp),
        grid=(nt, npair // pp),
        in_specs=tiles(P_R) + tiles(P_K) + tiles(P_V) + [
            pl.BlockSpec((tt, wx), lambda i, j: (i, 0)),
            pl.BlockSpec((hx, wx), lambda i, j: (jnp.maximum(i * (tt // hx) - 1, 0), 0)),
            pl.BlockSpec((hx, wx), lambda i, j: (jnp.minimum((i + 1) * (tt // hx), m // hx - 1), 0)),
            mu(P_R), mu(P_K), mu(P_V),
            pl.BlockSpec((1, wx), lambda i, j: (0, P_X * D_MODEL // wx)),
            pl.BlockSpec((2, wl), lambda i, j: (0, j)),
            pl.BlockSpec((2, R_LORA, wl), lambda i, j: (0, 0, j)),
            pl.BlockSpec((2, wl), lambda i, j: (0, j)),
            pl.BlockSpec((2, R_LORA, wl), lambda i, j: (0, 0, j)),
            vec, vec, vec,
        ],
        out_specs=[
            pl.BlockSpec((tt, wl), lambda i, j: (i, j)),
            pl.BlockSpec((tt, wl), lambda i, j: (i, j)),
            pl.BlockSpec((2, tt, wl), lambda i, j: (0, i, j)),
            pl.BlockSpec((2, cpt, 2 * pp, HEAD, PAIR), lambda i, j: (0, i, j, 0, 0)),
        ],
        out_shape=[
            jax.ShapeDtypeStruct((m, D_MODEL), bf16),
            jax.ShapeDtypeStruct((m, D_MODEL), bf16),
            jax.ShapeDtypeStruct((2, m, D_MODEL), bf16),
            jax.ShapeDtypeStruct((2, m // CHUNK, N_HEADS, HEAD, PAIR), bf16),
        ],
        scratch_shapes=(
            [pltpu.VMEM((pp, 2, tt, PAIR), bf16) for _ in range(4)]
            + [pltpu.VMEM((pp, 2, 2, tt, PAIR), bf16) for _ in range(3)]
            + [pltpu.VMEM((pp, 2, 2, tt, PAIR), f32),
               pltpu.VMEM((pp, 2, tt, PAIR), bf16),
               pltpu.VMEM((pp, 2, tt, PAIR), f32),
               pltpu.VMEM((pp, 3, tt + 2 * PADR, PAIR), f32),
               pltpu.VMEM((tt + 2 * PADR, wx), f32)]),
        compiler_params=pltpu.CompilerParams(
            dimension_semantics=("parallel", "parallel"), vmem_limit_bytes=VMEM_LIMIT),
        name="rwkv_p",
    )(p8, p8, p8, p8, p8, p8, p8, p8, p8, px, px, px, mu_p, mu_p, mu_p, mu_p, w0, wlu, a0, alu, k_k, k_a, r_k)


def _rwkv_s_body(mnf_ref, mnb_ref, sf_ref, sb_ref, st_ref, *, grp):
    @pl.when(pl.program_id(1) == 0)
    def _():
        st_ref[...] = jnp.zeros_like(st_ref)

    zero = jnp.zeros((HEAD, HEAD), bf16)

    def step(d, mn_ref, s_out_ref, cc):
        s = st_ref[d].astype(bf16)
        for p in range(N_HEADS // 2):
            s_out_ref[cc, p] = jnp.concatenate([jnp.concatenate([s[2 * p], zero], axis=1),
                                                jnp.concatenate([zero, s[2 * p + 1]], axis=1)], axis=0)
        m_t = mn_ref[0, cc, :, :, 0:HEAD]
        n_t = mn_ref[0, cc, :, :, HEAD:].astype(f32)
        st_ref[d] = jnp.einsum('hij,hjk->hik', m_t, s, preferred_element_type=f32) + n_t

    for cc in range(grp):
        step(0, mnf_ref, sf_ref, cc)
        step(1, mnb_ref, sb_ref, grp - 1 - cc)


def _rwkv_s(mn, *, n_seq, grp):
    nchunk = mn.shape[1]
    npair = N_HEADS // 2
    ng = nchunk // n_seq // grp
    blk_in = (1, grp, N_HEADS, HEAD, PAIR)
    blk_out = (grp, npair, PAIR, PAIR)
    out = jax.ShapeDtypeStruct((nchunk, npair, PAIR, PAIR), bf16)
    return pl.pallas_call(
        functools.partial(_rwkv_s_body, grp=grp),
        grid=(n_seq, ng),
        in_specs=[pl.BlockSpec(blk_in, lambda b, g: (0, b * ng + g, 0, 0, 0)),
                  pl.BlockSpec(blk_in, lambda b, g: (1, b * ng + ng - 1 - g, 0, 0, 0))],
        out_specs=[pl.BlockSpec(blk_out, lambda b, g: (b * ng + g, 0, 0, 0)),
                   pl.BlockSpec(blk_out, lambda b, g: (b * ng + ng - 1 - g, 0, 0, 0))],
        out_shape=[out, out],
        scratch_shapes=[pltpu.VMEM((2, N_HEADS, HEAD, HEAD), f32)],
        compiler_params=pltpu.CompilerParams(
            dimension_semantics=("parallel", "arbitrary"), vmem_limit_bytes=VMEM_LIMIT),
        name="rwkv_s",
    )(mn, mn)


def _rwkv_f_body(y0_ref, bonus_ref, rp_ref, sf_ref, sb_ref, gate_ref, lng_ref, lnb_ref, o_ref, *, tt):
    npair = D_MODEL // PAIR
    nc = tt // CHUNK
    cs = lambda c: slice(c * CHUNK, (c + 1) * CHUNK)
    ps = lambda p: slice(p * PAIR, (p + 1) * PAIR)
    yf = [[jnp.dot(rp_ref[0, cs(c), ps(p)], sf_ref[c, p], preferred_element_type=f32) for c in range(nc)]
          for p in range(npair)]
    yb = [[jnp.dot(rp_ref[1, cs(c), ps(p)], sb_ref[c, p], preferred_element_type=f32) for c in range(nc)]
          for p in range(npair)]
    seg1, seg2 = _seg_matrix(PAIR), _seg_matrix(2 * PAIR)
    ys = [y0_ref[:, ps(p)].astype(f32) + jnp.concatenate(yf[p], axis=0) + jnp.concatenate(yb[p], axis=0)
          for p in range(npair)]
    mus = [_segsum(y, seg2) * (1.0 / HEAD) for y in ys]
    dvs = [y - mu for y, mu in zip(ys, mus)]
    vrs = [_segsum(dv * dv, seg1) * (1.0 / HEAD) for dv in dvs]
    for p in range(npair):
        yn = dvs[p] * lax.rsqrt(vrs[p] + GN_EPS) * lng_ref[:, ps(p)] + lnb_ref[:, ps(p)] + bonus_ref[:, ps(p)].astype(f32)
        g = gate_ref[0, :, ps(p)].astype(f32)
        o_ref[:, ps(p)] = (yn * (g * _sigmoid(g))).astype(bf16)


def _rwkv_f(y0, bonus, rp, sf, sb, p8, ln_g, ln_b, *, tt):
    m = y0.shape[0]
    cpt = tt // CHUNK
    npair = D_MODEL // PAIR
    tile = pl.BlockSpec((tt, D_MODEL), lambda i: (i, 0))
    vec = pl.BlockSpec((1, D_MODEL), lambda i: (0, 0))
    state = pl.BlockSpec((cpt, npair, PAIR, PAIR), lambda i: (i, 0, 0, 0))
    return pl.pallas_call(
        functools.partial(_rwkv_f_body, tt=tt),
        grid=(m // tt,),
        in_specs=[
            tile, tile,
            pl.BlockSpec((2, tt, D_MODEL), lambda i: (0, i, 0)),
            state, state,
            pl.BlockSpec((1, tt, D_MODEL), lambda i: (P_GA, i, 0)),
            vec, vec,
        ],
        out_specs=tile,
        out_shape=jax.ShapeDtypeStruct((m, D_MODEL), bf16),
        compiler_params=pltpu.CompilerParams(dimension_semantics=("parallel",), vmem_limit_bytes=VMEM_LIMIT),
        name="rwkv_f",
    )(y0, bonus, rp, sf, sb, p8, ln_g, ln_b)


def _attn_body(q_ref, kp_ref, kc_ref, kn_ref, vp_ref, vc_ref, vn_ref, tp_ref, tc_ref, tn_ref, gate_ref,
               qg_ref, kg_ref, sink_ref, o_ref, *, seq, qb):
    step_rows = qb * BLOCK
    start = (pl.program_id(0) % (seq // step_rows)) * step_rows
    ctx = BLOCK + 2 * WINDOW
    qoff = lax.broadcasted_iota(jnp.int32, (BLOCK, ctx), 0)
    koff = lax.broadcasted_iota(jnp.int32, (BLOCK, ctx), 1) - WINDOW
    valid = []
    for b in range(qb):
        kpos = start + b * BLOCK + koff
        valid.append((jnp.abs(qoff - koff) <= WINDOW) & (kpos >= 0) & (kpos < seq))

    seg = _seg_matrix(PAIR)
    swap = lambda t: pltpu.roll(t, HEAD, axis=1)

    def norm_rope(x, gain, tab):
        lane = lax.broadcasted_iota(jnp.int32, x.shape, 1)
        xn = x * lax.rsqrt(_segsum(x * x, seg) * (1.0 / HEAD) + NORM_EPS) * gain
        rot = jnp.where(lane % HEAD < HEAD // 2, pltpu.roll(xn, PAIR - HEAD // 2, axis=1),
                        pltpu.roll(xn, HEAD // 2, axis=1))
        return xn * tab[:, :PAIR] + rot * tab[:, PAIR:]

    def place(t, e, want_lo):
        lo = lax.broadcasted_iota(jnp.int32, t.shape, 1) < HEAD
        src = t if (e == 0) == want_lo else swap(t)
        return (jnp.where(lo, src, 0.0) if want_lo else jnp.where(lo, 0.0, src)).astype(bf16)

    tab_k = jnp.concatenate([tp_ref[...], tc_ref[...], tn_ref[...]], axis=0)
    tab_q = tc_ref[...]
    k_lo, k_hi, v_lo, v_hi = [], [], [], []
    for kp in range(D_KV // PAIR):
        ps_ = slice(kp * PAIR, (kp + 1) * PAIR)
        kx = jnp.concatenate([kp_ref[:, ps_], kc_ref[:, ps_], kn_ref[:, ps_]], axis=0)
        vx = jnp.concatenate([vp_ref[:, ps_], vc_ref[:, ps_], vn_ref[:, ps_]], axis=0)
        kr = norm_rope(kx, kg_ref[...], tab_k)
        for e in range(2):
            k_lo.append(place(kr, e, True))
            k_hi.append(place(kr, e, False))
            v_lo.append(place(vx, e, True))
            v_hi.append(place(vx, e, False))
    npair = D_MODEL // PAIR
    psl = lambda p: slice(p * PAIR, (p + 1) * PAIR)
    log2e = math.log2(math.e)
    qrs = [(norm_rope(q_ref[0, :, psl(p)].astype(f32), qg_ref[...], tab_q) * (HEAD ** -0.5 * log2e)).astype(bf16)
           for p in range(npair)]
    inst = [(b, hq) for b in range(qb) for hq in range(N_HEADS)]
    rows = lambda b: slice(b * BLOCK, (b + 1) * BLOCK)
    keys = lambda b: slice(b * BLOCK, b * BLOCK + ctx)
    def masked(s, ok):
        return jnp.concatenate([jnp.where(ok[:, :WINDOW], s[:, :WINDOW], -1e30), s[:, WINDOW:WINDOW + BLOCK],
                                jnp.where(ok[:, WINDOW + BLOCK:], s[:, WINDOW + BLOCK:], -1e30)], axis=1)

    ss = [masked(lax.dot_general(qrs[hq // 2][rows(b)], (k_lo if hq % 2 == 0 else k_hi)[hq // GRP][keys(b)],
                                 (((1,), (1,)), ((), ())), preferred_element_type=f32), valid[b])
          for b, hq in inst]
    sinks = [sink_ref[0:1, hq:hq + 1] * log2e for b, hq in inst]
    mxs = [jnp.maximum(jnp.max(s, axis=-1, keepdims=True), sk) for s, sk in zip(ss, sinks)]
    ps = [jnp.exp2(s - mx) for s, mx in zip(ss, mxs)]
    dens = [jnp.sum(p, axis=-1, keepdims=True) + jnp.exp2(sk - mx) for p, sk, mx in zip(ps, sinks, mxs)]
    os_ = [jnp.dot(p.astype(bf16), (v_lo if hq % 2 == 0 else v_hi)[hq // GRP][keys(b)], preferred_element_type=f32) / den
           for (b, hq), p, den in zip(inst, ps, dens)]
    for b in range(qb):
        for p in range(npair):
            gt = gate_ref[0, rows(b), psl(p)].astype(f32)
            i0 = b * N_HEADS + 2 * p
            o_ref[rows(b), psl(p)] = ((os_[i0] + os_[i0 + 1]) * (gt * _sigmoid(gt))).astype(bf16)


def _attn(p8, px, rope_tab, q_g, k_g, sink, *, seq, qb):
    m = px.shape[0]
    nblk = seq // BLOCK
    rows = qb * BLOCK

    def nbr(delta):
        def f(i):
            t = (i * qb) % nblk
            return i * qb - t + jnp.clip(t + delta, 0, nblk - 1)
        return f

    def kv(col):
        return [pl.BlockSpec((BLOCK, D_KV), lambda i, f=nbr(-1): (f(i), col)),
                pl.BlockSpec((rows, D_KV), lambda i: (i, col)),
                pl.BlockSpec((BLOCK, D_KV), lambda i, f=nbr(qb): (f(i), col))]

    tabs = [pl.BlockSpec((BLOCK, 2 * PAIR), lambda i, f=nbr(-1): (f(i) % nblk, 0)),
            pl.BlockSpec((rows, 2 * PAIR), lambda i: (i % (nblk // qb), 0)),
            pl.BlockSpec((BLOCK, 2 * PAIR), lambda i, f=nbr(qb): (f(i) % nblk, 0))]

    return pl.pallas_call(
        functools.partial(_attn_body, seq=seq, qb=qb),
        grid=(m // rows,),
        in_specs=[pl.BlockSpec((1, rows, D_MODEL), lambda i: (P_Q, i, 0))] + kv(1) + kv(2) + tabs + [
            pl.BlockSpec((1, rows, D_MODEL), lambda i: (P_GB, i, 0)),
            pl.BlockSpec((1, PAIR), lambda i: (0, 0)),
            pl.BlockSpec((1, PAIR), lambda i: (0, 0)),
            pl.BlockSpec((1, N_HEADS), lambda i: (0, 0)),
        ],
        out_specs=pl.BlockSpec((rows, D_MODEL), lambda i: (i, 0)),
        out_shape=jax.ShapeDtypeStruct((m, D_MODEL), bf16),
        compiler_params=pltpu.CompilerParams(dimension_semantics=("parallel",), vmem_limit_bytes=VMEM_LIMIT),
        name="attn",
    )(p8, px, px, px, px, px, px, rope_tab, rope_tab, rope_tab, p8, q_g, k_g, sink)


def _rope_table(seq):
    inv = 1.0 / (ROPE_THETA ** (jnp.arange(0, HEAD, 2, dtype=f32) / HEAD))
    ang = jnp.arange(seq, dtype=f32)[:, None] * inv[None, :]
    cos, sin = jnp.cos(ang), jnp.sin(ang)
    return jnp.concatenate([cos, cos, cos, cos, -sin, sin, -sin, sin], axis=1)


def _out_proj_body(x_ref, ga_ref, gb_ref, ma_ref, mb_ref, wa_ref, wb_ref, wo_ref, o_ref):
    ya = jnp.dot(ga_ref[...], wa_ref[...], preferred_element_type=f32)
    yb = jnp.dot(gb_ref[...], wb_ref[...], preferred_element_type=f32)
    mixed = _sigmoid(ma_ref[0].astype(f32)) * ya + _sigmoid(mb_ref[0].astype(f32)) * yb
    o_ref[...] = x_ref[...] + jnp.dot(mixed.astype(bf16), wo_ref[...], preferred_element_type=f32)


def _out_proj(x2d, ga, gb, p8, wa, wb, wo, *, tm):
    m = x2d.shape[0]
    row = pl.BlockSpec((tm, D_MODEL), lambda i: (i, 0))
    wspec = pl.BlockSpec((D_MODEL, D_MODEL), lambda i: (0, 0))
    return pl.pallas_call(
        _out_proj_body,
        grid=(m // tm,),
        in_specs=[row, row, row,
                  pl.BlockSpec((1, tm, D_MODEL), lambda i: (P_MA, i, 0)),
                  pl.BlockSpec((1, tm, D_MODEL), lambda i: (P_MB, i, 0)),
                  wspec, wspec, wspec],
        out_specs=row,
        out_shape=jax.ShapeDtypeStruct((m, D_MODEL), f32),
        compiler_params=pltpu.CompilerParams(dimension_semantics=("parallel",), vmem_limit_bytes=VMEM_LIMIT),
        name="out_proj",
    )(x2d, ga, gb, p8, p8, wa, wb, wo)


def _layer(x2d, lw, rope_tab, *, n_seq, seq):
    tm = min(1024, seq)
    tt = min(256, seq)
    grp = min(8, seq // CHUNK)
    p8, px = _in_proj(x2d, lw["norm_g"], lw["w_p"], tm=min(2 * tm, seq))
    y0, bonus, rp, mn = _rwkv_p(p8, px, lw["mu_p"], lw["w0"], lw["wlu"], lw["a0"], lw["alu"], lw["k_k"], lw["k_a"],
                                lw["r_k"], tt=tt, seq=seq, pp=8)
    sf, sb = _rwkv_s(mn, n_seq=n_seq, grp=grp)
    ga = _rwkv_f(y0, bonus, rp, sf, sb, p8, lw["ln_g"], lw["ln_b"], tt=min(2 * tt, seq))
    gb = _attn(p8, px, rope_tab, lw["q_g"], lw["k_g"], lw["sink"], seq=seq, qb=2)
    return _out_proj(x2d, ga, gb, p8, lw["wa"], lw["wb"], lw["wo"], tm=tm)


def _trunk(x, layers):
    n_seq, seq, _ = x.shape
    rope_tab = _rope_table(seq)
    x2d = x.reshape(n_seq * seq, D_MODEL)
    for lw in layers:
        x2d = _layer(x2d, lw, rope_tab, n_seq=n_seq, seq=seq)
    return x2d.reshape(n_seq, seq, D_MODEL)


def _prep_layers(norm_g, w_in, shift_mu, w0, w_lora_up, a0, a_lora_up, k_k, k_a, r_k, ln_x_g, ln_x_b,
                 q_norm_g, k_norm_g, sink, w_proj_a, w_proj_b, w_out):
    layers = []
    row = lambda v: v.reshape(1, -1).astype(f32)
    for l in range(norm_g.shape[0]):
        w_p, mu_p = _prep_layer_weights(w_in[l], shift_mu[l])
        layers.append(dict(
            norm_g=row(norm_g[l]), w_p=w_p, mu_p=mu_p,
            w0=w0[l].astype(f32), wlu=w_lora_up[l].astype(bf16), a0=a0[l].astype(f32), alu=a_lora_up[l].astype(bf16),
            k_k=row(k_k[l]), k_a=row(k_a[l]), r_k=row(r_k[l]), ln_g=row(ln_x_g[l]), ln_b=row(ln_x_b[l]),
            q_g=row(jnp.tile(q_norm_g[l], 2)), k_g=row(jnp.tile(k_norm_g[l], 2)), sink=row(sink[l]),
            wa=w_proj_a[l].astype(bf16), wb=w_proj_b[l].astype(bf16), wo=w_out[l].astype(bf16)))
    return layers


def kernel(x_prompt, x_sample, norm_g, w_in, shift_mu, w0, w_lora_up, a0, a_lora_up, k_k, k_a, r_k,
           ln_x_g, ln_x_b, q_norm_g, k_norm_g, sink, w_proj_a, w_proj_b, w_out):
    layers = _prep_layers(norm_g, w_in, shift_mu, w0, w_lora_up, a0, a_lora_up, k_k, k_a, r_k, ln_x_g, ln_x_b,
                          q_norm_g, k_norm_g, sink, w_proj_a, w_proj_b, w_out)
    return _trunk(x_prompt, layers), _trunk(x_sample, layers)
```

```python
import functools
import math

import jax
import jax.numpy as jnp
from jax import lax
from jax.experimental import pallas as pl
from jax.experimental.pallas import tpu as pltpu

D_MODEL = 1024
HEAD = 64
N_HEADS = D_MODEL // HEAD
R_LORA = 64
HKV = 4
GRP = N_HEADS // HKV
D_KV = HKV * HEAD
WINDOW = 128
BLOCK = 128
GN_EPS = 64e-5
NORM_EPS = 1e-6
ROPE_THETA = 10000.0
C_SHIFT = 3 * D_MODEL + 4 * R_LORA
N_PLANES = 9
CHUNK = 64
VMEM_LIMIT = 56 * 1024 * 1024

TM_IN = 2048
TM_OUT = 1024
TT_P = 256
TT_F = 512
GRP_S = 8
PAIRS_P = 8
QB_ATTN = 4

P_R, P_K, P_V, P_GA, P_Q, P_GB, P_MA, P_MB, P_X = range(9)

f32 = jnp.float32
bf16 = jnp.bfloat16


def _dot(a, b):
    return jnp.dot(a.astype(bf16), b.astype(bf16), preferred_element_type=f32)


def _dot_tn(a, b):
    return lax.dot_general(a.astype(bf16), b.astype(bf16), (((0,), (0,)), ((), ())),
                           preferred_element_type=f32)


def _sigmoid(x):
    return 1.0 / (1.0 + jnp.exp(-x))


def _in_proj_body(x_ref, g_ref, w_ref, o_ref, ox_ref, h_ref):
    j = pl.program_id(1)

    @pl.when(j == 0)
    def _():
        x = x_ref[...]
        ms = jnp.mean(x * x, axis=-1, keepdims=True)
        h_ref[...] = (x * lax.rsqrt(ms + NORM_EPS) * g_ref[...]).astype(bf16)

    @pl.when(j < P_X)
    def _():
        o_ref[0] = jnp.dot(h_ref[...], w_ref[...], preferred_element_type=f32).astype(bf16)

    @pl.when(j == P_X)
    def _():
        ox_ref[...] = jnp.dot(h_ref[...], w_ref[...], preferred_element_type=f32)


def _in_proj(x2d, norm_g, w_p, *, tm):
    m = x2d.shape[0]
    return pl.pallas_call(
        _in_proj_body,
        grid=(m // tm, N_PLANES),
        in_specs=[
            pl.BlockSpec((tm, D_MODEL), lambda i, j: (i, 0)),
            pl.BlockSpec((1, D_MODEL), lambda i, j: (0, 0)),
            pl.BlockSpec((D_MODEL, D_MODEL), lambda i, j: (0, j)),
        ],
        out_specs=[pl.BlockSpec((1, tm, D_MODEL), lambda i, j: (jnp.minimum(j, P_X - 1), i, 0)),
                   pl.BlockSpec((tm, D_MODEL), lambda i, j: (i, 0))],
        out_shape=[jax.ShapeDtypeStruct((N_PLANES - 1, m, D_MODEL), bf16),
                   jax.ShapeDtypeStruct((m, D_MODEL), f32)],
        scratch_shapes=[pltpu.VMEM((tm, D_MODEL), bf16)],
        compiler_params=pltpu.CompilerParams(
            dimension_semantics=("parallel", "arbitrary"), vmem_limit_bytes=VMEM_LIMIT),
        name="in_proj",
    )(x2d, norm_g, w_p)


def _prep_layer_weights(w_in, shift_mu):
    c = C_SHIFT
    d = D_MODEL
    cols = [
        (0, d), (d, 2 * d), (2 * d, 3 * d),
        (c, c + d),
        (c + d, c + 2 * d),
        (c + 2 * d + 2 * D_KV, c + 3 * d + 2 * D_KV),
        (c + 3 * d + 2 * D_KV, c + 4 * d + 2 * D_KV),
        (c + 4 * d + 2 * D_KV, c + 5 * d + 2 * D_KV),
    ]
    planes = [w_in[:, a:b] for a, b in cols]
    small = jnp.concatenate([w_in[:, 3 * d:c], w_in[:, c + 2 * d:c + 2 * d + 2 * D_KV],
                             jnp.zeros((d, d - 4 * R_LORA - 2 * D_KV), w_in.dtype)], axis=1)
    w_p = jnp.concatenate(planes + [small], axis=1).astype(bf16)
    zeros = jnp.zeros((d,), f32)
    mu_small = jnp.concatenate([shift_mu[3 * d:c], jnp.zeros((d - 4 * R_LORA,), f32)])
    mu_p = jnp.concatenate([shift_mu[0:d], shift_mu[d:2 * d], shift_mu[2 * d:3 * d]] + [zeros] * 5 + [mu_small])
    return w_p, mu_p.reshape(1, N_PLANES * d)


PAIR = 2 * HEAD
SUBLANES = 8
PADR = SUBLANES
PRE, PRE_DONE, CHUNK_STAGE = "pre", "pre_done", "chunk"


def _token_shift(tt, seq):
    i = pl.program_id(0)
    first = (i * tt) % seq == 0
    last = ((i + 1) * tt) % seq == 0

    def shift(main, pv, nx, mu, buf):
        c = main.astype(f32)
        buf[PADR:PADR + tt, :] = c
        buf[PADR - 1:PADR, :] = jnp.where(first, 0.0, pv[pv.shape[0] - 1:, :].astype(f32))
        buf[PADR + tt:PADR + tt + 1, :] = jnp.where(last, 0.0, nx[:1, :].astype(f32))
        nbr = 0.5 * (buf[PADR - 1:PADR - 1 + tt, :] + buf[PADR + 1:PADR + 1 + tt, :])
        return c + mu * (nbr - c)

    return shift


def _rwkv_p_body(r_ref, r_pv, r_nx, k_ref, k_pv, k_nx, v_ref, v_pv, v_nx, mur_ref, muk_ref, muv_ref,
                 kk_ref, ka_ref, rk_ref,
                 y0_ref, bonus_ref, rp_ref, mn_ref,
                 at_s, rt_s, kp_s, bp_s, khm_s, bhm_s, atx_s, rtx_s, vx_s, pl_s, sh_s, *, tt, shift, zs, a_pre):
    nc = tt // CHUNK
    r = shift(r_ref[0], r_pv[0], r_nx[0], mur_ref[...], sh_s.at[0])
    k = shift(k_ref[0], k_pv[0], k_nx[0], muk_ref[...], sh_s.at[1])
    v = shift(v_ref[0], v_pv[0], v_nx[0], muv_ref[...], sh_s.at[2])
    yield PRE

    seg1, seg2 = _seg_matrix(PAIR), _seg_matrix(2 * PAIR)
    lo = lax.broadcasted_iota(jnp.int32, (tt, PAIR), 1) < HEAD
    swap = lambda t: pltpu.roll(t, HEAD, axis=1)

    def head_lo(t, h):
        return jnp.where(lo, t if h == 0 else swap(t), 0.0)

    def head_hi(t, h):
        return jnp.where(lo, 0.0, swap(t) if h == 0 else t)

    def head_own(t, h):
        return jnp.where(lo, t, 0.0) if h == 0 else jnp.where(lo, 0.0, t)

    for h in range(2):
        vx_s[h] = head_hi(v, h).astype(bf16)

    kkv = k * kk_ref[...]
    kk = kkv / jnp.maximum(jnp.sqrt(_segsum(kkv * kkv, seg1)), 1e-12)
    yield PRE

    rows = lax.broadcasted_iota(jnp.int32, (tt, tt), 0)
    cols = lax.broadcasted_iota(jnp.int32, (tt, tt), 1)
    same = (rows // CHUNK) == (cols // CHUNK)

    kmod_sum = jnp.zeros_like(k)
    for d in range(2):
        if d == 0:
            tri_bd = jnp.where(same & (cols <= rows), 1.0, 0.0)
        else:
            tri_bd = jnp.where(same & (cols >= rows), 1.0, 0.0)
        lw = -math.exp(-0.5) * _sigmoid(zs[d])
        a = _sigmoid(a_pre[d])
        yield PRE
        kmod = k * (1.0 + (a - 1.0) * ka_ref[...])
        kmod_sum = kmod_sum + kmod
        b = -(kk * a)
        lw0 = lw.astype(bf16)
        lw1 = (lw - lw0.astype(f32)).astype(bf16)
        cum2 = jnp.dot(tri_bd.astype(bf16), jnp.concatenate([lw0, lw1], axis=1), preferred_element_type=f32)
        cum = cum2[:, :PAIR] + cum2[:, PAIR:]
        edge = CHUNK - 1 if d == 0 else 0
        tot = jnp.concatenate([jnp.broadcast_to(cum[c * CHUNK + edge:c * CHUNK + edge + 1, :], (CHUNK, PAIR))
                               for c in range(nc)], axis=0)
        yield PRE
        p_inv = jnp.exp(-cum)
        p_end = jnp.exp(tot - cum)
        at = kk * jnp.exp(cum - lw)
        rt = r * jnp.exp(cum)
        kh = kmod * p_inv
        bh = b * p_inv
        p_tot = jnp.exp(tot)
        at_s[d] = at.astype(bf16)
        rt_s[d] = rt.astype(bf16)
        pl_s[d] = p_tot
        yield PRE
        kp_s[d] = (kmod * p_end).astype(bf16)
        bp_s[d] = (b * p_end).astype(bf16)
        for h in range(2):
            khm_s[d, h] = head_own(kh, h).astype(bf16)
            bhm_s[d, h] = head_own(bh, h).astype(bf16)
            atx_s[d, h] = head_lo(at, h).astype(bf16)
            rtx_s[d, h] = head_lo(rt, h)
        yield PRE

    bonus = _segsum(r * kmod_sum * rk_ref[...], seg2) * v
    yield PRE_DONE

    row = lax.broadcasted_iota(jnp.int32, (CHUNK, PAIR), 0)
    col = lax.broadcasted_iota(jnp.int32, (CHUNK, PAIR), 1)
    col_t = col % CHUNK
    lo_c = col < HEAD
    strict = (col_t < row, col_t > row)
    incl = (col_t <= row, col_t >= row)
    pairs = [(d, c) for d in range(2) for c in range(nc)]
    inst = [(d, c, h) for d, c in pairs for h in range(2)]
    cs = lambda c: slice(c * CHUNK, (c + 1) * CHUNK)
    swap_c = lambda t: pltpu.roll(t, HEAD, axis=1)

    top, bot = [], []
    for d, c in pairs:
        lhs = jnp.concatenate([at_s[d, cs(c)], rt_s[d, cs(c)]], axis=0)
        rhs = jnp.concatenate([bhm_s[d, 0, cs(c)], khm_s[d, 0, cs(c)],
                               bhm_s[d, 1, cs(c)], khm_s[d, 1, cs(c)]], axis=0)
        sc = lax.dot_general(lhs, rhs, (((1,), (1,)), ((), ())), preferred_element_type=f32)
        for h in range(2):
            top.append(jnp.where(strict[d], sc[:CHUNK, h * PAIR:(h + 1) * PAIR], 0.0))
            bot.append(jnp.where(incl[d], sc[CHUNK:, h * PAIR:(h + 1) * PAIR], 0.0))
    yield CHUNK_STAGE
    xs, aps = [], []
    for i, (d, c, h) in enumerate(inst):
        vx = vx_s[h, cs(c)]
        a_ak = jnp.where(lo_c, 0.0, top[i]).astype(bf16)
        akv = jnp.dot(a_ak, jnp.concatenate([vx, vx], axis=0), preferred_element_type=f32)
        xs.append(atx_s[d, h, cs(c)].astype(f32) + akv)
        aps.append(top[i][:, :CHUNK])
    n_dbl = CHUNK.bit_length() - 1
    for it in range(n_dbl):
        yield CHUNK_STAGE
        last = it + 1 == n_dbl
        res = [_dot(ap, x if last else jnp.concatenate([x, ap], axis=1)) for ap, x in zip(aps, xs)]
        xs = [x + rs[:, :PAIR] for x, rs in zip(xs, res)]
        if not last:
            aps = [rs[:, PAIR:] for rs in res]
    yield CHUNK_STAGE
    rhs_o = [jnp.concatenate([xs[i].astype(bf16), vx_s[h, cs(c)]], axis=0) for i, (d, c, h) in enumerate(inst)]
    rys = [_dot(bot[i], rhs_o[i]) + rtx_s[d, h, cs(c)] for i, (d, c, h) in enumerate(inst)]
    yield CHUNK_STAGE
    mns = [_dot_tn(jnp.concatenate([bp_s[d, cs(c), h * HEAD:(h + 1) * HEAD],
                                    kp_s[d, cs(c), h * HEAD:(h + 1) * HEAD]], axis=0), rhs_o[i])
           for i, (d, c, h) in enumerate(inst)]
    yield CHUNK_STAGE
    for i, (d, c, h) in enumerate(inst):
        p_blk = pl_s[d, c * CHUNK:c * CHUNK + SUBLANES, :]
        p_row = (p_blk if h == 0 else swap_c(p_blk))[0:1]
        mn_ref[d, c, h] = (mns[i] + jnp.where(row == col, p_row, 0.0)).astype(bf16)
    y0_fwd = {}
    for j, (d, c) in enumerate(pairs):
        ry0, ry1 = rys[2 * j], rys[2 * j + 1]
        rp_ref[d, cs(c), :] = jnp.where(lo_c, ry0, swap_c(ry1)).astype(bf16)
        y0 = jnp.where(lo_c, swap_c(ry0), ry1)
        if d == 0:
            y0_fwd[c] = y0
        else:
            y0_ref[cs(c), :] = (y0_fwd[c] + y0).astype(y0_ref.dtype)
    bonus_ref[...] = bonus.astype(bonus_ref.dtype)


N_IN_P = 23
N_OUT_P = 4


def _rwkv_p_multi(*refs, tt, seq, pp):
    ins, outs = refs[:N_IN_P], refs[N_IN_P:N_IN_P + N_OUT_P]
    scr, shx_s = refs[N_IN_P + N_OUT_P:-1], refs[-1]
    x_ref, x_pv, x_nx, mux_ref = ins[9], ins[10], ins[11], ins[15]
    w0_ref, wlu_ref, a0_ref, alu_ref = ins[16:20]
    shift = _token_shift(tt, seq)
    xs = shift(x_ref[...], x_pv[...], x_nx[...], mux_ref[...], shx_s)
    zs = [w0_ref[d:d + 1, :] + _dot(jnp.tanh(xs[:, d * R_LORA:(d + 1) * R_LORA]), wlu_ref[d]) for d in range(2)]
    a_pre = [a0_ref[d:d + 1, :] + _dot(xs[:, (2 + d) * R_LORA:(3 + d) * R_LORA], alu_ref[d]) for d in range(2)]
    stages = []
    for q in range(pp):
        lanes = pl.ds(q * PAIR, PAIR)
        lsl = slice(q * PAIR, (q + 1) * PAIR)
        view = lambda ref, lanes=lanes: ref.at[..., lanes]
        planes = [view(ref) for ref in ins[0:9]]
        gains = [view(ref) for ref in ins[12:15]]
        params = [view(ref) for ref in ins[20:23]]
        y0_ref, bonus_ref, rp_ref, mn_ref = outs
        stages.append(_rwkv_p_body(*planes, *gains, *params,
                                   view(y0_ref), view(bonus_ref), view(rp_ref), mn_ref.at[:, :, pl.ds(2 * q, 2)],
                                   *[s.at[q] for s in scr], tt=tt, shift=shift,
                                   zs=[z[:, lsl] for z in zs], a_pre=[a[:, lsl] for a in a_pre]))
    def run_pre(stage):
        while next(stage) != PRE_DONE:
            pass

    run_pre(stages[0])
    for q in range(pp):
        chunk, pre = stages[q], (stages[q + 1] if q + 1 < pp else None)
        chunk_live, pre_live = True, pre is not None
        while chunk_live or pre_live:
            if chunk_live:
                chunk_live = next(chunk, None) is not None
            if pre_live:
                pre_live = next(pre) != PRE_DONE


def _seg_matrix(n):
    r = lax.broadcasted_iota(jnp.int32, (n, n), 0) // HEAD
    c = lax.broadcasted_iota(jnp.int32, (n, n), 1) // HEAD
    return jnp.where(r == c, 1.0, 0.0).astype(bf16)


def _segsum(x, seg):
    x0 = x.astype(bf16)
    if seg.shape[0] == PAIR:
        return jnp.dot(x0, seg, preferred_element_type=f32)
    x1 = (x - x0.astype(f32)).astype(bf16)
    s = jnp.dot(jnp.concatenate([x0, x1], axis=1), seg, preferred_element_type=f32)
    return s[:, :PAIR] + s[:, PAIR:]


def _rwkv_p(p8, px, mu_p, w0, wlu, a0, alu, k_k, k_a, r_k, *, tt, seq, pp):
    m = px.shape[0]
    nt = m // tt
    npair = D_MODEL // PAIR
    cpt = tt // CHUNK
    wl = pp * PAIR
    hb, hx = 2 * SUBLANES, SUBLANES

    def tiles(plane):
        return [pl.BlockSpec((1, tt, wl), lambda i, j: (plane, i, j)),
                pl.BlockSpec((1, hb, wl), lambda i, j: (plane, jnp.maximum(i * (tt // hb) - 1, 0), j)),
                pl.BlockSpec((1, hb, wl), lambda i, j: (plane, jnp.minimum((i + 1) * (tt // hb), m // hb - 1), j))]

    vec = pl.BlockSpec((1, wl), lambda i, j: (0, j))
    mu = lambda plane: pl.BlockSpec((1, wl), lambda i, j: (0, plane * (npair // pp) + j))
    wx = 4 * R_LORA
    return pl.pallas_call(
        functools.partial(_rwkv_p_multi, tt=tt, seq=seq, pp=pp),
        grid=(nt, npair // pp),
        in_specs=tiles(P_R) + tiles(P_K) + tiles(P_V) + [
            pl.BlockSpec((tt, wx), lambda i, j: (i, 0)),
            pl.BlockSpec((hx, wx), lambda i, j: (jnp.maximum(i * (tt // hx) - 1, 0), 0)),
            pl.BlockSpec((hx, wx), lambda i, j: (jnp.minimum((i + 1) * (tt // hx), m // hx - 1), 0)),
            mu(P_R), mu(P_K), mu(P_V),
            pl.BlockSpec((1, wx), lambda i, j: (0, P_X * D_MODEL // wx)),
            pl.BlockSpec((2, wl), lambda i, j: (0, j)),
            pl.BlockSpec((2, R_LORA, wl), lambda i, j: (0, 0, j)),
            pl.BlockSpec((2, wl), lambda i, j: (0, j)),
            pl.BlockSpec((2, R_LORA, wl), lambda i, j: (0, 0, j)),
            vec, vec, vec,
        ],
        out_specs=[
            pl.BlockSpec((tt, wl), lambda i, j: (i, j)),
            pl.BlockSpec((tt, wl), lambda i, j: (i, j)),
            pl.BlockSpec((2, tt, wl), lambda i, j: (0, i, j)),
            pl.BlockSpec((2, cpt, 2 * pp, HEAD, PAIR), lambda i, j: (0, i, j, 0, 0)),
        ],
        out_shape=[
            jax.ShapeDtypeStruct((m, D_MODEL), bf16),
            jax.ShapeDtypeStruct((m, D_MODEL), bf16),
            jax.ShapeDtypeStruct((2, m, D_MODEL), bf16),
            jax.ShapeDtypeStruct((2, m // CHUNK, N_HEADS, HEAD, PAIR), bf16),
        ],
        scratch_shapes=(
            [pltpu.VMEM((pp, 2, tt, PAIR), bf16) for _ in range(4)]
            + [pltpu.VMEM((pp, 2, 2, tt, PAIR), bf16) for _ in range(3)]
            + [pltpu.VMEM((pp, 2, 2, tt, PAIR), f32),
               pltpu.VMEM((pp, 2, tt, PAIR), bf16),
               pltpu.VMEM((pp, 2, tt, PAIR), f32),
               pltpu.VMEM((pp, 3, tt + 2 * PADR, PAIR), f32),
               pltpu.VMEM((tt + 2 * PADR, wx), f32)]),
        compiler_params=pltpu.CompilerParams(
            dimension_semantics=("parallel", "parallel"), vmem_limit_bytes=VMEM_LIMIT),
        name="rwkv_p",
    )(p8, p8, p8, p8, p8, p8, p8, p8, p8, px, px, px, mu_p, mu_p, mu_p, mu_p, w0, wlu, a0, alu, k_k, k_a, r_k)


def _rwkv_s_body(mnf_ref, mnb_ref, sf_ref, sb_ref, st_ref, *, grp):
    @pl.when(pl.program_id(1) == 0)
    def _():
        st_ref[...] = jnp.zeros_like(st_ref)

    zero = jnp.zeros((HEAD, HEAD), bf16)

    def step(d, mn_ref, s_out_ref, cc):
        s = st_ref[d].astype(bf16)
        for p in range(N_HEADS // 2):
            s_out_ref[cc, p] = jnp.concatenate([jnp.concatenate([s[2 * p], zero], axis=1),
                                                jnp.concatenate([zero, s[2 * p + 1]], axis=1)], axis=0)
        m_t = mn_ref[0, cc, :, :, 0:HEAD]
        n_t = mn_ref[0, cc, :, :, HEAD:].astype(f32)
        st_ref[d] = jnp.einsum('hij,hjk->hik', m_t, s, preferred_element_type=f32) + n_t

    for cc in range(grp):
        step(0, mnf_ref, sf_ref, cc)
        step(1, mnb_ref, sb_ref, grp - 1 - cc)


def _rwkv_s(mn, *, n_seq, grp):
    nchunk = mn.shape[1]
    npair = N_HEADS // 2
    ng = nchunk // n_seq // grp
    blk_in = (1, grp, N_HEADS, HEAD, PAIR)
    blk_out = (grp, npair, PAIR, PAIR)
    out = jax.ShapeDtypeStruct((nchunk, npair, PAIR, PAIR), bf16)
    return pl.pallas_call(
        functools.partial(_rwkv_s_body, grp=grp),
        grid=(n_seq, ng),
        in_specs=[pl.BlockSpec(blk_in, lambda b, g: (0, b * ng + g, 0, 0, 0)),
                  pl.BlockSpec(blk_in, lambda b, g: (1, b * ng + ng - 1 - g, 0, 0, 0))],
        out_specs=[pl.BlockSpec(blk_out, lambda b, g: (b * ng + g, 0, 0, 0)),
                   pl.BlockSpec(blk_out, lambda b, g: (b * ng + ng - 1 - g, 0, 0, 0))],
        out_shape=[out, out],
        scratch_shapes=[pltpu.VMEM((2, N_HEADS, HEAD, HEAD), f32)],
        compiler_params=pltpu.CompilerParams(
            dimension_semantics=("parallel", "arbitrary"), vmem_limit_bytes=VMEM_LIMIT),
        name="rwkv_s",
    )(mn, mn)


def _rwkv_f_body(y0_ref, bonus_ref, rp_ref, sf_ref, sb_ref, gate_ref, lng_ref, lnb_ref, o_ref, *, tt):
    npair = D_MODEL // PAIR
    nc = tt // CHUNK
    cs = lambda c: slice(c * CHUNK, (c + 1) * CHUNK)
    ps = lambda p: slice(p * PAIR, (p + 1) * PAIR)
    yf = [[jnp.dot(rp_ref[0, cs(c), ps(p)], sf_ref[c, p], preferred_element_type=f32) for c in range(nc)]
          for p in range(npair)]
    yb = [[jnp.dot(rp_ref[1, cs(c), ps(p)], sb_ref[c, p], preferred_element_type=f32) for c in range(nc)]
          for p in range(npair)]
    seg1, seg2 = _seg_matrix(PAIR), _seg_matrix(2 * PAIR)
    ys = [y0_ref[:, ps(p)].astype(f32) + jnp.concatenate(yf[p], axis=0) + jnp.concatenate(yb[p], axis=0)
          for p in range(npair)]
    mus = [_segsum(y, seg2) * (1.0 / HEAD) for y in ys]
    dvs = [y - mu for y, mu in zip(ys, mus)]
    vrs = [_segsum(dv * dv, seg1) * (1.0 / HEAD) for dv in dvs]
    for p in range(npair):
        yn = dvs[p] * lax.rsqrt(vrs[p] + GN_EPS) * lng_ref[:, ps(p)] + lnb_ref[:, ps(p)] + bonus_ref[:, ps(p)].astype(f32)
        g = gate_ref[0, :, ps(p)].astype(f32)
        o_ref[:, ps(p)] = (yn * (g * _sigmoid(g))).astype(bf16)


def _rwkv_f(y0, bonus, rp, sf, sb, p8, ln_g, ln_b, *, tt):
    m = y0.shape[0]
    cpt = tt // CHUNK
    npair = D_MODEL // PAIR
    tile = pl.BlockSpec((tt, D_MODEL), lambda i: (i, 0))
    vec = pl.BlockSpec((1, D_MODEL), lambda i: (0, 0))
    state = pl.BlockSpec((cpt, npair, PAIR, PAIR), lambda i: (i, 0, 0, 0))
    return pl.pallas_call(
        functools.partial(_rwkv_f_body, tt=tt),
        grid=(m // tt,),
        in_specs=[
            tile, tile,
            pl.BlockSpec((2, tt, D_MODEL), lambda i: (0, i, 0)),
            state, state,
            pl.BlockSpec((1, tt, D_MODEL), lambda i: (P_GA, i, 0)),
            vec, vec,
        ],
        out_specs=tile,
        out_shape=jax.ShapeDtypeStruct((m, D_MODEL), bf16),
        compiler_params=pltpu.CompilerParams(dimension_semantics=("parallel",), vmem_limit_bytes=VMEM_LIMIT),
        name="rwkv_f",
    )(y0, bonus, rp, sf, sb, p8, ln_g, ln_b)


def _attn_body(q_ref, kp_ref, kc_ref, kn_ref, vp_ref, vc_ref, vn_ref, tp_ref, tc_ref, tn_ref, gate_ref,
               qg_ref, kg_ref, sink_ref, o_ref, *, seq, qb):
    step_rows = qb * BLOCK
    start = (pl.program_id(0) % (seq // step_rows)) * step_rows
    ctx = BLOCK + 2 * WINDOW
    qoff = lax.broadcasted_iota(jnp.int32, (BLOCK, ctx), 0)
    koff = lax.broadcasted_iota(jnp.int32, (BLOCK, ctx), 1) - WINDOW
    valid = []
    for b in range(qb):
        kpos = start + b * BLOCK + koff
        valid.append((jnp.abs(qoff - koff) <= WINDOW) & (kpos >= 0) & (kpos < seq))

    seg = _seg_matrix(PAIR)
    swap = lambda t: pltpu.roll(t, HEAD, axis=1)

    def norm_rope(x, gain, tab):
        lane = lax.broadcasted_iota(jnp.int32, x.shape, 1)
        xn = x * lax.rsqrt(_segsum(x * x, seg) * (1.0 / HEAD) + NORM_EPS) * gain
        rot = jnp.where(lane % HEAD < HEAD // 2, pltpu.roll(xn, PAIR - HEAD // 2, axis=1),
                        pltpu.roll(xn, HEAD // 2, axis=1))
        return xn * tab[:, :PAIR] + rot * tab[:, PAIR:]

    def place(t, e, want_lo):
        lo = lax.broadcasted_iota(jnp.int32, t.shape, 1) < HEAD
        src = t if (e == 0) == want_lo else swap(t)
        return (jnp.where(lo, src, 0.0) if want_lo else jnp.where(lo, 0.0, src)).astype(bf16)

    tab_k = jnp.concatenate([tp_ref[...], tc_ref[...], tn_ref[...]], axis=0)
    tab_q = tc_ref[...]
    k_lo, k_hi, v_lo, v_hi = [], [], [], []
    for kp in range(D_KV // PAIR):
        ps_ = slice(kp * PAIR, (kp + 1) * PAIR)
        kx = jnp.concatenate([kp_ref[:, ps_], kc_ref[:, ps_], kn_ref[:, ps_]], axis=0)
        vx = jnp.concatenate([vp_ref[:, ps_], vc_ref[:, ps_], vn_ref[:, ps_]], axis=0)
        kr = norm_rope(kx, kg_ref[...], tab_k)
        for e in range(2):
            k_lo.append(place(kr, e, True))
            k_hi.append(place(kr, e, False))
            v_lo.append(place(vx, e, True))
            v_hi.append(place(vx, e, False))
    npair = D_MODEL // PAIR
    psl = lambda p: slice(p * PAIR, (p + 1) * PAIR)
    log2e = math.log2(math.e)
    qrs = [(norm_rope(q_ref[0, :, psl(p)].astype(f32), qg_ref[...], tab_q) * (HEAD ** -0.5 * log2e)).astype(bf16)
           for p in range(npair)]
    inst = [(b, hq) for b in range(qb) for hq in range(N_HEADS)]
    rows = lambda b: slice(b * BLOCK, (b + 1) * BLOCK)
    keys = lambda b: slice(b * BLOCK, b * BLOCK + ctx)
    def masked(s, ok):
        return jnp.concatenate([jnp.where(ok[:, :WINDOW], s[:, :WINDOW], -1e30), s[:, WINDOW:WINDOW + BLOCK],
                                jnp.where(ok[:, WINDOW + BLOCK:], s[:, WINDOW + BLOCK:], -1e30)], axis=1)

    ss = [masked(lax.dot_general(qrs[hq // 2][rows(b)], (k_lo if hq % 2 == 0 else k_hi)[hq // GRP][keys(b)],
                                 (((1,), (1,)), ((), ())), preferred_element_type=f32), valid[b])
          for b, hq in inst]
    sinks = [sink_ref[0:1, hq:hq + 1] * log2e for b, hq in inst]
    mxs = [jnp.maximum(jnp.max(s, axis=-1, keepdims=True), sk) for s, sk in zip(ss, sinks)]
    ps = [jnp.exp2(s - mx) for s, mx in zip(ss, mxs)]
    dens = [jnp.sum(p, axis=-1, keepdims=True) + jnp.exp2(sk - mx) for p, sk, mx in zip(ps, sinks, mxs)]
    os_ = [jnp.dot(p.astype(bf16), (v_lo if hq % 2 == 0 else v_hi)[hq // GRP][keys(b)], preferred_element_type=f32) / den
           for (b, hq), p, den in zip(inst, ps, dens)]
    for b in range(qb):
        for p in range(npair):
            gt = gate_ref[0, rows(b), psl(p)].astype(f32)
            i0 = b * N_HEADS + 2 * p
            o_ref[rows(b), psl(p)] = ((os_[i0] + os_[i0 + 1]) * (gt * _sigmoid(gt))).astype(bf16)


def _attn(p8, px, rope_tab, q_g, k_g, sink, *, seq, qb):
    m = px.shape[0]
    nblk = seq // BLOCK
    rows = qb * BLOCK

    def nbr(delta):
        def f(i):
            t = (i * qb) % nblk
            return i * qb - t + jnp.clip(t + delta, 0, nblk - 1)
        return f

    def kv(col):
        return [pl.BlockSpec((BLOCK, D_KV), lambda i, f=nbr(-1): (f(i), col)),
                pl.BlockSpec((rows, D_KV), lambda i: (i, col)),
                pl.BlockSpec((BLOCK, D_KV), lambda i, f=nbr(qb): (f(i), col))]

    tabs = [pl.BlockSpec((BLOCK, 2 * PAIR), lambda i, f=nbr(-1): (f(i) % nblk, 0)),
            pl.BlockSpec((rows, 2 * PAIR), lambda i: (i % (nblk // qb), 0)),
            pl.BlockSpec((BLOCK, 2 * PAIR), lambda i, f=nbr(qb): (f(i) % nblk, 0))]

    return pl.pallas_call(
        functools.partial(_attn_body, seq=seq, qb=qb),
        grid=(m // rows,),
        in_specs=[pl.BlockSpec((1, rows, D_MODEL), lambda i: (P_Q, i, 0))] + kv(1) + kv(2) + tabs + [
            pl.BlockSpec((1, rows, D_MODEL), lambda i: (P_GB, i, 0)),
            pl.BlockSpec((1, PAIR), lambda i: (0, 0)),
            pl.BlockSpec((1, PAIR), lambda i: (0, 0)),
            pl.BlockSpec((1, N_HEADS), lambda i: (0, 0)),
        ],
        out_specs=pl.BlockSpec((rows, D_MODEL), lambda i: (i, 0)),
        out_shape=jax.ShapeDtypeStruct((m, D_MODEL), bf16),
        compiler_params=pltpu.CompilerParams(dimension_semantics=("parallel",), vmem_limit_bytes=VMEM_LIMIT),
        name="attn",
    )(p8, px, px, px, px, px, px, rope_tab, rope_tab, rope_tab, p8, q_g, k_g, sink)


def _rope_table(seq):
    inv = 1.0 / (ROPE_THETA ** (jnp.arange(0, HEAD, 2, dtype=f32) / HEAD))
    ang = jnp.arange(seq, dtype=f32)[:, None] * inv[None, :]
    cos, sin = jnp.cos(ang), jnp.sin(ang)
    return jnp.concatenate([cos, cos, cos, cos, -sin, sin, -sin, sin], axis=1)


def _out_proj_body(x_ref, ga_ref, gb_ref, ma_ref, mb_ref, wa_ref, wb_ref, wo_ref, o_ref):
    ya = jnp.dot(ga_ref[...], wa_ref[...], preferred_element_type=f32)
    yb = jnp.dot(gb_ref[...], wb_ref[...], preferred_element_type=f32)
    mixed = _sigmoid(ma_ref[0].astype(f32)) * ya + _sigmoid(mb_ref[0].astype(f32)) * yb
    o_ref[...] = x_ref[...] + jnp.dot(mixed.astype(bf16), wo_ref[...], preferred_element_type=f32)


def _out_proj(x2d, ga, gb, p8, wa, wb, wo, *, tm):
    m = x2d.shape[0]
    row = pl.BlockSpec((tm, D_MODEL), lambda i: (i, 0))
    wspec = pl.BlockSpec((D_MODEL, D_MODEL), lambda i: (0, 0))
    return pl.pallas_call(
        _out_proj_body,
        grid=(m // tm,),
        in_specs=[row, row, row,
                  pl.BlockSpec((1, tm, D_MODEL), lambda i: (P_MA, i, 0)),
                  pl.BlockSpec((1, tm, D_MODEL), lambda i: (P_MB, i, 0)),
                  wspec, wspec, wspec],
        out_specs=row,
        out_shape=jax.ShapeDtypeStruct((m, D_MODEL), f32),
        compiler_params=pltpu.CompilerParams(dimension_semantics=("parallel",), vmem_limit_bytes=VMEM_LIMIT),
        name="out_proj",
    )(x2d, ga, gb, p8, p8, wa, wb, wo)


def _layer(x2d, lw, rope_tab, *, n_seq, seq):
    for tile in (TM_IN, TM_OUT, TT_P, TT_F, GRP_S * CHUNK, QB_ATTN * BLOCK):
        assert seq % tile == 0, (seq, tile)
    p8, px = _in_proj(x2d, lw["norm_g"], lw["w_p"], tm=TM_IN)
    y0, bonus, rp, mn = _rwkv_p(p8, px, lw["mu_p"], lw["w0"], lw["wlu"], lw["a0"], lw["alu"], lw["k_k"], lw["k_a"],
                                lw["r_k"], tt=TT_P, seq=seq, pp=PAIRS_P)
    sf, sb = _rwkv_s(mn, n_seq=n_seq, grp=GRP_S)
    ga = _rwkv_f(y0, bonus, rp, sf, sb, p8, lw["ln_g"], lw["ln_b"], tt=TT_F)
    gb = _attn(p8, px, rope_tab, lw["q_g"], lw["k_g"], lw["sink"], seq=seq, qb=QB_ATTN)
    return _out_proj(x2d, ga, gb, p8, lw["wa"], lw["wb"], lw["wo"], tm=TM_OUT)


def _trunk(x, layers):
    n_seq, seq, _ = x.shape
    rope_tab = _rope_table(seq)
    x2d = x.reshape(n_seq * seq, D_MODEL)
    for lw in layers:
        x2d = _layer(x2d, lw, rope_tab, n_seq=n_seq, seq=seq)
    return x2d.reshape(n_seq, seq, D_MODEL)


def _prep_layers(norm_g, w_in, shift_mu, w0, w_lora_up, a0, a_lora_up, k_k, k_a, r_k, ln_x_g, ln_x_b,
                 q_norm_g, k_norm_g, sink, w_proj_a, w_proj_b, w_out):
    layers = []
    row = lambda v: v.reshape(1, -1).astype(f32)
    for l in range(norm_g.shape[0]):
        w_p, mu_p = _prep_layer_weights(w_in[l], shift_mu[l])
        layers.append(dict(
            norm_g=row(norm_g[l]), w_p=w_p, mu_p=mu_p,
            w0=w0[l].astype(f32), wlu=w_lora_up[l].astype(bf16), a0=a0[l].astype(f32), alu=a_lora_up[l].astype(bf16),
            k_k=row(k_k[l]), k_a=row(k_a[l]), r_k=row(r_k[l]), ln_g=row(ln_x_g[l]), ln_b=row(ln_x_b[l]),
            q_g=row(jnp.tile(q_norm_g[l], 2)), k_g=row(jnp.tile(k_norm_g[l], 2)), sink=row(sink[l]),
            wa=w_proj_a[l].astype(bf16), wb=w_proj_b[l].astype(bf16), wo=w_out[l].astype(bf16)))
    return layers


def kernel(x_prompt, x_sample, norm_g, w_in, shift_mu, w0, w_lora_up, a0, a_lora_up, k_k, k_a, r_k,
           ln_x_g, ln_x_b, q_norm_g, k_norm_g, sink, w_proj_a, w_proj_b, w_out):
    layers = _prep_layers(norm_g, w_in, shift_mu, w0, w_lora_up, a0, a_lora_up, k_k, k_a, r_k, ln_x_g, ln_x_b,
                          q_norm_g, k_norm_g, sink, w_proj_a, w_proj_b, w_out)
    return _trunk(x_prompt, layers), _trunk(x_sample, layers)
```

```python
import functools
import math

import jax
import jax.numpy as jnp
from jax import lax
from jax.experimental import pallas as pl
from jax.experimental.pallas import tpu as pltpu

D_MODEL = 1024
HEAD = 64
N_HEADS = D_MODEL // HEAD
R_LORA = 64
HKV = 4
GRP = N_HEADS // HKV
D_KV = HKV * HEAD
WINDOW = 128
BLOCK = 128
GN_EPS = 64e-5
NORM_EPS = 1e-6
ROPE_THETA = 10000.0
C_SHIFT = 3 * D_MODEL + 4 * R_LORA
N_PLANES = 9
CHUNK = 64
VMEM_LIMIT = 56 * 1024 * 1024

TM_IN = 2048
TM_OUT = 1024
TT_P = 256
TT_F = 512
GRP_S = 8
PAIRS_P = 8
QB_ATTN = 4

P_R, P_K, P_V, P_GA, P_Q, P_GB, P_MA, P_MB, P_X = range(9)

f32 = jnp.float32
bf16 = jnp.bfloat16


def _dot(a, b):
    return jnp.dot(a.astype(bf16), b.astype(bf16), preferred_element_type=f32)


def _dot_tn(a, b):
    return lax.dot_general(a.astype(bf16), b.astype(bf16), (((0,), (0,)), ((), ())),
                           preferred_element_type=f32)


def _sigmoid(x):
    return 1.0 / (1.0 + jnp.exp(-x))


def _in_proj_body(x_ref, g_ref, w_ref, o_ref, h_ref):
    @pl.when(pl.program_id(1) == 0)
    def _():
        x = x_ref[...]
        ms = jnp.mean(x * x, axis=-1, keepdims=True)
        h_ref[...] = (x * lax.rsqrt(ms + NORM_EPS) * g_ref[...]).astype(bf16)

    o_ref[0] = jnp.dot(h_ref[...], w_ref[...], preferred_element_type=f32).astype(bf16)


def _in_proj(x2d, norm_g, w_p, *, tm):
    m = x2d.shape[0]
    return pl.pallas_call(
        _in_proj_body,
        grid=(m // tm, N_PLANES),
        in_specs=[
            pl.BlockSpec((tm, D_MODEL), lambda i, j: (i, 0)),
            pl.BlockSpec((1, D_MODEL), lambda i, j: (0, 0)),
            pl.BlockSpec((D_MODEL, D_MODEL), lambda i, j: (0, j)),
        ],
        out_specs=pl.BlockSpec((1, tm, D_MODEL), lambda i, j: (j, i, 0)),
        out_shape=jax.ShapeDtypeStruct((N_PLANES, m, D_MODEL), bf16),
        scratch_shapes=[pltpu.VMEM((tm, D_MODEL), bf16)],
        compiler_params=pltpu.CompilerParams(
            dimension_semantics=("parallel", "arbitrary"), vmem_limit_bytes=VMEM_LIMIT),
        name="in_proj",
    )(x2d, norm_g, w_p)


def _prep_layer_weights(w_in, shift_mu):
    c = C_SHIFT
    d = D_MODEL
    cols = [
        (0, d), (d, 2 * d), (2 * d, 3 * d),
        (c, c + d),
        (c + d, c + 2 * d),
        (c + 2 * d + 2 * D_KV, c + 3 * d + 2 * D_KV),
        (c + 3 * d + 2 * D_KV, c + 4 * d + 2 * D_KV),
        (c + 4 * d + 2 * D_KV, c + 5 * d + 2 * D_KV),
    ]
    planes = [w_in[:, a:b] for a, b in cols]
    small = jnp.concatenate([w_in[:, 3 * d:c], w_in[:, c + 2 * d:c + 2 * d + 2 * D_KV],
                             jnp.zeros((d, d - 4 * R_LORA - 2 * D_KV), w_in.dtype)], axis=1)
    w_p = jnp.concatenate(planes + [small], axis=1).astype(bf16)
    zeros = jnp.zeros((d,), f32)
    mu_small = jnp.concatenate([shift_mu[3 * d:c], jnp.zeros((d - 4 * R_LORA,), f32)])
    mu_p = jnp.concatenate([shift_mu[0:d], shift_mu[d:2 * d], shift_mu[2 * d:3 * d]] + [zeros] * 5 + [mu_small])
    return w_p, mu_p.reshape(1, N_PLANES * d)


PAIR = 2 * HEAD
SUBLANES = 8
PADR = SUBLANES
PRE, PRE_DONE, CHUNK_STAGE = "pre", "pre_done", "chunk"


def _token_shift(tt, seq):
    i = pl.program_id(0)
    first = (i * tt) % seq == 0
    last = ((i + 1) * tt) % seq == 0

    def shift(main, pv, nx, mu, buf):
        c = main.astype(f32)
        buf[PADR:PADR + tt, :] = c
        buf[PADR - 1:PADR, :] = jnp.where(first, 0.0, pv[pv.shape[0] - 1:, :].astype(f32))
        buf[PADR + tt:PADR + tt + 1, :] = jnp.where(last, 0.0, nx[:1, :].astype(f32))
        nbr = 0.5 * (buf[PADR - 1:PADR - 1 + tt, :] + buf[PADR + 1:PADR + 1 + tt, :])
        return c + mu * (nbr - c)

    return shift


def _rwkv_p_body(r_ref, r_pv, r_nx, k_ref, k_pv, k_nx, v_ref, v_pv, v_nx, mur_ref, muk_ref, muv_ref,
                 kk_ref, ka_ref, rk_ref,
                 y0_ref, bonus_ref, rp_ref, mn_ref,
                 at_s, rt_s, kp_s, bp_s, khm_s, bhm_s, atx_s, rtx_s, vx_s, pl_s, sh_s, *, tt, shift, zs, a_pre):
    nc = tt // CHUNK
    r = shift(r_ref[0], r_pv[0], r_nx[0], mur_ref[...], sh_s.at[0])
    k = shift(k_ref[0], k_pv[0], k_nx[0], muk_ref[...], sh_s.at[1])
    v = shift(v_ref[0], v_pv[0], v_nx[0], muv_ref[...], sh_s.at[2])
    yield PRE

    seg1, seg2 = _seg_matrix(PAIR), _seg_matrix(2 * PAIR)
    lo = lax.broadcasted_iota(jnp.int32, (tt, PAIR), 1) < HEAD
    swap = lambda t: pltpu.roll(t, HEAD, axis=1)

    def head_lo(t, h):
        return jnp.where(lo, t if h == 0 else swap(t), 0.0)

    def head_hi(t, h):
        return jnp.where(lo, 0.0, swap(t) if h == 0 else t)

    def head_own(t, h):
        return jnp.where(lo, t, 0.0) if h == 0 else jnp.where(lo, 0.0, t)

    for h in range(2):
        vx_s[h] = head_hi(v, h).astype(bf16)

    kkv = k * kk_ref[...]
    kk = kkv / jnp.maximum(jnp.sqrt(_segsum(kkv * kkv, seg1)), 1e-12)
    yield PRE

    rows = lax.broadcasted_iota(jnp.int32, (tt, tt), 0)
    cols = lax.broadcasted_iota(jnp.int32, (tt, tt), 1)
    same = (rows // CHUNK) == (cols // CHUNK)

    kmod_sum = jnp.zeros_like(k)
    for d in range(2):
        if d == 0:
            tri_bd = jnp.where(same & (cols <= rows), 1.0, 0.0)
        else:
            tri_bd = jnp.where(same & (cols >= rows), 1.0, 0.0)
        lw = -math.exp(-0.5) * _sigmoid(zs[d])
        a = _sigmoid(a_pre[d])
        yield PRE
        kmod = k * (1.0 + (a - 1.0) * ka_ref[...])
        kmod_sum = kmod_sum + kmod
        b = -(kk * a)
        lw0 = lw.astype(bf16)
        lw1 = (lw - lw0.astype(f32)).astype(bf16)
        cum2 = jnp.dot(tri_bd.astype(bf16), jnp.concatenate([lw0, lw1], axis=1), preferred_element_type=f32)
        cum = cum2[:, :PAIR] + cum2[:, PAIR:]
        edge = CHUNK - 1 if d == 0 else 0
        tot = jnp.concatenate([jnp.broadcast_to(cum[c * CHUNK + edge:c * CHUNK + edge + 1, :], (CHUNK, PAIR))
                               for c in range(nc)], axis=0)
        yield PRE
        p_inv = jnp.exp(-cum)
        p_end = jnp.exp(tot - cum)
        at = kk * jnp.exp(cum - lw)
        rt = r * jnp.exp(cum)
        kh = kmod * p_inv
        bh = b * p_inv
        p_tot = jnp.exp(tot)
        at_s[d] = at.astype(bf16)
        rt_s[d] = rt.astype(bf16)
        pl_s[d] = p_tot
        yield PRE
        kp_s[d] = (kmod * p_end).astype(bf16)
        bp_s[d] = (b * p_end).astype(bf16)
        for h in range(2):
            khm_s[d, h] = head_own(kh, h).astype(bf16)
            bhm_s[d, h] = head_own(bh, h).astype(bf16)
            atx_s[d, h] = head_lo(at, h).astype(bf16)
            rtx_s[d, h] = head_lo(rt, h)
        yield PRE

    bonus = _segsum(r * kmod_sum * rk_ref[...], seg2) * v
    yield PRE_DONE

    row = lax.broadcasted_iota(jnp.int32, (CHUNK, PAIR), 0)
    col = lax.broadcasted_iota(jnp.int32, (CHUNK, PAIR), 1)
    col_t = col % CHUNK
    lo_c = col < HEAD
    strict = (col_t < row, col_t > row)
    incl = (col_t <= row, col_t >= row)
    pairs = [(d, c) for d in range(2) for c in range(nc)]
    inst = [(d, c, h) for d, c in pairs for h in range(2)]
    cs = lambda c: slice(c * CHUNK, (c + 1) * CHUNK)
    swap_c = lambda t: pltpu.roll(t, HEAD, axis=1)

    top, bot = [], []
    for d, c in pairs:
        lhs = jnp.concatenate([at_s[d, cs(c)], rt_s[d, cs(c)]], axis=0)
        rhs = jnp.concatenate([bhm_s[d, 0, cs(c)], khm_s[d, 0, cs(c)],
                               bhm_s[d, 1, cs(c)], khm_s[d, 1, cs(c)]], axis=0)
        sc = lax.dot_general(lhs, rhs, (((1,), (1,)), ((), ())), preferred_element_type=f32)
        for h in range(2):
            top.append(jnp.where(strict[d], sc[:CHUNK, h * PAIR:(h + 1) * PAIR], 0.0))
            bot.append(jnp.where(incl[d], sc[CHUNK:, h * PAIR:(h + 1) * PAIR], 0.0))
    yield CHUNK_STAGE
    xs, aps = [], []
    for i, (d, c, h) in enumerate(inst):
        vx = vx_s[h, cs(c)]
        a_ak = jnp.where(lo_c, 0.0, top[i]).astype(bf16)
        akv = jnp.dot(a_ak, jnp.concatenate([vx, vx], axis=0), preferred_element_type=f32)
        xs.append(atx_s[d, h, cs(c)].astype(f32) + akv)
        aps.append(top[i][:, :CHUNK])
    n_dbl = CHUNK.bit_length() - 1
    for it in range(n_dbl):
        yield CHUNK_STAGE
        last = it + 1 == n_dbl
        res = [_dot(ap, x if last else jnp.concatenate([x, ap], axis=1)) for ap, x in zip(aps, xs)]
        xs = [x + rs[:, :PAIR] for x, rs in zip(xs, res)]
        if not last:
            aps = [rs[:, PAIR:] for rs in res]
    yield CHUNK_STAGE
    rhs_o = [jnp.concatenate([xs[i].astype(bf16), vx_s[h, cs(c)]], axis=0) for i, (d, c, h) in enumerate(inst)]
    rys = [_dot(bot[i], rhs_o[i]) + rtx_s[d, h, cs(c)] for i, (d, c, h) in enumerate(inst)]
    yield CHUNK_STAGE
    mns = [_dot_tn(jnp.concatenate([bp_s[d, cs(c), h * HEAD:(h + 1) * HEAD],
                                    kp_s[d, cs(c), h * HEAD:(h + 1) * HEAD]], axis=0), rhs_o[i])
           for i, (d, c, h) in enumerate(inst)]
    yield CHUNK_STAGE
    for i, (d, c, h) in enumerate(inst):
        p_blk = pl_s[d, c * CHUNK:c * CHUNK + SUBLANES, :]
        p_row = (p_blk if h == 0 else swap_c(p_blk))[0:1]
        mn_ref[d, c, h] = (mns[i] + jnp.where(row == col, p_row, 0.0)).astype(bf16)
    y0_fwd = {}
    for j, (d, c) in enumerate(pairs):
        ry0, ry1 = rys[2 * j], rys[2 * j + 1]
        rp_ref[d, cs(c), :] = jnp.where(lo_c, ry0, swap_c(ry1)).astype(bf16)
        y0 = jnp.where(lo_c, swap_c(ry0), ry1)
        if d == 0:
            y0_fwd[c] = y0
        else:
            y0_ref[cs(c), :] = (y0_fwd[c] + y0).astype(y0_ref.dtype)
    bonus_ref[...] = bonus.astype(bonus_ref.dtype)


N_IN_P = 23
N_OUT_P = 4


def _rwkv_p_multi(*refs, tt, seq, pp):
    ins, outs = refs[:N_IN_P], refs[N_IN_P:N_IN_P + N_OUT_P]
    scr, shx_s = refs[N_IN_P + N_OUT_P:-1], refs[-1]
    x_ref, x_pv, x_nx, mux_ref = ins[9], ins[10], ins[11], ins[15]
    w0_ref, wlu_ref, a0_ref, alu_ref = ins[16:20]
    shift = _token_shift(tt, seq)
    xs = shift(x_ref[0], x_pv[0], x_nx[0], mux_ref[...], shx_s)
    zs = [w0_ref[d:d + 1, :] + _dot(jnp.tanh(xs[:, d * R_LORA:(d + 1) * R_LORA]), wlu_ref[d]) for d in range(2)]
    a_pre = [a0_ref[d:d + 1, :] + _dot(xs[:, (2 + d) * R_LORA:(3 + d) * R_LORA], alu_ref[d]) for d in range(2)]
    stages = []
    for q in range(pp):
        lanes = pl.ds(q * PAIR, PAIR)
        lsl = slice(q * PAIR, (q + 1) * PAIR)
        view = lambda ref, lanes=lanes: ref.at[..., lanes]
        planes = [view(ref) for ref in ins[0:9]]
        gains = [view(ref) for ref in ins[12:15]]
        params = [view(ref) for ref in ins[20:23]]
        y0_ref, bonus_ref, rp_ref, mn_ref = outs
        stages.append(_rwkv_p_body(*planes, *gains, *params,
                                   view(y0_ref), view(bonus_ref), view(rp_ref), mn_ref.at[:, :, pl.ds(2 * q, 2)],
                                   *[s.at[q] for s in scr], tt=tt, shift=shift,
                                   zs=[z[:, lsl] for z in zs], a_pre=[a[:, lsl] for a in a_pre]))
    def run_pre(stage):
        while next(stage) != PRE_DONE:
            pass

    run_pre(stages[0])
    for q in range(pp):
        chunk, pre = stages[q], (stages[q + 1] if q + 1 < pp else None)
        chunk_live, pre_live = True, pre is not None
        while chunk_live or pre_live:
            if chunk_live:
                chunk_live = next(chunk, None) is not None
            if pre_live:
                pre_live = next(pre) != PRE_DONE


def _seg_matrix(n):
    r = lax.broadcasted_iota(jnp.int32, (n, n), 0) // HEAD
    c = lax.broadcasted_iota(jnp.int32, (n, n), 1) // HEAD
    return jnp.where(r == c, 1.0, 0.0).astype(bf16)


def _segsum(x, seg):
    x0 = x.astype(bf16)
    if seg.shape[0] == PAIR:
        return jnp.dot(x0, seg, preferred_element_type=f32)
    x1 = (x - x0.astype(f32)).astype(bf16)
    s = jnp.dot(jnp.concatenate([x0, x1], axis=1), seg, preferred_element_type=f32)
    return s[:, :PAIR] + s[:, PAIR:]


def _rwkv_p(p9, mu_p, w0, wlu, a0, alu, k_k, k_a, r_k, *, tt, seq, pp):
    m = p9.shape[1]
    nt = m // tt
    npair = D_MODEL // PAIR
    cpt = tt // CHUNK
    wl = pp * PAIR
    hb = 2 * SUBLANES

    def tiles(plane):
        return [pl.BlockSpec((1, tt, wl), lambda i, j: (plane, i, j)),
                pl.BlockSpec((1, hb, wl), lambda i, j: (plane, jnp.maximum(i * (tt // hb) - 1, 0), j)),
                pl.BlockSpec((1, hb, wl), lambda i, j: (plane, jnp.minimum((i + 1) * (tt // hb), m // hb - 1), j))]

    vec = pl.BlockSpec((1, wl), lambda i, j: (0, j))
    mu = lambda plane: pl.BlockSpec((1, wl), lambda i, j: (0, plane * (npair // pp) + j))
    wx = 4 * R_LORA
    return pl.pallas_call(
        functools.partial(_rwkv_p_multi, tt=tt, seq=seq, pp=pp),
        grid=(nt, npair // pp),
        in_specs=tiles(P_R) + tiles(P_K) + tiles(P_V) + [
            pl.BlockSpec((1, tt, wx), lambda i, j: (P_X, i, 0)),
            pl.BlockSpec((1, hb, wx), lambda i, j: (P_X, jnp.maximum(i * (tt // hb) - 1, 0), 0)),
            pl.BlockSpec((1, hb, wx), lambda i, j: (P_X, jnp.minimum((i + 1) * (tt // hb), m // hb - 1), 0)),
            mu(P_R), mu(P_K), mu(P_V),
            pl.BlockSpec((1, wx), lambda i, j: (0, P_X * D_MODEL // wx)),
            pl.BlockSpec((2, wl), lambda i, j: (0, j)),
            pl.BlockSpec((2, R_LORA, wl), lambda i, j: (0, 0, j)),
            pl.BlockSpec((2, wl), lambda i, j: (0, j)),
            pl.BlockSpec((2, R_LORA, wl), lambda i, j: (0, 0, j)),
            vec, vec, vec,
        ],
        out_specs=[
            pl.BlockSpec((tt, wl), lambda i, j: (i, j)),
            pl.BlockSpec((tt, wl), lambda i, j: (i, j)),
            pl.BlockSpec((2, tt, wl), lambda i, j: (0, i, j)),
            pl.BlockSpec((2, cpt, 2 * pp, HEAD, PAIR), lambda i, j: (0, i, j, 0, 0)),
        ],
        out_shape=[
            jax.ShapeDtypeStruct((m, D_MODEL), bf16),
            jax.ShapeDtypeStruct((m, D_MODEL), bf16),
            jax.ShapeDtypeStruct((2, m, D_MODEL), bf16),
            jax.ShapeDtypeStruct((2, m // CHUNK, N_HEADS, HEAD, PAIR), bf16),
        ],
        scratch_shapes=(
            [pltpu.VMEM((pp, 2, tt, PAIR), bf16) for _ in range(4)]
            + [pltpu.VMEM((pp, 2, 2, tt, PAIR), bf16) for _ in range(3)]
            + [pltpu.VMEM((pp, 2, 2, tt, PAIR), f32),
               pltpu.VMEM((pp, 2, tt, PAIR), bf16),
               pltpu.VMEM((pp, 2, tt, PAIR), f32),
               pltpu.VMEM((pp, 3, tt + 2 * PADR, PAIR), f32),
               pltpu.VMEM((tt + 2 * PADR, wx), f32)]),
        compiler_params=pltpu.CompilerParams(
            dimension_semantics=("parallel", "parallel"), vmem_limit_bytes=VMEM_LIMIT),
        name="rwkv_p",
    )(*([p9] * 12), mu_p, mu_p, mu_p, mu_p, w0, wlu, a0, alu, k_k, k_a, r_k)


def _rwkv_s_body(mnf_ref, mnb_ref, sf_ref, sb_ref, st_ref, *, grp):
    @pl.when(pl.program_id(1) == 0)
    def _():
        st_ref[...] = jnp.zeros_like(st_ref)

    zero = jnp.zeros((HEAD, HEAD), bf16)

    def step(d, mn_ref, s_out_ref, cc):
        s = st_ref[d].astype(bf16)
        for p in range(N_HEADS // 2):
            s_out_ref[cc, p] = jnp.concatenate([jnp.concatenate([s[2 * p], zero], axis=1),
                                                jnp.concatenate([zero, s[2 * p + 1]], axis=1)], axis=0)
        m_t = mn_ref[0, cc, :, :, 0:HEAD]
        n_t = mn_ref[0, cc, :, :, HEAD:].astype(f32)
        st_ref[d] = jnp.einsum('hij,hjk->hik', m_t, s, preferred_element_type=f32) + n_t

    for cc in range(grp):
        step(0, mnf_ref, sf_ref, cc)
        step(1, mnb_ref, sb_ref, grp - 1 - cc)


def _rwkv_s(mn, *, n_seq, grp):
    nchunk = mn.shape[1]
    npair = N_HEADS // 2
    ng = nchunk // n_seq // grp
    blk_in = (1, grp, N_HEADS, HEAD, PAIR)
    blk_out = (grp, npair, PAIR, PAIR)
    out = jax.ShapeDtypeStruct((nchunk, npair, PAIR, PAIR), bf16)
    return pl.pallas_call(
        functools.partial(_rwkv_s_body, grp=grp),
        grid=(n_seq, ng),
        in_specs=[pl.BlockSpec(blk_in, lambda b, g: (0, b * ng + g, 0, 0, 0)),
                  pl.BlockSpec(blk_in, lambda b, g: (1, b * ng + ng - 1 - g, 0, 0, 0))],
        out_specs=[pl.BlockSpec(blk_out, lambda b, g: (b * ng + g, 0, 0, 0)),
                   pl.BlockSpec(blk_out, lambda b, g: (b * ng + ng - 1 - g, 0, 0, 0))],
        out_shape=[out, out],
        scratch_shapes=[pltpu.VMEM((2, N_HEADS, HEAD, HEAD), f32)],
        compiler_params=pltpu.CompilerParams(
            dimension_semantics=("parallel", "arbitrary"), vmem_limit_bytes=VMEM_LIMIT),
        name="rwkv_s",
    )(mn, mn)


def _rwkv_f_body(y0_ref, bonus_ref, rp_ref, sf_ref, sb_ref, gate_ref, lng_ref, lnb_ref, o_ref, *, tt):
    npair = D_MODEL // PAIR
    nc = tt // CHUNK
    cs = lambda c: slice(c * CHUNK, (c + 1) * CHUNK)
    ps = lambda p: slice(p * PAIR, (p + 1) * PAIR)
    yf = [[jnp.dot(rp_ref[0, cs(c), ps(p)], sf_ref[c, p], preferred_element_type=f32) for c in range(nc)]
          for p in range(npair)]
    yb = [[jnp.dot(rp_ref[1, cs(c), ps(p)], sb_ref[c, p], preferred_element_type=f32) for c in range(nc)]
          for p in range(npair)]
    seg1, seg2 = _seg_matrix(PAIR), _seg_matrix(2 * PAIR)
    ys = [y0_ref[:, ps(p)].astype(f32) + jnp.concatenate(yf[p], axis=0) + jnp.concatenate(yb[p], axis=0)
          for p in range(npair)]
    mus = [_segsum(y, seg2) * (1.0 / HEAD) for y in ys]
    dvs = [y - mu for y, mu in zip(ys, mus)]
    vrs = [_segsum(dv * dv, seg1) * (1.0 / HEAD) for dv in dvs]
    for p in range(npair):
        yn = dvs[p] * lax.rsqrt(vrs[p] + GN_EPS) * lng_ref[:, ps(p)] + lnb_ref[:, ps(p)] + bonus_ref[:, ps(p)].astype(f32)
        g = gate_ref[0, :, ps(p)].astype(f32)
        o_ref[:, ps(p)] = (yn * (g * _sigmoid(g))).astype(bf16)


def _rwkv_f(y0, bonus, rp, sf, sb, p9, ln_g, ln_b, *, tt):
    m = y0.shape[0]
    cpt = tt // CHUNK
    npair = D_MODEL // PAIR
    tile = pl.BlockSpec((tt, D_MODEL), lambda i: (i, 0))
    vec = pl.BlockSpec((1, D_MODEL), lambda i: (0, 0))
    state = pl.BlockSpec((cpt, npair, PAIR, PAIR), lambda i: (i, 0, 0, 0))
    return pl.pallas_call(
        functools.partial(_rwkv_f_body, tt=tt),
        grid=(m // tt,),
        in_specs=[
            tile, tile,
            pl.BlockSpec((2, tt, D_MODEL), lambda i: (0, i, 0)),
            state, state,
            pl.BlockSpec((1, tt, D_MODEL), lambda i: (P_GA, i, 0)),
            vec, vec,
        ],
        out_specs=tile,
        out_shape=jax.ShapeDtypeStruct((m, D_MODEL), bf16),
        compiler_params=pltpu.CompilerParams(dimension_semantics=("parallel",), vmem_limit_bytes=VMEM_LIMIT),
        name="rwkv_f",
    )(y0, bonus, rp, sf, sb, p9, ln_g, ln_b)


def _attn_body(q_ref, kp_ref, kc_ref, kn_ref, vp_ref, vc_ref, vn_ref, tp_ref, tc_ref, tn_ref, gate_ref,
               qg_ref, kg_ref, sink_ref, o_ref, *, seq, qb):
    step_rows = qb * BLOCK
    start = (pl.program_id(0) % (seq // step_rows)) * step_rows
    ctx = BLOCK + 2 * WINDOW
    qoff = lax.broadcasted_iota(jnp.int32, (BLOCK, ctx), 0)
    koff = lax.broadcasted_iota(jnp.int32, (BLOCK, ctx), 1) - WINDOW
    valid = []
    for b in range(qb):
        kpos = start + b * BLOCK + koff
        valid.append((jnp.abs(qoff - koff) <= WINDOW) & (kpos >= 0) & (kpos < seq))

    seg = _seg_matrix(PAIR)
    swap = lambda t: pltpu.roll(t, HEAD, axis=1)

    def norm_rope(x, gain, tab):
        lane = lax.broadcasted_iota(jnp.int32, x.shape, 1)
        xn = x * lax.rsqrt(_segsum(x * x, seg) * (1.0 / HEAD) + NORM_EPS) * gain
        rot = jnp.where(lane % HEAD < HEAD // 2, pltpu.roll(xn, PAIR - HEAD // 2, axis=1),
                        pltpu.roll(xn, HEAD // 2, axis=1))
        return xn * tab[:, :PAIR] + rot * tab[:, PAIR:]

    def place(t, e, want_lo):
        lo = lax.broadcasted_iota(jnp.int32, t.shape, 1) < HEAD
        src = t if (e == 0) == want_lo else swap(t)
        return (jnp.where(lo, src, 0.0) if want_lo else jnp.where(lo, 0.0, src)).astype(bf16)

    tab_k = jnp.concatenate([tp_ref[...], tc_ref[...], tn_ref[...]], axis=0)
    tab_q = tc_ref[...]
    k_lo, k_hi, v_lo, v_hi = [], [], [], []
    for kp in range(D_KV // PAIR):
        ps_ = slice(kp * PAIR, (kp + 1) * PAIR)
        kx = jnp.concatenate([kp_ref[0, :, ps_], kc_ref[0, :, ps_], kn_ref[0, :, ps_]], axis=0)
        vx = jnp.concatenate([vp_ref[0, :, ps_], vc_ref[0, :, ps_], vn_ref[0, :, ps_]], axis=0).astype(f32)
        kr = norm_rope(kx.astype(f32), kg_ref[...], tab_k)
        for e in range(2):
            k_lo.append(place(kr, e, True))
            k_hi.append(place(kr, e, False))
            v_lo.append(place(vx, e, True))
            v_hi.append(place(vx, e, False))
    npair = D_MODEL // PAIR
    psl = lambda p: slice(p * PAIR, (p + 1) * PAIR)
    log2e = math.log2(math.e)
    qrs = [(norm_rope(q_ref[0, :, psl(p)].astype(f32), qg_ref[...], tab_q) * (HEAD ** -0.5 * log2e)).astype(bf16)
           for p in range(npair)]
    inst = [(b, hq) for b in range(qb) for hq in range(N_HEADS)]
    rows = lambda b: slice(b * BLOCK, (b + 1) * BLOCK)
    keys = lambda b: slice(b * BLOCK, b * BLOCK + ctx)
    def masked(s, ok):
        return jnp.concatenate([jnp.where(ok[:, :WINDOW], s[:, :WINDOW], -1e30), s[:, WINDOW:WINDOW + BLOCK],
                                jnp.where(ok[:, WINDOW + BLOCK:], s[:, WINDOW + BLOCK:], -1e30)], axis=1)

    ss = [masked(lax.dot_general(qrs[hq // 2][rows(b)], (k_lo if hq % 2 == 0 else k_hi)[hq // GRP][keys(b)],
                                 (((1,), (1,)), ((), ())), preferred_element_type=f32), valid[b])
          for b, hq in inst]
    sinks = [sink_ref[0:1, hq:hq + 1] * log2e for b, hq in inst]
    mxs = [jnp.maximum(jnp.max(s, axis=-1, keepdims=True), sk) for s, sk in zip(ss, sinks)]
    ps = [jnp.exp2(s - mx) for s, mx in zip(ss, mxs)]
    dens = [jnp.sum(p, axis=-1, keepdims=True) + jnp.exp2(sk - mx) for p, sk, mx in zip(ps, sinks, mxs)]
    os_ = [jnp.dot(p.astype(bf16), (v_lo if hq % 2 == 0 else v_hi)[hq // GRP][keys(b)], preferred_element_type=f32) / den
           for (b, hq), p, den in zip(inst, ps, dens)]
    for b in range(qb):
        for p in range(npair):
            gt = gate_ref[0, rows(b), psl(p)].astype(f32)
            i0 = b * N_HEADS + 2 * p
            o_ref[rows(b), psl(p)] = ((os_[i0] + os_[i0 + 1]) * (gt * _sigmoid(gt))).astype(bf16)


def _attn(p9, rope_tab, q_g, k_g, sink, *, seq, qb):
    m = p9.shape[1]
    nblk = seq // BLOCK
    rows = qb * BLOCK

    def nbr(delta):
        def f(i):
            t = (i * qb) % nblk
            return i * qb - t + jnp.clip(t + delta, 0, nblk - 1)
        return f

    def kv(col):
        return [pl.BlockSpec((1, BLOCK, D_KV), lambda i, f=nbr(-1): (P_X, f(i), col)),
                pl.BlockSpec((1, rows, D_KV), lambda i: (P_X, i, col)),
                pl.BlockSpec((1, BLOCK, D_KV), lambda i, f=nbr(qb): (P_X, f(i), col))]

    tabs = [pl.BlockSpec((BLOCK, 2 * PAIR), lambda i, f=nbr(-1): (f(i) % nblk, 0)),
            pl.BlockSpec((rows, 2 * PAIR), lambda i: (i % (nblk // qb), 0)),
            pl.BlockSpec((BLOCK, 2 * PAIR), lambda i, f=nbr(qb): (f(i) % nblk, 0))]

    return pl.pallas_call(
        functools.partial(_attn_body, seq=seq, qb=qb),
        grid=(m // rows,),
        in_specs=[pl.BlockSpec((1, rows, D_MODEL), lambda i: (P_Q, i, 0))] + kv(1) + kv(2) + tabs + [
            pl.BlockSpec((1, rows, D_MODEL), lambda i: (P_GB, i, 0)),
            pl.BlockSpec((1, PAIR), lambda i: (0, 0)),
            pl.BlockSpec((1, PAIR), lambda i: (0, 0)),
            pl.BlockSpec((1, N_HEADS), lambda i: (0, 0)),
        ],
        out_specs=pl.BlockSpec((rows, D_MODEL), lambda i: (i, 0)),
        out_shape=jax.ShapeDtypeStruct((m, D_MODEL), bf16),
        compiler_params=pltpu.CompilerParams(dimension_semantics=("parallel",), vmem_limit_bytes=VMEM_LIMIT),
        name="attn",
    )(*([p9] * 7), rope_tab, rope_tab, rope_tab, p9, q_g, k_g, sink)


def _rope_table(seq):
    inv = 1.0 / (ROPE_THETA ** (jnp.arange(0, HEAD, 2, dtype=f32) / HEAD))
    ang = jnp.arange(seq, dtype=f32)[:, None] * inv[None, :]
    cos, sin = jnp.cos(ang), jnp.sin(ang)
    return jnp.concatenate([cos, cos, cos, cos, -sin, sin, -sin, sin], axis=1)


def _out_proj_body(x_ref, ga_ref, gb_ref, ma_ref, mb_ref, wa_ref, wb_ref, wo_ref, o_ref):
    ya = jnp.dot(ga_ref[...], wa_ref[...], preferred_element_type=f32)
    yb = jnp.dot(gb_ref[...], wb_ref[...], preferred_element_type=f32)
    mixed = _sigmoid(ma_ref[0].astype(f32)) * ya + _sigmoid(mb_ref[0].astype(f32)) * yb
    o_ref[...] = x_ref[...] + jnp.dot(mixed.astype(bf16), wo_ref[...], preferred_element_type=f32)


def _out_proj(x2d, ga, gb, p9, wa, wb, wo, *, tm):
    m = x2d.shape[0]
    row = pl.BlockSpec((tm, D_MODEL), lambda i: (i, 0))
    wspec = pl.BlockSpec((D_MODEL, D_MODEL), lambda i: (0, 0))
    return pl.pallas_call(
        _out_proj_body,
        grid=(m // tm,),
        in_specs=[row, row, row,
                  pl.BlockSpec((1, tm, D_MODEL), lambda i: (P_MA, i, 0)),
                  pl.BlockSpec((1, tm, D_MODEL), lambda i: (P_MB, i, 0)),
                  wspec, wspec, wspec],
        out_specs=row,
        out_shape=jax.ShapeDtypeStruct((m, D_MODEL), f32),
        compiler_params=pltpu.CompilerParams(dimension_semantics=("parallel",), vmem_limit_bytes=VMEM_LIMIT),
        name="out_proj",
    )(x2d, ga, gb, p9, p9, wa, wb, wo)


def _layer(x2d, lw, rope_tab, *, n_seq, seq):
    for tile in (TM_IN, TM_OUT, TT_P, TT_F, GRP_S * CHUNK, QB_ATTN * BLOCK):
        assert seq % tile == 0, (seq, tile)
    p9 = _in_proj(x2d, lw["norm_g"], lw["w_p"], tm=TM_IN)
    y0, bonus, rp, mn = _rwkv_p(p9, lw["mu_p"], lw["w0"], lw["wlu"], lw["a0"], lw["alu"], lw["k_k"], lw["k_a"],
                                lw["r_k"], tt=TT_P, seq=seq, pp=PAIRS_P)
    sf, sb = _rwkv_s(mn, n_seq=n_seq, grp=GRP_S)
    ga = _rwkv_f(y0, bonus, rp, sf, sb, p9, lw["ln_g"], lw["ln_b"], tt=TT_F)
    gb = _attn(p9, rope_tab, lw["q_g"], lw["k_g"], lw["sink"], seq=seq, qb=QB_ATTN)
    return _out_proj(x2d, ga, gb, p9, lw["wa"], lw["wb"], lw["wo"], tm=TM_OUT)


def _trunk(x, layers):
    n_seq, seq, _ = x.shape
    rope_tab = _rope_table(seq)
    x2d = x.reshape(n_seq * seq, D_MODEL)
    for lw in layers:
        x2d = _layer(x2d, lw, rope_tab, n_seq=n_seq, seq=seq)
    return x2d.reshape(n_seq, seq, D_MODEL)


def _prep_layers(norm_g, w_in, shift_mu, w0, w_lora_up, a0, a_lora_up, k_k, k_a, r_k, ln_x_g, ln_x_b,
                 q_norm_g, k_norm_g, sink, w_proj_a, w_proj_b, w_out):
    layers = []
    row = lambda v: v.reshape(1, -1).astype(f32)
    for l in range(norm_g.shape[0]):
        w_p, mu_p = _prep_layer_weights(w_in[l], shift_mu[l])
        layers.append(dict(
            norm_g=row(norm_g[l]), w_p=w_p, mu_p=mu_p,
            w0=w0[l].astype(f32), wlu=w_lora_up[l].astype(bf16), a0=a0[l].astype(f32), alu=a_lora_up[l].astype(bf16),
            k_k=row(k_k[l]), k_a=row(k_a[l]), r_k=row(r_k[l]), ln_g=row(ln_x_g[l]), ln_b=row(ln_x_b[l]),
            q_g=row(jnp.tile(q_norm_g[l], 2)), k_g=row(jnp.tile(k_norm_g[l], 2)), sink=row(sink[l]),
            wa=w_proj_a[l].astype(bf16), wb=w_proj_b[l].astype(bf16), wo=w_out[l].astype(bf16)))
    return layers


def kernel(x_prompt, x_sample, norm_g, w_in, shift_mu, w0, w_lora_up, a0, a_lora_up, k_k, k_a, r_k,
           ln_x_g, ln_x_b, q_norm_g, k_norm_g, sink, w_proj_a, w_proj_b, w_out):
    layers = _prep_layers(norm_g, w_in, shift_mu, w0, w_lora_up, a0, a_lora_up, k_k, k_a, r_k, ln_x_g, ln_x_b,
                          q_norm_g, k_norm_g, sink, w_proj_a, w_proj_b, w_out)
    return _trunk(x_prompt, layers), _trunk(x_sample, layers)
```

```python
import functools
import math

import jax
import jax.numpy as jnp
from jax import lax
from jax.experimental import pallas as pl
from jax.experimental.pallas import tpu as pltpu

D_MODEL = 1024
HEAD = 64
N_HEADS = D_MODEL // HEAD
R_LORA = 64
HKV = 4
GRP = N_HEADS // HKV
D_KV = HKV * HEAD
WINDOW = 128
BLOCK = 128
GN_EPS = 64e-5
NORM_EPS = 1e-6
ROPE_THETA = 10000.0
C_SHIFT = 3 * D_MODEL + 4 * R_LORA
N_PLANES = 9
CHUNK = 64
VMEM_LIMIT = 56 * 1024 * 1024

TM_IN = 2048
TM_OUT = 1024
TT_P = 256
TT_F = 512
GRP_S = 16
PAIRS_P = 8
QB_ATTN = 4

P_R, P_K, P_V, P_GA, P_Q, P_GB, P_MA, P_MB, P_X = range(9)

f32 = jnp.float32
bf16 = jnp.bfloat16


def _dot(a, b):
    return jnp.dot(a.astype(bf16), b.astype(bf16), preferred_element_type=f32)


def _dot_tn(a, b):
    return lax.dot_general(a.astype(bf16), b.astype(bf16), (((0,), (0,)), ((), ())),
                           preferred_element_type=f32)


def _sigmoid(x):
    return 1.0 / (1.0 + jnp.exp(-x))


def _in_proj_body(x_ref, g_ref, w_ref, o_ref, h_ref):
    @pl.when(pl.program_id(1) == 0)
    def _():
        x = x_ref[...]
        ms = jnp.mean(x * x, axis=-1, keepdims=True)
        h_ref[...] = (x * lax.rsqrt(ms + NORM_EPS) * g_ref[...]).astype(bf16)

    o_ref[0] = jnp.dot(h_ref[...], w_ref[...], preferred_element_type=f32).astype(bf16)


def _in_proj(x2d, norm_g, w_p, *, tm):
    m = x2d.shape[0]
    return pl.pallas_call(
        _in_proj_body,
        grid=(m // tm, N_PLANES),
        in_specs=[
            pl.BlockSpec((tm, D_MODEL), lambda i, j: (i, 0)),
            pl.BlockSpec((1, D_MODEL), lambda i, j: (0, 0)),
            pl.BlockSpec((D_MODEL, D_MODEL), lambda i, j: (0, j)),
        ],
        out_specs=pl.BlockSpec((1, tm, D_MODEL), lambda i, j: (j, i, 0)),
        out_shape=jax.ShapeDtypeStruct((N_PLANES, m, D_MODEL), bf16),
        scratch_shapes=[pltpu.VMEM((tm, D_MODEL), bf16)],
        compiler_params=pltpu.CompilerParams(
            dimension_semantics=("parallel", "arbitrary"), vmem_limit_bytes=VMEM_LIMIT),
        name="in_proj",
    )(x2d, norm_g, w_p)


def _prep_layer_weights(w_in, shift_mu):
    c = C_SHIFT
    d = D_MODEL
    cols = [
        (0, d), (d, 2 * d), (2 * d, 3 * d),
        (c, c + d),
        (c + d, c + 2 * d),
        (c + 2 * d + 2 * D_KV, c + 3 * d + 2 * D_KV),
        (c + 3 * d + 2 * D_KV, c + 4 * d + 2 * D_KV),
        (c + 4 * d + 2 * D_KV, c + 5 * d + 2 * D_KV),
    ]
    planes = [w_in[:, a:b] for a, b in cols]
    small = jnp.concatenate([w_in[:, 3 * d:c], w_in[:, c + 2 * d:c + 2 * d + 2 * D_KV],
                             jnp.zeros((d, d - 4 * R_LORA - 2 * D_KV), w_in.dtype)], axis=1)
    w_p = jnp.concatenate(planes + [small], axis=1).astype(bf16)
    zeros = jnp.zeros((d,), f32)
    mu_small = jnp.concatenate([shift_mu[3 * d:c], jnp.zeros((d - 4 * R_LORA,), f32)])
    mu_p = jnp.concatenate([shift_mu[0:d], shift_mu[d:2 * d], shift_mu[2 * d:3 * d]] + [zeros] * 5 + [mu_small])
    return w_p, mu_p.reshape(1, N_PLANES * d)


PAIR = 2 * HEAD
SUBLANES = 8
PADR = SUBLANES
PRE, PRE_DONE, CHUNK_STAGE = "pre", "pre_done", "chunk"


def _token_shift(tt, seq):
    i = pl.program_id(0)
    first = (i * tt) % seq == 0
    last = ((i + 1) * tt) % seq == 0

    def shift(main, pv, nx, mu, buf):
        c = main.astype(f32)
        buf[PADR:PADR + tt, :] = c
        buf[PADR - 1:PADR, :] = jnp.where(first, 0.0, pv[pv.shape[0] - 1:, :].astype(f32))
        buf[PADR + tt:PADR + tt + 1, :] = jnp.where(last, 0.0, nx[:1, :].astype(f32))
        nbr = 0.5 * (buf[PADR - 1:PADR - 1 + tt, :] + buf[PADR + 1:PADR + 1 + tt, :])
        return c + mu * (nbr - c)

    return shift


def _rwkv_p_body(r_ref, r_pv, r_nx, k_ref, k_pv, k_nx, v_ref, v_pv, v_nx, mur_ref, muk_ref, muv_ref,
                 kk_ref, ka_ref, rk_ref,
                 y0_ref, bonus_ref, rp_ref, mn_ref,
                 at_s, rt_s, kp_s, bp_s, khm_s, bhm_s, atx_s, rtx_s, vx_s, pl_s, sh_s, *, tt, shift, zs, a_pre):
    nc = tt // CHUNK
    r = shift(r_ref[0], r_pv[0], r_nx[0], mur_ref[...], sh_s.at[0])
    k = shift(k_ref[0], k_pv[0], k_nx[0], muk_ref[...], sh_s.at[1])
    v = shift(v_ref[0], v_pv[0], v_nx[0], muv_ref[...], sh_s.at[2])
    yield PRE

    seg1, seg2 = _seg_matrix(PAIR), _seg_matrix(2 * PAIR)
    lo = lax.broadcasted_iota(jnp.int32, (tt, PAIR), 1) < HEAD
    swap = lambda t: pltpu.roll(t, HEAD, axis=1)

    def head_lo(t, h):
        return jnp.where(lo, t if h == 0 else swap(t), 0.0)

    def head_hi(t, h):
        return jnp.where(lo, 0.0, swap(t) if h == 0 else t)

    def head_own(t, h):
        return jnp.where(lo, t, 0.0) if h == 0 else jnp.where(lo, 0.0, t)

    for h in range(2):
        vx_s[h] = head_hi(v, h).astype(bf16)

    kkv = k * kk_ref[...]
    kk = kkv / jnp.maximum(jnp.sqrt(_segsum(kkv * kkv, seg1)), 1e-12)
    yield PRE

    rows = lax.broadcasted_iota(jnp.int32, (tt, tt), 0)
    cols = lax.broadcasted_iota(jnp.int32, (tt, tt), 1)
    same = (rows // CHUNK) == (cols // CHUNK)

    kmod_sum = jnp.zeros_like(k)
    for d in range(2):
        if d == 0:
            tri_bd = jnp.where(same & (cols <= rows), 1.0, 0.0)
        else:
            tri_bd = jnp.where(same & (cols >= rows), 1.0, 0.0)
        lw = -math.exp(-0.5) * _sigmoid(zs[d])
        a = _sigmoid(a_pre[d])
        yield PRE
        kmod = k * (1.0 + (a - 1.0) * ka_ref[...])
        kmod_sum = kmod_sum + kmod
        b = -(kk * a)
        lw0 = lw.astype(bf16)
        lw1 = (lw - lw0.astype(f32)).astype(bf16)
        cum2 = jnp.dot(tri_bd.astype(bf16), jnp.concatenate([lw0, lw1], axis=1), preferred_element_type=f32)
        cum = cum2[:, :PAIR] + cum2[:, PAIR:]
        edge = CHUNK - 1 if d == 0 else 0
        tot = jnp.concatenate([jnp.broadcast_to(cum[c * CHUNK + edge:c * CHUNK + edge + 1, :], (CHUNK, PAIR))
                               for c in range(nc)], axis=0)
        yield PRE
        p_inv = jnp.exp(-cum)
        p_end = jnp.exp(tot - cum)
        at = kk * jnp.exp(cum - lw)
        rt = r * jnp.exp(cum)
        kh = kmod * p_inv
        bh = b * p_inv
        p_tot = jnp.exp(tot)
        at_s[d] = at.astype(bf16)
        rt_s[d] = rt.astype(bf16)
        pl_s[d] = p_tot
        yield PRE
        kp_s[d] = (kmod * p_end).astype(bf16)
        bp_s[d] = (b * p_end).astype(bf16)
        for h in range(2):
            khm_s[d, h] = head_own(kh, h).astype(bf16)
            bhm_s[d, h] = head_own(bh, h).astype(bf16)
            atx_s[d, h] = head_lo(at, h).astype(bf16)
            rtx_s[d, h] = head_lo(rt, h)
        yield PRE

    bonus = _segsum(r * kmod_sum * rk_ref[...], seg2) * v
    yield PRE_DONE

    row = lax.broadcasted_iota(jnp.int32, (CHUNK, PAIR), 0)
    col = lax.broadcasted_iota(jnp.int32, (CHUNK, PAIR), 1)
    col_t = col % CHUNK
    lo_c = col < HEAD
    strict = (col_t < row, col_t > row)
    incl = (col_t <= row, col_t >= row)
    pairs = [(d, c) for d in range(2) for c in range(nc)]
    inst = [(d, c, h) for d, c in pairs for h in range(2)]
    cs = lambda c: slice(c * CHUNK, (c + 1) * CHUNK)
    swap_c = lambda t: pltpu.roll(t, HEAD, axis=1)

    top, bot = [], []
    for d, c in pairs:
        lhs = jnp.concatenate([at_s[d, cs(c)], rt_s[d, cs(c)]], axis=0)
        rhs = jnp.concatenate([bhm_s[d, 0, cs(c)], khm_s[d, 0, cs(c)],
                               bhm_s[d, 1, cs(c)], khm_s[d, 1, cs(c)]], axis=0)
        sc = lax.dot_general(lhs, rhs, (((1,), (1,)), ((), ())), preferred_element_type=f32)
        for h in range(2):
            top.append(jnp.where(strict[d], sc[:CHUNK, h * PAIR:(h + 1) * PAIR], 0.0))
            bot.append(jnp.where(incl[d], sc[CHUNK:, h * PAIR:(h + 1) * PAIR], 0.0))
    yield CHUNK_STAGE
    xs, aps = [], []
    for i, (d, c, h) in enumerate(inst):
        vx = vx_s[h, cs(c)]
        a_ak = jnp.where(lo_c, 0.0, top[i]).astype(bf16)
        akv = jnp.dot(a_ak, jnp.concatenate([vx, vx], axis=0), preferred_element_type=f32)
        xs.append(atx_s[d, h, cs(c)].astype(f32) + akv)
        aps.append(top[i][:, :CHUNK])
    n_dbl = CHUNK.bit_length() - 1
    for it in range(n_dbl):
        yield CHUNK_STAGE
        last = it + 1 == n_dbl
        res = [_dot(ap, x if last else jnp.concatenate([x, ap], axis=1)) for ap, x in zip(aps, xs)]
        xs = [x + rs[:, :PAIR] for x, rs in zip(xs, res)]
        if not last:
            aps = [rs[:, PAIR:] for rs in res]
    yield CHUNK_STAGE
    rhs_o = [jnp.concatenate([xs[i].astype(bf16), vx_s[h, cs(c)]], axis=0) for i, (d, c, h) in enumerate(inst)]
    rys = [_dot(bot[i], rhs_o[i]) + rtx_s[d, h, cs(c)] for i, (d, c, h) in enumerate(inst)]
    yield CHUNK_STAGE
    mns = [_dot_tn(jnp.concatenate([bp_s[d, cs(c), h * HEAD:(h + 1) * HEAD],
                                    kp_s[d, cs(c), h * HEAD:(h + 1) * HEAD]], axis=0), rhs_o[i])
           for i, (d, c, h) in enumerate(inst)]
    yield CHUNK_STAGE
    for i, (d, c, h) in enumerate(inst):
        p_blk = pl_s[d, c * CHUNK:c * CHUNK + SUBLANES, :]
        p_row = (p_blk if h == 0 else swap_c(p_blk))[0:1]
        mn_ref[d, c, h] = (mns[i] + jnp.where(row == col, p_row, 0.0)).astype(bf16)
    y0_fwd = {}
    for j, (d, c) in enumerate(pairs):
        ry0, ry1 = rys[2 * j], rys[2 * j + 1]
        rp_ref[d, cs(c), :] = jnp.where(lo_c, ry0, swap_c(ry1)).astype(bf16)
        y0 = jnp.where(lo_c, swap_c(ry0), ry1)
        if d == 0:
            y0_fwd[c] = y0
        else:
            y0_ref[cs(c), :] = (y0_fwd[c] + y0).astype(y0_ref.dtype)
    bonus_ref[...] = bonus.astype(bonus_ref.dtype)


N_IN_P = 23
N_OUT_P = 4


def _rwkv_p_multi(*refs, tt, seq, pp):
    ins, outs = refs[:N_IN_P], refs[N_IN_P:N_IN_P + N_OUT_P]
    scr, shx_s = refs[N_IN_P + N_OUT_P:-1], refs[-1]
    x_ref, x_pv, x_nx, mux_ref = ins[9], ins[10], ins[11], ins[15]
    w0_ref, wlu_ref, a0_ref, alu_ref = ins[16:20]
    shift = _token_shift(tt, seq)
    xs = shift(x_ref[0], x_pv[0], x_nx[0], mux_ref[...], shx_s)
    zs = [w0_ref[d:d + 1, :] + _dot(jnp.tanh(xs[:, d * R_LORA:(d + 1) * R_LORA]), wlu_ref[d]) for d in range(2)]
    a_pre = [a0_ref[d:d + 1, :] + _dot(xs[:, (2 + d) * R_LORA:(3 + d) * R_LORA], alu_ref[d]) for d in range(2)]
    stages = []
    for q in range(pp):
        lanes = pl.ds(q * PAIR, PAIR)
        lsl = slice(q * PAIR, (q + 1) * PAIR)
        view = lambda ref, lanes=lanes: ref.at[..., lanes]
        planes = [view(ref) for ref in ins[0:9]]
        gains = [view(ref) for ref in ins[12:15]]
        params = [view(ref) for ref in ins[20:23]]
        y0_ref, bonus_ref, rp_ref, mn_ref = outs
        stages.append(_rwkv_p_body(*planes, *gains, *params,
                                   view(y0_ref), view(bonus_ref), view(rp_ref), mn_ref.at[:, :, pl.ds(2 * q, 2)],
                                   *[s.at[q] for s in scr], tt=tt, shift=shift,
                                   zs=[z[:, lsl] for z in zs], a_pre=[a[:, lsl] for a in a_pre]))
    def run_pre(stage):
        while next(stage) != PRE_DONE:
            pass

    run_pre(stages[0])
    for q in range(pp):
        chunk, pre = stages[q], (stages[q + 1] if q + 1 < pp else None)
        chunk_live, pre_live = True, pre is not None
        while chunk_live or pre_live:
            if chunk_live:
                chunk_live = next(chunk, None) is not None
            if pre_live:
                pre_live = next(pre) != PRE_DONE


def _seg_matrix(n):
    r = lax.broadcasted_iota(jnp.int32, (n, n), 0) // HEAD
    c = lax.broadcasted_iota(jnp.int32, (n, n), 1) // HEAD
    return jnp.where(r == c, 1.0, 0.0).astype(bf16)


def _segsum(x, seg):
    x0 = x.astype(bf16)
    if seg.shape[0] == PAIR:
        return jnp.dot(x0, seg, preferred_element_type=f32)
    x1 = (x - x0.astype(f32)).astype(bf16)
    s = jnp.dot(jnp.concatenate([x0, x1], axis=1), seg, preferred_element_type=f32)
    return s[:, :PAIR] + s[:, PAIR:]


def _rwkv_p(p9, mu_p, w0, wlu, a0, alu, k_k, k_a, r_k, *, tt, seq, pp):
    m = p9.shape[1]
    nt = m // tt
    npair = D_MODEL // PAIR
    cpt = tt // CHUNK
    wl = pp * PAIR
    hb = 2 * SUBLANES

    def tiles(plane):
        return [pl.BlockSpec((1, tt, wl), lambda i, j: (plane, i, j)),
                pl.BlockSpec((1, hb, wl), lambda i, j: (plane, jnp.maximum(i * (tt // hb) - 1, 0), j)),
                pl.BlockSpec((1, hb, wl), lambda i, j: (plane, jnp.minimum((i + 1) * (tt // hb), m // hb - 1), j))]

    vec = pl.BlockSpec((1, wl), lambda i, j: (0, j))
    mu = lambda plane: pl.BlockSpec((1, wl), lambda i, j: (0, plane * (npair // pp) + j))
    wx = 4 * R_LORA
    return pl.pallas_call(
        functools.partial(_rwkv_p_multi, tt=tt, seq=seq, pp=pp),
        grid=(nt, npair // pp),
        in_specs=tiles(P_R) + tiles(P_K) + tiles(P_V) + [
            pl.BlockSpec((1, tt, wx), lambda i, j: (P_X, i, 0)),
            pl.BlockSpec((1, hb, wx), lambda i, j: (P_X, jnp.maximum(i * (tt // hb) - 1, 0), 0)),
            pl.BlockSpec((1, hb, wx), lambda i, j: (P_X, jnp.minimum((i + 1) * (tt // hb), m // hb - 1), 0)),
            mu(P_R), mu(P_K), mu(P_V),
            pl.BlockSpec((1, wx), lambda i, j: (0, P_X * D_MODEL // wx)),
            pl.BlockSpec((2, wl), lambda i, j: (0, j)),
            pl.BlockSpec((2, R_LORA, wl), lambda i, j: (0, 0, j)),
            pl.BlockSpec((2, wl), lambda i, j: (0, j)),
            pl.BlockSpec((2, R_LORA, wl), lambda i, j: (0, 0, j)),
            vec, vec, vec,
        ],
        out_specs=[
            pl.BlockSpec((tt, wl), lambda i, j: (i, j)),
            pl.BlockSpec((tt, wl), lambda i, j: (i, j)),
            pl.BlockSpec((2, tt, wl), lambda i, j: (0, i, j)),
            pl.BlockSpec((2, cpt, 2 * pp, HEAD, PAIR), lambda i, j: (0, i, j, 0, 0)),
        ],
        out_shape=[
            jax.ShapeDtypeStruct((m, D_MODEL), bf16),
            jax.ShapeDtypeStruct((m, D_MODEL), bf16),
            jax.ShapeDtypeStruct((2, m, D_MODEL), bf16),
            jax.ShapeDtypeStruct((2, m // CHUNK, N_HEADS, HEAD, PAIR), bf16),
        ],
        scratch_shapes=(
            [pltpu.VMEM((pp, 2, tt, PAIR), bf16) for _ in range(4)]
            + [pltpu.VMEM((pp, 2, 2, tt, PAIR), bf16) for _ in range(3)]
            + [pltpu.VMEM((pp, 2, 2, tt, PAIR), f32),
               pltpu.VMEM((pp, 2, tt, PAIR), bf16),
               pltpu.VMEM((pp, 2, tt, PAIR), f32),
               pltpu.VMEM((pp, 3, tt + 2 * PADR, PAIR), f32),
               pltpu.VMEM((tt + 2 * PADR, wx), f32)]),
        compiler_params=pltpu.CompilerParams(
            dimension_semantics=("parallel", "parallel"), vmem_limit_bytes=VMEM_LIMIT),
        name="rwkv_p",
    )(*([p9] * 12), mu_p, mu_p, mu_p, mu_p, w0, wlu, a0, alu, k_k, k_a, r_k)


def _rwkv_s_body(mnf_ref, mnb_ref, sf_ref, sb_ref, st_ref, *, grp):
    @pl.when(pl.program_id(1) == 0)
    def _():
        st_ref[...] = jnp.zeros_like(st_ref)

    zero = jnp.zeros((HEAD, HEAD), bf16)

    def step(d, mn_ref, s_out_ref, cc):
        s = st_ref[d].astype(bf16)
        for p in range(N_HEADS // 2):
            s_out_ref[cc, p] = jnp.concatenate([jnp.concatenate([s[2 * p], zero], axis=1),
                                                jnp.concatenate([zero, s[2 * p + 1]], axis=1)], axis=0)
        m_t = mn_ref[0, cc, :, :, 0:HEAD]
        n_t = mn_ref[0, cc, :, :, HEAD:].astype(f32)
        st_ref[d] = jnp.einsum('hij,hjk->hik', m_t, s, preferred_element_type=f32) + n_t

    for cc in range(grp):
        step(0, mnf_ref, sf_ref, cc)
        step(1, mnb_ref, sb_ref, grp - 1 - cc)


def _rwkv_s(mn, *, n_seq, grp):
    nchunk = mn.shape[1]
    npair = N_HEADS // 2
    ng = nchunk // n_seq // grp
    blk_in = (1, grp, N_HEADS, HEAD, PAIR)
    blk_out = (grp, npair, PAIR, PAIR)
    out = jax.ShapeDtypeStruct((nchunk, npair, PAIR, PAIR), bf16)
    return pl.pallas_call(
        functools.partial(_rwkv_s_body, grp=grp),
        grid=(n_seq, ng),
        in_specs=[pl.BlockSpec(blk_in, lambda b, g: (0, b * ng + g, 0, 0, 0)),
                  pl.BlockSpec(blk_in, lambda b, g: (1, b * ng + ng - 1 - g, 0, 0, 0))],
        out_specs=[pl.BlockSpec(blk_out, lambda b, g: (b * ng + g, 0, 0, 0)),
                   pl.BlockSpec(blk_out, lambda b, g: (b * ng + ng - 1 - g, 0, 0, 0))],
        out_shape=[out, out],
        scratch_shapes=[pltpu.VMEM((2, N_HEADS, HEAD, HEAD), f32)],
        compiler_params=pltpu.CompilerParams(
            dimension_semantics=("parallel", "arbitrary"), vmem_limit_bytes=VMEM_LIMIT),
        name="rwkv_s",
    )(mn, mn)


def _rwkv_f_body(y0_ref, bonus_ref, rp_ref, sf_ref, sb_ref, gate_ref, lng_ref, lnb_ref, o_ref, *, tt):
    npair = D_MODEL // PAIR
    nc = tt // CHUNK
    cs = lambda c: slice(c * CHUNK, (c + 1) * CHUNK)
    ps = lambda p: slice(p * PAIR, (p + 1) * PAIR)
    yf = [[jnp.dot(rp_ref[0, cs(c), ps(p)], sf_ref[c, p], preferred_element_type=f32) for c in range(nc)]
          for p in range(npair)]
    yb = [[jnp.dot(rp_ref[1, cs(c), ps(p)], sb_ref[c, p], preferred_element_type=f32) for c in range(nc)]
          for p in range(npair)]
    seg1, seg2 = _seg_matrix(PAIR), _seg_matrix(2 * PAIR)
    ys = [y0_ref[:, ps(p)].astype(f32) + jnp.concatenate(yf[p], axis=0) + jnp.concatenate(yb[p], axis=0)
          for p in range(npair)]
    mus = [_segsum(y, seg2) * (1.0 / HEAD) for y in ys]
    dvs = [y - mu for y, mu in zip(ys, mus)]
    vrs = [_segsum(dv * dv, seg1) * (1.0 / HEAD) for dv in dvs]
    for p in range(npair):
        yn = dvs[p] * lax.rsqrt(vrs[p] + GN_EPS) * lng_ref[:, ps(p)] + lnb_ref[:, ps(p)] + bonus_ref[:, ps(p)].astype(f32)
        g = gate_ref[0, :, ps(p)].astype(f32)
        o_ref[:, ps(p)] = (yn * (g * _sigmoid(g))).astype(bf16)


def _rwkv_f(y0, bonus, rp, sf, sb, p9, ln_g, ln_b, *, tt):
    m = y0.shape[0]
    cpt = tt // CHUNK
    npair = D_MODEL // PAIR
    tile = pl.BlockSpec((tt, D_MODEL), lambda i: (i, 0))
    vec = pl.BlockSpec((1, D_MODEL), lambda i: (0, 0))
    state = pl.BlockSpec((cpt, npair, PAIR, PAIR), lambda i: (i, 0, 0, 0))
    return pl.pallas_call(
        functools.partial(_rwkv_f_body, tt=tt),
        grid=(m // tt,),
        in_specs=[
            tile, tile,
            pl.BlockSpec((2, tt, D_MODEL), lambda i: (0, i, 0)),
            state, state,
            pl.BlockSpec((1, tt, D_MODEL), lambda i: (P_GA, i, 0)),
            vec, vec,
        ],
        out_specs=tile,
        out_shape=jax.ShapeDtypeStruct((m, D_MODEL), bf16),
        compiler_params=pltpu.CompilerParams(dimension_semantics=("parallel",), vmem_limit_bytes=VMEM_LIMIT),
        name="rwkv_f",
    )(y0, bonus, rp, sf, sb, p9, ln_g, ln_b)


def _attn_body(q_ref, kp_ref, kc_ref, kn_ref, vp_ref, vc_ref, vn_ref, tp_ref, tc_ref, tn_ref, gate_ref,
               qg_ref, kg_ref, sink_ref, o_ref, *, seq, qb):
    step_rows = qb * BLOCK
    start = (pl.program_id(0) % (seq // step_rows)) * step_rows
    ctx = BLOCK + 2 * WINDOW
    qoff = lax.broadcasted_iota(jnp.int32, (BLOCK, ctx), 0)
    koff = lax.broadcasted_iota(jnp.int32, (BLOCK, ctx), 1) - WINDOW
    valid = []
    for b in range(qb):
        kpos = start + b * BLOCK + koff
        valid.append((jnp.abs(qoff - koff) <= WINDOW) & (kpos >= 0) & (kpos < seq))

    seg = _seg_matrix(PAIR)
    swap = lambda t: pltpu.roll(t, HEAD, axis=1)

    def norm_rope(x, gain, tab):
        lane = lax.broadcasted_iota(jnp.int32, x.shape, 1)
        xn = x * lax.rsqrt(_segsum(x * x, seg) * (1.0 / HEAD) + NORM_EPS) * gain
        rot = jnp.where(lane % HEAD < HEAD // 2, pltpu.roll(xn, PAIR - HEAD // 2, axis=1),
                        pltpu.roll(xn, HEAD // 2, axis=1))
        return xn * tab[:, :PAIR] + rot * tab[:, PAIR:]

    def place(t, e, want_lo):
        lo = lax.broadcasted_iota(jnp.int32, t.shape, 1) < HEAD
        src = t if (e == 0) == want_lo else swap(t)
        return (jnp.where(lo, src, 0.0) if want_lo else jnp.where(lo, 0.0, src)).astype(bf16)

    tab_k = jnp.concatenate([tp_ref[...], tc_ref[...], tn_ref[...]], axis=0)
    tab_q = tc_ref[...]
    k_lo, k_hi, v_lo, v_hi = [], [], [], []
    for kp in range(D_KV // PAIR):
        ps_ = slice(kp * PAIR, (kp + 1) * PAIR)
        kx = jnp.concatenate([kp_ref[0, :, ps_], kc_ref[0, :, ps_], kn_ref[0, :, ps_]], axis=0)
        vx = jnp.concatenate([vp_ref[0, :, ps_], vc_ref[0, :, ps_], vn_ref[0, :, ps_]], axis=0).astype(f32)
        kr = norm_rope(kx.astype(f32), kg_ref[...], tab_k)
        for e in range(2):
            k_lo.append(place(kr, e, True))
            k_hi.append(place(kr, e, False))
            v_lo.append(place(vx, e, True))
            v_hi.append(place(vx, e, False))
    npair = D_MODEL // PAIR
    psl = lambda p: slice(p * PAIR, (p + 1) * PAIR)
    log2e = math.log2(math.e)
    qrs = [(norm_rope(q_ref[0, :, psl(p)].astype(f32), qg_ref[...], tab_q) * (HEAD ** -0.5 * log2e)).astype(bf16)
           for p in range(npair)]
    inst = [(b, hq) for b in range(qb) for hq in range(N_HEADS)]
    rows = lambda b: slice(b * BLOCK, (b + 1) * BLOCK)
    keys = lambda b: slice(b * BLOCK, b * BLOCK + ctx)
    def masked(s, ok):
        return jnp.concatenate([jnp.where(ok[:, :WINDOW], s[:, :WINDOW], -1e30), s[:, WINDOW:WINDOW + BLOCK],
                                jnp.where(ok[:, WINDOW + BLOCK:], s[:, WINDOW + BLOCK:], -1e30)], axis=1)

    ss = [masked(lax.dot_general(qrs[hq // 2][rows(b)], (k_lo if hq % 2 == 0 else k_hi)[hq // GRP][keys(b)],
                                 (((1,), (1,)), ((), ())), preferred_element_type=f32), valid[b])
          for b, hq in inst]
    sinks = [sink_ref[0:1, hq:hq + 1] * log2e for b, hq in inst]
    mxs = [jnp.maximum(jnp.max(s, axis=-1, keepdims=True), sk) for s, sk in zip(ss, sinks)]
    ps = [jnp.exp2(s - mx) for s, mx in zip(ss, mxs)]
    dens = [jnp.sum(p, axis=-1, keepdims=True) + jnp.exp2(sk - mx) for p, sk, mx in zip(ps, sinks, mxs)]
    os_ = [jnp.dot(p.astype(bf16), (v_lo if hq % 2 == 0 else v_hi)[hq // GRP][keys(b)], preferred_element_type=f32) / den
           for (b, hq), p, den in zip(inst, ps, dens)]
    for b in range(qb):
        for p in range(npair):
            gt = gate_ref[0, rows(b), psl(p)].astype(f32)
            i0 = b * N_HEADS + 2 * p
            o_ref[rows(b), psl(p)] = ((os_[i0] + os_[i0 + 1]) * (gt * _sigmoid(gt))).astype(bf16)


def _attn(p9, rope_tab, q_g, k_g, sink, *, seq, qb):
    m = p9.shape[1]
    nblk = seq // BLOCK
    rows = qb * BLOCK

    def nbr(delta):
        def f(i):
            t = (i * qb) % nblk
            return i * qb - t + jnp.clip(t + delta, 0, nblk - 1)
        return f

    def kv(col):
        return [pl.BlockSpec((1, BLOCK, D_KV), lambda i, f=nbr(-1): (P_X, f(i), col)),
                pl.BlockSpec((1, rows, D_KV), lambda i: (P_X, i, col)),
                pl.BlockSpec((1, BLOCK, D_KV), lambda i, f=nbr(qb): (P_X, f(i), col))]

    tabs = [pl.BlockSpec((BLOCK, 2 * PAIR), lambda i, f=nbr(-1): (f(i) % nblk, 0)),
            pl.BlockSpec((rows, 2 * PAIR), lambda i: (i % (nblk // qb), 0)),
            pl.BlockSpec((BLOCK, 2 * PAIR), lambda i, f=nbr(qb): (f(i) % nblk, 0))]

    return pl.pallas_call(
        functools.partial(_attn_body, seq=seq, qb=qb),
        grid=(m // rows,),
        in_specs=[pl.BlockSpec((1, rows, D_MODEL), lambda i: (P_Q, i, 0))] + kv(1) + kv(2) + tabs + [
            pl.BlockSpec((1, rows, D_MODEL), lambda i: (P_GB, i, 0)),
            pl.BlockSpec((1, PAIR), lambda i: (0, 0)),
            pl.BlockSpec((1, PAIR), lambda i: (0, 0)),
            pl.BlockSpec((1, N_HEADS), lambda i: (0, 0)),
        ],
        out_specs=pl.BlockSpec((rows, D_MODEL), lambda i: (i, 0)),
        out_shape=jax.ShapeDtypeStruct((m, D_MODEL), bf16),
        compiler_params=pltpu.CompilerParams(dimension_semantics=("parallel",), vmem_limit_bytes=VMEM_LIMIT),
        name="attn",
    )(*([p9] * 7), rope_tab, rope_tab, rope_tab, p9, q_g, k_g, sink)


def _rope_table(seq):
    inv = 1.0 / (ROPE_THETA ** (jnp.arange(0, HEAD, 2, dtype=f32) / HEAD))
    ang = jnp.arange(seq, dtype=f32)[:, None] * inv[None, :]
    cos, sin = jnp.cos(ang), jnp.sin(ang)
    return jnp.concatenate([cos, cos, cos, cos, -sin, sin, -sin, sin], axis=1)


def _out_proj_body(x_ref, ga_ref, gb_ref, ma_ref, mb_ref, wa_ref, wb_ref, wo_ref, o_ref):
    ya = jnp.dot(ga_ref[...], wa_ref[...], preferred_element_type=f32)
    yb = jnp.dot(gb_ref[...], wb_ref[...], preferred_element_type=f32)
    mixed = _sigmoid(ma_ref[0].astype(f32)) * ya + _sigmoid(mb_ref[0].astype(f32)) * yb
    o_ref[...] = x_ref[...] + jnp.dot(mixed.astype(bf16), wo_ref[...], preferred_element_type=f32)


def _out_proj(x2d, ga, gb, p9, wa, wb, wo, *, tm):
    m = x2d.shape[0]
    row = pl.BlockSpec((tm, D_MODEL), lambda i: (i, 0))
    wspec = pl.BlockSpec((D_MODEL, D_MODEL), lambda i: (0, 0))
    return pl.pallas_call(
        _out_proj_body,
        grid=(m // tm,),
        in_specs=[row, row, row,
                  pl.BlockSpec((1, tm, D_MODEL), lambda i: (P_MA, i, 0)),
                  pl.BlockSpec((1, tm, D_MODEL), lambda i: (P_MB, i, 0)),
                  wspec, wspec, wspec],
        out_specs=row,
        out_shape=jax.ShapeDtypeStruct((m, D_MODEL), f32),
        compiler_params=pltpu.CompilerParams(dimension_semantics=("parallel",), vmem_limit_bytes=VMEM_LIMIT),
        name="out_proj",
    )(x2d, ga, gb, p9, p9, wa, wb, wo)


def _layer(x2d, lw, rope_tab, *, n_seq, seq):
    for tile in (TM_IN, TM_OUT, TT_P, TT_F, GRP_S * CHUNK, QB_ATTN * BLOCK):
        assert seq % tile == 0, (seq, tile)
    p9 = _in_proj(x2d, lw["norm_g"], lw["w_p"], tm=TM_IN)
    y0, bonus, rp, mn = _rwkv_p(p9, lw["mu_p"], lw["w0"], lw["wlu"], lw["a0"], lw["alu"], lw["k_k"], lw["k_a"],
                                lw["r_k"], tt=TT_P, seq=seq, pp=PAIRS_P)
    sf, sb = _rwkv_s(mn, n_seq=n_seq, grp=GRP_S)
    ga = _rwkv_f(y0, bonus, rp, sf, sb, p9, lw["ln_g"], lw["ln_b"], tt=TT_F)
    gb = _attn(p9, rope_tab, lw["q_g"], lw["k_g"], lw["sink"], seq=seq, qb=QB_ATTN)
    return _out_proj(x2d, ga, gb, p9, lw["wa"], lw["wb"], lw["wo"], tm=TM_OUT)


def _trunk(x, layers):
    n_seq, seq, _ = x.shape
    rope_tab = _rope_table(seq)
    x2d = x.reshape(n_seq * seq, D_MODEL)
    for lw in layers:
        x2d = _layer(x2d, lw, rope_tab, n_seq=n_seq, seq=seq)
    return x2d.reshape(n_seq, seq, D_MODEL)


def _prep_layers(norm_g, w_in, shift_mu, w0, w_lora_up, a0, a_lora_up, k_k, k_a, r_k, ln_x_g, ln_x_b,
                 q_norm_g, k_norm_g, sink, w_proj_a, w_proj_b, w_out):
    layers = []
    row = lambda v: v.reshape(1, -1).astype(f32)
    for l in range(norm_g.shape[0]):
        w_p, mu_p = _prep_layer_weights(w_in[l], shift_mu[l])
        layers.append(dict(
            norm_g=row(norm_g[l]), w_p=w_p, mu_p=mu_p,
            w0=w0[l].astype(f32), wlu=w_lora_up[l].astype(bf16), a0=a0[l].astype(f32), alu=a_lora_up[l].astype(bf16),
            k_k=row(k_k[l]), k_a=row(k_a[l]), r_k=row(r_k[l]), ln_g=row(ln_x_g[l]), ln_b=row(ln_x_b[l]),
            q_g=row(jnp.tile(q_norm_g[l], 2)), k_g=row(jnp.tile(k_norm_g[l], 2)), sink=row(sink[l]),
            wa=w_proj_a[l].astype(bf16), wb=w_proj_b[l].astype(bf16), wo=w_out[l].astype(bf16)))
    return layers


def kernel(x_prompt, x_sample, norm_g, w_in, shift_mu, w0, w_lora_up, a0, a_lora_up, k_k, k_a, r_k,
           ln_x_g, ln_x_b, q_norm_g, k_norm_g, sink, w_proj_a, w_proj_b, w_out):
    layers = _prep_layers(norm_g, w_in, shift_mu, w0, w_lora_up, a0, a_lora_up, k_k, k_a, r_k, ln_x_g, ln_x_b,
                          q_norm_g, k_norm_g, sink, w_proj_a, w_proj_b, w_out)
    return _trunk(x_prompt, layers), _trunk(x_sample, layers)
```

```python
import functools
import math

import jax
import jax.numpy as jnp
from jax import lax
from jax.experimental import pallas as pl
from jax.experimental.pallas import tpu as pltpu

D_MODEL = 1024
HEAD = 64
N_HEADS = D_MODEL // HEAD
R_LORA = 64
HKV = 4
GRP = N_HEADS // HKV
D_KV = HKV * HEAD
WINDOW = 128
BLOCK = 128
GN_EPS = 64e-5
NORM_EPS = 1e-6
ROPE_THETA = 10000.0
C_SHIFT = 3 * D_MODEL + 4 * R_LORA
N_PLANES = 9
CHUNK = 64
VMEM_LIMIT = 56 * 1024 * 1024

TM_IN = 2048
TM_OUT = 1024
TT_P = 256
TT_F = 1024
GRP_S = 16
PAIRS_P = 8
QB_ATTN = 4

P_R, P_K, P_V, P_GA, P_Q, P_GB, P_MA, P_MB, P_X = range(9)

f32 = jnp.float32
bf16 = jnp.bfloat16


def _dot(a, b):
    return jnp.dot(a.astype(bf16), b.astype(bf16), preferred_element_type=f32)


def _dot_tn(a, b):
    return lax.dot_general(a.astype(bf16), b.astype(bf16), (((0,), (0,)), ((), ())),
                           preferred_element_type=f32)


def _sigmoid(x):
    return 1.0 / (1.0 + jnp.exp(-x))


def _in_proj_body(x_ref, g_ref, w_ref, o_ref, h_ref):
    @pl.when(pl.program_id(1) == 0)
    def _():
        x = x_ref[...]
        ms = jnp.mean(x * x, axis=-1, keepdims=True)
        h_ref[...] = (x * lax.rsqrt(ms + NORM_EPS) * g_ref[...]).astype(bf16)

    o_ref[0] = jnp.dot(h_ref[...], w_ref[...], preferred_element_type=f32).astype(bf16)


def _in_proj(x2d, norm_g, w_p, *, tm):
    m = x2d.shape[0]
    return pl.pallas_call(
        _in_proj_body,
        grid=(m // tm, N_PLANES),
        in_specs=[
            pl.BlockSpec((tm, D_MODEL), lambda i, j: (i, 0)),
            pl.BlockSpec((1, D_MODEL), lambda i, j: (0, 0)),
            pl.BlockSpec((D_MODEL, D_MODEL), lambda i, j: (0, j)),
        ],
        out_specs=pl.BlockSpec((1, tm, D_MODEL), lambda i, j: (j, i, 0)),
        out_shape=jax.ShapeDtypeStruct((N_PLANES, m, D_MODEL), bf16),
        scratch_shapes=[pltpu.VMEM((tm, D_MODEL), bf16)],
        compiler_params=pltpu.CompilerParams(
            dimension_semantics=("parallel", "arbitrary"), vmem_limit_bytes=VMEM_LIMIT),
        name="in_proj",
    )(x2d, norm_g, w_p)


def _prep_layer_weights(w_in, shift_mu):
    c = C_SHIFT
    d = D_MODEL
    cols = [
        (0, d), (d, 2 * d), (2 * d, 3 * d),
        (c, c + d),
        (c + d, c + 2 * d),
        (c + 2 * d + 2 * D_KV, c + 3 * d + 2 * D_KV),
        (c + 3 * d + 2 * D_KV, c + 4 * d + 2 * D_KV),
        (c + 4 * d + 2 * D_KV, c + 5 * d + 2 * D_KV),
    ]
    planes = [w_in[:, a:b] for a, b in cols]
    small = jnp.concatenate([w_in[:, 3 * d:c], w_in[:, c + 2 * d:c + 2 * d + 2 * D_KV],
                             jnp.zeros((d, d - 4 * R_LORA - 2 * D_KV), w_in.dtype)], axis=1)
    w_p = jnp.concatenate(planes + [small], axis=1).astype(bf16)
    zeros = jnp.zeros((d,), f32)
    mu_small = jnp.concatenate([shift_mu[3 * d:c], jnp.zeros((d - 4 * R_LORA,), f32)])
    mu_p = jnp.concatenate([shift_mu[0:d], shift_mu[d:2 * d], shift_mu[2 * d:3 * d]] + [zeros] * 5 + [mu_small])
    return w_p, mu_p.reshape(1, N_PLANES * d)


PAIR = 2 * HEAD
SUBLANES = 8
PADR = SUBLANES
PRE, PRE_DONE, CHUNK_STAGE = "pre", "pre_done", "chunk"


def _token_shift(tt, seq):
    i = pl.program_id(0)
    first = (i * tt) % seq == 0
    last = ((i + 1) * tt) % seq == 0

    def shift(main, pv, nx, mu, buf):
        c = main.astype(f32)
        buf[PADR:PADR + tt, :] = c
        buf[PADR - 1:PADR, :] = jnp.where(first, 0.0, pv[pv.shape[0] - 1:, :].astype(f32))
        buf[PADR + tt:PADR + tt + 1, :] = jnp.where(last, 0.0, nx[:1, :].astype(f32))
        nbr = 0.5 * (buf[PADR - 1:PADR - 1 + tt, :] + buf[PADR + 1:PADR + 1 + tt, :])
        return c + mu * (nbr - c)

    return shift


def _rwkv_p_body(r_ref, r_pv, r_nx, k_ref, k_pv, k_nx, v_ref, v_pv, v_nx, mur_ref, muk_ref, muv_ref,
                 kk_ref, ka_ref, rk_ref,
                 y0_ref, bonus_ref, rp_ref, mn_ref,
                 at_s, rt_s, kp_s, bp_s, khm_s, bhm_s, atx_s, rtx_s, vx_s, pl_s, sh_s, *, tt, shift, zs, a_pre):
    nc = tt // CHUNK
    r = shift(r_ref[0], r_pv[0], r_nx[0], mur_ref[...], sh_s.at[0])
    k = shift(k_ref[0], k_pv[0], k_nx[0], muk_ref[...], sh_s.at[1])
    v = shift(v_ref[0], v_pv[0], v_nx[0], muv_ref[...], sh_s.at[2])
    yield PRE

    seg1, seg2 = _seg_matrix(PAIR), _seg_matrix(2 * PAIR)
    lo = lax.broadcasted_iota(jnp.int32, (tt, PAIR), 1) < HEAD
    swap = lambda t: pltpu.roll(t, HEAD, axis=1)

    def head_lo(t, h):
        return jnp.where(lo, t if h == 0 else swap(t), 0.0)

    def head_hi(t, h):
        return jnp.where(lo, 0.0, swap(t) if h == 0 else t)

    def head_own(t, h):
        return jnp.where(lo, t, 0.0) if h == 0 else jnp.where(lo, 0.0, t)

    for h in range(2):
        vx_s[h] = head_hi(v, h).astype(bf16)

    kkv = k * kk_ref[...]
    kk = kkv / jnp.maximum(jnp.sqrt(_segsum(kkv * kkv, seg1)), 1e-12)
    yield PRE

    rows = lax.broadcasted_iota(jnp.int32, (tt, tt), 0)
    cols = lax.broadcasted_iota(jnp.int32, (tt, tt), 1)
    same = (rows // CHUNK) == (cols // CHUNK)

    kmod_sum = jnp.zeros_like(k)
    for d in range(2):
        if d == 0:
            tri_bd = jnp.where(same & (cols <= rows), 1.0, 0.0)
        else:
            tri_bd = jnp.where(same & (cols >= rows), 1.0, 0.0)
        lw = -math.exp(-0.5) * _sigmoid(zs[d])
        a = _sigmoid(a_pre[d])
        yield PRE
        kmod = k * (1.0 + (a - 1.0) * ka_ref[...])
        kmod_sum = kmod_sum + kmod
        b = -(kk * a)
        lw0 = lw.astype(bf16)
        lw1 = (lw - lw0.astype(f32)).astype(bf16)
        cum2 = jnp.dot(tri_bd.astype(bf16), jnp.concatenate([lw0, lw1], axis=1), preferred_element_type=f32)
        cum = cum2[:, :PAIR] + cum2[:, PAIR:]
        edge = CHUNK - 1 if d == 0 else 0
        tot = jnp.concatenate([jnp.broadcast_to(cum[c * CHUNK + edge:c * CHUNK + edge + 1, :], (CHUNK, PAIR))
                               for c in range(nc)], axis=0)
        yield PRE
        p_inv = jnp.exp(-cum)
        p_end = jnp.exp(tot - cum)
        at = kk * jnp.exp(cum - lw)
        rt = r * jnp.exp(cum)
        kh = kmod * p_inv
        bh = b * p_inv
        p_tot = jnp.exp(tot)
        at_s[d] = at.astype(bf16)
        rt_s[d] = rt.astype(bf16)
        pl_s[d] = p_tot
        yield PRE
        kp_s[d] = (kmod * p_end).astype(bf16)
        bp_s[d] = (b * p_end).astype(bf16)
        for h in range(2):
            khm_s[d, h] = head_own(kh, h).astype(bf16)
            bhm_s[d, h] = head_own(bh, h).astype(bf16)
            atx_s[d, h] = head_lo(at, h).astype(bf16)
            rtx_s[d, h] = head_lo(rt, h)
        yield PRE

    bonus = _segsum(r * kmod_sum * rk_ref[...], seg2) * v
    yield PRE_DONE

    row = lax.broadcasted_iota(jnp.int32, (CHUNK, PAIR), 0)
    col = lax.broadcasted_iota(jnp.int32, (CHUNK, PAIR), 1)
    col_t = col % CHUNK
    lo_c = col < HEAD
    strict = (col_t < row, col_t > row)
    incl = (col_t <= row, col_t >= row)
    pairs = [(d, c) for d in range(2) for c in range(nc)]
    inst = [(d, c, h) for d, c in pairs for h in range(2)]
    cs = lambda c: slice(c * CHUNK, (c + 1) * CHUNK)
    swap_c = lambda t: pltpu.roll(t, HEAD, axis=1)

    top, bot = [], []
    for d, c in pairs:
        lhs = jnp.concatenate([at_s[d, cs(c)], rt_s[d, cs(c)]], axis=0)
        rhs = jnp.concatenate([bhm_s[d, 0, cs(c)], khm_s[d, 0, cs(c)],
                               bhm_s[d, 1, cs(c)], khm_s[d, 1, cs(c)]], axis=0)
        sc = lax.dot_general(lhs, rhs, (((1,), (1,)), ((), ())), preferred_element_type=f32)
        for h in range(2):
            top.append(jnp.where(strict[d], sc[:CHUNK, h * PAIR:(h + 1) * PAIR], 0.0))
            bot.append(jnp.where(incl[d], sc[CHUNK:, h * PAIR:(h + 1) * PAIR], 0.0))
    yield CHUNK_STAGE
    xs, aps = [], []
    for i, (d, c, h) in enumerate(inst):
        vx = vx_s[h, cs(c)]
        a_ak = jnp.where(lo_c, 0.0, top[i]).astype(bf16)
        akv = jnp.dot(a_ak, jnp.concatenate([vx, vx], axis=0), preferred_element_type=f32)
        xs.append(atx_s[d, h, cs(c)].astype(f32) + akv)
        aps.append(top[i][:, :CHUNK])
    n_dbl = CHUNK.bit_length() - 1
    for it in range(n_dbl):
        yield CHUNK_STAGE
        last = it + 1 == n_dbl
        res = [_dot(ap, x if last else jnp.concatenate([x, ap], axis=1)) for ap, x in zip(aps, xs)]
        xs = [x + rs[:, :PAIR] for x, rs in zip(xs, res)]
        if not last:
            aps = [rs[:, PAIR:] for rs in res]
    yield CHUNK_STAGE
    rhs_o = [jnp.concatenate([xs[i].astype(bf16), vx_s[h, cs(c)]], axis=0) for i, (d, c, h) in enumerate(inst)]
    rys = [_dot(bot[i], rhs_o[i]) + rtx_s[d, h, cs(c)] for i, (d, c, h) in enumerate(inst)]
    yield CHUNK_STAGE
    mns = [_dot_tn(jnp.concatenate([bp_s[d, cs(c), h * HEAD:(h + 1) * HEAD],
                                    kp_s[d, cs(c), h * HEAD:(h + 1) * HEAD]], axis=0), rhs_o[i])
           for i, (d, c, h) in enumerate(inst)]
    yield CHUNK_STAGE
    for i, (d, c, h) in enumerate(inst):
        p_blk = pl_s[d, c * CHUNK:c * CHUNK + SUBLANES, :]
        p_row = (p_blk if h == 0 else swap_c(p_blk))[0:1]
        mn_ref[d, c, h] = (mns[i] + jnp.where(row == col, p_row, 0.0)).astype(bf16)
    y0_fwd = {}
    for j, (d, c) in enumerate(pairs):
        ry0, ry1 = rys[2 * j], rys[2 * j + 1]
        rp_ref[d, cs(c), :] = jnp.where(lo_c, ry0, swap_c(ry1)).astype(bf16)
        y0 = jnp.where(lo_c, swap_c(ry0), ry1)
        if d == 0:
            y0_fwd[c] = y0
        else:
            y0_ref[cs(c), :] = (y0_fwd[c] + y0).astype(y0_ref.dtype)
    bonus_ref[...] = bonus.astype(bonus_ref.dtype)


N_IN_P = 23
N_OUT_P = 4


def _rwkv_p_multi(*refs, tt, seq, pp):
    ins, outs = refs[:N_IN_P], refs[N_IN_P:N_IN_P + N_OUT_P]
    scr, shx_s = refs[N_IN_P + N_OUT_P:-1], refs[-1]
    x_ref, x_pv, x_nx, mux_ref = ins[9], ins[10], ins[11], ins[15]
    w0_ref, wlu_ref, a0_ref, alu_ref = ins[16:20]
    shift = _token_shift(tt, seq)
    xs = shift(x_ref[0], x_pv[0], x_nx[0], mux_ref[...], shx_s)
    zs = [w0_ref[d:d + 1, :] + _dot(jnp.tanh(xs[:, d * R_LORA:(d + 1) * R_LORA]), wlu_ref[d]) for d in range(2)]
    a_pre = [a0_ref[d:d + 1, :] + _dot(xs[:, (2 + d) * R_LORA:(3 + d) * R_LORA], alu_ref[d]) for d in range(2)]
    stages = []
    for q in range(pp):
        lanes = pl.ds(q * PAIR, PAIR)
        lsl = slice(q * PAIR, (q + 1) * PAIR)
        view = lambda ref, lanes=lanes: ref.at[..., lanes]
        planes = [view(ref) for ref in ins[0:9]]
        gains = [view(ref) for ref in ins[12:15]]
        params = [view(ref) for ref in ins[20:23]]
        y0_ref, bonus_ref, rp_ref, mn_ref = outs
        stages.append(_rwkv_p_body(*planes, *gains, *params,
                                   view(y0_ref), view(bonus_ref), view(rp_ref), mn_ref.at[:, :, pl.ds(2 * q, 2)],
                                   *[s.at[q] for s in scr], tt=tt, shift=shift,
                                   zs=[z[:, lsl] for z in zs], a_pre=[a[:, lsl] for a in a_pre]))
    def run_pre(stage):
        while next(stage) != PRE_DONE:
            pass

    run_pre(stages[0])
    for q in range(pp):
        chunk, pre = stages[q], (stages[q + 1] if q + 1 < pp else None)
        chunk_live, pre_live = True, pre is not None
        while chunk_live or pre_live:
            if chunk_live:
                chunk_live = next(chunk, None) is not None
            if pre_live:
                pre_live = next(pre) != PRE_DONE


def _seg_matrix(n):
    r = lax.broadcasted_iota(jnp.int32, (n, n), 0) // HEAD
    c = lax.broadcasted_iota(jnp.int32, (n, n), 1) // HEAD
    return jnp.where(r == c, 1.0, 0.0).astype(bf16)


def _segsum(x, seg):
    x0 = x.astype(bf16)
    if seg.shape[0] == PAIR:
        return jnp.dot(x0, seg, preferred_element_type=f32)
    x1 = (x - x0.astype(f32)).astype(bf16)
    s = jnp.dot(jnp.concatenate([x0, x1], axis=1), seg, preferred_element_type=f32)
    return s[:, :PAIR] + s[:, PAIR:]


def _rwkv_p(p9, mu_p, w0, wlu, a0, alu, k_k, k_a, r_k, *, tt, seq, pp):
    m = p9.shape[1]
    nt = m // tt
    npair = D_MODEL // PAIR
    cpt = tt // CHUNK
    wl = pp * PAIR
    hb = 2 * SUBLANES

    def tiles(plane):
        return [pl.BlockSpec((1, tt, wl), lambda i, j: (plane, i, j)),
                pl.BlockSpec((1, hb, wl), lambda i, j: (plane, jnp.maximum(i * (tt // hb) - 1, 0), j)),
                pl.BlockSpec((1, hb, wl), lambda i, j: (plane, jnp.minimum((i + 1) * (tt // hb), m // hb - 1), j))]

    vec = pl.BlockSpec((1, wl), lambda i, j: (0, j))
    mu = lambda plane: pl.BlockSpec((1, wl), lambda i, j: (0, plane * (npair // pp) + j))
    wx = 4 * R_LORA
    return pl.pallas_call(
        functools.partial(_rwkv_p_multi, tt=tt, seq=seq, pp=pp),
        grid=(nt, npair // pp),
        in_specs=tiles(P_R) + tiles(P_K) + tiles(P_V) + [
            pl.BlockSpec((1, tt, wx), lambda i, j: (P_X, i, 0)),
            pl.BlockSpec((1, hb, wx), lambda i, j: (P_X, jnp.maximum(i * (tt // hb) - 1, 0), 0)),
            pl.BlockSpec((1, hb, wx), lambda i, j: (P_X, jnp.minimum((i + 1) * (tt // hb), m // hb - 1), 0)),
            mu(P_R), mu(P_K), mu(P_V),
            pl.BlockSpec((1, wx), lambda i, j: (0, P_X * D_MODEL // wx)),
            pl.BlockSpec((2, wl), lambda i, j: (0, j)),
            pl.BlockSpec((2, R_LORA, wl), lambda i, j: (0, 0, j)),
            pl.BlockSpec((2, wl), lambda i, j: (0, j)),
            pl.BlockSpec((2, R_LORA, wl), lambda i, j: (0, 0, j)),
            vec, vec, vec,
        ],
        out_specs=[
            pl.BlockSpec((tt, wl), lambda i, j: (i, j)),
            pl.BlockSpec((tt, wl), lambda i, j: (i, j)),
            pl.BlockSpec((2, tt, wl), lambda i, j: (0, i, j)),
            pl.BlockSpec((2, cpt, 2 * pp, HEAD, PAIR), lambda i, j: (0, i, j, 0, 0)),
        ],
        out_shape=[
            jax.ShapeDtypeStruct((m, D_MODEL), bf16),
            jax.ShapeDtypeStruct((m, D_MODEL), bf16),
            jax.ShapeDtypeStruct((2, m, D_MODEL), bf16),
            jax.ShapeDtypeStruct((2, m // CHUNK, N_HEADS, HEAD, PAIR), bf16),
        ],
        scratch_shapes=(
            [pltpu.VMEM((pp, 2, tt, PAIR), bf16) for _ in range(4)]
            + [pltpu.VMEM((pp, 2, 2, tt, PAIR), bf16) for _ in range(3)]
            + [pltpu.VMEM((pp, 2, 2, tt, PAIR), f32),
               pltpu.VMEM((pp, 2, tt, PAIR), bf16),
               pltpu.VMEM((pp, 2, tt, PAIR), f32),
               pltpu.VMEM((pp, 3, tt + 2 * PADR, PAIR), f32),
               pltpu.VMEM((tt + 2 * PADR, wx), f32)]),
        compiler_params=pltpu.CompilerParams(
            dimension_semantics=("parallel", "parallel"), vmem_limit_bytes=VMEM_LIMIT),
        name="rwkv_p",
    )(*([p9] * 12), mu_p, mu_p, mu_p, mu_p, w0, wlu, a0, alu, k_k, k_a, r_k)


def _rwkv_s_body(mnf_ref, mnb_ref, sf_ref, sb_ref, st_ref, *, grp):
    @pl.when(pl.program_id(1) == 0)
    def _():
        st_ref[...] = jnp.zeros_like(st_ref)

    zero = jnp.zeros((HEAD, HEAD), bf16)

    def step(d, mn_ref, s_out_ref, cc):
        s = st_ref[d].astype(bf16)
        for p in range(N_HEADS // 2):
            s_out_ref[cc, p] = jnp.concatenate([jnp.concatenate([s[2 * p], zero], axis=1),
                                                jnp.concatenate([zero, s[2 * p + 1]], axis=1)], axis=0)
        m_t = mn_ref[0, cc, :, :, 0:HEAD]
        n_t = mn_ref[0, cc, :, :, HEAD:].astype(f32)
        st_ref[d] = jnp.einsum('hij,hjk->hik', m_t, s, preferred_element_type=f32) + n_t

    for cc in range(grp):
        step(0, mnf_ref, sf_ref, cc)
        step(1, mnb_ref, sb_ref, grp - 1 - cc)


def _rwkv_s(mn, *, n_seq, grp):
    nchunk = mn.shape[1]
    npair = N_HEADS // 2
    ng = nchunk // n_seq // grp
    blk_in = (1, grp, N_HEADS, HEAD, PAIR)
    blk_out = (grp, npair, PAIR, PAIR)
    out = jax.ShapeDtypeStruct((nchunk, npair, PAIR, PAIR), bf16)
    return pl.pallas_call(
        functools.partial(_rwkv_s_body, grp=grp),
        grid=(n_seq, ng),
        in_specs=[pl.BlockSpec(blk_in, lambda b, g: (0, b * ng + g, 0, 0, 0)),
                  pl.BlockSpec(blk_in, lambda b, g: (1, b * ng + ng - 1 - g, 0, 0, 0))],
        out_specs=[pl.BlockSpec(blk_out, lambda b, g: (b * ng + g, 0, 0, 0)),
                   pl.BlockSpec(blk_out, lambda b, g: (b * ng + ng - 1 - g, 0, 0, 0))],
        out_shape=[out, out],
        scratch_shapes=[pltpu.VMEM((2, N_HEADS, HEAD, HEAD), f32)],
        compiler_params=pltpu.CompilerParams(
            dimension_semantics=("parallel", "arbitrary"), vmem_limit_bytes=VMEM_LIMIT),
        name="rwkv_s",
    )(mn, mn)


def _rwkv_f_body(y0_ref, bonus_ref, rp_ref, sf_ref, sb_ref, gate_ref, lng_ref, lnb_ref, o_ref, *, tt):
    npair = D_MODEL // PAIR
    nc = tt // CHUNK
    cs = lambda c: slice(c * CHUNK, (c + 1) * CHUNK)
    ps = lambda p: slice(p * PAIR, (p + 1) * PAIR)
    yf = [[jnp.dot(rp_ref[0, cs(c), ps(p)], sf_ref[c, p], preferred_element_type=f32) for c in range(nc)]
          for p in range(npair)]
    yb = [[jnp.dot(rp_ref[1, cs(c), ps(p)], sb_ref[c, p], preferred_element_type=f32) for c in range(nc)]
          for p in range(npair)]
    seg1, seg2 = _seg_matrix(PAIR), _seg_matrix(2 * PAIR)
    ys = [y0_ref[:, ps(p)].astype(f32) + jnp.concatenate(yf[p], axis=0) + jnp.concatenate(yb[p], axis=0)
          for p in range(npair)]
    mus = [_segsum(y, seg2) * (1.0 / HEAD) for y in ys]
    dvs = [y - mu for y, mu in zip(ys, mus)]
    vrs = [_segsum(dv * dv, seg1) * (1.0 / HEAD) for dv in dvs]
    for p in range(npair):
        yn = dvs[p] * lax.rsqrt(vrs[p] + GN_EPS) * lng_ref[:, ps(p)] + lnb_ref[:, ps(p)] + bonus_ref[:, ps(p)].astype(f32)
        g = gate_ref[0, :, ps(p)].astype(f32)
        o_ref[:, ps(p)] = (yn * (g * _sigmoid(g))).astype(bf16)


def _rwkv_f(y0, bonus, rp, sf, sb, p9, ln_g, ln_b, *, tt):
    m = y0.shape[0]
    cpt = tt // CHUNK
    npair = D_MODEL // PAIR
    tile = pl.BlockSpec((tt, D_MODEL), lambda i: (i, 0))
    vec = pl.BlockSpec((1, D_MODEL), lambda i: (0, 0))
    state = pl.BlockSpec((cpt, npair, PAIR, PAIR), lambda i: (i, 0, 0, 0))
    return pl.pallas_call(
        functools.partial(_rwkv_f_body, tt=tt),
        grid=(m // tt,),
        in_specs=[
            tile, tile,
            pl.BlockSpec((2, tt, D_MODEL), lambda i: (0, i, 0)),
            state, state,
            pl.BlockSpec((1, tt, D_MODEL), lambda i: (P_GA, i, 0)),
            vec, vec,
        ],
        out_specs=tile,
        out_shape=jax.ShapeDtypeStruct((m, D_MODEL), bf16),
        compiler_params=pltpu.CompilerParams(dimension_semantics=("parallel",), vmem_limit_bytes=VMEM_LIMIT),
        name="rwkv_f",
    )(y0, bonus, rp, sf, sb, p9, ln_g, ln_b)


def _attn_body(q_ref, kp_ref, kc_ref, kn_ref, vp_ref, vc_ref, vn_ref, tp_ref, tc_ref, tn_ref, gate_ref,
               qg_ref, kg_ref, sink_ref, o_ref, *, seq, qb):
    step_rows = qb * BLOCK
    start = (pl.program_id(0) % (seq // step_rows)) * step_rows
    ctx = BLOCK + 2 * WINDOW
    qoff = lax.broadcasted_iota(jnp.int32, (BLOCK, ctx), 0)
    koff = lax.broadcasted_iota(jnp.int32, (BLOCK, ctx), 1) - WINDOW
    valid = []
    for b in range(qb):
        kpos = start + b * BLOCK + koff
        valid.append((jnp.abs(qoff - koff) <= WINDOW) & (kpos >= 0) & (kpos < seq))

    seg = _seg_matrix(PAIR)
    swap = lambda t: pltpu.roll(t, HEAD, axis=1)

    def norm_rope(x, gain, tab):
        lane = lax.broadcasted_iota(jnp.int32, x.shape, 1)
        xn = x * lax.rsqrt(_segsum(x * x, seg) * (1.0 / HEAD) + NORM_EPS) * gain
        rot = jnp.where(lane % HEAD < HEAD // 2, pltpu.roll(xn, PAIR - HEAD // 2, axis=1),
                        pltpu.roll(xn, HEAD // 2, axis=1))
        return xn * tab[:, :PAIR] + rot * tab[:, PAIR:]

    def place(t, e, want_lo):
        lo = lax.broadcasted_iota(jnp.int32, t.shape, 1) < HEAD
        src = t if (e == 0) == want_lo else swap(t)
        return (jnp.where(lo, src, 0.0) if want_lo else jnp.where(lo, 0.0, src)).astype(bf16)

    tab_k = jnp.concatenate([tp_ref[...], tc_ref[...], tn_ref[...]], axis=0)
    tab_q = tc_ref[...]
    k_lo, k_hi, v_lo, v_hi = [], [], [], []
    for kp in range(D_KV // PAIR):
        ps_ = slice(kp * PAIR, (kp + 1) * PAIR)
        kx = jnp.concatenate([kp_ref[0, :, ps_], kc_ref[0, :, ps_], kn_ref[0, :, ps_]], axis=0)
        vx = jnp.concatenate([vp_ref[0, :, ps_], vc_ref[0, :, ps_], vn_ref[0, :, ps_]], axis=0).astype(f32)
        kr = norm_rope(kx.astype(f32), kg_ref[...], tab_k)
        for e in range(2):
            k_lo.append(place(kr, e, True))
            k_hi.append(place(kr, e, False))
            v_lo.append(place(vx, e, True))
            v_hi.append(place(vx, e, False))
    npair = D_MODEL // PAIR
    psl = lambda p: slice(p * PAIR, (p + 1) * PAIR)
    log2e = math.log2(math.e)
    qrs = [(norm_rope(q_ref[0, :, psl(p)].astype(f32), qg_ref[...], tab_q) * (HEAD ** -0.5 * log2e)).astype(bf16)
           for p in range(npair)]
    inst = [(b, hq) for b in range(qb) for hq in range(N_HEADS)]
    rows = lambda b: slice(b * BLOCK, (b + 1) * BLOCK)
    keys = lambda b: slice(b * BLOCK, b * BLOCK + ctx)
    def masked(s, ok):
        return jnp.concatenate([jnp.where(ok[:, :WINDOW], s[:, :WINDOW], -1e30), s[:, WINDOW:WINDOW + BLOCK],
                                jnp.where(ok[:, WINDOW + BLOCK:], s[:, WINDOW + BLOCK:], -1e30)], axis=1)

    ss = [masked(lax.dot_general(qrs[hq // 2][rows(b)], (k_lo if hq % 2 == 0 else k_hi)[hq // GRP][keys(b)],
                                 (((1,), (1,)), ((), ())), preferred_element_type=f32), valid[b])
          for b, hq in inst]
    sinks = [sink_ref[0:1, hq:hq + 1] * log2e for b, hq in inst]
    mxs = [jnp.maximum(jnp.max(s, axis=-1, keepdims=True), sk) for s, sk in zip(ss, sinks)]
    ps = [jnp.exp2(s - mx) for s, mx in zip(ss, mxs)]
    dens = [jnp.sum(p, axis=-1, keepdims=True) + jnp.exp2(sk - mx) for p, sk, mx in zip(ps, sinks, mxs)]
    os_ = [jnp.dot(p.astype(bf16), (v_lo if hq % 2 == 0 else v_hi)[hq // GRP][keys(b)], preferred_element_type=f32) / den
           for (b, hq), p, den in zip(inst, ps, dens)]
    for b in range(qb):
        for p in range(npair):
            gt = gate_ref[0, rows(b), psl(p)].astype(f32)
            i0 = b * N_HEADS + 2 * p
            o_ref[rows(b), psl(p)] = ((os_[i0] + os_[i0 + 1]) * (gt * _sigmoid(gt))).astype(bf16)


def _attn(p9, rope_tab, q_g, k_g, sink, *, seq, qb):
    m = p9.shape[1]
    nblk = seq // BLOCK
    rows = qb * BLOCK

    def nbr(delta):
        def f(i):
            t = (i * qb) % nblk
            return i * qb - t + jnp.clip(t + delta, 0, nblk - 1)
        return f

    def kv(col):
        return [pl.BlockSpec((1, BLOCK, D_KV), lambda i, f=nbr(-1): (P_X, f(i), col)),
                pl.BlockSpec((1, rows, D_KV), lambda i: (P_X, i, col)),
                pl.BlockSpec((1, BLOCK, D_KV), lambda i, f=nbr(qb): (P_X, f(i), col))]

    tabs = [pl.BlockSpec((BLOCK, 2 * PAIR), lambda i, f=nbr(-1): (f(i) % nblk, 0)),
            pl.BlockSpec((rows, 2 * PAIR), lambda i: (i % (nblk // qb), 0)),
            pl.BlockSpec((BLOCK, 2 * PAIR), lambda i, f=nbr(qb): (f(i) % nblk, 0))]

    return pl.pallas_call(
        functools.partial(_attn_body, seq=seq, qb=qb),
        grid=(m // rows,),
        in_specs=[pl.BlockSpec((1, rows, D_MODEL), lambda i: (P_Q, i, 0))] + kv(1) + kv(2) + tabs + [
            pl.BlockSpec((1, rows, D_MODEL), lambda i: (P_GB, i, 0)),
            pl.BlockSpec((1, PAIR), lambda i: (0, 0)),
            pl.BlockSpec((1, PAIR), lambda i: (0, 0)),
            pl.BlockSpec((1, N_HEADS), lambda i: (0, 0)),
        ],
        out_specs=pl.BlockSpec((rows, D_MODEL), lambda i: (i, 0)),
        out_shape=jax.ShapeDtypeStruct((m, D_MODEL), bf16),
        compiler_params=pltpu.CompilerParams(dimension_semantics=("parallel",), vmem_limit_bytes=VMEM_LIMIT),
        name="attn",
    )(*([p9] * 7), rope_tab, rope_tab, rope_tab, p9, q_g, k_g, sink)


def _rope_table(seq):
    inv = 1.0 / (ROPE_THETA ** (jnp.arange(0, HEAD, 2, dtype=f32) / HEAD))
    ang = jnp.arange(seq, dtype=f32)[:, None] * inv[None, :]
    cos, sin = jnp.cos(ang), jnp.sin(ang)
    return jnp.concatenate([cos, cos, cos, cos, -sin, sin, -sin, sin], axis=1)


def _out_proj_body(x_ref, ga_ref, gb_ref, ma_ref, mb_ref, wa_ref, wb_ref, wo_ref, o_ref):
    ya = jnp.dot(ga_ref[...], wa_ref[...], preferred_element_type=f32)
    yb = jnp.dot(gb_ref[...], wb_ref[...], preferred_element_type=f32)
    mixed = _sigmoid(ma_ref[0].astype(f32)) * ya + _sigmoid(mb_ref[0].astype(f32)) * yb
    o_ref[...] = x_ref[...] + jnp.dot(mixed.astype(bf16), wo_ref[...], preferred_element_type=f32)


def _out_proj(x2d, ga, gb, p9, wa, wb, wo, *, tm):
    m = x2d.shape[0]
    row = pl.BlockSpec((tm, D_MODEL), lambda i: (i, 0))
    wspec = pl.BlockSpec((D_MODEL, D_MODEL), lambda i: (0, 0))
    return pl.pallas_call(
        _out_proj_body,
        grid=(m // tm,),
        in_specs=[row, row, row,
                  pl.BlockSpec((1, tm, D_MODEL), lambda i: (P_MA, i, 0)),
                  pl.BlockSpec((1, tm, D_MODEL), lambda i: (P_MB, i, 0)),
                  wspec, wspec, wspec],
        out_specs=row,
        out_shape=jax.ShapeDtypeStruct((m, D_MODEL), f32),
        compiler_params=pltpu.CompilerParams(dimension_semantics=("parallel",), vmem_limit_bytes=VMEM_LIMIT),
        name="out_proj",
    )(x2d, ga, gb, p9, p9, wa, wb, wo)


def _layer(x2d, lw, rope_tab, *, n_seq, seq):
    for tile in (TM_IN, TM_OUT, TT_P, TT_F, GRP_S * CHUNK, QB_ATTN * BLOCK):
        assert seq % tile == 0, (seq, tile)
    p9 = _in_proj(x2d, lw["norm_g"], lw["w_p"], tm=TM_IN)
    y0, bonus, rp, mn = _rwkv_p(p9, lw["mu_p"], lw["w0"], lw["wlu"], lw["a0"], lw["alu"], lw["k_k"], lw["k_a"],
                                lw["r_k"], tt=TT_P, seq=seq, pp=PAIRS_P)
    sf, sb = _rwkv_s(mn, n_seq=n_seq, grp=GRP_S)
    ga = _rwkv_f(y0, bonus, rp, sf, sb, p9, lw["ln_g"], lw["ln_b"], tt=TT_F)
    gb = _attn(p9, rope_tab, lw["q_g"], lw["k_g"], lw["sink"], seq=seq, qb=QB_ATTN)
    return _out_proj(x2d, ga, gb, p9, lw["wa"], lw["wb"], lw["wo"], tm=TM_OUT)


def _trunk(x, layers):
    n_seq, seq, _ = x.shape
    rope_tab = _rope_table(seq)
    x2d = x.reshape(n_seq * seq, D_MODEL)
    for lw in layers:
        x2d = _layer(x2d, lw, rope_tab, n_seq=n_seq, seq=seq)
    return x2d.reshape(n_seq, seq, D_MODEL)


def _prep_layers(norm_g, w_in, shift_mu, w0, w_lora_up, a0, a_lora_up, k_k, k_a, r_k, ln_x_g, ln_x_b,
                 q_norm_g, k_norm_g, sink, w_proj_a, w_proj_b, w_out):
    layers = []
    row = lambda v: v.reshape(1, -1).astype(f32)
    for l in range(norm_g.shape[0]):
        w_p, mu_p = _prep_layer_weights(w_in[l], shift_mu[l])
        layers.append(dict(
            norm_g=row(norm_g[l]), w_p=w_p, mu_p=mu_p,
            w0=w0[l].astype(f32), wlu=w_lora_up[l].astype(bf16), a0=a0[l].astype(f32), alu=a_lora_up[l].astype(bf16),
            k_k=row(k_k[l]), k_a=row(k_a[l]), r_k=row(r_k[l]), ln_g=row(ln_x_g[l]), ln_b=row(ln_x_b[l]),
            q_g=row(jnp.tile(q_norm_g[l], 2)), k_g=row(jnp.tile(k_norm_g[l], 2)), sink=row(sink[l]),
            wa=w_proj_a[l].astype(bf16), wb=w_proj_b[l].astype(bf16), wo=w_out[l].astype(bf16)))
    return layers


def kernel(x_prompt, x_sample, norm_g, w_in, shift_mu, w0, w_lora_up, a0, a_lora_up, k_k, k_a, r_k,
           ln_x_g, ln_x_b, q_norm_g, k_norm_g, sink, w_proj_a, w_proj_b, w_out):
    layers = _prep_layers(norm_g, w_in, shift_mu, w0, w_lora_up, a0, a_lora_up, k_k, k_a, r_k, ln_x_g, ln_x_b,
                          q_norm_g, k_norm_g, sink, w_proj_a, w_proj_b, w_out)
    return _trunk(x_prompt, layers), _trunk(x_sample, layers)
```

```python
import functools
import math

import jax
import jax.numpy as jnp
from jax import lax
from jax.experimental import pallas as pl
from jax.experimental.pallas import tpu as pltpu

D_MODEL = 1024
HEAD = 64
N_HEADS = D_MODEL // HEAD
R_LORA = 64
HKV = 4
GRP = N_HEADS // HKV
D_KV = HKV * HEAD
WINDOW = 128
BLOCK = 128
GN_EPS = 64e-5
NORM_EPS = 1e-6
ROPE_THETA = 10000.0
C_SHIFT = 3 * D_MODEL + 4 * R_LORA
N_PLANES = 9
CHUNK = 64
VMEM_LIMIT = 58 * 1024 * 1024

TM_IN = 2048
TM_OUT = 1024
TT_P = 256
TT_F = 1024
GRP_S = 16
PAIRS_P = 8
QB_ATTN = 8

P_R, P_K, P_V, P_GA, P_Q, P_GB, P_MA, P_MB, P_X = range(9)

f32 = jnp.float32
bf16 = jnp.bfloat16


def _dot(a, b):
    return jnp.dot(a.astype(bf16), b.astype(bf16), preferred_element_type=f32)


def _dot_tn(a, b):
    return lax.dot_general(a.astype(bf16), b.astype(bf16), (((0,), (0,)), ((), ())),
                           preferred_element_type=f32)


def _sigmoid(x):
    return 1.0 / (1.0 + jnp.exp(-x))


def _in_proj_body(x_ref, g_ref, w_ref, o_ref, h_ref):
    @pl.when(pl.program_id(1) == 0)
    def _():
        x = x_ref[...]
        ms = jnp.mean(x * x, axis=-1, keepdims=True)
        h_ref[...] = (x * lax.rsqrt(ms + NORM_EPS) * g_ref[...]).astype(bf16)

    o_ref[0] = jnp.dot(h_ref[...], w_ref[...], preferred_element_type=f32).astype(bf16)


def _in_proj(x2d, norm_g, w_p, *, tm):
    m = x2d.shape[0]
    return pl.pallas_call(
        _in_proj_body,
        grid=(m // tm, N_PLANES),
        in_specs=[
            pl.BlockSpec((tm, D_MODEL), lambda i, j: (i, 0)),
            pl.BlockSpec((1, D_MODEL), lambda i, j: (0, 0)),
            pl.BlockSpec((D_MODEL, D_MODEL), lambda i, j: (0, j)),
        ],
        out_specs=pl.BlockSpec((1, tm, D_MODEL), lambda i, j: (j, i, 0)),
        out_shape=jax.ShapeDtypeStruct((N_PLANES, m, D_MODEL), bf16),
        scratch_shapes=[pltpu.VMEM((tm, D_MODEL), bf16)],
        compiler_params=pltpu.CompilerParams(
            dimension_semantics=("parallel", "arbitrary"), vmem_limit_bytes=VMEM_LIMIT),
        name="in_proj",
    )(x2d, norm_g, w_p)


def _prep_layer_weights(w_in, shift_mu):
    c = C_SHIFT
    d = D_MODEL
    cols = [
        (0, d), (d, 2 * d), (2 * d, 3 * d),
        (c, c + d),
        (c + d, c + 2 * d),
        (c + 2 * d + 2 * D_KV, c + 3 * d + 2 * D_KV),
        (c + 3 * d + 2 * D_KV, c + 4 * d + 2 * D_KV),
        (c + 4 * d + 2 * D_KV, c + 5 * d + 2 * D_KV),
    ]
    planes = [w_in[:, a:b] for a, b in cols]
    small = jnp.concatenate([w_in[:, 3 * d:c], w_in[:, c + 2 * d:c + 2 * d + 2 * D_KV],
                             jnp.zeros((d, d - 4 * R_LORA - 2 * D_KV), w_in.dtype)], axis=1)
    w_p = jnp.concatenate(planes + [small], axis=1).astype(bf16)
    zeros = jnp.zeros((d,), f32)
    mu_small = jnp.concatenate([shift_mu[3 * d:c], jnp.zeros((d - 4 * R_LORA,), f32)])
    mu_p = jnp.concatenate([shift_mu[0:d], shift_mu[d:2 * d], shift_mu[2 * d:3 * d]] + [zeros] * 5 + [mu_small])
    return w_p, mu_p.reshape(1, N_PLANES * d)


PAIR = 2 * HEAD
SUBLANES = 8
PADR = SUBLANES
PRE, PRE_DONE, CHUNK_STAGE = "pre", "pre_done", "chunk"


def _token_shift(tt, seq):
    i = pl.program_id(0)
    first = (i * tt) % seq == 0
    last = ((i + 1) * tt) % seq == 0

    def shift(main, pv, nx, mu, buf):
        c = main.astype(f32)
        buf[PADR:PADR + tt, :] = c
        buf[PADR - 1:PADR, :] = jnp.where(first, 0.0, pv[pv.shape[0] - 1:, :].astype(f32))
        buf[PADR + tt:PADR + tt + 1, :] = jnp.where(last, 0.0, nx[:1, :].astype(f32))
        nbr = 0.5 * (buf[PADR - 1:PADR - 1 + tt, :] + buf[PADR + 1:PADR + 1 + tt, :])
        return c + mu * (nbr - c)

    return shift


def _rwkv_p_body(r_ref, r_pv, r_nx, k_ref, k_pv, k_nx, v_ref, v_pv, v_nx, mur_ref, muk_ref, muv_ref,
                 kk_ref, ka_ref, rk_ref,
                 y0_ref, bonus_ref, rp_ref, mn_ref,
                 at_s, rt_s, kp_s, bp_s, khm_s, bhm_s, atx_s, rtx_s, vx_s, pl_s, sh_s, *, tt, shift, zs, a_pre):
    nc = tt // CHUNK
    r = shift(r_ref[0], r_pv[0], r_nx[0], mur_ref[...], sh_s.at[0])
    k = shift(k_ref[0], k_pv[0], k_nx[0], muk_ref[...], sh_s.at[1])
    v = shift(v_ref[0], v_pv[0], v_nx[0], muv_ref[...], sh_s.at[2])
    yield PRE

    seg1, seg2 = _seg_matrix(PAIR), _seg_matrix(2 * PAIR)
    lo = lax.broadcasted_iota(jnp.int32, (tt, PAIR), 1) < HEAD
    swap = lambda t: pltpu.roll(t, HEAD, axis=1)

    def head_lo(t, h):
        return jnp.where(lo, t if h == 0 else swap(t), 0.0)

    def head_hi(t, h):
        return jnp.where(lo, 0.0, swap(t) if h == 0 else t)

    def head_own(t, h):
        return jnp.where(lo, t, 0.0) if h == 0 else jnp.where(lo, 0.0, t)

    for h in range(2):
        vx_s[h] = head_hi(v, h).astype(bf16)

    kkv = k * kk_ref[...]
    kk = kkv / jnp.maximum(jnp.sqrt(_segsum(kkv * kkv, seg1)), 1e-12)
    yield PRE

    rows = lax.broadcasted_iota(jnp.int32, (tt, tt), 0)
    cols = lax.broadcasted_iota(jnp.int32, (tt, tt), 1)
    same = (rows // CHUNK) == (cols // CHUNK)

    kmod_sum = jnp.zeros_like(k)
    for d in range(2):
        if d == 0:
            tri_bd = jnp.where(same & (cols <= rows), 1.0, 0.0)
        else:
            tri_bd = jnp.where(same & (cols >= rows), 1.0, 0.0)
        lw = -math.exp(-0.5) * _sigmoid(zs[d])
        a = _sigmoid(a_pre[d])
        yield PRE
        kmod = k * (1.0 + (a - 1.0) * ka_ref[...])
        kmod_sum = kmod_sum + kmod
        b = -(kk * a)
        lw0 = lw.astype(bf16)
        lw1 = (lw - lw0.astype(f32)).astype(bf16)
        cum2 = jnp.dot(tri_bd.astype(bf16), jnp.concatenate([lw0, lw1], axis=1), preferred_element_type=f32)
        cum = cum2[:, :PAIR] + cum2[:, PAIR:]
        edge = CHUNK - 1 if d == 0 else 0
        tot = jnp.concatenate([jnp.broadcast_to(cum[c * CHUNK + edge:c * CHUNK + edge + 1, :], (CHUNK, PAIR))
                               for c in range(nc)], axis=0)
        yield PRE
        p_inv = jnp.exp(-cum)
        p_end = jnp.exp(tot - cum)
        at = kk * jnp.exp(cum - lw)
        rt = r * jnp.exp(cum)
        kh = kmod * p_inv
        bh = b * p_inv
        p_tot = jnp.exp(tot)
        at_s[d] = at.astype(bf16)
        rt_s[d] = rt.astype(bf16)
        pl_s[d] = p_tot
        yield PRE
        kp_s[d] = (kmod * p_end).astype(bf16)
        bp_s[d] = (b * p_end).astype(bf16)
        for h in range(2):
            khm_s[d, h] = head_own(kh, h).astype(bf16)
            bhm_s[d, h] = head_own(bh, h).astype(bf16)
            atx_s[d, h] = head_lo(at, h).astype(bf16)
            rtx_s[d, h] = head_lo(rt, h)
        yield PRE

    bonus = _segsum(r * kmod_sum * rk_ref[...], seg2) * v
    yield PRE_DONE

    row = lax.broadcasted_iota(jnp.int32, (CHUNK, PAIR), 0)
    col = lax.broadcasted_iota(jnp.int32, (CHUNK, PAIR), 1)
    col_t = col % CHUNK
    lo_c = col < HEAD
    strict = (col_t < row, col_t > row)
    incl = (col_t <= row, col_t >= row)
    pairs = [(d, c) for d in range(2) for c in range(nc)]
    inst = [(d, c, h) for d, c in pairs for h in range(2)]
    cs = lambda c: slice(c * CHUNK, (c + 1) * CHUNK)
    swap_c = lambda t: pltpu.roll(t, HEAD, axis=1)

    top, bot = [], []
    for d, c in pairs:
        lhs = jnp.concatenate([at_s[d, cs(c)], rt_s[d, cs(c)]], axis=0)
        rhs = jnp.concatenate([bhm_s[d, 0, cs(c)], khm_s[d, 0, cs(c)],
                               bhm_s[d, 1, cs(c)], khm_s[d, 1, cs(c)]], axis=0)
        sc = lax.dot_general(lhs, rhs, (((1,), (1,)), ((), ())), preferred_element_type=f32)
        for h in range(2):
            top.append(jnp.where(strict[d], sc[:CHUNK, h * PAIR:(h + 1) * PAIR], 0.0))
            bot.append(jnp.where(incl[d], sc[CHUNK:, h * PAIR:(h + 1) * PAIR], 0.0))
    yield CHUNK_STAGE
    xs, aps = [], []
    for i, (d, c, h) in enumerate(inst):
        vx = vx_s[h, cs(c)]
        a_ak = jnp.where(lo_c, 0.0, top[i]).astype(bf16)
        akv = jnp.dot(a_ak, jnp.concatenate([vx, vx], axis=0), preferred_element_type=f32)
        xs.append(atx_s[d, h, cs(c)].astype(f32) + akv)
        aps.append(top[i][:, :CHUNK])
    n_dbl = CHUNK.bit_length() - 1
    for it in range(n_dbl):
        yield CHUNK_STAGE
        last = it + 1 == n_dbl
        res = [_dot(ap, x if last else jnp.concatenate([x, ap], axis=1)) for ap, x in zip(aps, xs)]
        xs = [x + rs[:, :PAIR] for x, rs in zip(xs, res)]
        if not last:
            aps = [rs[:, PAIR:] for rs in res]
    yield CHUNK_STAGE
    rhs_o = [jnp.concatenate([xs[i].astype(bf16), vx_s[h, cs(c)]], axis=0) for i, (d, c, h) in enumerate(inst)]
    rys = [_dot(bot[i], rhs_o[i]) + rtx_s[d, h, cs(c)] for i, (d, c, h) in enumerate(inst)]
    yield CHUNK_STAGE
    mns = [_dot_tn(jnp.concatenate([bp_s[d, cs(c), h * HEAD:(h + 1) * HEAD],
                                    kp_s[d, cs(c), h * HEAD:(h + 1) * HEAD]], axis=0), rhs_o[i])
           for i, (d, c, h) in enumerate(inst)]
    yield CHUNK_STAGE
    for i, (d, c, h) in enumerate(inst):
        p_blk = pl_s[d, c * CHUNK:c * CHUNK + SUBLANES, :]
        p_row = (p_blk if h == 0 else swap_c(p_blk))[0:1]
        mn_ref[d, c, h] = (mns[i] + jnp.where(row == col, p_row, 0.0)).astype(bf16)
    y0_fwd = {}
    for j, (d, c) in enumerate(pairs):
        ry0, ry1 = rys[2 * j], rys[2 * j + 1]
        rp_ref[d, cs(c), :] = jnp.where(lo_c, ry0, swap_c(ry1)).astype(bf16)
        y0 = jnp.where(lo_c, swap_c(ry0), ry1)
        if d == 0:
            y0_fwd[c] = y0
        else:
            y0_ref[cs(c), :] = (y0_fwd[c] + y0).astype(y0_ref.dtype)
    bonus_ref[...] = bonus.astype(bonus_ref.dtype)


N_IN_P = 23
N_OUT_P = 4


def _rwkv_p_multi(*refs, tt, seq, pp):
    ins, outs = refs[:N_IN_P], refs[N_IN_P:N_IN_P + N_OUT_P]
    scr, shx_s = refs[N_IN_P + N_OUT_P:-1], refs[-1]
    x_ref, x_pv, x_nx, mux_ref = ins[9], ins[10], ins[11], ins[15]
    w0_ref, wlu_ref, a0_ref, alu_ref = ins[16:20]
    shift = _token_shift(tt, seq)
    xs = shift(x_ref[0], x_pv[0], x_nx[0], mux_ref[...], shx_s)
    zs = [w0_ref[d:d + 1, :] + _dot(jnp.tanh(xs[:, d * R_LORA:(d + 1) * R_LORA]), wlu_ref[d]) for d in range(2)]
    a_pre = [a0_ref[d:d + 1, :] + _dot(xs[:, (2 + d) * R_LORA:(3 + d) * R_LORA], alu_ref[d]) for d in range(2)]
    stages = []
    for q in range(pp):
        lanes = pl.ds(q * PAIR, PAIR)
        lsl = slice(q * PAIR, (q + 1) * PAIR)
        view = lambda ref, lanes=lanes: ref.at[..., lanes]
        planes = [view(ref) for ref in ins[0:9]]
        gains = [view(ref) for ref in ins[12:15]]
        params = [view(ref) for ref in ins[20:23]]
        y0_ref, bonus_ref, rp_ref, mn_ref = outs
        stages.append(_rwkv_p_body(*planes, *gains, *params,
                                   view(y0_ref), view(bonus_ref), view(rp_ref), mn_ref.at[:, :, pl.ds(2 * q, 2)],
                                   *[s.at[q] for s in scr], tt=tt, shift=shift,
                                   zs=[z[:, lsl] for z in zs], a_pre=[a[:, lsl] for a in a_pre]))
    def run_pre(stage):
        while next(stage) != PRE_DONE:
            pass

    run_pre(stages[0])
    for q in range(pp):
        chunk, pre = stages[q], (stages[q + 1] if q + 1 < pp else None)
        chunk_live, pre_live = True, pre is not None
        while chunk_live or pre_live:
            if chunk_live:
                chunk_live = next(chunk, None) is not None
            if pre_live:
                pre_live = next(pre) != PRE_DONE


def _seg_matrix(n):
    r = lax.broadcasted_iota(jnp.int32, (n, n), 0) // HEAD
    c = lax.broadcasted_iota(jnp.int32, (n, n), 1) // HEAD
    return jnp.where(r == c, 1.0, 0.0).astype(bf16)


def _segsum(x, seg):
    x0 = x.astype(bf16)
    if seg.shape[0] == PAIR:
        return jnp.dot(x0, seg, preferred_element_type=f32)
    x1 = (x - x0.astype(f32)).astype(bf16)
    s = jnp.dot(jnp.concatenate([x0, x1], axis=1), seg, preferred_element_type=f32)
    return s[:, :PAIR] + s[:, PAIR:]


def _rwkv_p(p9, mu_p, w0, wlu, a0, alu, k_k, k_a, r_k, *, tt, seq, pp):
    m = p9.shape[1]
    nt = m // tt
    npair = D_MODEL // PAIR
    cpt = tt // CHUNK
    wl = pp * PAIR
    hb = 2 * SUBLANES

    def tiles(plane):
        return [pl.BlockSpec((1, tt, wl), lambda i, j: (plane, i, j)),
                pl.BlockSpec((1, hb, wl), lambda i, j: (plane, jnp.maximum(i * (tt // hb) - 1, 0), j)),
                pl.BlockSpec((1, hb, wl), lambda i, j: (plane, jnp.minimum((i + 1) * (tt // hb), m // hb - 1), j))]

    vec = pl.BlockSpec((1, wl), lambda i, j: (0, j))
    mu = lambda plane: pl.BlockSpec((1, wl), lambda i, j: (0, plane * (npair // pp) + j))
    wx = 4 * R_LORA
    return pl.pallas_call(
        functools.partial(_rwkv_p_multi, tt=tt, seq=seq, pp=pp),
        grid=(nt, npair // pp),
        in_specs=tiles(P_R) + tiles(P_K) + tiles(P_V) + [
            pl.BlockSpec((1, tt, wx), lambda i, j: (P_X, i, 0)),
            pl.BlockSpec((1, hb, wx), lambda i, j: (P_X, jnp.maximum(i * (tt // hb) - 1, 0), 0)),
            pl.BlockSpec((1, hb, wx), lambda i, j: (P_X, jnp.minimum((i + 1) * (tt // hb), m // hb - 1), 0)),
            mu(P_R), mu(P_K), mu(P_V),
            pl.BlockSpec((1, wx), lambda i, j: (0, P_X * D_MODEL // wx)),
            pl.BlockSpec((2, wl), lambda i, j: (0, j)),
            pl.BlockSpec((2, R_LORA, wl), lambda i, j: (0, 0, j)),
            pl.BlockSpec((2, wl), lambda i, j: (0, j)),
            pl.BlockSpec((2, R_LORA, wl), lambda i, j: (0, 0, j)),
            vec, vec, vec,
        ],
        out_specs=[
            pl.BlockSpec((tt, wl), lambda i, j: (i, j)),
            pl.BlockSpec((tt, wl), lambda i, j: (i, j)),
            pl.BlockSpec((2, tt, wl), lambda i, j: (0, i, j)),
            pl.BlockSpec((2, cpt, 2 * pp, HEAD, PAIR), lambda i, j: (0, i, j, 0, 0)),
        ],
        out_shape=[
            jax.ShapeDtypeStruct((m, D_MODEL), bf16),
            jax.ShapeDtypeStruct((m, D_MODEL), bf16),
            jax.ShapeDtypeStruct((2, m, D_MODEL), bf16),
            jax.ShapeDtypeStruct((2, m // CHUNK, N_HEADS, HEAD, PAIR), bf16),
        ],
        scratch_shapes=(
            [pltpu.VMEM((pp, 2, tt, PAIR), bf16) for _ in range(4)]
            + [pltpu.VMEM((pp, 2, 2, tt, PAIR), bf16) for _ in range(3)]
            + [pltpu.VMEM((pp, 2, 2, tt, PAIR), f32),
               pltpu.VMEM((pp, 2, tt, PAIR), bf16),
               pltpu.VMEM((pp, 2, tt, PAIR), f32),
               pltpu.VMEM((pp, 3, tt + 2 * PADR, PAIR), f32),
               pltpu.VMEM((tt + 2 * PADR, wx), f32)]),
        compiler_params=pltpu.CompilerParams(
            dimension_semantics=("parallel", "parallel"), vmem_limit_bytes=VMEM_LIMIT),
        name="rwkv_p",
    )(*([p9] * 12), mu_p, mu_p, mu_p, mu_p, w0, wlu, a0, alu, k_k, k_a, r_k)


def _rwkv_s_body(mnf_ref, mnb_ref, sf_ref, sb_ref, st_ref, *, grp):
    @pl.when(pl.program_id(1) == 0)
    def _():
        st_ref[...] = jnp.zeros_like(st_ref)

    zero = jnp.zeros((HEAD, HEAD), bf16)

    def step(d, mn_ref, s_out_ref, cc):
        s = st_ref[d].astype(bf16)
        for p in range(N_HEADS // 2):
            s_out_ref[cc, p] = jnp.concatenate([jnp.concatenate([s[2 * p], zero], axis=1),
                                                jnp.concatenate([zero, s[2 * p + 1]], axis=1)], axis=0)
        m_t = mn_ref[0, cc, :, :, 0:HEAD]
        n_t = mn_ref[0, cc, :, :, HEAD:].astype(f32)
        st_ref[d] = jnp.einsum('hij,hjk->hik', m_t, s, preferred_element_type=f32) + n_t

    for cc in range(grp):
        step(0, mnf_ref, sf_ref, cc)
        step(1, mnb_ref, sb_ref, grp - 1 - cc)


def _rwkv_s(mn, *, n_seq, grp):
    nchunk = mn.shape[1]
    npair = N_HEADS // 2
    ng = nchunk // n_seq // grp
    blk_in = (1, grp, N_HEADS, HEAD, PAIR)
    blk_out = (grp, npair, PAIR, PAIR)
    out = jax.ShapeDtypeStruct((nchunk, npair, PAIR, PAIR), bf16)
    return pl.pallas_call(
        functools.partial(_rwkv_s_body, grp=grp),
        grid=(n_seq, ng),
        in_specs=[pl.BlockSpec(blk_in, lambda b, g: (0, b * ng + g, 0, 0, 0)),
                  pl.BlockSpec(blk_in, lambda b, g: (1, b * ng + ng - 1 - g, 0, 0, 0))],
        out_specs=[pl.BlockSpec(blk_out, lambda b, g: (b * ng + g, 0, 0, 0)),
                   pl.BlockSpec(blk_out, lambda b, g: (b * ng + ng - 1 - g, 0, 0, 0))],
        out_shape=[out, out],
        scratch_shapes=[pltpu.VMEM((2, N_HEADS, HEAD, HEAD), f32)],
        compiler_params=pltpu.CompilerParams(
            dimension_semantics=("parallel", "arbitrary"), vmem_limit_bytes=VMEM_LIMIT),
        name="rwkv_s",
    )(mn, mn)


def _rwkv_f_body(y0_ref, bonus_ref, rp_ref, sf_ref, sb_ref, gate_ref, lng_ref, lnb_ref, o_ref, *, tt):
    npair = D_MODEL // PAIR
    nc = tt // CHUNK
    cs = lambda c: slice(c * CHUNK, (c + 1) * CHUNK)
    ps = lambda p: slice(p * PAIR, (p + 1) * PAIR)
    yf = [[jnp.dot(rp_ref[0, cs(c), ps(p)], sf_ref[c, p], preferred_element_type=f32) for c in range(nc)]
          for p in range(npair)]
    yb = [[jnp.dot(rp_ref[1, cs(c), ps(p)], sb_ref[c, p], preferred_element_type=f32) for c in range(nc)]
          for p in range(npair)]
    seg1, seg2 = _seg_matrix(PAIR), _seg_matrix(2 * PAIR)
    ys = [y0_ref[:, ps(p)].astype(f32) + jnp.concatenate(yf[p], axis=0) + jnp.concatenate(yb[p], axis=0)
          for p in range(npair)]
    mus = [_segsum(y, seg2) * (1.0 / HEAD) for y in ys]
    dvs = [y - mu for y, mu in zip(ys, mus)]
    vrs = [_segsum(dv * dv, seg1) * (1.0 / HEAD) for dv in dvs]
    for p in range(npair):
        yn = dvs[p] * lax.rsqrt(vrs[p] + GN_EPS) * lng_ref[:, ps(p)] + lnb_ref[:, ps(p)] + bonus_ref[:, ps(p)].astype(f32)
        g = gate_ref[0, :, ps(p)].astype(f32)
        o_ref[:, ps(p)] = (yn * (g * _sigmoid(g))).astype(bf16)


def _rwkv_f(y0, bonus, rp, sf, sb, p9, ln_g, ln_b, *, tt):
    m = y0.shape[0]
    cpt = tt // CHUNK
    npair = D_MODEL // PAIR
    tile = pl.BlockSpec((tt, D_MODEL), lambda i: (i, 0))
    vec = pl.BlockSpec((1, D_MODEL), lambda i: (0, 0))
    state = pl.BlockSpec((cpt, npair, PAIR, PAIR), lambda i: (i, 0, 0, 0))
    return pl.pallas_call(
        functools.partial(_rwkv_f_body, tt=tt),
        grid=(m // tt,),
        in_specs=[
            tile, tile,
            pl.BlockSpec((2, tt, D_MODEL), lambda i: (0, i, 0)),
            state, state,
            pl.BlockSpec((1, tt, D_MODEL), lambda i: (P_GA, i, 0)),
            vec, vec,
        ],
        out_specs=tile,
        out_shape=jax.ShapeDtypeStruct((m, D_MODEL), bf16),
        compiler_params=pltpu.CompilerParams(dimension_semantics=("parallel",), vmem_limit_bytes=VMEM_LIMIT),
        name="rwkv_f",
    )(y0, bonus, rp, sf, sb, p9, ln_g, ln_b)


def _attn_body(q_ref, kp_ref, kc_ref, kn_ref, vp_ref, vc_ref, vn_ref, tp_ref, tc_ref, tn_ref, gate_ref,
               qg_ref, kg_ref, sink_ref, o_ref, *, seq, qb):
    step_rows = qb * BLOCK
    start = (pl.program_id(0) % (seq // step_rows)) * step_rows
    ctx = BLOCK + 2 * WINDOW
    qoff = lax.broadcasted_iota(jnp.int32, (BLOCK, ctx), 0)
    koff = lax.broadcasted_iota(jnp.int32, (BLOCK, ctx), 1) - WINDOW
    valid = []
    for b in range(qb):
        kpos = start + b * BLOCK + koff
        valid.append((jnp.abs(qoff - koff) <= WINDOW) & (kpos >= 0) & (kpos < seq))

    seg = _seg_matrix(PAIR)
    swap = lambda t: pltpu.roll(t, HEAD, axis=1)

    def norm_rope(x, gain, tab):
        lane = lax.broadcasted_iota(jnp.int32, x.shape, 1)
        xn = x * lax.rsqrt(_segsum(x * x, seg) * (1.0 / HEAD) + NORM_EPS) * gain
        rot = jnp.where(lane % HEAD < HEAD // 2, pltpu.roll(xn, PAIR - HEAD // 2, axis=1),
                        pltpu.roll(xn, HEAD // 2, axis=1))
        return xn * tab[:, :PAIR] + rot * tab[:, PAIR:]

    def place(t, e, want_lo):
        lo = lax.broadcasted_iota(jnp.int32, t.shape, 1) < HEAD
        src = t if (e == 0) == want_lo else swap(t)
        return (jnp.where(lo, src, 0.0) if want_lo else jnp.where(lo, 0.0, src)).astype(bf16)

    tab_k = jnp.concatenate([tp_ref[...], tc_ref[...], tn_ref[...]], axis=0)
    tab_q = tc_ref[...]
    k_lo, k_hi, v_lo, v_hi = [], [], [], []
    for kp in range(D_KV // PAIR):
        ps_ = slice(kp * PAIR, (kp + 1) * PAIR)
        kx = jnp.concatenate([kp_ref[0, :, ps_], kc_ref[0, :, ps_], kn_ref[0, :, ps_]], axis=0)
        vx = jnp.concatenate([vp_ref[0, :, ps_], vc_ref[0, :, ps_], vn_ref[0, :, ps_]], axis=0).astype(f32)
        kr = norm_rope(kx.astype(f32), kg_ref[...], tab_k)
        for e in range(2):
            k_lo.append(place(kr, e, True))
            k_hi.append(place(kr, e, False))
            v_lo.append(place(vx, e, True))
            v_hi.append(place(vx, e, False))
    npair = D_MODEL // PAIR
    psl = lambda p: slice(p * PAIR, (p + 1) * PAIR)
    log2e = math.log2(math.e)
    qrs = [(norm_rope(q_ref[0, :, psl(p)].astype(f32), qg_ref[...], tab_q) * (HEAD ** -0.5 * log2e)).astype(bf16)
           for p in range(npair)]
    inst = [(b, hq) for b in range(qb) for hq in range(N_HEADS)]
    rows = lambda b: slice(b * BLOCK, (b + 1) * BLOCK)
    keys = lambda b: slice(b * BLOCK, b * BLOCK + ctx)
    def masked(s, ok):
        return jnp.concatenate([jnp.where(ok[:, :WINDOW], s[:, :WINDOW], -1e30), s[:, WINDOW:WINDOW + BLOCK],
                                jnp.where(ok[:, WINDOW + BLOCK:], s[:, WINDOW + BLOCK:], -1e30)], axis=1)

    ss = [masked(lax.dot_general(qrs[hq // 2][rows(b)], (k_lo if hq % 2 == 0 else k_hi)[hq // GRP][keys(b)],
                                 (((1,), (1,)), ((), ())), preferred_element_type=f32), valid[b])
          for b, hq in inst]
    sinks = [sink_ref[0:1, hq:hq + 1] * log2e for b, hq in inst]
    mxs = [jnp.maximum(jnp.max(s, axis=-1, keepdims=True), sk) for s, sk in zip(ss, sinks)]
    ps = [jnp.exp2(s - mx) for s, mx in zip(ss, mxs)]
    dens = [jnp.sum(p, axis=-1, keepdims=True) + jnp.exp2(sk - mx) for p, sk, mx in zip(ps, sinks, mxs)]
    os_ = [jnp.dot(p.astype(bf16), (v_lo if hq % 2 == 0 else v_hi)[hq // GRP][keys(b)], preferred_element_type=f32) / den
           for (b, hq), p, den in zip(inst, ps, dens)]
    for b in range(qb):
        for p in range(npair):
            gt = gate_ref[0, rows(b), psl(p)].astype(f32)
            i0 = b * N_HEADS + 2 * p
            o_ref[rows(b), psl(p)] = ((os_[i0] + os_[i0 + 1]) * (gt * _sigmoid(gt))).astype(bf16)


def _attn(p9, rope_tab, q_g, k_g, sink, *, seq, qb):
    m = p9.shape[1]
    nblk = seq // BLOCK
    rows = qb * BLOCK

    def nbr(delta):
        def f(i):
            t = (i * qb) % nblk
            return i * qb - t + jnp.clip(t + delta, 0, nblk - 1)
        return f

    def kv(col):
        return [pl.BlockSpec((1, BLOCK, D_KV), lambda i, f=nbr(-1): (P_X, f(i), col)),
                pl.BlockSpec((1, rows, D_KV), lambda i: (P_X, i, col)),
                pl.BlockSpec((1, BLOCK, D_KV), lambda i, f=nbr(qb): (P_X, f(i), col))]

    tabs = [pl.BlockSpec((BLOCK, 2 * PAIR), lambda i, f=nbr(-1): (f(i) % nblk, 0)),
            pl.BlockSpec((rows, 2 * PAIR), lambda i: (i % (nblk // qb), 0)),
            pl.BlockSpec((BLOCK, 2 * PAIR), lambda i, f=nbr(qb): (f(i) % nblk, 0))]

    return pl.pallas_call(
        functools.partial(_attn_body, seq=seq, qb=qb),
        grid=(m // rows,),
        in_specs=[pl.BlockSpec((1, rows, D_MODEL), lambda i: (P_Q, i, 0))] + kv(1) + kv(2) + tabs + [
            pl.BlockSpec((1, rows, D_MODEL), lambda i: (P_GB, i, 0)),
            pl.BlockSpec((1, PAIR), lambda i: (0, 0)),
            pl.BlockSpec((1, PAIR), lambda i: (0, 0)),
            pl.BlockSpec((1, N_HEADS), lambda i: (0, 0)),
        ],
        out_specs=pl.BlockSpec((rows, D_MODEL), lambda i: (i, 0)),
        out_shape=jax.ShapeDtypeStruct((m, D_MODEL), bf16),
        compiler_params=pltpu.CompilerParams(dimension_semantics=("parallel",), vmem_limit_bytes=VMEM_LIMIT),
        name="attn",
    )(*([p9] * 7), rope_tab, rope_tab, rope_tab, p9, q_g, k_g, sink)


def _rope_table(seq):
    inv = 1.0 / (ROPE_THETA ** (jnp.arange(0, HEAD, 2, dtype=f32) / HEAD))
    ang = jnp.arange(seq, dtype=f32)[:, None] * inv[None, :]
    cos, sin = jnp.cos(ang), jnp.sin(ang)
    return jnp.concatenate([cos, cos, cos, cos, -sin, sin, -sin, sin], axis=1)


def _out_proj_body(x_ref, ga_ref, gb_ref, ma_ref, mb_ref, wa_ref, wb_ref, wo_ref, o_ref):
    ya = jnp.dot(ga_ref[...], wa_ref[...], preferred_element_type=f32)
    yb = jnp.dot(gb_ref[...], wb_ref[...], preferred_element_type=f32)
    mixed = _sigmoid(ma_ref[0].astype(f32)) * ya + _sigmoid(mb_ref[0].astype(f32)) * yb
    o_ref[...] = x_ref[...] + jnp.dot(mixed.astype(bf16), wo_ref[...], preferred_element_type=f32)


def _out_proj(x2d, ga, gb, p9, wa, wb, wo, *, tm):
    m = x2d.shape[0]
    row = pl.BlockSpec((tm, D_MODEL), lambda i: (i, 0))
    wspec = pl.BlockSpec((D_MODEL, D_MODEL), lambda i: (0, 0))
    return pl.pallas_call(
        _out_proj_body,
        grid=(m // tm,),
        in_specs=[row, row, row,
                  pl.BlockSpec((1, tm, D_MODEL), lambda i: (P_MA, i, 0)),
                  pl.BlockSpec((1, tm, D_MODEL), lambda i: (P_MB, i, 0)),
                  wspec, wspec, wspec],
        out_specs=row,
        out_shape=jax.ShapeDtypeStruct((m, D_MODEL), f32),
        compiler_params=pltpu.CompilerParams(dimension_semantics=("parallel",), vmem_limit_bytes=VMEM_LIMIT),
        name="out_proj",
    )(x2d, ga, gb, p9, p9, wa, wb, wo)


def _layer(x2d, lw, rope_tab, *, n_seq, seq):
    for tile in (TM_IN, TM_OUT, TT_P, TT_F, GRP_S * CHUNK, QB_ATTN * BLOCK):
        assert seq % tile == 0, (seq, tile)
    p9 = _in_proj(x2d, lw["norm_g"], lw["w_p"], tm=TM_IN)
    y0, bonus, rp, mn = _rwkv_p(p9, lw["mu_p"], lw["w0"], lw["wlu"], lw["a0"], lw["alu"], lw["k_k"], lw["k_a"],
                                lw["r_k"], tt=TT_P, seq=seq, pp=PAIRS_P)
    sf, sb = _rwkv_s(mn, n_seq=n_seq, grp=GRP_S)
    ga = _rwkv_f(y0, bonus, rp, sf, sb, p9, lw["ln_g"], lw["ln_b"], tt=TT_F)
    gb = _attn(p9, rope_tab, lw["q_g"], lw["k_g"], lw["sink"], seq=seq, qb=QB_ATTN)
    return _out_proj(x2d, ga, gb, p9, lw["wa"], lw["wb"], lw["wo"], tm=TM_OUT)


def _trunk(x, layers):
    n_seq, seq, _ = x.shape
    rope_tab = _rope_table(seq)
    x2d = x.reshape(n_seq * seq, D_MODEL)
    for lw in layers:
        x2d = _layer(x2d, lw, rope_tab, n_seq=n_seq, seq=seq)
    return x2d.reshape(n_seq, seq, D_MODEL)


def _prep_layers(norm_g, w_in, shift_mu, w0, w_lora_up, a0, a_lora_up, k_k, k_a, r_k, ln_x_g, ln_x_b,
                 q_norm_g, k_norm_g, sink, w_proj_a, w_proj_b, w_out):
    layers = []
    row = lambda v: v.reshape(1, -1).astype(f32)
    for l in range(norm_g.shape[0]):
        w_p, mu_p = _prep_layer_weights(w_in[l], shift_mu[l])
        layers.append(dict(
            norm_g=row(norm_g[l]), w_p=w_p, mu_p=mu_p,
            w0=w0[l].astype(f32), wlu=w_lora_up[l].astype(bf16), a0=a0[l].astype(f32), alu=a_lora_up[l].astype(bf16),
            k_k=row(k_k[l]), k_a=row(k_a[l]), r_k=row(r_k[l]), ln_g=row(ln_x_g[l]), ln_b=row(ln_x_b[l]),
            q_g=row(jnp.tile(q_norm_g[l], 2)), k_g=row(jnp.tile(k_norm_g[l], 2)), sink=row(sink[l]),
            wa=w_proj_a[l].astype(bf16), wb=w_proj_b[l].astype(bf16), wo=w_out[l].astype(bf16)))
    return layers


def kernel(x_prompt, x_sample, norm_g, w_in, shift_mu, w0, w_lora_up, a0, a_lora_up, k_k, k_a, r_k,
           ln_x_g, ln_x_b, q_norm_g, k_norm_g, sink, w_proj_a, w_proj_b, w_out):
    layers = _prep_layers(norm_g, w_in, shift_mu, w0, w_lora_up, a0, a_lora_up, k_k, k_a, r_k, ln_x_g, ln_x_b,
                          q_norm_g, k_norm_g, sink, w_proj_a, w_proj_b, w_out)
    return _trunk(x_prompt, layers), _trunk(x_sample, layers)
```

```python
import functools
import math

import jax
import jax.numpy as jnp
from jax import lax
from jax.experimental import pallas as pl
from jax.experimental.pallas import tpu as pltpu

D_MODEL = 1024
HEAD = 64
N_HEADS = D_MODEL // HEAD
R_LORA = 64
HKV = 4
GRP = N_HEADS // HKV
D_KV = HKV * HEAD
WINDOW = 128
BLOCK = 128
GN_EPS = 64e-5
NORM_EPS = 1e-6
ROPE_THETA = 10000.0
C_SHIFT = 3 * D_MODEL + 4 * R_LORA
N_PLANES = 9
CHUNK = 64
VMEM_LIMIT = 58 * 1024 * 1024

TM_IN = 2048
TM_OUT = 1024
TT_P = 256
TT_F = 1024
GRP_S = 16
PAIRS_P = 8
QB_ATTN = 4

P_R, P_K, P_V, P_GA, P_Q, P_GB, P_MA, P_MB, P_X = range(9)

f32 = jnp.float32
bf16 = jnp.bfloat16


def _dot(a, b):
    return jnp.dot(a.astype(bf16), b.astype(bf16), preferred_element_type=f32)


def _dot_tn(a, b):
    return lax.dot_general(a.astype(bf16), b.astype(bf16), (((0,), (0,)), ((), ())),
                           preferred_element_type=f32)


def _sigmoid(x):
    return 1.0 / (1.0 + jnp.exp(-x))


def _in_proj_body(x_ref, g_ref, w_ref, o_ref, h_ref, s_ref):
    @pl.when(pl.program_id(1) == 0)
    def _():
        x = x_ref[...]
        h_ref[...] = (x * g_ref[...]).astype(bf16)
        s_ref[...] = lax.rsqrt(jnp.mean(x * x, axis=-1, keepdims=True) + NORM_EPS)

    o_ref[0] = (jnp.dot(h_ref[...], w_ref[...], preferred_element_type=f32) * s_ref[...]).astype(bf16)


def _in_proj(x2d, norm_g, w_p, *, tm):
    m = x2d.shape[0]
    return pl.pallas_call(
        _in_proj_body,
        grid=(m // tm, N_PLANES),
        in_specs=[
            pl.BlockSpec((tm, D_MODEL), lambda i, j: (i, 0)),
            pl.BlockSpec((1, D_MODEL), lambda i, j: (0, 0)),
            pl.BlockSpec((D_MODEL, D_MODEL), lambda i, j: (0, j)),
        ],
        out_specs=pl.BlockSpec((1, tm, D_MODEL), lambda i, j: (j, i, 0)),
        out_shape=jax.ShapeDtypeStruct((N_PLANES, m, D_MODEL), bf16),
        scratch_shapes=[pltpu.VMEM((tm, D_MODEL), bf16), pltpu.VMEM((tm, 1), f32)],
        compiler_params=pltpu.CompilerParams(
            dimension_semantics=("parallel", "arbitrary"), vmem_limit_bytes=VMEM_LIMIT),
        name="in_proj",
    )(x2d, norm_g, w_p)


def _prep_layer_weights(w_in, shift_mu):
    c = C_SHIFT
    d = D_MODEL
    cols = [
        (0, d), (d, 2 * d), (2 * d, 3 * d),
        (c, c + d),
        (c + d, c + 2 * d),
        (c + 2 * d + 2 * D_KV, c + 3 * d + 2 * D_KV),
        (c + 3 * d + 2 * D_KV, c + 4 * d + 2 * D_KV),
        (c + 4 * d + 2 * D_KV, c + 5 * d + 2 * D_KV),
    ]
    planes = [w_in[:, a:b] for a, b in cols]
    small = jnp.concatenate([w_in[:, 3 * d:c], w_in[:, c + 2 * d:c + 2 * d + 2 * D_KV],
                             jnp.zeros((d, d - 4 * R_LORA - 2 * D_KV), w_in.dtype)], axis=1)
    w_p = jnp.concatenate(planes + [small], axis=1).astype(bf16)
    zeros = jnp.zeros((d,), f32)
    mu_small = jnp.concatenate([shift_mu[3 * d:c], jnp.zeros((d - 4 * R_LORA,), f32)])
    mu_p = jnp.concatenate([shift_mu[0:d], shift_mu[d:2 * d], shift_mu[2 * d:3 * d]] + [zeros] * 5 + [mu_small])
    return w_p, mu_p.reshape(1, N_PLANES * d)


PAIR = 2 * HEAD
SUBLANES = 8
PADR = SUBLANES
PRE, PRE_DONE, CHUNK_STAGE = "pre", "pre_done", "chunk"


def _token_shift(tt, seq):
    i = pl.program_id(0)
    first = (i * tt) % seq == 0
    last = ((i + 1) * tt) % seq == 0

    def shift(main, pv, nx, mu, buf):
        c = main.astype(f32)
        buf[PADR:PADR + tt, :] = c
        buf[PADR - 1:PADR, :] = jnp.where(first, 0.0, pv[pv.shape[0] - 1:, :].astype(f32))
        buf[PADR + tt:PADR + tt + 1, :] = jnp.where(last, 0.0, nx[:1, :].astype(f32))
        nbr = 0.5 * (buf[PADR - 1:PADR - 1 + tt, :] + buf[PADR + 1:PADR + 1 + tt, :])
        return c + mu * (nbr - c)

    return shift


def _rwkv_p_body(r_ref, r_pv, r_nx, k_ref, k_pv, k_nx, v_ref, v_pv, v_nx, mur_ref, muk_ref, muv_ref,
                 kk_ref, ka_ref, rk_ref,
                 y0_ref, bonus_ref, rp_ref, mn_ref,
                 at_s, rt_s, kp_s, bp_s, khm_s, bhm_s, atx_s, rtx_s, vx_s, pl_s, sh_s, *, tt, shift, zs, a_pre):
    nc = tt // CHUNK
    r = shift(r_ref[0], r_pv[0], r_nx[0], mur_ref[...], sh_s.at[0])
    k = shift(k_ref[0], k_pv[0], k_nx[0], muk_ref[...], sh_s.at[1])
    v = shift(v_ref[0], v_pv[0], v_nx[0], muv_ref[...], sh_s.at[2])
    yield PRE

    seg1, seg2 = _seg_matrix(PAIR), _seg_matrix(2 * PAIR)
    lo = lax.broadcasted_iota(jnp.int32, (tt, PAIR), 1) < HEAD
    swap = lambda t: pltpu.roll(t, HEAD, axis=1)

    def head_lo(t, h):
        return jnp.where(lo, t if h == 0 else swap(t), 0.0)

    def head_hi(t, h):
        return jnp.where(lo, 0.0, swap(t) if h == 0 else t)

    def head_own(t, h):
        return jnp.where(lo, t, 0.0) if h == 0 else jnp.where(lo, 0.0, t)

    for h in range(2):
        vx_s[h] = head_hi(v, h).astype(bf16)

    kkv = k * kk_ref[...]
    kk = kkv / jnp.maximum(jnp.sqrt(_segsum(kkv * kkv, seg1)), 1e-12)
    yield PRE

    rows = lax.broadcasted_iota(jnp.int32, (tt, tt), 0)
    cols = lax.broadcasted_iota(jnp.int32, (tt, tt), 1)
    same = (rows // CHUNK) == (cols // CHUNK)

    kmod_sum = jnp.zeros_like(k)
    for d in range(2):
        if d == 0:
            tri_bd = jnp.where(same & (cols <= rows), 1.0, 0.0)
        else:
            tri_bd = jnp.where(same & (cols >= rows), 1.0, 0.0)
        lw = -math.exp(-0.5) * _sigmoid(zs[d])
        a = _sigmoid(a_pre[d])
        yield PRE
        kmod = k * (1.0 + (a - 1.0) * ka_ref[...])
        kmod_sum = kmod_sum + kmod
        b = -(kk * a)
        lw0 = lw.astype(bf16)
        lw1 = (lw - lw0.astype(f32)).astype(bf16)
        cum2 = jnp.dot(tri_bd.astype(bf16), jnp.concatenate([lw0, lw1], axis=1), preferred_element_type=f32)
        cum = cum2[:, :PAIR] + cum2[:, PAIR:]
        edge = CHUNK - 1 if d == 0 else 0
        tot = jnp.concatenate([jnp.broadcast_to(cum[c * CHUNK + edge:c * CHUNK + edge + 1, :], (CHUNK, PAIR))
                               for c in range(nc)], axis=0)
        yield PRE
        p_inv = jnp.exp(-cum)
        p_end = jnp.exp(tot - cum)
        at = kk * jnp.exp(cum - lw)
        rt = r * jnp.exp(cum)
        kh = kmod * p_inv
        bh = b * p_inv
        p_tot = jnp.exp(tot)
        at_s[d] = at.astype(bf16)
        rt_s[d] = rt.astype(bf16)
        pl_s[d] = p_tot
        yield PRE
        kp_s[d] = (kmod * p_end).astype(bf16)
        bp_s[d] = (b * p_end).astype(bf16)
        for h in range(2):
            khm_s[d, h] = head_own(kh, h).astype(bf16)
            bhm_s[d, h] = head_own(bh, h).astype(bf16)
            atx_s[d, h] = head_lo(at, h).astype(bf16)
            rtx_s[d, h] = head_lo(rt, h)
        yield PRE

    bonus = _segsum(r * kmod_sum * rk_ref[...], seg2) * v
    yield PRE_DONE

    row = lax.broadcasted_iota(jnp.int32, (CHUNK, PAIR), 0)
    col = lax.broadcasted_iota(jnp.int32, (CHUNK, PAIR), 1)
    col_t = col % CHUNK
    lo_c = col < HEAD
    strict = (col_t < row, col_t > row)
    incl = (col_t <= row, col_t >= row)
    pairs = [(d, c) for d in range(2) for c in range(nc)]
    inst = [(d, c, h) for d, c in pairs for h in range(2)]
    cs = lambda c: slice(c * CHUNK, (c + 1) * CHUNK)
    swap_c = lambda t: pltpu.roll(t, HEAD, axis=1)

    top, bot = [], []
    for d, c in pairs:
        lhs = jnp.concatenate([at_s[d, cs(c)], rt_s[d, cs(c)]], axis=0)
        rhs = jnp.concatenate([bhm_s[d, 0, cs(c)], khm_s[d, 0, cs(c)],
                               bhm_s[d, 1, cs(c)], khm_s[d, 1, cs(c)]], axis=0)
        sc = lax.dot_general(lhs, rhs, (((1,), (1,)), ((), ())), preferred_element_type=f32)
        for h in range(2):
            top.append(jnp.where(strict[d], sc[:CHUNK, h * PAIR:(h + 1) * PAIR], 0.0))
            bot.append(jnp.where(incl[d], sc[CHUNK:, h * PAIR:(h + 1) * PAIR], 0.0))
    yield CHUNK_STAGE
    xs, aps = [], []
    for i, (d, c, h) in enumerate(inst):
        vx = vx_s[h, cs(c)]
        a_ak = jnp.where(lo_c, 0.0, top[i]).astype(bf16)
        akv = jnp.dot(a_ak, jnp.concatenate([vx, vx], axis=0), preferred_element_type=f32)
        xs.append(atx_s[d, h, cs(c)].astype(f32) + akv)
        aps.append(top[i][:, :CHUNK])
    n_dbl = CHUNK.bit_length() - 1
    for it in range(n_dbl):
        yield CHUNK_STAGE
        last = it + 1 == n_dbl
        res = [_dot(ap, x if last else jnp.concatenate([x, ap], axis=1)) for ap, x in zip(aps, xs)]
        xs = [x + rs[:, :PAIR] for x, rs in zip(xs, res)]
        if not last:
            aps = [rs[:, PAIR:] for rs in res]
    yield CHUNK_STAGE
    rhs_o = [jnp.concatenate([xs[i].astype(bf16), vx_s[h, cs(c)]], axis=0) for i, (d, c, h) in enumerate(inst)]
    rys = [_dot(bot[i], rhs_o[i]) + rtx_s[d, h, cs(c)] for i, (d, c, h) in enumerate(inst)]
    yield CHUNK_STAGE
    mns = [_dot_tn(jnp.concatenate([bp_s[d, cs(c), h * HEAD:(h + 1) * HEAD],
                                    kp_s[d, cs(c), h * HEAD:(h + 1) * HEAD]], axis=0), rhs_o[i])
           for i, (d, c, h) in enumerate(inst)]
    yield CHUNK_STAGE
    for i, (d, c, h) in enumerate(inst):
        p_blk = pl_s[d, c * CHUNK:c * CHUNK + SUBLANES, :]
        p_row = (p_blk if h == 0 else swap_c(p_blk))[0:1]
        mn_ref[d, c, h] = (mns[i] + jnp.where(row == col, p_row, 0.0)).astype(bf16)
    y0_fwd = {}
    for j, (d, c) in enumerate(pairs):
        ry0, ry1 = rys[2 * j], rys[2 * j + 1]
        rp_ref[d, cs(c), :] = jnp.where(lo_c, ry0, swap_c(ry1)).astype(bf16)
        y0 = jnp.where(lo_c, swap_c(ry0), ry1)
        if d == 0:
            y0_fwd[c] = y0
        else:
            y0_ref[cs(c), :] = (y0_fwd[c] + y0).astype(y0_ref.dtype)
    bonus_ref[...] = bonus.astype(bonus_ref.dtype)


N_IN_P = 23
N_OUT_P = 4


def _rwkv_p_multi(*refs, tt, seq, pp):
    ins, outs = refs[:N_IN_P], refs[N_IN_P:N_IN_P + N_OUT_P]
    scr, shx_s = refs[N_IN_P + N_OUT_P:-1], refs[-1]
    x_ref, x_pv, x_nx, mux_ref = ins[9], ins[10], ins[11], ins[15]
    w0_ref, wlu_ref, a0_ref, alu_ref = ins[16:20]
    shift = _token_shift(tt, seq)
    xs = shift(x_ref[0], x_pv[0], x_nx[0], mux_ref[...], shx_s)
    zs = [w0_ref[d:d + 1, :] + _dot(jnp.tanh(xs[:, d * R_LORA:(d + 1) * R_LORA]), wlu_ref[d]) for d in range(2)]
    a_pre = [a0_ref[d:d + 1, :] + _dot(xs[:, (2 + d) * R_LORA:(3 + d) * R_LORA], alu_ref[d]) for d in range(2)]
    stages = []
    for q in range(pp):
        lanes = pl.ds(q * PAIR, PAIR)
        lsl = slice(q * PAIR, (q + 1) * PAIR)
        view = lambda ref, lanes=lanes: ref.at[..., lanes]
        planes = [view(ref) for ref in ins[0:9]]
        gains = [view(ref) for ref in ins[12:15]]
        params = [view(ref) for ref in ins[20:23]]
        y0_ref, bonus_ref, rp_ref, mn_ref = outs
        stages.append(_rwkv_p_body(*planes, *gains, *params,
                                   view(y0_ref), view(bonus_ref), view(rp_ref), mn_ref.at[:, :, pl.ds(2 * q, 2)],
                                   *[s.at[q] for s in scr], tt=tt, shift=shift,
                                   zs=[z[:, lsl] for z in zs], a_pre=[a[:, lsl] for a in a_pre]))
    def run_pre(stage):
        while next(stage) != PRE_DONE:
            pass

    run_pre(stages[0])
    for q in range(pp):
        chunk, pre = stages[q], (stages[q + 1] if q + 1 < pp else None)
        chunk_live, pre_live = True, pre is not None
        while chunk_live or pre_live:
            if chunk_live:
                chunk_live = next(chunk, None) is not None
            if pre_live:
                pre_live = next(pre) != PRE_DONE


def _seg_matrix(n):
    r = lax.broadcasted_iota(jnp.int32, (n, n), 0) // HEAD
    c = lax.broadcasted_iota(jnp.int32, (n, n), 1) // HEAD
    return jnp.where(r == c, 1.0, 0.0).astype(bf16)


def _segsum(x, seg):
    x0 = x.astype(bf16)
    if seg.shape[0] == PAIR:
        return jnp.dot(x0, seg, preferred_element_type=f32)
    x1 = (x - x0.astype(f32)).astype(bf16)
    s = jnp.dot(jnp.concatenate([x0, x1], axis=1), seg, preferred_element_type=f32)
    return s[:, :PAIR] + s[:, PAIR:]


def _rwkv_p(p9, mu_p, w0, wlu, a0, alu, k_k, k_a, r_k, *, tt, seq, pp):
    m = p9.shape[1]
    nt = m // tt
    npair = D_MODEL // PAIR
    cpt = tt // CHUNK
    wl = pp * PAIR
    hb = 2 * SUBLANES

    def tiles(plane):
        return [pl.BlockSpec((1, tt, wl), lambda i, j: (plane, i, j)),
                pl.BlockSpec((1, hb, wl), lambda i, j: (plane, jnp.maximum(i * (tt // hb) - 1, 0), j)),
                pl.BlockSpec((1, hb, wl), lambda i, j: (plane, jnp.minimum((i + 1) * (tt // hb), m // hb - 1), j))]

    vec = pl.BlockSpec((1, wl), lambda i, j: (0, j))
    mu = lambda plane: pl.BlockSpec((1, wl), lambda i, j: (0, plane * (npair // pp) + j))
    wx = 4 * R_LORA
    return pl.pallas_call(
        functools.partial(_rwkv_p_multi, tt=tt, seq=seq, pp=pp),
        grid=(nt, npair // pp),
        in_specs=tiles(P_R) + tiles(P_K) + tiles(P_V) + [
            pl.BlockSpec((1, tt, wx), lambda i, j: (P_X, i, 0)),
            pl.BlockSpec((1, hb, wx), lambda i, j: (P_X, jnp.maximum(i * (tt // hb) - 1, 0), 0)),
            pl.BlockSpec((1, hb, wx), lambda i, j: (P_X, jnp.minimum((i + 1) * (tt // hb), m // hb - 1), 0)),
            mu(P_R), mu(P_K), mu(P_V),
            pl.BlockSpec((1, wx), lambda i, j: (0, P_X * D_MODEL // wx)),
            pl.BlockSpec((2, wl), lambda i, j: (0, j)),
            pl.BlockSpec((2, R_LORA, wl), lambda i, j: (0, 0, j)),
            pl.BlockSpec((2, wl), lambda i, j: (0, j)),
            pl.BlockSpec((2, R_LORA, wl), lambda i, j: (0, 0, j)),
            vec, vec, vec,
        ],
        out_specs=[
            pl.BlockSpec((tt, wl), lambda i, j: (i, j)),
            pl.BlockSpec((tt, wl), lambda i, j: (i, j)),
            pl.BlockSpec((2, tt, wl), lambda i, j: (0, i, j)),
            pl.BlockSpec((2, cpt, 2 * pp, HEAD, PAIR), lambda i, j: (0, i, j, 0, 0)),
        ],
        out_shape=[
            jax.ShapeDtypeStruct((m, D_MODEL), bf16),
            jax.ShapeDtypeStruct((m, D_MODEL), bf16),
            jax.ShapeDtypeStruct((2, m, D_MODEL), bf16),
            jax.ShapeDtypeStruct((2, m // CHUNK, N_HEADS, HEAD, PAIR), bf16),
        ],
        scratch_shapes=(
            [pltpu.VMEM((pp, 2, tt, PAIR), bf16) for _ in range(4)]
            + [pltpu.VMEM((pp, 2, 2, tt, PAIR), bf16) for _ in range(3)]
            + [pltpu.VMEM((pp, 2, 2, tt, PAIR), f32),
               pltpu.VMEM((pp, 2, tt, PAIR), bf16),
               pltpu.VMEM((pp, 2, tt, PAIR), f32),
               pltpu.VMEM((pp, 3, tt + 2 * PADR, PAIR), f32),
               pltpu.VMEM((tt + 2 * PADR, wx), f32)]),
        compiler_params=pltpu.CompilerParams(
            dimension_semantics=("parallel", "parallel"), vmem_limit_bytes=VMEM_LIMIT),
        name="rwkv_p",
    )(*([p9] * 12), mu_p, mu_p, mu_p, mu_p, w0, wlu, a0, alu, k_k, k_a, r_k)


def _rwkv_s_body(mnf_ref, mnb_ref, sf_ref, sb_ref, st_ref, *, grp):
    @pl.when(pl.program_id(1) == 0)
    def _():
        st_ref[...] = jnp.zeros_like(st_ref)

    zero = jnp.zeros((HEAD, HEAD), bf16)

    def step(d, mn_ref, s_out_ref, cc):
        s = st_ref[d].astype(bf16)
        for p in range(N_HEADS // 2):
            s_out_ref[cc, p] = jnp.concatenate([jnp.concatenate([s[2 * p], zero], axis=1),
                                                jnp.concatenate([zero, s[2 * p + 1]], axis=1)], axis=0)
        m_t = mn_ref[0, cc, :, :, 0:HEAD]
        n_t = mn_ref[0, cc, :, :, HEAD:].astype(f32)
        st_ref[d] = jnp.einsum('hij,hjk->hik', m_t, s, preferred_element_type=f32) + n_t

    for cc in range(grp):
        step(0, mnf_ref, sf_ref, cc)
        step(1, mnb_ref, sb_ref, grp - 1 - cc)


def _rwkv_s(mn, *, n_seq, grp):
    nchunk = mn.shape[1]
    npair = N_HEADS // 2
    ng = nchunk // n_seq // grp
    blk_in = (1, grp, N_HEADS, HEAD, PAIR)
    blk_out = (grp, npair, PAIR, PAIR)
    out = jax.ShapeDtypeStruct((nchunk, npair, PAIR, PAIR), bf16)
    return pl.pallas_call(
        functools.partial(_rwkv_s_body, grp=grp),
        grid=(n_seq, ng),
        in_specs=[pl.BlockSpec(blk_in, lambda b, g: (0, b * ng + g, 0, 0, 0)),
                  pl.BlockSpec(blk_in, lambda b, g: (1, b * ng + ng - 1 - g, 0, 0, 0))],
        out_specs=[pl.BlockSpec(blk_out, lambda b, g: (b * ng + g, 0, 0, 0)),
                   pl.BlockSpec(blk_out, lambda b, g: (b * ng + ng - 1 - g, 0, 0, 0))],
        out_shape=[out, out],
        scratch_shapes=[pltpu.VMEM((2, N_HEADS, HEAD, HEAD), f32)],
        compiler_params=pltpu.CompilerParams(
            dimension_semantics=("parallel", "arbitrary"), vmem_limit_bytes=VMEM_LIMIT),
        name="rwkv_s",
    )(mn, mn)


def _rwkv_f_body(y0_ref, bonus_ref, rp_ref, sf_ref, sb_ref, gate_ref, lng_ref, lnb_ref, o_ref, *, tt):
    npair = D_MODEL // PAIR
    nc = tt // CHUNK
    cs = lambda c: slice(c * CHUNK, (c + 1) * CHUNK)
    ps = lambda p: slice(p * PAIR, (p + 1) * PAIR)
    yf = [[jnp.dot(rp_ref[0, cs(c), ps(p)], sf_ref[c, p], preferred_element_type=f32) for c in range(nc)]
          for p in range(npair)]
    yb = [[jnp.dot(rp_ref[1, cs(c), ps(p)], sb_ref[c, p], preferred_element_type=f32) for c in range(nc)]
          for p in range(npair)]
    seg1, seg2 = _seg_matrix(PAIR), _seg_matrix(2 * PAIR)
    ys = [y0_ref[:, ps(p)].astype(f32) + jnp.concatenate(yf[p], axis=0) + jnp.concatenate(yb[p], axis=0)
          for p in range(npair)]
    mus = [_segsum(y, seg2) * (1.0 / HEAD) for y in ys]
    dvs = [y - mu for y, mu in zip(ys, mus)]
    vrs = [_segsum(dv * dv, seg1) * (1.0 / HEAD) for dv in dvs]
    for p in range(npair):
        yn = dvs[p] * lax.rsqrt(vrs[p] + GN_EPS) * lng_ref[:, ps(p)] + lnb_ref[:, ps(p)] + bonus_ref[:, ps(p)].astype(f32)
        g = gate_ref[0, :, ps(p)].astype(f32)
        o_ref[:, ps(p)] = (yn * (g * _sigmoid(g))).astype(bf16)


def _rwkv_f(y0, bonus, rp, sf, sb, p9, ln_g, ln_b, *, tt):
    m = y0.shape[0]
    cpt = tt // CHUNK
    npair = D_MODEL // PAIR
    tile = pl.BlockSpec((tt, D_MODEL), lambda i: (i, 0))
    vec = pl.BlockSpec((1, D_MODEL), lambda i: (0, 0))
    state = pl.BlockSpec((cpt, npair, PAIR, PAIR), lambda i: (i, 0, 0, 0))
    return pl.pallas_call(
        functools.partial(_rwkv_f_body, tt=tt),
        grid=(m // tt,),
        in_specs=[
            tile, tile,
            pl.BlockSpec((2, tt, D_MODEL), lambda i: (0, i, 0)),
            state, state,
            pl.BlockSpec((1, tt, D_MODEL), lambda i: (P_GA, i, 0)),
            vec, vec,
        ],
        out_specs=tile,
        out_shape=jax.ShapeDtypeStruct((m, D_MODEL), bf16),
        compiler_params=pltpu.CompilerParams(dimension_semantics=("parallel",), vmem_limit_bytes=VMEM_LIMIT),
        name="rwkv_f",
    )(y0, bonus, rp, sf, sb, p9, ln_g, ln_b)


def _attn_body(q_ref, kp_ref, kc_ref, kn_ref, vp_ref, vc_ref, vn_ref, tp_ref, tc_ref, tn_ref, gate_ref,
               qg_ref, kg_ref, sink_ref, o_ref, *, seq, qb):
    step_rows = qb * BLOCK
    start = (pl.program_id(0) % (seq // step_rows)) * step_rows
    ctx = BLOCK + 2 * WINDOW
    qoff = lax.broadcasted_iota(jnp.int32, (BLOCK, ctx), 0)
    koff = lax.broadcasted_iota(jnp.int32, (BLOCK, ctx), 1) - WINDOW
    valid = []
    for b in range(qb):
        kpos = start + b * BLOCK + koff
        valid.append((jnp.abs(qoff - koff) <= WINDOW) & (kpos >= 0) & (kpos < seq))

    seg = _seg_matrix(PAIR)
    swap = lambda t: pltpu.roll(t, HEAD, axis=1)

    def norm_rope(x, gain, tab):
        lane = lax.broadcasted_iota(jnp.int32, x.shape, 1)
        xn = x * lax.rsqrt(_segsum(x * x, seg) * (1.0 / HEAD) + NORM_EPS) * gain
        rot = jnp.where(lane % HEAD < HEAD // 2, pltpu.roll(xn, PAIR - HEAD // 2, axis=1),
                        pltpu.roll(xn, HEAD // 2, axis=1))
        return xn * tab[:, :PAIR] + rot * tab[:, PAIR:]

    def place(t, e, want_lo):
        lo = lax.broadcasted_iota(jnp.int32, t.shape, 1) < HEAD
        src = t if (e == 0) == want_lo else swap(t)
        return (jnp.where(lo, src, 0.0) if want_lo else jnp.where(lo, 0.0, src)).astype(bf16)

    tab_k = jnp.concatenate([tp_ref[...], tc_ref[...], tn_ref[...]], axis=0)
    tab_q = tc_ref[...]
    k_lo, k_hi, v_lo, v_hi = [], [], [], []
    for kp in range(D_KV // PAIR):
        ps_ = slice(kp * PAIR, (kp + 1) * PAIR)
        kx = jnp.concatenate([kp_ref[0, :, ps_], kc_ref[0, :, ps_], kn_ref[0, :, ps_]], axis=0)
        vx = jnp.concatenate([vp_ref[0, :, ps_], vc_ref[0, :, ps_], vn_ref[0, :, ps_]], axis=0).astype(f32)
        kr = norm_rope(kx.astype(f32), kg_ref[...], tab_k)
        for e in range(2):
            k_lo.append(place(kr, e, True))
            k_hi.append(place(kr, e, False))
            v_lo.append(place(vx, e, True))
            v_hi.append(place(vx, e, False))
    npair = D_MODEL // PAIR
    psl = lambda p: slice(p * PAIR, (p + 1) * PAIR)
    log2e = math.log2(math.e)
    qrs = [(norm_rope(q_ref[0, :, psl(p)].astype(f32), qg_ref[...], tab_q) * (HEAD ** -0.5 * log2e)).astype(bf16)
           for p in range(npair)]
    inst = [(b, hq) for b in range(qb) for hq in range(N_HEADS)]
    rows = lambda b: slice(b * BLOCK, (b + 1) * BLOCK)
    keys = lambda b: slice(b * BLOCK, b * BLOCK + ctx)
    def masked(s, ok):
        return jnp.concatenate([jnp.where(ok[:, :WINDOW], s[:, :WINDOW], -1e30), s[:, WINDOW:WINDOW + BLOCK],
                                jnp.where(ok[:, WINDOW + BLOCK:], s[:, WINDOW + BLOCK:], -1e30)], axis=1)

    ss = [masked(lax.dot_general(qrs[hq // 2][rows(b)], (k_lo if hq % 2 == 0 else k_hi)[hq // GRP][keys(b)],
                                 (((1,), (1,)), ((), ())), preferred_element_type=f32), valid[b])
          for b, hq in inst]
    sinks = [sink_ref[0:1, hq:hq + 1] * log2e for b, hq in inst]
    mxs = [jnp.maximum(jnp.max(s, axis=-1, keepdims=True), sk) for s, sk in zip(ss, sinks)]
    ps = [jnp.exp2(s - mx) for s, mx in zip(ss, mxs)]
    dens = [jnp.sum(p, axis=-1, keepdims=True) + jnp.exp2(sk - mx) for p, sk, mx in zip(ps, sinks, mxs)]
    os_ = [jnp.dot(p.astype(bf16), (v_lo if hq % 2 == 0 else v_hi)[hq // GRP][keys(b)], preferred_element_type=f32) / den
           for (b, hq), p, den in zip(inst, ps, dens)]
    for b in range(qb):
        for p in range(npair):
            gt = gate_ref[0, rows(b), psl(p)].astype(f32)
            i0 = b * N_HEADS + 2 * p
            o_ref[rows(b), psl(p)] = ((os_[i0] + os_[i0 + 1]) * (gt * _sigmoid(gt))).astype(bf16)


def _attn(p9, rope_tab, q_g, k_g, sink, *, seq, qb):
    m = p9.shape[1]
    nblk = seq // BLOCK
    rows = qb * BLOCK

    def nbr(delta):
        def f(i):
            t = (i * qb) % nblk
            return i * qb - t + jnp.clip(t + delta, 0, nblk - 1)
        return f

    def kv(col):
        return [pl.BlockSpec((1, BLOCK, D_KV), lambda i, f=nbr(-1): (P_X, f(i), col)),
                pl.BlockSpec((1, rows, D_KV), lambda i: (P_X, i, col)),
                pl.BlockSpec((1, BLOCK, D_KV), lambda i, f=nbr(qb): (P_X, f(i), col))]

    tabs = [pl.BlockSpec((BLOCK, 2 * PAIR), lambda i, f=nbr(-1): (f(i) % nblk, 0)),
            pl.BlockSpec((rows, 2 * PAIR), lambda i: (i % (nblk // qb), 0)),
            pl.BlockSpec((BLOCK, 2 * PAIR), lambda i, f=nbr(qb): (f(i) % nblk, 0))]

    return pl.pallas_call(
        functools.partial(_attn_body, seq=seq, qb=qb),
        grid=(m // rows,),
        in_specs=[pl.BlockSpec((1, rows, D_MODEL), lambda i: (P_Q, i, 0))] + kv(1) + kv(2) + tabs + [
            pl.BlockSpec((1, rows, D_MODEL), lambda i: (P_GB, i, 0)),
            pl.BlockSpec((1, PAIR), lambda i: (0, 0)),
            pl.BlockSpec((1, PAIR), lambda i: (0, 0)),
            pl.BlockSpec((1, N_HEADS), lambda i: (0, 0)),
        ],
        out_specs=pl.BlockSpec((rows, D_MODEL), lambda i: (i, 0)),
        out_shape=jax.ShapeDtypeStruct((m, D_MODEL), bf16),
        compiler_params=pltpu.CompilerParams(dimension_semantics=("parallel",), vmem_limit_bytes=VMEM_LIMIT),
        name="attn",
    )(*([p9] * 7), rope_tab, rope_tab, rope_tab, p9, q_g, k_g, sink)


def _rope_table(seq):
    inv = 1.0 / (ROPE_THETA ** (jnp.arange(0, HEAD, 2, dtype=f32) / HEAD))
    ang = jnp.arange(seq, dtype=f32)[:, None] * inv[None, :]
    cos, sin = jnp.cos(ang), jnp.sin(ang)
    return jnp.concatenate([cos, cos, cos, cos, -sin, sin, -sin, sin], axis=1)


def _out_proj_body(x_ref, ga_ref, gb_ref, ma_ref, mb_ref, wa_ref, wb_ref, wo_ref, o_ref):
    ya = jnp.dot(ga_ref[...], wa_ref[...], preferred_element_type=f32)
    yb = jnp.dot(gb_ref[...], wb_ref[...], preferred_element_type=f32)
    mixed = _sigmoid(ma_ref[0].astype(f32)) * ya + _sigmoid(mb_ref[0].astype(f32)) * yb
    o_ref[...] = x_ref[...] + jnp.dot(mixed.astype(bf16), wo_ref[...], preferred_element_type=f32)


def _out_proj(x2d, ga, gb, p9, wa, wb, wo, *, tm):
    m = x2d.shape[0]
    row = pl.BlockSpec((tm, D_MODEL), lambda i: (i, 0))
    wspec = pl.BlockSpec((D_MODEL, D_MODEL), lambda i: (0, 0))
    return pl.pallas_call(
        _out_proj_body,
        grid=(m // tm,),
        in_specs=[row, row, row,
                  pl.BlockSpec((1, tm, D_MODEL), lambda i: (P_MA, i, 0)),
                  pl.BlockSpec((1, tm, D_MODEL), lambda i: (P_MB, i, 0)),
                  wspec, wspec, wspec],
        out_specs=row,
        out_shape=jax.ShapeDtypeStruct((m, D_MODEL), f32),
        compiler_params=pltpu.CompilerParams(dimension_semantics=("parallel",), vmem_limit_bytes=VMEM_LIMIT),
        name="out_proj",
    )(x2d, ga, gb, p9, p9, wa, wb, wo)


def _layer(x2d, lw, rope_tab, *, n_seq, seq):
    for tile in (TM_IN, TM_OUT, TT_P, TT_F, GRP_S * CHUNK, QB_ATTN * BLOCK):
        assert seq % tile == 0, (seq, tile)
    p9 = _in_proj(x2d, lw["norm_g"], lw["w_p"], tm=TM_IN)
    y0, bonus, rp, mn = _rwkv_p(p9, lw["mu_p"], lw["w0"], lw["wlu"], lw["a0"], lw["alu"], lw["k_k"], lw["k_a"],
                                lw["r_k"], tt=TT_P, seq=seq, pp=PAIRS_P)
    sf, sb = _rwkv_s(mn, n_seq=n_seq, grp=GRP_S)
    ga = _rwkv_f(y0, bonus, rp, sf, sb, p9, lw["ln_g"], lw["ln_b"], tt=TT_F)
    gb = _attn(p9, rope_tab, lw["q_g"], lw["k_g"], lw["sink"], seq=seq, qb=QB_ATTN)
    return _out_proj(x2d, ga, gb, p9, lw["wa"], lw["wb"], lw["wo"], tm=TM_OUT)


def _trunk(x, layers):
    n_seq, seq, _ = x.shape
    rope_tab = _rope_table(seq)
    x2d = x.reshape(n_seq * seq, D_MODEL)
    for lw in layers:
        x2d = _layer(x2d, lw, rope_tab, n_seq=n_seq, seq=seq)
    return x2d.reshape(n_seq, seq, D_MODEL)


def _prep_layers(norm_g, w_in, shift_mu, w0, w_lora_up, a0, a_lora_up, k_k, k_a, r_k, ln_x_g, ln_x_b,
                 q_norm_g, k_norm_g, sink, w_proj_a, w_proj_b, w_out):
    layers = []
    row = lambda v: v.reshape(1, -1).astype(f32)
    for l in range(norm_g.shape[0]):
        w_p, mu_p = _prep_layer_weights(w_in[l], shift_mu[l])
        layers.append(dict(
            norm_g=row(norm_g[l]), w_p=w_p, mu_p=mu_p,
            w0=w0[l].astype(f32), wlu=w_lora_up[l].astype(bf16), a0=a0[l].astype(f32), alu=a_lora_up[l].astype(bf16),
            k_k=row(k_k[l]), k_a=row(k_a[l]), r_k=row(r_k[l]), ln_g=row(ln_x_g[l]), ln_b=row(ln_x_b[l]),
            q_g=row(jnp.tile(q_norm_g[l], 2)), k_g=row(jnp.tile(k_norm_g[l], 2)), sink=row(sink[l]),
            wa=w_proj_a[l].astype(bf16), wb=w_proj_b[l].astype(bf16), wo=w_out[l].astype(bf16)))
    return layers


def kernel(x_prompt, x_sample, norm_g, w_in, shift_mu, w0, w_lora_up, a0, a_lora_up, k_k, k_a, r_k,
           ln_x_g, ln_x_b, q_norm_g, k_norm_g, sink, w_proj_a, w_proj_b, w_out):
    layers = _prep_layers(norm_g, w_in, shift_mu, w0, w_lora_up, a0, a_lora_up, k_k, k_a, r_k, ln_x_g, ln_x_b,
                          q_norm_g, k_norm_g, sink, w_proj_a, w_proj_b, w_out)
    return _trunk(x_prompt, layers), _trunk(x_sample, layers)
```
